```python
import jax, jax.numpy as jnp
from jax import lax
import numpy as np

D_MODEL = 1024
BATCH = 8
SEQ = 2048
DEPTH = 2
DEC_BATCH = 128
DEC_SEQ = 1
PAST_LEN = 16384
PAGE_SIZE = 128

MIX_WIDTH = D_MODEL
CONV_WIDTH = MIX_WIDTH // 2
POOL_WIDTH = MIX_WIDTH - CONV_WIDTH
CONV_HEADS = 8
CONV_K = 3
POOL_WINDOWS = (2, 4, 8, 16)
N_POOL_GROUPS = len(POOL_WINDOWS)
POOL_GROUP = POOL_WIDTH // N_POOL_GROUPS
POOL_HIST = max(POOL_WINDOWS) - 1
IN_COLS = 3 * CONV_WIDTH + POOL_WIDTH
D_FF = ((8 * D_MODEL // 3) + 127) // 128 * 128
N_EXPERTS = 8
TOP_K = 2
D_FF_EXPERT = (7 * D_MODEL) // 2
N_DENSE = (DEPTH + 1) // 2
N_MOE = DEPTH // 2
EPS = 1e-6
ADA_SCALE = 0.5

kernel_name = "hybrid_conv_pool_adaln_moe_decode_step"


def rms_norm(x, g):
    xf = x.astype(jnp.float32)
    y = xf * lax.rsqrt(jnp.mean(xf * xf, axis=-1, keepdims=True) + EPS)
    return (y * g.astype(jnp.float32)).astype(x.dtype)


def swiglu(h, wg, wu, wd):
    a = jnp.einsum('bld,df->blf', h, wg)
    b = jnp.einsum('bld,df->blf', h, wu)
    return jnp.einsum('blf,fd->bld', jax.nn.silu(a) * b, wd)


def short_conv_mixer(b_gate, c_gate, hv, conv_buf, conv_w):
    v = c_gate * hv
    z = jnp.concatenate([conv_buf.astype(v.dtype), v], axis=1)
    y = lax.conv_general_dilated(
        z, conv_w[:, None, :].astype(v.dtype), window_strides=(1,), padding='VALID',
        dimension_numbers=('NWC', 'WIO', 'NWC'), feature_group_count=CONV_WIDTH)
    return b_gate * y, z[:, -(CONV_K - 1):]


def pool_mixer(u, pool_buf, start, pool_w, pool_scale):
    L = u.shape[1]
    z = jnp.concatenate([pool_buf.astype(u.dtype), u], axis=1)
    s = jnp.cumsum(z.astype(jnp.float32), axis=1)
    s = jnp.pad(s, ((0, 0), (1, 0), (0, 0)))
    pos = start + jnp.arange(L, dtype=jnp.int32)
    uf = u.astype(jnp.float32)
    outs = []
    for g, w in enumerate(POOL_WINDOWS):
        sl = slice(g * POOL_GROUP, (g + 1) * POOL_GROUP)
        win = s[:, POOL_HIST + 1:POOL_HIST + 1 + L, sl] - s[:, POOL_HIST + 1 - w:POOL_HIST + 1 - w + L, sl]
        cnt = jnp.minimum(pos + 1, w).astype(jnp.float32)[None, :, None]
        d = (win / cnt - uf[..., sl]).astype(u.dtype)
        outs.append(jnp.einsum('blc,cd->bld', d, pool_w[g]))
    y = jnp.concatenate(outs, axis=-1) * pool_scale
    return y, z[:, -POOL_HIST:]


def moe_swiglu(h, router_w, wg, wu, wd):
    logits = jnp.einsum('bld,de->ble', h, router_w).astype(jnp.float32)
    top_v, top_i = lax.top_k(logits, TOP_K)
    probs = jax.nn.softmax(top_v, axis=-1)
    gates = jnp.sum(probs[..., None] * jax.nn.one_hot(top_i, N_EXPERTS, dtype=jnp.float32), axis=-2)
    gates = gates.astype(h.dtype)
    out = jnp.zeros_like(h)
    for e in range(N_EXPERTS):
        out = out + gates[..., e:e + 1] * swiglu(h, wg[e], wu[e], wd[e])
    return out


def run_trunk(x, c, conv_state, pool_state, start,
              w_ada, b_ada, g_mix, g_ffn, w_in, conv_w, pool_w, pool_scale,
              g_conv_out, g_pool_out, w_out, dense_w_gate, dense_w_up, dense_w_down,
              router_w, moe_w_gate, moe_w_up, moe_w_down, g_final):
    Bn = x.shape[0]
    new_conv, new_pool = [], []
    for i in range(DEPTH):
        if conv_state is None:
            cbuf = jnp.zeros((Bn, CONV_K - 1, CONV_WIDTH), x.dtype)
            pbuf = jnp.zeros((Bn, POOL_HIST, POOL_WIDTH), x.dtype)
        else:
            cbuf, pbuf = conv_state[i], pool_state[i]
        mod = (jax.nn.silu(c) @ w_ada[i] + b_ada[i])[:, None, :]
        sh1, sc1, gt1, sh2, sc2, gt2 = jnp.split(mod, 6, axis=-1)
        h = rms_norm(x, g_mix[i]) * (1 + sc1) + sh1
        p = jnp.einsum('bld,dn->bln', h, w_in[i])
        b_g = p[..., :CONV_WIDTH]
        c_g = p[..., CONV_WIDTH:2 * CONV_WIDTH]
        hv = p[..., 2 * CONV_WIDTH:3 * CONV_WIDTH]
        u = p[..., 3 * CONV_WIDTH:]
        ya, cb_new = short_conv_mixer(b_g, c_g, hv, cbuf, conv_w[i])
        yb, pb_new = pool_mixer(u, pbuf, start, pool_w[i], pool_scale[i])
        m = jnp.concatenate([rms_norm(ya, g_conv_out[i]), rms_norm(yb, g_pool_out[i])], axis=-1)
        x = x + gt1 * jnp.einsum('blm,md->bld', m, w_out[i])
        h = rms_norm(x, g_ffn[i]) * (1 + sc2) + sh2
        j = i // 2
        if i % 2 == 0:
            f = swiglu(h, dense_w_gate[j], dense_w_up[j], dense_w_down[j])
        else:
            f = moe_swiglu(h, router_w[j], moe_w_gate[j], moe_w_up[j], moe_w_down[j])
        x = x + gt2 * f
        new_conv.append(cb_new)
        new_pool.append(pb_new)
    y = rms_norm(x, g_final)
    return y, jnp.stack(new_conv, axis=0), jnp.stack(new_pool, axis=0)


def setup_inputs(seed: int = 0) -> dict:
    key = jax.random.key(seed)
    ks = jax.random.split(key, 28)
    f32 = jnp.float32
    nrm = lambda k, shape, s: jax.random.normal(k, shape, f32) * s
    D = D_MODEL
    return {
        "x_prompt": nrm(ks[0], (BATCH, SEQ, D), 1.0),
        "x_sample": nrm(ks[1], (DEC_BATCH, DEC_SEQ, D), 1.0),
        "c_prompt": nrm(ks[2], (BATCH, D), 1.0),
        "c_sample": nrm(ks[3], (DEC_BATCH, D), 1.0),
        "state_conv": nrm(ks[4], (DEPTH, DEC_BATCH, CONV_K - 1, CONV_WIDTH), 1.0),
        "state_pool": nrm(ks[5], (DEPTH, DEC_BATCH, POOL_HIST, POOL_WIDTH), 1.0),
        "w_ada": nrm(ks[6], (DEPTH, D, 6 * D), ADA_SCALE * D ** -0.5),
        "b_ada": nrm(ks[7], (DEPTH, 6 * D), 0.02),
        "g_mix": 1.0 + nrm(ks[8], (DEPTH, D), 0.05),
        "g_ffn": 1.0 + nrm(ks[9], (DEPTH, D), 0.05),
        "w_in": nrm(ks[10], (DEPTH, D, IN_COLS), D ** -0.5),
        "conv_w": nrm(ks[11], (DEPTH, CONV_K, CONV_WIDTH), CONV_K ** -0.5),
        "pool_w": nrm(ks[12], (DEPTH, N_POOL_GROUPS, POOL_GROUP, POOL_GROUP), POOL_GROUP ** -0.5),
        "pool_scale": 1.0 + nrm(ks[13], (DEPTH, POOL_WIDTH), 0.1),
        "g_conv_out": 1.0 + nrm(ks[14], (DEPTH, CONV_WIDTH), 0.05),
        "g_pool_out": 1.0 + nrm(ks[15], (DEPTH, POOL_WIDTH), 0.05),
        "w_out": nrm(ks[16], (DEPTH, MIX_WIDTH, D), MIX_WIDTH ** -0.5),
        "dense_w_gate": nrm(ks[17], (N_DENSE, D, D_FF), D ** -0.5),
        "dense_w_up": nrm(ks[18], (N_DENSE, D, D_FF), D ** -0.5),
        "dense_w_down": nrm(ks[19], (N_DENSE, D_FF, D), D_FF ** -0.5),
        "router_w": nrm(ks[20], (N_MOE, D, N_EXPERTS), D ** -0.5),
        "moe_w_gate": nrm(ks[21], (N_MOE, N_EXPERTS, D, D_FF_EXPERT), D ** -0.5),
        "moe_w_up": nrm(ks[22], (N_MOE, N_EXPERTS, D, D_FF_EXPERT), D ** -0.5),
        "moe_w_down": nrm(ks[23], (N_MOE, N_EXPERTS, D_FF_EXPERT, D), D_FF_EXPERT ** -0.5),
        "g_final": 1.0 + nrm(ks[24], (D,), 0.05),
    }


def reference(x_prompt, x_sample, c_prompt, c_sample, state_conv, state_pool,
              w_ada, b_ada, g_mix, g_ffn, w_in, conv_w, pool_w, pool_scale,
              g_conv_out, g_pool_out, w_out, dense_w_gate, dense_w_up, dense_w_down,
              router_w, moe_w_gate, moe_w_up, moe_w_down, g_final):
    params = (w_ada, b_ada, g_mix, g_ffn, w_in, conv_w, pool_w, pool_scale,
              g_conv_out, g_pool_out, w_out, dense_w_gate, dense_w_up, dense_w_down,
              router_w, moe_w_gate, moe_w_up, moe_w_down, g_final)
    y_prompt, conv_p, pool_p = run_trunk(x_prompt, c_prompt, None, None, 0, *params)
    y_sample, conv_s, pool_s = run_trunk(x_sample, c_sample, state_conv, state_pool, PAST_LEN, *params)
    return (y_prompt, y_sample, conv_p, pool_p, conv_s, pool_s)
```

```python
import functools

import jax
import jax.numpy as jnp
from jax import lax
from jax.experimental import pallas as pl
from jax.experimental.pallas import tpu as pltpu

F32 = jnp.float32
BF16 = jnp.bfloat16

EPS = 1e-6
CONV_K = 3
POOL_WINDOWS = (2, 4, 8, 16)
POOL_HIST = max(POOL_WINDOWS) - 1
TOP_K = 2

CONV_PAD = 8
POOL_PAD = 16

VMEM_LIMIT_BYTES = 56 * 1024 * 1024


def _params(*sem):
    return pltpu.CompilerParams(dimension_semantics=sem, vmem_limit_bytes=VMEM_LIMIT_BYTES)


def _resident(shape):
    nd = len(shape)
    return pl.BlockSpec(shape, lambda *_: (0,) * nd, pipeline_mode=pl.Buffered(1))


def _rms(x, g):
    ms = jnp.mean(x * x, axis=-1, keepdims=True)
    return x * lax.rsqrt(ms + EPS) * g


def _mod_norm(x, g, sc, sh):
    ms = jnp.mean(x * x, axis=-1, keepdims=True)
    return x * lax.rsqrt(ms + EPS) * (g * (1.0 + sc)) + sh


def _dot(a, b):
    return jnp.dot(a, b, preferred_element_type=F32)


def _ada_kernel(c_ref, w_ref, b_ref, o_ref):
    c = c_ref[...]
    a = (c * jax.nn.sigmoid(c)).astype(BF16)
    o_ref[0] = _dot(a, w_ref[0].astype(BF16)) + b_ref[0]


def _ada(c_all, w_ada, b_ada):
    depth, d, n = w_ada.shape
    m = c_all.shape[0]
    tn = 1024
    return pl.pallas_call(
        _ada_kernel,
        grid=(depth, n // tn),
        in_specs=[
            pl.BlockSpec((m, d), lambda i, j: (0, 0)),
            pl.BlockSpec((1, d, tn), lambda i, j: (i, 0, j)),
            pl.BlockSpec((1, 1, tn), lambda i, j: (i, 0, j)),
        ],
        out_specs=pl.BlockSpec((1, m, tn), lambda i, j: (i, 0, j)),
        out_shape=jax.ShapeDtypeStruct((depth, m, n), F32),
        compiler_params=_params("arbitrary", "arbitrary"),
        name="ada",
    )(c_all, w_ada, b_ada.reshape(depth, 1, n))


def _mix_tail(x, gt, bg, y, u_groups, d_groups, poolw_ref, pscale_ref, gco_ref, gpo_ref, wout_ref):
    del u_groups
    cw = gco_ref.shape[-1]
    ya = bg * y
    yb = jnp.concatenate(
        [_dot(d.astype(BF16), poolw_ref[g]) for g, d in enumerate(d_groups)], axis=-1
    ) * pscale_ref[...]
    ma = _rms(ya, gco_ref[...]).astype(BF16)
    mb = _rms(yb, gpo_ref[...]).astype(BF16)
    o = _dot(ma, wout_ref[0:cw, :]) + _dot(mb, wout_ref[cw:, :])
    return x + gt * o


def _mix_prompt_kernel(x_ref, sh_ref, sc_ref, gt_ref, g_ref, win_ref, convw_ref, poolw_ref,
                       pscale_ref, gco_ref, gpo_ref, wout_ref,
                       xo_ref, cs_ref, ps_ref, vbuf, ubuf, *, tile):
    l = pl.program_id(1)
    cw = convw_ref.shape[-1]
    pg = poolw_ref.shape[-1]

    @pl.when(l == 0)
    def _():
        vbuf[0:CONV_PAD, :] = jnp.zeros((CONV_PAD, cw), F32)
        ubuf[0:POOL_PAD, :] = jnp.zeros((POOL_PAD, ubuf.shape[-1]), F32)

    x = x_ref[0]
    h = _mod_norm(x, g_ref[...], sc_ref[0], sh_ref[0]).astype(BF16)
    p = _dot(h, win_ref[...])
    bg = p[:, 0:cw]
    v = p[:, cw:2 * cw] * p[:, 2 * cw:3 * cw]
    u = p[:, 3 * cw:]
    vbuf[CONV_PAD:CONV_PAD + tile, :] = v
    ubuf[POOL_PAD:POOL_PAD + tile, :] = u

    w = convw_ref[...]
    y = w[CONV_K - 1:CONV_K] * v
    for k in range(1, CONV_K):
        y = y + w[CONV_K - 1 - k:CONV_K - k] * vbuf[CONV_PAD - k:CONV_PAD - k + tile, :]

    pos = l * tile + lax.broadcasted_iota(jnp.int32, (tile, 1), 0)
    d_groups = []
    for g, win in enumerate(POOL_WINDOWS):
        ug = u[:, g * pg:(g + 1) * pg]
        acc = ug
        for k in range(1, win):
            acc = acc + ubuf[POOL_PAD - k:POOL_PAD - k + tile, g * pg:(g + 1) * pg]
        cnt = jnp.minimum(pos + 1, win).astype(F32)
        d_groups.append(acc / cnt - ug)

    xo_ref[0] = _mix_tail(x, gt_ref[0], bg, y, None, d_groups, poolw_ref, pscale_ref,
                          gco_ref, gpo_ref, wout_ref)

    cs_ref[0] = vbuf[CONV_PAD + tile - (CONV_K - 1):CONV_PAD + tile, :]
    ps_ref[0] = ubuf[POOL_PAD + tile - POOL_HIST:POOL_PAD + tile, :]
    vbuf[0:CONV_PAD, :] = vbuf[tile:tile + CONV_PAD, :]
    ubuf[0:POOL_PAD, :] = ubuf[tile:tile + POOL_PAD, :]


def _mix_prompt(x, sh, sc, gt, g, w_in, conv_w, pool_w, pool_scale, g_co, g_po, w_out, tile):
    b, seq, d = x.shape
    cw = conv_w.shape[-1]
    pw = pool_scale.shape[-1]
    row = lambda bi, li: (bi, 0, 0)
    return pl.pallas_call(
        functools.partial(_mix_prompt_kernel, tile=tile),
        grid=(b, seq // tile),
        in_specs=[
            pl.BlockSpec((1, tile, d), lambda bi, li: (bi, li, 0)),
            pl.BlockSpec((1, 1, d), row),
            pl.BlockSpec((1, 1, d), row),
            pl.BlockSpec((1, 1, d), row),
            _resident((1, d)),
            _resident(w_in.shape),
            _resident(conv_w.shape),
            _resident(pool_w.shape),
            _resident((1, pw)),
            _resident((1, cw)),
            _resident((1, pw)),
            _resident(w_out.shape),
        ],
        out_specs=[
            pl.BlockSpec((1, tile, d), lambda bi, li: (bi, li, 0)),
            pl.BlockSpec((1, CONV_K - 1, cw), row),
            pl.BlockSpec((1, POOL_HIST, pw), row),
        ],
        out_shape=[
            jax.ShapeDtypeStruct((b, seq, d), F32),
            jax.ShapeDtypeStruct((b, CONV_K - 1, cw), F32),
            jax.ShapeDtypeStruct((b, POOL_HIST, pw), F32),
        ],
        scratch_shapes=[
            pltpu.VMEM((CONV_PAD + tile, cw), F32),
            pltpu.VMEM((POOL_PAD + tile, pw), F32),
        ],
        compiler_params=_params("arbitrary", "arbitrary"),
        name="mix_prompt",
    )(x, sh, sc, gt, g.reshape(1, d), w_in, conv_w, pool_w, pool_scale.reshape(1, pw),
      g_co.reshape(1, cw), g_po.reshape(1, pw), w_out)


def _mix_sample_kernel(x_ref, sh_ref, sc_ref, gt_ref, g_ref, win_ref, convw_ref, poolw_ref,
                       pscale_ref, gco_ref, gpo_ref, wout_ref, cb_ref, pb_ref,
                       xo_ref, v_ref, u_ref):
    cw = convw_ref.shape[-1]
    pg = poolw_ref.shape[-1]
    x = x_ref[...]
    h = _mod_norm(x, g_ref[...], sc_ref[...], sh_ref[...]).astype(BF16)
    p = _dot(h, win_ref[...])
    bg = p[:, 0:cw]
    v = p[:, cw:2 * cw] * p[:, 2 * cw:3 * cw]
    u = p[:, 3 * cw:]
    v_ref[...] = v
    u_ref[...] = u

    w = convw_ref[...]
    y = w[CONV_K - 1:CONV_K] * v
    for k in range(1, CONV_K):
        y = y + w[CONV_K - 1 - k:CONV_K - k] * cb_ref[CONV_K - 1 - k]

    d_groups = []
    for g, win in enumerate(POOL_WINDOWS):
        ug = u[:, g * pg:(g + 1) * pg]
        acc = ug
        for k in range(1, win):
            acc = acc + pb_ref[POOL_HIST - k, :, g * pg:(g + 1) * pg]
        d_groups.append(acc / float(win) - ug)

    xo_ref[...] = _mix_tail(x, gt_ref[...], bg, y, None, d_groups, poolw_ref, pscale_ref,
                            gco_ref, gpo_ref, wout_ref)


def _mix_sample(x, sh, sc, gt, g, w_in, conv_w, pool_w, pool_scale, g_co, g_po, w_out, cb, pb):
    n, d = x.shape
    cw = conv_w.shape[-1]
    pw = pool_scale.shape[-1]
    return pl.pallas_call(
        _mix_sample_kernel,
        out_shape=[
            jax.ShapeDtypeStruct((n, d), F32),
            jax.ShapeDtypeStruct((n, cw), F32),
            jax.ShapeDtypeStruct((n, pw), F32),
        ],
        compiler_params=pltpu.CompilerParams(vmem_limit_bytes=VMEM_LIMIT_BYTES),
        name="mix_sample",
    )(x, sh, sc, gt, g.reshape(1, d), w_in, conv_w, pool_w, pool_scale.reshape(1, pw),
      g_co.reshape(1, cw), g_po.reshape(1, pw), w_out, cb, pb)


def _ffn_kernel(x_ref, sh_ref, sc_ref, gt_ref, g_ref, wg_ref, wu_ref, wd_ref, o_ref, *, chunks):
    x = x_ref[0]
    h = _mod_norm(x, g_ref[...], sc_ref[0], sh_ref[0]).astype(BF16)
    acc = None
    lo = 0
    for fc in chunks:
        a = _dot(h, wg_ref[:, lo:lo + fc])
        b = _dot(h, wu_ref[:, lo:lo + fc])
        s = (a * jax.nn.sigmoid(a) * b).astype(BF16)
        part = _dot(s, wd_ref[lo:lo + fc, :])
        acc = part if acc is None else acc + part
        lo += fc
    o_ref[0] = x + gt_ref[0] * acc


def _ffn(x, sh, sc, gt, g, wg, wu, wd, tile, chunks):
    b, seq, d = x.shape
    tm = sh.shape[1]
    tmod = tile if tm > 1 else 1
    mod_map = (lambda bi, li: (bi, li, 0)) if tm > 1 else (lambda bi, li: (bi, 0, 0))
    return pl.pallas_call(
        functools.partial(_ffn_kernel, chunks=chunks),
        grid=(b, seq // tile),
        in_specs=[
            pl.BlockSpec((1, tile, d), lambda bi, li: (bi, li, 0)),
            pl.BlockSpec((1, tmod, d), mod_map),
            pl.BlockSpec((1, tmod, d), mod_map),
            pl.BlockSpec((1, tmod, d), mod_map),
            _resident((1, d)),
            _resident(wg.shape),
            _resident(wu.shape),
            _resident(wd.shape),
        ],
        out_specs=pl.BlockSpec((1, tile, d), lambda bi, li: (bi, li, 0)),
        out_shape=jax.ShapeDtypeStruct((b, seq, d), F32),
        compiler_params=_params("arbitrary", "arbitrary"),
        name="ffn",
    )(x, sh, sc, gt, g.reshape(1, d), wg, wu, wd)


def _router_kernel(x_ref, sh_ref, sc_ref, g_ref, rw_ref, gates_ref):
    x = x_ref[0]
    h = _mod_norm(x, g_ref[...], sc_ref[0], sh_ref[0])
    logits = jnp.dot(h, rw_ref[...], preferred_element_type=F32, precision=lax.Precision.HIGHEST)
    ne = logits.shape[-1]
    idx = lax.broadcasted_iota(jnp.int32, logits.shape, 1)
    m1 = jnp.max(logits, axis=-1, keepdims=True)
    i1 = jnp.min(jnp.where(logits == m1, idx, ne), axis=-1, keepdims=True)
    rest = jnp.where(idx == i1, -jnp.inf, logits)
    m2 = jnp.max(rest, axis=-1, keepdims=True)
    i2 = jnp.min(jnp.where(rest == m2, idx, ne), axis=-1, keepdims=True)
    e = jnp.exp(m2 - m1)
    p1 = 1.0 / (1.0 + e)
    p2 = e / (1.0 + e)
    gates_ref[0] = jnp.where(idx == i1, p1, 0.0) + jnp.where(idx == i2, p2, 0.0)


def _router(x, sh, sc, g, router_w, tile):
    b, seq, d = x.shape
    ne = router_w.shape[-1]
    tm = sh.shape[1]
    tmod = tile if tm > 1 else 1
    mod_map = (lambda bi, li: (bi, li, 0)) if tm > 1 else (lambda bi, li: (bi, 0, 0))
    return pl.pallas_call(
        _router_kernel,
        grid=(b, seq // tile),
        in_specs=[
            pl.BlockSpec((1, tile, d), lambda bi, li: (bi, li, 0)),
            pl.BlockSpec((1, tmod, d), mod_map),
            pl.BlockSpec((1, tmod, d), mod_map),
            _resident((1, d)),
            _resident(router_w.shape),
        ],
        out_specs=pl.BlockSpec((1, tile, ne), lambda bi, li: (bi, li, 0)),
        out_shape=jax.ShapeDtypeStruct((b, seq, ne), F32),
        compiler_params=_params("arbitrary", "arbitrary"),
        name="router",
    )(x, sh, sc, g.reshape(1, d), router_w)


def _moe_kernel(x_ref, sh_ref, sc_ref, gt_ref, g_ref, gates_ref, wg_ref, wu_ref, wd_ref, gf_ref,
                o_ref, hbuf, acc):
    e = pl.program_id(2)
    c = pl.program_id(3)
    first = jnp.logical_and(e == 0, c == 0)
    last = jnp.logical_and(e == pl.num_programs(2) - 1, c == pl.num_programs(3) - 1)

    @pl.when(first)
    def _():
        hbuf[...] = _mod_norm(x_ref[0], g_ref[...], sc_ref[0], sh_ref[0]).astype(BF16)
        acc[...] = jnp.zeros_like(acc)

    h = hbuf[...]
    gates = gates_ref[0]
    lane = lax.broadcasted_iota(jnp.int32, gates.shape, 1)
    gate = jnp.sum(jnp.where(lane == e, gates, 0.0), axis=-1, keepdims=True)
    a = _dot(h, wg_ref[0])
    b = _dot(h, wu_ref[0])
    s = (a * jax.nn.sigmoid(a) * b * gate).astype(BF16)
    acc[...] += _dot(s, wd_ref[0])

    @pl.when(last)
    def _():
        xn = x_ref[0] + gt_ref[0] * acc[...]
        o_ref[0] = _rms(xn, gf_ref[...])


def _moe(x, sh, sc, gt, g, gates, wg, wu, wd, g_final, tile, fc):
    b, seq, d = x.shape
    ne, _, dff = wg.shape
    tm = sh.shape[1]
    tmod = tile if tm > 1 else 1
    mod_map = ((lambda bi, li, e, c: (bi, li, 0)) if tm > 1 else (lambda bi, li, e, c: (bi, 0, 0)))
    tok = lambda bi, li, e, c: (bi, li, 0)
    return pl.pallas_call(
        _moe_kernel,
        grid=(b, seq // tile, ne, dff // fc),
        in_specs=[
            pl.BlockSpec((1, tile, d), tok),
            pl.BlockSpec((1, tmod, d), mod_map),
            pl.BlockSpec((1, tmod, d), mod_map),
            pl.BlockSpec((1, tmod, d), mod_map),
            _resident((1, d)),
            pl.BlockSpec((1, tile, ne), tok),
            pl.BlockSpec((1, d, fc), lambda bi, li, e, c: (e, 0, c)),
            pl.BlockSpec((1, d, fc), lambda bi, li, e, c: (e, 0, c)),
            pl.BlockSpec((1, fc, d), lambda bi, li, e, c: (e, c, 0)),
            _resident((1, d)),
        ],
        out_specs=pl.BlockSpec((1, tile, d), tok),
        out_shape=jax.ShapeDtypeStruct((b, seq, d), F32),
        scratch_shapes=[pltpu.VMEM((tile, d), BF16), pltpu.VMEM((tile, d), F32)],
        compiler_params=_params("arbitrary", "arbitrary", "arbitrary", "arbitrary"),
        name="moe",
    )(x, sh, sc, gt, g.reshape(1, d), gates, wg, wu, wd, g_final.reshape(1, d))


PROMPT_TILE = 512
MOE_TILE = 1024
MOE_FF_CHUNK = 512
DENSE_FF_CHUNKS = (768, 768, 768, 512)


def kernel(x_prompt, x_sample, c_prompt, c_sample, state_conv, state_pool, w_ada, b_ada, g_mix,
           g_ffn, w_in, conv_w, pool_w, pool_scale, g_conv_out, g_pool_out, w_out, dense_w_gate,
           dense_w_up, dense_w_down, router_w, moe_w_gate, moe_w_up, moe_w_down, g_final):
    depth = w_ada.shape[0]
    nb, seq, d = x_prompt.shape
    ns = x_sample.shape[0]
    assert x_sample.shape[1] == 1 and depth == 2
    assert sum(DENSE_FF_CHUNKS) == dense_w_gate.shape[-1]

    mod = _ada(jnp.concatenate([c_prompt, c_sample], axis=0), w_ada, b_ada)
    mod = mod.reshape(depth, nb + ns, 6, d)
    mod_p = [jnp.transpose(mod[i, :nb], (1, 0, 2))[:, :, None, :] for i in range(depth)]
    mod_s = [jnp.transpose(mod[i, nb:], (1, 0, 2))[:, None, :, :] for i in range(depth)]

    w_in_b = w_in.astype(BF16)
    w_out_b = w_out.astype(BF16)
    pool_w_b = pool_w.astype(BF16)
    dense_b = [w.astype(BF16) for w in (dense_w_gate, dense_w_up, dense_w_down)]
    moe_b = [w.astype(BF16) for w in (moe_w_gate, moe_w_up, moe_w_down)]
    cb_t = jnp.transpose(state_conv, (0, 2, 1, 3))
    pb_t = jnp.transpose(state_pool, (0, 2, 1, 3))

    xp = x_prompt
    xs = x_sample.reshape(ns, d)
    conv_p, pool_p, conv_s, pool_s = [], [], [], []
    for i in range(depth):
        sh1, sc1, gt1, sh2, sc2, gt2 = mod_p[i]
        mix_w = (g_mix[i], w_in_b[i], conv_w[i], pool_w_b[i], pool_scale[i], g_conv_out[i],
                 g_pool_out[i], w_out_b[i])
        xp, cs, ps = _mix_prompt(xp, sh1, sc1, gt1, *mix_w, tile=PROMPT_TILE)
        conv_p.append(cs)
        pool_p.append(ps)
        s1, c1, t1, s2, c2, t2 = mod_s[i]
        xs, v_new, u_new = _mix_sample(xs, s1[0], c1[0], t1[0], *mix_w, cb_t[i], pb_t[i])
        conv_s.append(jnp.concatenate([state_conv[i][:, 1:], v_new[:, None, :]], axis=1))
        pool_s.append(jnp.concatenate([state_pool[i][:, 1:], u_new[:, None, :]], axis=1))
        xs3 = xs[None]
        j = i // 2
        if i % 2 == 0:
            wg, wu, wd = (w[j] for w in dense_b)
            xp = _ffn(xp, sh2, sc2, gt2, g_ffn[i], wg, wu, wd, PROMPT_TILE, DENSE_FF_CHUNKS)
            xs3 = _ffn(xs3, s2, c2, t2, g_ffn[i], wg, wu, wd, ns, DENSE_FF_CHUNKS)
        else:
            wg, wu, wd = (w[j] for w in moe_b)
            gates_p = _router(xp, sh2, sc2, g_ffn[i], router_w[j], MOE_TILE)
            gates_s = _router(xs3, s2, c2, g_ffn[i], router_w[j], ns)
            xp = _moe(xp, sh2, sc2, gt2, g_ffn[i], gates_p, wg, wu, wd, g_final, MOE_TILE,
                      MOE_FF_CHUNK)
            xs3 = _moe(xs3, s2, c2, t2, g_ffn[i], gates_s, wg, wu, wd, g_final, ns, MOE_FF_CHUNK)
        xs = xs3[0]

    return (xp, xs.reshape(ns, 1, d), jnp.stack(conv_p), jnp.stack(pool_p),
            jnp.stack(conv_s), jnp.stack(pool_s))
```

```python
import functools

import jax
import jax.numpy as jnp
from jax import lax
from jax.experimental import pallas as pl
from jax.experimental.pallas import tpu as pltpu

F32 = jnp.float32
BF16 = jnp.bfloat16
I32 = jnp.int32

EPS = 1e-6
CONV_K = 3
POOL_WINDOWS = (2, 4, 8, 16)
POOL_HIST = max(POOL_WINDOWS) - 1
TOP_K = 2

CONV_PAD = 8
POOL_PAD = 16

ROUTE_COLS = 8
COL_EXPERT, COL_RANK, COL_PROB = 0, 2, 4

VMEM_LIMIT_BYTES = 56 * 1024 * 1024

PROMPT_TILE = 512
ROUTER_TILE = 1024
COMBINE_TILE = 512
MOE_ROW_TILE = 1024
MOE_FF_CHUNK = 512
DENSE_FF_CHUNKS = (768, 768, 768, 512)


def _params(*sem):
    return pltpu.CompilerParams(dimension_semantics=sem, vmem_limit_bytes=VMEM_LIMIT_BYTES)


def _resident(shape):
    nd = len(shape)
    return pl.BlockSpec(shape, lambda *_: (0,) * nd, pipeline_mode=pl.Buffered(1))


def _rms(x, g):
    ms = jnp.mean(x * x, axis=-1, keepdims=True)
    return x * lax.rsqrt(ms + EPS) * g


def _mod_norm(x, g, sc, sh):
    ms = jnp.mean(x * x, axis=-1, keepdims=True)
    return x * lax.rsqrt(ms + EPS) * (g * (1.0 + sc)) + sh


def _dot(a, b):
    return jnp.dot(a, b, preferred_element_type=F32)


def _silu_mul(a, b):
    return a * jax.nn.sigmoid(a) * b


def _ada_kernel(c_ref, w_ref, b_ref, o_ref):
    c = c_ref[...]
    a = (c * jax.nn.sigmoid(c)).astype(BF16)
    o_ref[0] = _dot(a, w_ref[0].astype(BF16)) + b_ref[0]


def _ada(c_all, w_ada, b_ada):
    depth, d, n = w_ada.shape
    m = c_all.shape[0]
    tn = 1024
    return pl.pallas_call(
        _ada_kernel,
        grid=(depth, n // tn),
        in_specs=[
            pl.BlockSpec((m, d), lambda i, j: (0, 0)),
            pl.BlockSpec((1, d, tn), lambda i, j: (i, 0, j)),
            pl.BlockSpec((1, 1, tn), lambda i, j: (i, 0, j)),
        ],
        out_specs=pl.BlockSpec((1, m, tn), lambda i, j: (i, 0, j)),
        out_shape=jax.ShapeDtypeStruct((depth, m, n), F32),
        compiler_params=_params("arbitrary", "arbitrary"),
        name="ada",
    )(c_all, w_ada, b_ada.reshape(depth, 1, n))


def _mix_tail(x, gt, bg, y, d_groups, poolw_ref, pscale_ref, gco_ref, gpo_ref, wout_ref):
    cw = gco_ref.shape[-1]
    ya = bg * y
    yb = jnp.concatenate(
        [_dot(d.astype(BF16), poolw_ref[g]) for g, d in enumerate(d_groups)], axis=-1
    ) * pscale_ref[...]
    ma = _rms(ya, gco_ref[...]).astype(BF16)
    mb = _rms(yb, gpo_ref[...]).astype(BF16)
    o = _dot(ma, wout_ref[0:cw, :]) + _dot(mb, wout_ref[cw:, :])
    return x + gt * o


def _mix_prompt_kernel(x_ref, sh_ref, sc_ref, gt_ref, g_ref, win_ref, convw_ref, poolw_ref,
                       pscale_ref, gco_ref, gpo_ref, wout_ref,
                       xo_ref, cs_ref, ps_ref, vbuf, ubuf, *, tile):
    l = pl.program_id(1)
    cw = convw_ref.shape[-1]
    pg = poolw_ref.shape[-1]

    @pl.when(l == 0)
    def _():
        vbuf[0:CONV_PAD, :] = jnp.zeros((CONV_PAD, cw), F32)
        ubuf[0:POOL_PAD, :] = jnp.zeros((POOL_PAD, ubuf.shape[-1]), F32)

    x = x_ref[0]
    h = _mod_norm(x, g_ref[...], sc_ref[0], sh_ref[0]).astype(BF16)
    p = _dot(h, win_ref[...])
    bg = p[:, 0:cw]
    v = p[:, cw:2 * cw] * p[:, 2 * cw:3 * cw]
    u = p[:, 3 * cw:]
    vbuf[CONV_PAD:CONV_PAD + tile, :] = v
    ubuf[POOL_PAD:POOL_PAD + tile, :] = u

    w = convw_ref[...]
    y = w[CONV_K - 1:CONV_K] * v
    for k in range(1, CONV_K):
        y = y + w[CONV_K - 1 - k:CONV_K - k] * vbuf[CONV_PAD - k:CONV_PAD - k + tile, :]

    pos = l * tile + lax.broadcasted_iota(I32, (tile, 1), 0)
    d_groups = []
    for g, win in enumerate(POOL_WINDOWS):
        ug = u[:, g * pg:(g + 1) * pg]
        acc = ug
        for k in range(1, win):
            acc = acc + ubuf[POOL_PAD - k:POOL_PAD - k + tile, g * pg:(g + 1) * pg]
        cnt = jnp.minimum(pos + 1, win).astype(F32)
        d_groups.append(acc / cnt - ug)

    xo_ref[0] = _mix_tail(x, gt_ref[0], bg, y, d_groups, poolw_ref, pscale_ref,
                          gco_ref, gpo_ref, wout_ref)

    cs_ref[0] = vbuf[CONV_PAD + tile - (CONV_K - 1):CONV_PAD + tile, :]
    ps_ref[0] = ubuf[POOL_PAD + tile - POOL_HIST:POOL_PAD + tile, :]
    vbuf[0:CONV_PAD, :] = vbuf[tile:tile + CONV_PAD, :]
    ubuf[0:POOL_PAD, :] = ubuf[tile:tile + POOL_PAD, :]


def _mix_prompt(x, sh, sc, gt, g, w_in, conv_w, pool_w, pool_scale, g_co, g_po, w_out, tile):
    b, seq, d = x.shape
    cw = conv_w.shape[-1]
    pw = pool_scale.shape[-1]
    row = lambda bi, li: (bi, 0, 0)
    return pl.pallas_call(
        functools.partial(_mix_prompt_kernel, tile=tile),
        grid=(b, seq // tile),
        in_specs=[
            pl.BlockSpec((1, tile, d), lambda bi, li: (bi, li, 0)),
            pl.BlockSpec((1, 1, d), row),
            pl.BlockSpec((1, 1, d), row),
            pl.BlockSpec((1, 1, d), row),
            _resident((1, d)),
            _resident(w_in.shape),
            _resident(conv_w.shape),
            _resident(pool_w.shape),
            _resident((1, pw)),
            _resident((1, cw)),
            _resident((1, pw)),
            _resident(w_out.shape),
        ],
        out_specs=[
            pl.BlockSpec((1, tile, d), lambda bi, li: (bi, li, 0)),
            pl.BlockSpec((1, CONV_K - 1, cw), row),
            pl.BlockSpec((1, POOL_HIST, pw), row),
        ],
        out_shape=[
            jax.ShapeDtypeStruct((b, seq, d), F32),
            jax.ShapeDtypeStruct((b, CONV_K - 1, cw), F32),
            jax.ShapeDtypeStruct((b, POOL_HIST, pw), F32),
        ],
        scratch_shapes=[
            pltpu.VMEM((CONV_PAD + tile, cw), F32),
            pltpu.VMEM((POOL_PAD + tile, pw), F32),
        ],
        compiler_params=_params("arbitrary", "arbitrary"),
        name="mix_prompt",
    )(x, sh, sc, gt, g.reshape(1, d), w_in, conv_w, pool_w, pool_scale.reshape(1, pw),
      g_co.reshape(1, cw), g_po.reshape(1, pw), w_out)


def _mix_sample_kernel(x_ref, sh_ref, sc_ref, gt_ref, g_ref, win_ref, convw_ref, poolw_ref,
                       pscale_ref, gco_ref, gpo_ref, wout_ref, cb_ref, pb_ref,
                       xo_ref, v_ref, u_ref):
    cw = convw_ref.shape[-1]
    pg = poolw_ref.shape[-1]
    x = x_ref[...]
    h = _mod_norm(x, g_ref[...], sc_ref[...], sh_ref[...]).astype(BF16)
    p = _dot(h, win_ref[...])
    bg = p[:, 0:cw]
    v = p[:, cw:2 * cw] * p[:, 2 * cw:3 * cw]
    u = p[:, 3 * cw:]
    v_ref[...] = v
    u_ref[...] = u

    w = convw_ref[...]
    y = w[CONV_K - 1:CONV_K] * v
    for k in range(1, CONV_K):
        y = y + w[CONV_K - 1 - k:CONV_K - k] * cb_ref[CONV_K - 1 - k]

    d_groups = []
    for g, win in enumerate(POOL_WINDOWS):
        ug = u[:, g * pg:(g + 1) * pg]
        acc = ug
        for k in range(1, win):
            acc = acc + pb_ref[POOL_HIST - k, :, g * pg:(g + 1) * pg]
        d_groups.append(acc / float(win) - ug)

    xo_ref[...] = _mix_tail(x, gt_ref[...], bg, y, d_groups, poolw_ref, pscale_ref,
                            gco_ref, gpo_ref, wout_ref)


def _mix_sample(x, sh, sc, gt, g, w_in, conv_w, pool_w, pool_scale, g_co, g_po, w_out, cb, pb):
    n, d = x.shape
    cw = conv_w.shape[-1]
    pw = pool_scale.shape[-1]
    return pl.pallas_call(
        _mix_sample_kernel,
        out_shape=[
            jax.ShapeDtypeStruct((n, d), F32),
            jax.ShapeDtypeStruct((n, cw), F32),
            jax.ShapeDtypeStruct((n, pw), F32),
        ],
        compiler_params=pltpu.CompilerParams(vmem_limit_bytes=VMEM_LIMIT_BYTES),
        name="mix_sample",
    )(x, sh, sc, gt, g.reshape(1, d), w_in, conv_w, pool_w, pool_scale.reshape(1, pw),
      g_co.reshape(1, cw), g_po.reshape(1, pw), w_out, cb, pb)


def _ffn_kernel(x_ref, sh_ref, sc_ref, gt_ref, g_ref, wg_ref, wu_ref, wd_ref, o_ref, *, chunks):
    x = x_ref[0]
    h = _mod_norm(x, g_ref[...], sc_ref[0], sh_ref[0]).astype(BF16)
    acc = None
    lo = 0
    for fc in chunks:
        a = _dot(h, wg_ref[:, lo:lo + fc])
        b = _dot(h, wu_ref[:, lo:lo + fc])
        part = _dot(_silu_mul(a, b).astype(BF16), wd_ref[lo:lo + fc, :])
        acc = part if acc is None else acc + part
        lo += fc
    o_ref[0] = x + gt_ref[0] * acc


def _ffn(x, sh, sc, gt, g, wg, wu, wd, tile, chunks):
    b, seq, d = x.shape
    tm = sh.shape[1]
    tmod = tile if tm > 1 else 1
    mod_map = (lambda bi, li: (bi, li, 0)) if tm > 1 else (lambda bi, li: (bi, 0, 0))
    return pl.pallas_call(
        functools.partial(_ffn_kernel, chunks=chunks),
        grid=(b, seq // tile),
        in_specs=[
            pl.BlockSpec((1, tile, d), lambda bi, li: (bi, li, 0)),
            pl.BlockSpec((1, tmod, d), mod_map),
            pl.BlockSpec((1, tmod, d), mod_map),
            pl.BlockSpec((1, tmod, d), mod_map),
            _resident((1, d)),
            _resident(wg.shape),
            _resident(wu.shape),
            _resident(wd.shape),
        ],
        out_specs=pl.BlockSpec((1, tile, d), lambda bi, li: (bi, li, 0)),
        out_shape=jax.ShapeDtypeStruct((b, seq, d), F32),
        compiler_params=_params("arbitrary", "arbitrary"),
        name="ffn",
    )(x, sh, sc, gt, g.reshape(1, d), wg, wu, wd)


def _split_bf16(a):
    hi = a.astype(BF16)
    return hi, (a - hi.astype(F32)).astype(BF16)


def _router_kernel(x_ref, sh_ref, sc_ref, g_ref, rw_ref, base_ref, h_ref, route_ref, cnt_ref,
                   tri, carry, *, n_steps):
    t = x_ref.shape[0]
    step = pl.program_id(0)

    @pl.when(step == 0)
    def _():
        r = lax.broadcasted_iota(I32, (t, t), 0)
        c = lax.broadcasted_iota(I32, (t, t), 1)
        tri[...] = jnp.where(c < r, 1.0, 0.0).astype(BF16)
        carry[...] = base_ref[...]

    @pl.when(step >= n_steps)
    def _():
        h_ref[...] = jnp.zeros_like(h_ref)

    @pl.when(step < n_steps)
    def _():
        h = _mod_norm(x_ref[...], g_ref[...], sc_ref[0], sh_ref[0])
        h_ref[...] = h
        h_hi, h_lo = _split_bf16(h)
        rw_hi, rw_lo = _split_bf16(rw_ref[...])
        logits = _dot(h_hi, rw_hi) + _dot(h_lo, rw_hi) + _dot(h_hi, rw_lo)

        ne = logits.shape[-1]
        idx = lax.broadcasted_iota(I32, logits.shape, 1)
        m1 = jnp.max(logits, axis=-1, keepdims=True)
        i1 = jnp.min(jnp.where(logits == m1, idx, ne), axis=-1, keepdims=True)
        sel1 = idx == i1
        rest = jnp.where(sel1, -jnp.inf, logits)
        m2 = jnp.max(rest, axis=-1, keepdims=True)
        i2 = jnp.min(jnp.where(rest == m2, idx, ne), axis=-1, keepdims=True)
        sel2 = idx == i2
        e = jnp.exp(m2 - m1)
        p1 = 1.0 / (1.0 + e)
        p2 = e / (1.0 + e)

        chosen = jnp.where(sel1, 1.0, jnp.where(sel2, 1.0, 0.0))
        before = _dot(tri[...], chosen.astype(BF16)) + carry[...]
        r1 = jnp.sum(jnp.where(sel1, before, 0.0), axis=-1, keepdims=True)
        r2 = jnp.sum(jnp.where(sel2, before, 0.0), axis=-1, keepdims=True)
        carry[...] += jnp.sum(chosen, axis=0, keepdims=True)
        cnt_ref[...] = carry[...]

        cols = (i1.astype(F32), i2.astype(F32), r1, r2, p1, p2)
        route = jnp.zeros(logits.shape, F32)
        for k, col in enumerate(cols):
            route = jnp.where(idx == k, col, route)
        route_ref[...] = route


def _router(x2d, sh, sc, g, router_w, base_cnt, tile, rows_per_mod, h_rows, h_prev=None,
            row_off=0):
    n, d = x2d.shape
    ne = router_w.shape[-1]
    assert ne == ROUTE_COLS and n % tile == 0 and row_off % tile == 0
    n_steps = n // tile
    tail = h_rows - (row_off + n)
    assert 0 <= tail < tile
    fill_tail = h_prev is None and tail > 0
    last = n_steps - 1
    clamp = (lambda i: jnp.minimum(i, last)) if fill_tail else (lambda i: i)
    per_row_mod = sh.shape[1] > 1
    if per_row_mod:
        mod_spec = pl.BlockSpec((1, tile, d), lambda i: (0, clamp(i), 0))
    else:
        assert rows_per_mod % tile == 0
        mod_spec = pl.BlockSpec((1, 1, d), lambda i: (clamp(i) // (rows_per_mod // tile), 0, 0))
    in_specs = [
        pl.BlockSpec((tile, d), lambda i: (clamp(i), 0)),
        mod_spec,
        mod_spec,
        _resident((1, d)),
        _resident(router_w.shape),
        _resident((1, ne)),
    ]
    args = [x2d, sh, sc, g.reshape(1, d), router_w, base_cnt]
    n_in = len(args)
    aliases = {}
    body = functools.partial(_router_kernel, n_steps=n_steps)
    if h_prev is not None:
        assert h_prev.shape == (h_rows, d)
        in_specs.append(pl.BlockSpec(memory_space=pl.ANY))
        args.append(h_prev)
        aliases = {n_in: 0}

        def body(*refs):
            _router_kernel(*refs[:n_in], *refs[n_in + 1:], n_steps=n_steps)

    blk_off = row_off // tile
    return pl.pallas_call(
        body,
        grid=(n_steps + int(fill_tail),),
        in_specs=in_specs,
        out_specs=[
            pl.BlockSpec((tile, d), lambda i: (i + blk_off, 0)),
            pl.BlockSpec((tile, ne), lambda i: (clamp(i), 0)),
            pl.BlockSpec((1, ne), lambda i: (0, 0)),
        ],
        out_shape=[
            jax.ShapeDtypeStruct((h_rows, d), F32),
            jax.ShapeDtypeStruct((n, ne), F32),
            jax.ShapeDtypeStruct((1, ne), F32),
        ],
        scratch_shapes=[pltpu.VMEM((tile, tile), BF16), pltpu.VMEM((1, ne), F32)],
        input_output_aliases=aliases,
        compiler_params=_params("arbitrary"),
        name="router",
    )(*args)


def _row_copy(src_hbm, row, dst, dst_row, sem):
    return pltpu.make_async_copy(src_hbm.at[pl.ds(row, 1), :], dst.at[pl.ds(dst_row, 1), :], sem)


def _moe_grouped_kernel(te_ref, na_ref, src_ref, h_hbm, wg_ref, wu_ref, wd_ref, o_ref,
                        xs, hbuf, sem, *, tm, rows_per_step):
    del te_ref
    i = pl.program_id(0)
    c = pl.program_id(1)
    n_active = na_ref[0]

    def request(tile, lo, hi):
        slot = tile % 2

        def body(r, carry):
            _row_copy(h_hbm, src_ref[tile * tm + r], xs.at[slot], r, sem.at[slot]).start()
            return carry

        lax.fori_loop(lo, hi, body, 0)

    @pl.when(jnp.logical_and(i == 0, c == 0))
    def _():
        request(0, 0, tm)

    @pl.when(i < n_active)
    def _():
        slot = i % 2

        @pl.when(c == 0)
        def _():
            pltpu.make_async_copy(h_hbm.at[pl.ds(0, tm), :], xs.at[slot], sem.at[slot]).wait()
            hbuf[...] = xs[slot].astype(BF16)

        @pl.when(i + 1 < n_active)
        def _():
            lo = jnp.minimum(c * rows_per_step, tm)
            request(i + 1, lo, jnp.minimum(lo + rows_per_step, tm))

        h = hbuf[...]
        a = _dot(h, wg_ref[0].astype(BF16))
        b = _dot(h, wu_ref[0].astype(BF16))
        part = _dot(_silu_mul(a, b).astype(BF16), wd_ref[0].astype(BF16))

        @pl.when(c == 0)
        def _():
            o_ref[...] = part

        @pl.when(c > 0)
        def _():
            o_ref[...] += part

    @pl.when(jnp.logical_and(i >= n_active, c == 0))
    def _():
        o_ref[...] = jnp.zeros_like(o_ref)


def _moe_grouped(h_all, tile_expert, n_active, src, wg, wu, wd, tm, fc):
    ne, d, dff = wg.shape
    n_tiles = tile_expert.shape[0]
    nc = dff // fc
    assert dff % fc == 0
    rows_per_step = -(-tm // nc)

    def w_col(i, c, te, na, src_):
        return (te[i], 0, jnp.where(i < na[0], c, nc - 1))

    def w_row(i, c, te, na, src_):
        return (te[i], jnp.where(i < na[0], c, nc - 1), 0)

    grid_spec = pltpu.PrefetchScalarGridSpec(
        num_scalar_prefetch=3,
        grid=(n_tiles, nc),
        in_specs=[
            pl.BlockSpec(memory_space=pl.ANY),
            pl.BlockSpec((1, d, fc), w_col),
            pl.BlockSpec((1, d, fc), w_col),
            pl.BlockSpec((1, fc, d), w_row),
        ],
        out_specs=pl.BlockSpec((tm, d), lambda i, c, te, na, src_: (i, 0)),
        scratch_shapes=[
            pltpu.VMEM((2, tm, d), F32),
            pltpu.VMEM((tm, d), BF16),
            pltpu.SemaphoreType.DMA((2,)),
        ],
    )
    return pl.pallas_call(
        functools.partial(_moe_grouped_kernel, tm=tm, rows_per_step=rows_per_step),
        grid_spec=grid_spec,
        out_shape=jax.ShapeDtypeStruct((n_tiles * tm, d), F32),
        compiler_params=_params("arbitrary", "arbitrary"),
        name="moe_grouped",
    )(tile_expert, n_active, src, h_all, wg, wu, wd)


def _combine_kernel(pos_ref, x_ref, gt_ref, route_ref, gf_ref, f_hbm, o_ref, fbuf, sem,
                    *, tile, tok_off):
    j = pl.program_id(0)

    def request(step):
        slot = step % 2
        base = (tok_off + step * tile) * TOP_K

        def body(r, carry):
            for k in range(TOP_K):
                _row_copy(f_hbm, pos_ref[base + TOP_K * r + k], fbuf.at[slot, k], r,
                          sem.at[slot]).start()
            return carry

        lax.fori_loop(0, tile, body, 0)

    @pl.when(j == 0)
    def _():
        request(0)

    @pl.when(j + 1 < pl.num_programs(0))
    def _():
        request(j + 1)

    slot = j % 2
    for k in range(TOP_K):
        pltpu.make_async_copy(f_hbm.at[pl.ds(0, tile), :], fbuf.at[slot, k], sem.at[slot]).wait()
    route = route_ref[...]
    f = (route[:, COL_PROB:COL_PROB + 1] * fbuf[slot, 0]
         + route[:, COL_PROB + 1:COL_PROB + 2] * fbuf[slot, 1])
    o_ref[...] = _rms(x_ref[...] + gt_ref[0] * f, gf_ref[...])


def _combine(x2d, gt, route, g_final, f_sorted, pos_flat, tile, rows_per_mod, tok_off):
    n, d = x2d.shape
    assert n % tile == 0
    per_row_mod = gt.shape[1] > 1
    if per_row_mod:
        mod_spec = pl.BlockSpec((1, tile, d), lambda j, pos: (0, j, 0))
    else:
        assert rows_per_mod % tile == 0
        mod_spec = pl.BlockSpec((1, 1, d), lambda j, pos: (j // (rows_per_mod // tile), 0, 0))
    grid_spec = pltpu.PrefetchScalarGridSpec(
        num_scalar_prefetch=1,
        grid=(n // tile,),
        in_specs=[
            pl.BlockSpec((tile, d), lambda j, pos: (j, 0)),
            mod_spec,
            pl.BlockSpec((tile, ROUTE_COLS), lambda j, pos: (j, 0)),
            pl.BlockSpec((1, d), lambda j, pos: (0, 0)),
            pl.BlockSpec(memory_space=pl.ANY),
        ],
        out_specs=pl.BlockSpec((tile, d), lambda j, pos: (j, 0)),
        scratch_shapes=[
            pltpu.VMEM((2, TOP_K, tile, d), F32),
            pltpu.SemaphoreType.DMA((2,)),
        ],
    )
    return pl.pallas_call(
        functools.partial(_combine_kernel, tile=tile, tok_off=tok_off),
        grid_spec=grid_spec,
        out_shape=jax.ShapeDtypeStruct((n, d), F32),
        compiler_params=_params("arbitrary"),
        name="combine",
    )(pos_flat, x2d, gt, route, g_final.reshape(1, d), f_sorted)


def _routing_tables(route_all, counts, tm, n_tiles):
    ne = counts.shape[0]
    n = route_all.shape[0]
    e_idx = route_all[:, COL_EXPERT:COL_EXPERT + TOP_K].astype(I32)
    rank = route_all[:, COL_RANK:COL_RANK + TOP_K].astype(I32)
    padded = (counts + tm - 1) // tm * tm
    ends = jnp.cumsum(padded)
    offs = ends - padded
    onehot = e_idx[..., None] == jnp.arange(ne, dtype=I32)
    pos = jnp.sum(jnp.where(onehot, offs, 0), axis=-1) + rank
    n_active = ends[-1] // tm
    starts = jnp.arange(n_tiles, dtype=I32) * tm
    tile_expert = jnp.sum(starts[:, None] >= ends[None, :], axis=1).astype(I32)
    last_expert = jnp.sum((n_active - 1) * tm >= ends).astype(I32)
    tile_expert = jnp.minimum(tile_expert, last_expert)
    pos_flat = pos.reshape(-1)
    tok = jnp.repeat(jnp.arange(n, dtype=I32), TOP_K)
    src = jnp.zeros((n_tiles * tm,), I32).at[pos_flat].set(tok)
    return pos_flat, src, tile_expert, n_active.astype(I32).reshape(1)


def _moe_layer(xp, xs, mod_p, mod_s, g, router_w, wg, wu, wd, g_final):
    nb, seq, d = xp.shape
    ns = xs.shape[0]
    ne = router_w.shape[-1]
    n_all = nb * seq + ns
    sh_p, sc_p, gt_p = mod_p
    sh_s, sc_s, gt_s = mod_s
    xp2 = xp.reshape(nb * seq, d)

    zero_cnt = jnp.zeros((1, ne), F32)
    h_all, route_p, cnt_p = _router(xp2, sh_p, sc_p, g, router_w, zero_cnt, ROUTER_TILE, seq,
                                    n_all)
    h_all, route_s, cnt_s = _router(xs, sh_s, sc_s, g, router_w, cnt_p, ns, ns, n_all,
                                    h_prev=h_all, row_off=nb * seq)

    tm = MOE_ROW_TILE
    n_tiles = (n_all * TOP_K) // tm + ne
    route_all = jnp.concatenate([route_p, route_s], axis=0)
    pos_flat, src, tile_expert, n_active = _routing_tables(
        route_all, cnt_s[0].astype(I32), tm, n_tiles)

    f_sorted = _moe_grouped(h_all, tile_expert, n_active, src, wg, wu, wd, tm, MOE_FF_CHUNK)
    yp = _combine(xp2, gt_p, route_p, g_final, f_sorted, pos_flat, COMBINE_TILE, seq, 0)
    ys = _combine(xs, gt_s, route_s, g_final, f_sorted, pos_flat, ns, ns, nb * seq)
    return yp.reshape(nb, seq, d), ys


def kernel(x_prompt, x_sample, c_prompt, c_sample, state_conv, state_pool, w_ada, b_ada, g_mix,
           g_ffn, w_in, conv_w, pool_w, pool_scale, g_conv_out, g_pool_out, w_out, dense_w_gate,
           dense_w_up, dense_w_down, router_w, moe_w_gate, moe_w_up, moe_w_down, g_final):
    depth = w_ada.shape[0]
    nb, seq, d = x_prompt.shape
    ns = x_sample.shape[0]
    assert x_sample.shape[1] == 1 and depth == 2
    assert sum(DENSE_FF_CHUNKS) == dense_w_gate.shape[-1]

    mod = _ada(jnp.concatenate([c_prompt, c_sample], axis=0), w_ada, b_ada)
    mod = mod.reshape(depth, nb + ns, 6, d)
    mod_p = [jnp.transpose(mod[i, :nb], (1, 0, 2))[:, :, None, :] for i in range(depth)]
    mod_s = [jnp.transpose(mod[i, nb:], (1, 0, 2))[:, None, :, :] for i in range(depth)]

    w_in_b = w_in.astype(BF16)
    w_out_b = w_out.astype(BF16)
    pool_w_b = pool_w.astype(BF16)
    dense_b = [w.astype(BF16) for w in (dense_w_gate, dense_w_up, dense_w_down)]
    cb_t = jnp.transpose(state_conv, (0, 2, 1, 3))
    pb_t = jnp.transpose(state_pool, (0, 2, 1, 3))

    xp = x_prompt
    xs = x_sample.reshape(ns, d)
    conv_p, pool_p, conv_s, pool_s = [], [], [], []
    for i in range(depth):
        sh1, sc1, gt1, sh2, sc2, gt2 = mod_p[i]
        mix_w = (g_mix[i], w_in_b[i], conv_w[i], pool_w_b[i], pool_scale[i], g_conv_out[i],
                 g_pool_out[i], w_out_b[i])
        xp, cs, ps = _mix_prompt(xp, sh1, sc1, gt1, *mix_w, tile=PROMPT_TILE)
        conv_p.append(cs)
        pool_p.append(ps)
        s1, c1, t1, s2, c2, t2 = mod_s[i]
        xs, v_new, u_new = _mix_sample(xs, s1[0], c1[0], t1[0], *mix_w, cb_t[i], pb_t[i])
        conv_s.append(jnp.concatenate([state_conv[i][:, 1:], v_new[:, None, :]], axis=1))
        pool_s.append(jnp.concatenate([state_pool[i][:, 1:], u_new[:, None, :]], axis=1))
        j = i // 2
        if i % 2 == 0:
            wg, wu, wd = (w[j] for w in dense_b)
            xp = _ffn(xp, sh2, sc2, gt2, g_ffn[i], wg, wu, wd, PROMPT_TILE, DENSE_FF_CHUNKS)
            xs = _ffn(xs[None], s2, c2, t2, g_ffn[i], wg, wu, wd, ns, DENSE_FF_CHUNKS)[0]
        else:
            xp, xs = _moe_layer(xp, xs, (sh2, sc2, gt2), (s2, c2, t2), g_ffn[i], router_w[j],
                                moe_w_gate[j], moe_w_up[j], moe_w_down[j], g_final)

    return (xp, xs.reshape(ns, 1, d), jnp.stack(conv_p), jnp.stack(pool_p),
            jnp.stack(conv_s), jnp.stack(pool_s))
```

```python
import functools

import jax
import jax.numpy as jnp
from jax import lax
from jax.experimental import pallas as pl
from jax.experimental.pallas import tpu as pltpu

F32 = jnp.float32
BF16 = jnp.bfloat16
I32 = jnp.int32

EPS = 1e-6
CONV_K = 3
POOL_WINDOWS = (2, 4, 8, 16)
POOL_HIST = max(POOL_WINDOWS) - 1
TOP_K = 2

CONV_PAD = 8
POOL_PAD = 16

ROUTE_COLS = 8
COL_EXPERT, COL_RANK, COL_PROB = 0, 2, 4

VMEM_LIMIT_BYTES = 56 * 1024 * 1024

PROMPT_TILE = 512
ROUTER_TILE = 1024
COMBINE_TILE = 512
MOE_ROW_TILE = 896
MOE_FF_CHUNK = 512
DENSE_FF_CHUNKS = (768, 768, 768, 512)


def _params(*sem):
    return pltpu.CompilerParams(dimension_semantics=sem, vmem_limit_bytes=VMEM_LIMIT_BYTES)


def _resident(shape):
    nd = len(shape)
    return pl.BlockSpec(shape, lambda *_: (0,) * nd, pipeline_mode=pl.Buffered(1))


def _rms(x, g):
    ms = jnp.mean(x * x, axis=-1, keepdims=True)
    return x * lax.rsqrt(ms + EPS) * g


def _mod_norm(x, g, sc, sh):
    ms = jnp.mean(x * x, axis=-1, keepdims=True)
    return x * lax.rsqrt(ms + EPS) * (g * (1.0 + sc)) + sh


def _dot(a, b):
    return jnp.dot(a, b, preferred_element_type=F32)


def _silu_mul(a, b):
    return a * jax.nn.sigmoid(a) * b


def _ada_kernel(c_ref, w_ref, b_ref, o_ref):
    c = c_ref[...]
    a = (c * jax.nn.sigmoid(c)).astype(BF16)
    o_ref[0] = _dot(a, w_ref[0].astype(BF16)) + b_ref[0]


def _ada(c_all, w_ada, b_ada):
    depth, d, n = w_ada.shape
    m = c_all.shape[0]
    tn = 1024
    return pl.pallas_call(
        _ada_kernel,
        grid=(depth, n // tn),
        in_specs=[
            pl.BlockSpec((m, d), lambda i, j: (0, 0)),
            pl.BlockSpec((1, d, tn), lambda i, j: (i, 0, j)),
            pl.BlockSpec((1, 1, tn), lambda i, j: (i, 0, j)),
        ],
        out_specs=pl.BlockSpec((1, m, tn), lambda i, j: (i, 0, j)),
        out_shape=jax.ShapeDtypeStruct((depth, m, n), F32),
        compiler_params=_params("arbitrary", "arbitrary"),
        name="ada",
    )(c_all, w_ada, b_ada.reshape(depth, 1, n))


def _mix_tail(x, gt, bg, y, d_groups, poolw_ref, pscale_ref, gco_ref, gpo_ref, wout_ref):
    cw = gco_ref.shape[-1]
    ya = bg * y
    yb = jnp.concatenate(
        [_dot(d.astype(BF16), poolw_ref[g]) for g, d in enumerate(d_groups)], axis=-1
    ) * pscale_ref[...]
    ma = _rms(ya, gco_ref[...]).astype(BF16)
    mb = _rms(yb, gpo_ref[...]).astype(BF16)
    o = _dot(ma, wout_ref[0:cw, :]) + _dot(mb, wout_ref[cw:, :])
    return x + gt * o


def _mix_prompt_kernel(x_ref, sh_ref, sc_ref, gt_ref, g_ref, win_ref, convw_ref, poolw_ref,
                       pscale_ref, gco_ref, gpo_ref, wout_ref,
                       xo_ref, cs_ref, ps_ref, vbuf, ubuf, *, tile):
    l = pl.program_id(1)
    cw = convw_ref.shape[-1]
    pg = poolw_ref.shape[-1]

    @pl.when(l == 0)
    def _():
        vbuf[0:CONV_PAD, :] = jnp.zeros((CONV_PAD, cw), F32)
        ubuf[0:POOL_PAD, :] = jnp.zeros((POOL_PAD, ubuf.shape[-1]), F32)

    x = x_ref[0]
    h = _mod_norm(x, g_ref[...], sc_ref[0], sh_ref[0]).astype(BF16)
    p = _dot(h, win_ref[...])
    bg = p[:, 0:cw]
    v = p[:, cw:2 * cw] * p[:, 2 * cw:3 * cw]
    u = p[:, 3 * cw:]
    vbuf[CONV_PAD:CONV_PAD + tile, :] = v
    ubuf[POOL_PAD:POOL_PAD + tile, :] = u

    w = convw_ref[...]
    y = w[CONV_K - 1:CONV_K] * v
    for k in range(1, CONV_K):
        y = y + w[CONV_K - 1 - k:CONV_K - k] * vbuf[CONV_PAD - k:CONV_PAD - k + tile, :]

    pos = l * tile + lax.broadcasted_iota(I32, (tile, 1), 0)
    d_groups = []
    for g, win in enumerate(POOL_WINDOWS):
        ug = u[:, g * pg:(g + 1) * pg]
        acc = ug
        for k in range(1, win):
            acc = acc + ubuf[POOL_PAD - k:POOL_PAD - k + tile, g * pg:(g + 1) * pg]
        cnt = jnp.minimum(pos + 1, win).astype(F32)
        d_groups.append(acc / cnt - ug)

    xo_ref[0] = _mix_tail(x, gt_ref[0], bg, y, d_groups, poolw_ref, pscale_ref,
                          gco_ref, gpo_ref, wout_ref)

    cs_ref[0] = vbuf[CONV_PAD + tile - (CONV_K - 1):CONV_PAD + tile, :]
    ps_ref[0] = ubuf[POOL_PAD + tile - POOL_HIST:POOL_PAD + tile, :]
    vbuf[0:CONV_PAD, :] = vbuf[tile:tile + CONV_PAD, :]
    ubuf[0:POOL_PAD, :] = ubuf[tile:tile + POOL_PAD, :]


def _mix_prompt(x, sh, sc, gt, g, w_in, conv_w, pool_w, pool_scale, g_co, g_po, w_out, tile):
    b, seq, d = x.shape
    cw = conv_w.shape[-1]
    pw = pool_scale.shape[-1]
    row = lambda bi, li: (bi, 0, 0)
    return pl.pallas_call(
        functools.partial(_mix_prompt_kernel, tile=tile),
        grid=(b, seq // tile),
        in_specs=[
            pl.BlockSpec((1, tile, d), lambda bi, li: (bi, li, 0)),
            pl.BlockSpec((1, 1, d), row),
            pl.BlockSpec((1, 1, d), row),
            pl.BlockSpec((1, 1, d), row),
            _resident((1, d)),
            _resident(w_in.shape),
            _resident(conv_w.shape),
            _resident(pool_w.shape),
            _resident((1, pw)),
            _resident((1, cw)),
            _resident((1, pw)),
            _resident(w_out.shape),
        ],
        out_specs=[
            pl.BlockSpec((1, tile, d), lambda bi, li: (bi, li, 0)),
            pl.BlockSpec((1, CONV_K - 1, cw), row),
            pl.BlockSpec((1, POOL_HIST, pw), row),
        ],
        out_shape=[
            jax.ShapeDtypeStruct((b, seq, d), F32),
            jax.ShapeDtypeStruct((b, CONV_K - 1, cw), F32),
            jax.ShapeDtypeStruct((b, POOL_HIST, pw), F32),
        ],
        scratch_shapes=[
            pltpu.VMEM((CONV_PAD + tile, cw), F32),
            pltpu.VMEM((POOL_PAD + tile, pw), F32),
        ],
        compiler_params=_params("arbitrary", "arbitrary"),
        name="mix_prompt",
    )(x, sh, sc, gt, g.reshape(1, d), w_in, conv_w, pool_w, pool_scale.reshape(1, pw),
      g_co.reshape(1, cw), g_po.reshape(1, pw), w_out)


def _mix_sample_kernel(x_ref, sh_ref, sc_ref, gt_ref, g_ref, win_ref, convw_ref, poolw_ref,
                       pscale_ref, gco_ref, gpo_ref, wout_ref, cb_ref, pb_ref,
                       xo_ref, v_ref, u_ref):
    cw = convw_ref.shape[-1]
    pg = poolw_ref.shape[-1]
    x = x_ref[...]
    h = _mod_norm(x, g_ref[...], sc_ref[...], sh_ref[...]).astype(BF16)
    p = _dot(h, win_ref[...])
    bg = p[:, 0:cw]
    v = p[:, cw:2 * cw] * p[:, 2 * cw:3 * cw]
    u = p[:, 3 * cw:]
    v_ref[...] = v
    u_ref[...] = u

    w = convw_ref[...]
    y = w[CONV_K - 1:CONV_K] * v
    for k in range(1, CONV_K):
        y = y + w[CONV_K - 1 - k:CONV_K - k] * cb_ref[CONV_K - 1 - k]

    d_groups = []
    for g, win in enumerate(POOL_WINDOWS):
        ug = u[:, g * pg:(g + 1) * pg]
        acc = ug
        for k in range(1, win):
            acc = acc + pb_ref[POOL_HIST - k, :, g * pg:(g + 1) * pg]
        d_groups.append(acc / float(win) - ug)

    xo_ref[...] = _mix_tail(x, gt_ref[...], bg, y, d_groups, poolw_ref, pscale_ref,
                            gco_ref, gpo_ref, wout_ref)


def _mix_sample(x, sh, sc, gt, g, w_in, conv_w, pool_w, pool_scale, g_co, g_po, w_out, cb, pb):
    n, d = x.shape
    cw = conv_w.shape[-1]
    pw = pool_scale.shape[-1]
    return pl.pallas_call(
        _mix_sample_kernel,
        out_shape=[
            jax.ShapeDtypeStruct((n, d), F32),
            jax.ShapeDtypeStruct((n, cw), F32),
            jax.ShapeDtypeStruct((n, pw), F32),
        ],
        compiler_params=pltpu.CompilerParams(vmem_limit_bytes=VMEM_LIMIT_BYTES),
        name="mix_sample",
    )(x, sh, sc, gt, g.reshape(1, d), w_in, conv_w, pool_w, pool_scale.reshape(1, pw),
      g_co.reshape(1, cw), g_po.reshape(1, pw), w_out, cb, pb)


def _ffn_kernel(x_ref, sh_ref, sc_ref, gt_ref, g_ref, wg_ref, wu_ref, wd_ref, o_ref, *, chunks):
    x = x_ref[0]
    h = _mod_norm(x, g_ref[...], sc_ref[0], sh_ref[0]).astype(BF16)
    acc = None
    lo = 0
    for fc in chunks:
        a = _dot(h, wg_ref[:, lo:lo + fc])
        b = _dot(h, wu_ref[:, lo:lo + fc])
        part = _dot(_silu_mul(a, b).astype(BF16), wd_ref[lo:lo + fc, :])
        acc = part if acc is None else acc + part
        lo += fc
    o_ref[0] = x + gt_ref[0] * acc


def _ffn(x, sh, sc, gt, g, wg, wu, wd, tile, chunks):
    b, seq, d = x.shape
    tm = sh.shape[1]
    tmod = tile if tm > 1 else 1
    mod_map = (lambda bi, li: (bi, li, 0)) if tm > 1 else (lambda bi, li: (bi, 0, 0))
    return pl.pallas_call(
        functools.partial(_ffn_kernel, chunks=chunks),
        grid=(b, seq // tile),
        in_specs=[
            pl.BlockSpec((1, tile, d), lambda bi, li: (bi, li, 0)),
            pl.BlockSpec((1, tmod, d), mod_map),
            pl.BlockSpec((1, tmod, d), mod_map),
            pl.BlockSpec((1, tmod, d), mod_map),
            _resident((1, d)),
            _resident(wg.shape),
            _resident(wu.shape),
            _resident(wd.shape),
        ],
        out_specs=pl.BlockSpec((1, tile, d), lambda bi, li: (bi, li, 0)),
        out_shape=jax.ShapeDtypeStruct((b, seq, d), F32),
        compiler_params=_params("arbitrary", "arbitrary"),
        name="ffn",
    )(x, sh, sc, gt, g.reshape(1, d), wg, wu, wd)


def _split_bf16(a):
    hi = a.astype(BF16)
    return hi, (a - hi.astype(F32)).astype(BF16)


def _router_kernel(x_ref, sh_ref, sc_ref, g_ref, rw_ref, base_ref, h_ref, route_ref, cnt_ref,
                   tri, carry, *, n_steps):
    t = x_ref.shape[0]
    step = pl.program_id(0)

    @pl.when(step == 0)
    def _():
        r = lax.broadcasted_iota(I32, (t, t), 0)
        c = lax.broadcasted_iota(I32, (t, t), 1)
        tri[...] = jnp.where(c < r, 1.0, 0.0).astype(BF16)
        carry[...] = base_ref[...]

    @pl.when(step >= n_steps)
    def _():
        h_ref[...] = jnp.zeros_like(h_ref)

    @pl.when(step < n_steps)
    def _():
        h = _mod_norm(x_ref[...], g_ref[...], sc_ref[0], sh_ref[0])
        h_ref[...] = h
        h_hi, h_lo = _split_bf16(h)
        rw_hi, rw_lo = _split_bf16(rw_ref[...])
        logits = _dot(h_hi, rw_hi) + _dot(h_lo, rw_hi) + _dot(h_hi, rw_lo)

        ne = logits.shape[-1]
        idx = lax.broadcasted_iota(I32, logits.shape, 1)
        m1 = jnp.max(logits, axis=-1, keepdims=True)
        i1 = jnp.min(jnp.where(logits == m1, idx, ne), axis=-1, keepdims=True)
        sel1 = idx == i1
        rest = jnp.where(sel1, -jnp.inf, logits)
        m2 = jnp.max(rest, axis=-1, keepdims=True)
        i2 = jnp.min(jnp.where(rest == m2, idx, ne), axis=-1, keepdims=True)
        sel2 = idx == i2
        e = jnp.exp(m2 - m1)
        p1 = 1.0 / (1.0 + e)
        p2 = e / (1.0 + e)

        chosen = jnp.where(sel1, 1.0, jnp.where(sel2, 1.0, 0.0))
        before = _dot(tri[...], chosen.astype(BF16)) + carry[...]
        r1 = jnp.sum(jnp.where(sel1, before, 0.0), axis=-1, keepdims=True)
        r2 = jnp.sum(jnp.where(sel2, before, 0.0), axis=-1, keepdims=True)
        carry[...] += jnp.sum(chosen, axis=0, keepdims=True)
        cnt_ref[...] = carry[...]

        cols = (i1.astype(F32), i2.astype(F32), r1, r2, p1, p2)
        route = jnp.zeros(logits.shape, F32)
        for k, col in enumerate(cols):
            route = jnp.where(idx == k, col, route)
        route_ref[...] = route


def _router(x2d, sh, sc, g, router_w, base_cnt, tile, rows_per_mod, h_rows, h_prev=None,
            row_off=0):
    n, d = x2d.shape
    ne = router_w.shape[-1]
    assert ne == ROUTE_COLS and n % tile == 0 and row_off % tile == 0
    n_steps = n // tile
    tail = h_rows - (row_off + n)
    assert 0 <= tail < tile
    fill_tail = h_prev is None and tail > 0
    last = n_steps - 1
    clamp = (lambda i: jnp.minimum(i, last)) if fill_tail else (lambda i: i)
    per_row_mod = sh.shape[1] > 1
    if per_row_mod:
        mod_spec = pl.BlockSpec((1, tile, d), lambda i: (0, clamp(i), 0))
    else:
        assert rows_per_mod % tile == 0
        mod_spec = pl.BlockSpec((1, 1, d), lambda i: (clamp(i) // (rows_per_mod // tile), 0, 0))
    in_specs = [
        pl.BlockSpec((tile, d), lambda i: (clamp(i), 0)),
        mod_spec,
        mod_spec,
        _resident((1, d)),
        _resident(router_w.shape),
        _resident((1, ne)),
    ]
    args = [x2d, sh, sc, g.reshape(1, d), router_w, base_cnt]
    n_in = len(args)
    aliases = {}
    body = functools.partial(_router_kernel, n_steps=n_steps)
    if h_prev is not None:
        assert h_prev.shape == (h_rows, d)
        in_specs.append(pl.BlockSpec(memory_space=pl.ANY))
        args.append(h_prev)
        aliases = {n_in: 0}

        def body(*refs):
            _router_kernel(*refs[:n_in], *refs[n_in + 1:], n_steps=n_steps)

    blk_off = row_off // tile
    return pl.pallas_call(
        body,
        grid=(n_steps + int(fill_tail),),
        in_specs=in_specs,
        out_specs=[
            pl.BlockSpec((tile, d), lambda i: (i + blk_off, 0)),
            pl.BlockSpec((tile, ne), lambda i: (clamp(i), 0)),
            pl.BlockSpec((1, ne), lambda i: (0, 0)),
        ],
        out_shape=[
            jax.ShapeDtypeStruct((h_rows, d), F32),
            jax.ShapeDtypeStruct((n, ne), F32),
            jax.ShapeDtypeStruct((1, ne), F32),
        ],
        scratch_shapes=[pltpu.VMEM((tile, tile), BF16), pltpu.VMEM((1, ne), F32)],
        input_output_aliases=aliases,
        compiler_params=_params("arbitrary"),
        name="router",
    )(*args)


def _row_copy(src_hbm, row, dst, dst_row, sem):
    return pltpu.make_async_copy(src_hbm.at[pl.ds(row, 1), :], dst.at[pl.ds(dst_row, 1), :], sem)


def _moe_grouped_kernel(te_ref, na_ref, src_ref, h_hbm, wg_ref, wu_ref, wd_ref, o_ref,
                        xs, hbuf, sem, *, tm, rows_per_step):
    del te_ref
    i = pl.program_id(0)
    c = pl.program_id(1)
    n_active = na_ref[0]
    slot = i % 2

    @pl.when(jnp.logical_and(i == 0, c == 0))
    def _():
        def body(r, carry):
            _row_copy(h_hbm, src_ref[r], xs.at[0], r, sem.at[0]).start()
            return carry

        lax.fori_loop(0, tm, body, 0)

    @pl.when(jnp.logical_and(i <= n_active, c == 0))
    def _():
        pltpu.make_async_copy(h_hbm.at[pl.ds(0, tm), :], xs.at[slot], sem.at[slot]).wait()

    @pl.when(c == 0)
    def _():
        o_ref[...] = jnp.zeros_like(o_ref)

    @pl.when(jnp.logical_and(i < n_active, c == 0))
    def _():
        hbuf[...] = xs[slot].astype(BF16)

    @pl.when(i < n_active)
    def _():
        nslot = (i + 1) % 2
        for k in range(rows_per_step):
            r = c * rows_per_step + k
            _row_copy(h_hbm, src_ref[(i + 1) * tm + r], xs.at[nslot], r, sem.at[nslot]).start()

        h = hbuf[...]
        a = _dot(h, wg_ref[0].astype(BF16))
        b = _dot(h, wu_ref[0].astype(BF16))
        o_ref[...] += _dot(_silu_mul(a, b).astype(BF16), wd_ref[0].astype(BF16))


def _moe_grouped(h_all, tile_expert, n_active, src, wg, wu, wd, tm, fc):
    ne, d, dff = wg.shape
    n_tiles = tile_expert.shape[0]
    nc = dff // fc
    assert dff % fc == 0 and tm % nc == 0 and src.shape[0] == n_tiles * tm
    rows_per_step = tm // nc

    def w_col(i, c, te, na, src_):
        return (te[i], 0, jnp.where(i < na[0], c, nc - 1))

    def w_row(i, c, te, na, src_):
        return (te[i], jnp.where(i < na[0], c, nc - 1), 0)

    grid_spec = pltpu.PrefetchScalarGridSpec(
        num_scalar_prefetch=3,
        grid=(n_tiles, nc),
        in_specs=[
            pl.BlockSpec(memory_space=pl.ANY),
            pl.BlockSpec((1, d, fc), w_col),
            pl.BlockSpec((1, d, fc), w_col),
            pl.BlockSpec((1, fc, d), w_row),
        ],
        out_specs=pl.BlockSpec((tm, d), lambda i, c, te, na, src_: (i, 0)),
        scratch_shapes=[
            pltpu.VMEM((2, tm, d), F32),
            pltpu.VMEM((tm, d), BF16),
            pltpu.SemaphoreType.DMA((2,)),
        ],
    )
    return pl.pallas_call(
        functools.partial(_moe_grouped_kernel, tm=tm, rows_per_step=rows_per_step),
        grid_spec=grid_spec,
        out_shape=jax.ShapeDtypeStruct((n_tiles * tm, d), F32),
        compiler_params=_params("arbitrary", "arbitrary"),
        name="moe_grouped",
    )(tile_expert, n_active, src, h_all, wg, wu, wd)


def _combine_kernel(pos_ref, x_ref, gt_ref, route_ref, gf_ref, f_hbm, o_ref, fbuf, sem,
                    *, tile, tok_off):
    j = pl.program_id(0)

    def request(step):
        slot = step % 2
        base = (tok_off + step * tile) * TOP_K

        def body(r, carry):
            for k in range(TOP_K):
                _row_copy(f_hbm, pos_ref[base + TOP_K * r + k], fbuf.at[slot, k], r,
                          sem.at[slot]).start()
            return carry

        lax.fori_loop(0, tile, body, 0)

    @pl.when(j == 0)
    def _():
        request(0)

    @pl.when(j + 1 < pl.num_programs(0))
    def _():
        request(j + 1)

    slot = j % 2
    for k in range(TOP_K):
        pltpu.make_async_copy(f_hbm.at[pl.ds(0, tile), :], fbuf.at[slot, k], sem.at[slot]).wait()
    route = route_ref[...]
    f = (route[:, COL_PROB:COL_PROB + 1] * fbuf[slot, 0]
         + route[:, COL_PROB + 1:COL_PROB + 2] * fbuf[slot, 1])
    o_ref[...] = _rms(x_ref[...] + gt_ref[0] * f, gf_ref[...])


def _combine(x2d, gt, route, g_final, f_sorted, pos_flat, tile, rows_per_mod, tok_off):
    n, d = x2d.shape
    assert n % tile == 0
    per_row_mod = gt.shape[1] > 1
    if per_row_mod:
        mod_spec = pl.BlockSpec((1, tile, d), lambda j, pos: (0, j, 0))
    else:
        assert rows_per_mod % tile == 0
        mod_spec = pl.BlockSpec((1, 1, d), lambda j, pos: (j // (rows_per_mod // tile), 0, 0))
    grid_spec = pltpu.PrefetchScalarGridSpec(
        num_scalar_prefetch=1,
        grid=(n // tile,),
        in_specs=[
            pl.BlockSpec((tile, d), lambda j, pos: (j, 0)),
            mod_spec,
            pl.BlockSpec((tile, ROUTE_COLS), lambda j, pos: (j, 0)),
            pl.BlockSpec((1, d), lambda j, pos: (0, 0)),
            pl.BlockSpec(memory_space=pl.ANY),
        ],
        out_specs=pl.BlockSpec((tile, d), lambda j, pos: (j, 0)),
        scratch_shapes=[
            pltpu.VMEM((2, TOP_K, tile, d), F32),
            pltpu.SemaphoreType.DMA((2,)),
        ],
    )
    return pl.pallas_call(
        functools.partial(_combine_kernel, tile=tile, tok_off=tok_off),
        grid_spec=grid_spec,
        out_shape=jax.ShapeDtypeStruct((n, d), F32),
        compiler_params=_params("arbitrary"),
        name="combine",
    )(pos_flat, x2d, gt, route, g_final.reshape(1, d), f_sorted)


def _routing_tables(route_all, counts, tm, n_tiles):
    ne = counts.shape[0]
    n = route_all.shape[0]
    e_idx = route_all[:, COL_EXPERT:COL_EXPERT + TOP_K].astype(I32)
    rank = route_all[:, COL_RANK:COL_RANK + TOP_K].astype(I32)
    padded = (counts + tm - 1) // tm * tm
    ends = jnp.cumsum(padded)
    offs = ends - padded
    onehot = e_idx[..., None] == jnp.arange(ne, dtype=I32)
    pos = jnp.sum(jnp.where(onehot, offs, 0), axis=-1) + rank
    n_active = ends[-1] // tm
    starts = jnp.arange(n_tiles, dtype=I32) * tm
    tile_expert = jnp.sum(starts[:, None] >= ends[None, :], axis=1).astype(I32)
    last_expert = jnp.sum((n_active - 1) * tm >= ends).astype(I32)
    tile_expert = jnp.minimum(tile_expert, last_expert)
    pos_flat = pos.reshape(-1)
    tok = jnp.repeat(jnp.arange(n, dtype=I32), TOP_K)
    src = jnp.zeros((n_tiles * tm,), I32).at[pos_flat].set(tok)
    return pos_flat, src, tile_expert, n_active.astype(I32).reshape(1)


def _moe_layer(xp, xs, mod_p, mod_s, g, router_w, wg, wu, wd, g_final):
    nb, seq, d = xp.shape
    ns = xs.shape[0]
    ne = router_w.shape[-1]
    n_all = nb * seq + ns
    sh_p, sc_p, gt_p = mod_p
    sh_s, sc_s, gt_s = mod_s
    xp2 = xp.reshape(nb * seq, d)

    zero_cnt = jnp.zeros((1, ne), F32)
    h_all, route_p, cnt_p = _router(xp2, sh_p, sc_p, g, router_w, zero_cnt, ROUTER_TILE, seq,
                                    n_all)
    h_all, route_s, cnt_s = _router(xs, sh_s, sc_s, g, router_w, cnt_p, ns, ns, n_all,
                                    h_prev=h_all, row_off=nb * seq)

    tm = MOE_ROW_TILE
    n_tiles = (n_all * TOP_K) // tm + ne + 1
    route_all = jnp.concatenate([route_p, route_s], axis=0)
    pos_flat, src, tile_expert, n_active = _routing_tables(
        route_all, cnt_s[0].astype(I32), tm, n_tiles)

    f_sorted = _moe_grouped(h_all, tile_expert, n_active, src, wg, wu, wd, tm, MOE_FF_CHUNK)
    yp = _combine(xp2, gt_p, route_p, g_final, f_sorted, pos_flat, COMBINE_TILE, seq, 0)
    ys = _combine(xs, gt_s, route_s, g_final, f_sorted, pos_flat, ns, ns, nb * seq)
    return yp.reshape(nb, seq, d), ys


def kernel(x_prompt, x_sample, c_prompt, c_sample, state_conv, state_pool, w_ada, b_ada, g_mix,
           g_ffn, w_in, conv_w, pool_w, pool_scale, g_conv_out, g_pool_out, w_out, dense_w_gate,
           dense_w_up, dense_w_down, router_w, moe_w_gate, moe_w_up, moe_w_down, g_final):
    depth = w_ada.shape[0]
    nb, seq, d = x_prompt.shape
    ns = x_sample.shape[0]
    assert x_sample.shape[1] == 1 and depth == 2
    assert sum(DENSE_FF_CHUNKS) == dense_w_gate.shape[-1]

    mod = _ada(jnp.concatenate([c_prompt, c_sample], axis=0), w_ada, b_ada)
    mod = mod.reshape(depth, nb + ns, 6, d)
    mod_p = [jnp.transpose(mod[i, :nb], (1, 0, 2))[:, :, None, :] for i in range(depth)]
    mod_s = [jnp.transpose(mod[i, nb:], (1, 0, 2))[:, None, :, :] for i in range(depth)]

    w_in_b = w_in.astype(BF16)
    w_out_b = w_out.astype(BF16)
    pool_w_b = pool_w.astype(BF16)
    dense_b = [w.astype(BF16) for w in (dense_w_gate, dense_w_up, dense_w_down)]
    cb_t = jnp.transpose(state_conv, (0, 2, 1, 3))
    pb_t = jnp.transpose(state_pool, (0, 2, 1, 3))

    xp = x_prompt
    xs = x_sample.reshape(ns, d)
    conv_p, pool_p, conv_s, pool_s = [], [], [], []
    for i in range(depth):
        sh1, sc1, gt1, sh2, sc2, gt2 = mod_p[i]
        mix_w = (g_mix[i], w_in_b[i], conv_w[i], pool_w_b[i], pool_scale[i], g_conv_out[i],
                 g_pool_out[i], w_out_b[i])
        xp, cs, ps = _mix_prompt(xp, sh1, sc1, gt1, *mix_w, tile=PROMPT_TILE)
        conv_p.append(cs)
        pool_p.append(ps)
        s1, c1, t1, s2, c2, t2 = mod_s[i]
        xs, v_new, u_new = _mix_sample(xs, s1[0], c1[0], t1[0], *mix_w, cb_t[i], pb_t[i])
        conv_s.append(jnp.concatenate([state_conv[i][:, 1:], v_new[:, None, :]], axis=1))
        pool_s.append(jnp.concatenate([state_pool[i][:, 1:], u_new[:, None, :]], axis=1))
        j = i // 2
        if i % 2 == 0:
            wg, wu, wd = (w[j] for w in dense_b)
            xp = _ffn(xp, sh2, sc2, gt2, g_ffn[i], wg, wu, wd, PROMPT_TILE, DENSE_FF_CHUNKS)
            xs = _ffn(xs[None], s2, c2, t2, g_ffn[i], wg, wu, wd, ns, DENSE_FF_CHUNKS)[0]
        else:
            xp, xs = _moe_layer(xp, xs, (sh2, sc2, gt2), (s2, c2, t2), g_ffn[i], router_w[j],
                                moe_w_gate[j], moe_w_up[j], moe_w_down[j], g_final)

    return (xp, xs.reshape(ns, 1, d), jnp.stack(conv_p), jnp.stack(pool_p),
            jnp.stack(conv_s), jnp.stack(pool_s))
```

```python
import functools

import jax
import jax.numpy as jnp
from jax import lax
from jax.experimental import pallas as pl
from jax.experimental.pallas import tpu as pltpu

F32 = jnp.float32
BF16 = jnp.bfloat16
I32 = jnp.int32

EPS = 1e-6
CONV_K = 3
POOL_WINDOWS = (2, 4, 8, 16)
POOL_HIST = max(POOL_WINDOWS) - 1
TOP_K = 2

CONV_PAD = 8
POOL_PAD = 16

ROUTE_COLS = 8
COL_EXPERT, COL_RANK, COL_PROB = 0, 2, 4

VMEM_LIMIT_BYTES = 56 * 1024 * 1024

PROMPT_TILE = 512
ROUTER_TILE = 1024
COMBINE_TILE = 512
MOE_ROW_TILE = 896
MOE_FF_CHUNK = 512
DENSE_FF_CHUNKS = (768, 768, 768, 512)


def _params(*sem):
    return pltpu.CompilerParams(dimension_semantics=sem, vmem_limit_bytes=VMEM_LIMIT_BYTES)


def _resident(shape):
    nd = len(shape)
    return pl.BlockSpec(shape, lambda *_: (0,) * nd, pipeline_mode=pl.Buffered(1))


def _rms(x, g):
    ms = jnp.mean(x * x, axis=-1, keepdims=True)
    return x * lax.rsqrt(ms + EPS) * g


def _mod_norm(x, g, sc, sh):
    ms = jnp.mean(x * x, axis=-1, keepdims=True)
    return x * lax.rsqrt(ms + EPS) * (g * (1.0 + sc)) + sh


def _dot(a, b):
    return jnp.dot(a, b, preferred_element_type=F32)


def _silu_mul(a, b):
    return a * jax.nn.sigmoid(a) * b


def _ada_kernel(c_ref, w_ref, b_ref, o_ref):
    c = c_ref[...]
    a = (c * jax.nn.sigmoid(c)).astype(BF16)
    o_ref[0] = _dot(a, w_ref[0].astype(BF16)) + b_ref[0]


def _ada(c_all, w_ada, b_ada):
    depth, d, n = w_ada.shape
    m = c_all.shape[0]
    tn = 1024
    return pl.pallas_call(
        _ada_kernel,
        grid=(depth, n // tn),
        in_specs=[
            pl.BlockSpec((m, d), lambda i, j: (0, 0)),
            pl.BlockSpec((1, d, tn), lambda i, j: (i, 0, j)),
            pl.BlockSpec((1, 1, tn), lambda i, j: (i, 0, j)),
        ],
        out_specs=pl.BlockSpec((1, m, tn), lambda i, j: (i, 0, j)),
        out_shape=jax.ShapeDtypeStruct((depth, m, n), F32),
        compiler_params=_params("arbitrary", "arbitrary"),
        name="ada",
    )(c_all, w_ada, b_ada.reshape(depth, 1, n))


def _mix_tail(x, gt, bg, y, d_groups, poolw_ref, pscale_ref, gco_ref, gpo_ref, wout_ref):
    cw = gco_ref.shape[-1]
    ya = bg * y
    yb = jnp.concatenate(
        [_dot(d.astype(BF16), poolw_ref[g]) for g, d in enumerate(d_groups)], axis=-1
    ) * pscale_ref[...]
    ma = _rms(ya, gco_ref[...]).astype(BF16)
    mb = _rms(yb, gpo_ref[...]).astype(BF16)
    o = _dot(ma, wout_ref[0:cw, :]) + _dot(mb, wout_ref[cw:, :])
    return x + gt * o


def _mix_prompt_kernel(x_ref, sh_ref, sc_ref, gt_ref, g_ref, win_ref, convw_ref, poolw_ref,
                       pscale_ref, gco_ref, gpo_ref, wout_ref,
                       xo_ref, cs_ref, ps_ref, vbuf, ubuf, *, tile):
    l = pl.program_id(1)
    cw = convw_ref.shape[-1]
    pg = poolw_ref.shape[-1]

    @pl.when(l == 0)
    def _():
        vbuf[0:CONV_PAD, :] = jnp.zeros((CONV_PAD, cw), F32)
        ubuf[0:POOL_PAD, :] = jnp.zeros((POOL_PAD, ubuf.shape[-1]), F32)

    x = x_ref[0]
    h = _mod_norm(x, g_ref[...], sc_ref[0], sh_ref[0]).astype(BF16)
    p = _dot(h, win_ref[...])
    bg = p[:, 0:cw]
    v = p[:, cw:2 * cw] * p[:, 2 * cw:3 * cw]
    u = p[:, 3 * cw:]
    vbuf[CONV_PAD:CONV_PAD + tile, :] = v
    ubuf[POOL_PAD:POOL_PAD + tile, :] = u

    w = convw_ref[...]
    y = w[CONV_K - 1:CONV_K] * v
    for k in range(1, CONV_K):
        y = y + w[CONV_K - 1 - k:CONV_K - k] * vbuf[CONV_PAD - k:CONV_PAD - k + tile, :]

    pos = l * tile + lax.broadcasted_iota(I32, (tile, 1), 0)
    d_groups = []
    for g, win in enumerate(POOL_WINDOWS):
        ug = u[:, g * pg:(g + 1) * pg]
        acc = ug
        for k in range(1, win):
            acc = acc + ubuf[POOL_PAD - k:POOL_PAD - k + tile, g * pg:(g + 1) * pg]
        cnt = jnp.minimum(pos + 1, win).astype(F32)
        d_groups.append(acc / cnt - ug)

    xo_ref[0] = _mix_tail(x, gt_ref[0], bg, y, d_groups, poolw_ref, pscale_ref,
                          gco_ref, gpo_ref, wout_ref)

    cs_ref[0] = vbuf[CONV_PAD + tile - (CONV_K - 1):CONV_PAD + tile, :]
    ps_ref[0] = ubuf[POOL_PAD + tile - POOL_HIST:POOL_PAD + tile, :]
    vbuf[0:CONV_PAD, :] = vbuf[tile:tile + CONV_PAD, :]
    ubuf[0:POOL_PAD, :] = ubuf[tile:tile + POOL_PAD, :]


def _mix_prompt(x, sh, sc, gt, g, w_in, conv_w, pool_w, pool_scale, g_co, g_po, w_out, tile):
    b, seq, d = x.shape
    cw = conv_w.shape[-1]
    pw = pool_scale.shape[-1]
    row = lambda bi, li: (bi, 0, 0)
    return pl.pallas_call(
        functools.partial(_mix_prompt_kernel, tile=tile),
        grid=(b, seq // tile),
        in_specs=[
            pl.BlockSpec((1, tile, d), lambda bi, li: (bi, li, 0)),
            pl.BlockSpec((1, 1, d), row),
            pl.BlockSpec((1, 1, d), row),
            pl.BlockSpec((1, 1, d), row),
            _resident((1, d)),
            _resident(w_in.shape),
            _resident(conv_w.shape),
            _resident(pool_w.shape),
            _resident((1, pw)),
            _resident((1, cw)),
            _resident((1, pw)),
            _resident(w_out.shape),
        ],
        out_specs=[
            pl.BlockSpec((1, tile, d), lambda bi, li: (bi, li, 0)),
            pl.BlockSpec((1, CONV_K - 1, cw), row),
            pl.BlockSpec((1, POOL_HIST, pw), row),
        ],
        out_shape=[
            jax.ShapeDtypeStruct((b, seq, d), F32),
            jax.ShapeDtypeStruct((b, CONV_K - 1, cw), F32),
            jax.ShapeDtypeStruct((b, POOL_HIST, pw), F32),
        ],
        scratch_shapes=[
            pltpu.VMEM((CONV_PAD + tile, cw), F32),
            pltpu.VMEM((POOL_PAD + tile, pw), F32),
        ],
        compiler_params=_params("arbitrary", "arbitrary"),
        name="mix_prompt",
    )(x, sh, sc, gt, g.reshape(1, d), w_in, conv_w, pool_w, pool_scale.reshape(1, pw),
      g_co.reshape(1, cw), g_po.reshape(1, pw), w_out)


def _mix_sample_kernel(x_ref, sh_ref, sc_ref, gt_ref, g_ref, win_ref, convw_ref, poolw_ref,
                       pscale_ref, gco_ref, gpo_ref, wout_ref, cb_ref, pb_ref,
                       xo_ref, v_ref, u_ref):
    cw = convw_ref.shape[-1]
    pg = poolw_ref.shape[-1]
    x = x_ref[...]
    h = _mod_norm(x, g_ref[...], sc_ref[...], sh_ref[...]).astype(BF16)
    p = _dot(h, win_ref[...])
    bg = p[:, 0:cw]
    v = p[:, cw:2 * cw] * p[:, 2 * cw:3 * cw]
    u = p[:, 3 * cw:]
    v_ref[...] = v
    u_ref[...] = u

    w = convw_ref[...]
    y = w[CONV_K - 1:CONV_K] * v
    for k in range(1, CONV_K):
        y = y + w[CONV_K - 1 - k:CONV_K - k] * cb_ref[CONV_K - 1 - k]

    d_groups = []
    for g, win in enumerate(POOL_WINDOWS):
        ug = u[:, g * pg:(g + 1) * pg]
        acc = ug
        for k in range(1, win):
            acc = acc + pb_ref[POOL_HIST - k, :, g * pg:(g + 1) * pg]
        d_groups.append(acc / float(win) - ug)

    xo_ref[...] = _mix_tail(x, gt_ref[...], bg, y, d_groups, poolw_ref, pscale_ref,
                            gco_ref, gpo_ref, wout_ref)


def _mix_sample(x, sh, sc, gt, g, w_in, conv_w, pool_w, pool_scale, g_co, g_po, w_out, cb, pb):
    n, d = x.shape
    cw = conv_w.shape[-1]
    pw = pool_scale.shape[-1]
    return pl.pallas_call(
        _mix_sample_kernel,
        out_shape=[
            jax.ShapeDtypeStruct((n, d), F32),
            jax.ShapeDtypeStruct((n, cw), F32),
            jax.ShapeDtypeStruct((n, pw), F32),
        ],
        compiler_params=pltpu.CompilerParams(vmem_limit_bytes=VMEM_LIMIT_BYTES),
        name="mix_sample",
    )(x, sh, sc, gt, g.reshape(1, d), w_in, conv_w, pool_w, pool_scale.reshape(1, pw),
      g_co.reshape(1, cw), g_po.reshape(1, pw), w_out, cb, pb)


def _ffn_kernel(x_ref, sh_ref, sc_ref, gt_ref, g_ref, wg_ref, wu_ref, wd_ref, o_ref, *, chunks):
    x = x_ref[0]
    h = _mod_norm(x, g_ref[...], sc_ref[0], sh_ref[0]).astype(BF16)
    acc = None
    lo = 0
    for fc in chunks:
        a = _dot(h, wg_ref[:, lo:lo + fc])
        b = _dot(h, wu_ref[:, lo:lo + fc])
        part = _dot(_silu_mul(a, b).astype(BF16), wd_ref[lo:lo + fc, :])
        acc = part if acc is None else acc + part
        lo += fc
    o_ref[0] = x + gt_ref[0] * acc


def _ffn(x, sh, sc, gt, g, wg, wu, wd, tile, chunks):
    b, seq, d = x.shape
    tm = sh.shape[1]
    tmod = tile if tm > 1 else 1
    mod_map = (lambda bi, li: (bi, li, 0)) if tm > 1 else (lambda bi, li: (bi, 0, 0))
    return pl.pallas_call(
        functools.partial(_ffn_kernel, chunks=chunks),
        grid=(b, seq // tile),
        in_specs=[
            pl.BlockSpec((1, tile, d), lambda bi, li: (bi, li, 0)),
            pl.BlockSpec((1, tmod, d), mod_map),
            pl.BlockSpec((1, tmod, d), mod_map),
            pl.BlockSpec((1, tmod, d), mod_map),
            _resident((1, d)),
            _resident(wg.shape),
            _resident(wu.shape),
            _resident(wd.shape),
        ],
        out_specs=pl.BlockSpec((1, tile, d), lambda bi, li: (bi, li, 0)),
        out_shape=jax.ShapeDtypeStruct((b, seq, d), F32),
        compiler_params=_params("arbitrary", "arbitrary"),
        name="ffn",
    )(x, sh, sc, gt, g.reshape(1, d), wg, wu, wd)


def _split_bf16(a):
    hi = a.astype(BF16)
    return hi, (a - hi.astype(F32)).astype(BF16)


def _router_kernel(x_ref, sh_ref, sc_ref, g_ref, rw_ref, base_ref, h_ref, route_ref, cnt_ref,
                   tri, carry, *, n_steps):
    t = x_ref.shape[0]
    step = pl.program_id(0)

    @pl.when(step == 0)
    def _():
        r = lax.broadcasted_iota(I32, (t, t), 0)
        c = lax.broadcasted_iota(I32, (t, t), 1)
        tri[...] = jnp.where(c < r, 1.0, 0.0).astype(BF16)
        carry[...] = base_ref[...]

    @pl.when(step >= n_steps)
    def _():
        h_ref[...] = jnp.zeros_like(h_ref)

    @pl.when(step < n_steps)
    def _():
        h = _mod_norm(x_ref[...], g_ref[...], sc_ref[0], sh_ref[0])
        h_ref[...] = h
        h_hi, h_lo = _split_bf16(h)
        rw_hi, rw_lo = _split_bf16(rw_ref[...])
        logits = _dot(h_hi, rw_hi) + _dot(h_lo, rw_hi) + _dot(h_hi, rw_lo)

        ne = logits.shape[-1]
        idx = lax.broadcasted_iota(I32, logits.shape, 1)
        m1 = jnp.max(logits, axis=-1, keepdims=True)
        i1 = jnp.min(jnp.where(logits == m1, idx, ne), axis=-1, keepdims=True)
        sel1 = idx == i1
        rest = jnp.where(sel1, -jnp.inf, logits)
        m2 = jnp.max(rest, axis=-1, keepdims=True)
        i2 = jnp.min(jnp.where(rest == m2, idx, ne), axis=-1, keepdims=True)
        sel2 = idx == i2
        e = jnp.exp(m2 - m1)
        p1 = 1.0 / (1.0 + e)
        p2 = e / (1.0 + e)

        chosen = jnp.where(sel1, 1.0, jnp.where(sel2, 1.0, 0.0))
        before = _dot(tri[...], chosen.astype(BF16)) + carry[...]
        r1 = jnp.sum(jnp.where(sel1, before, 0.0), axis=-1, keepdims=True)
        r2 = jnp.sum(jnp.where(sel2, before, 0.0), axis=-1, keepdims=True)
        carry[...] += jnp.sum(chosen, axis=0, keepdims=True)
        cnt_ref[...] = carry[...]

        cols = (i1.astype(F32), i2.astype(F32), r1, r2, p1, p2)
        route = jnp.zeros(logits.shape, F32)
        for k, col in enumerate(cols):
            route = jnp.where(idx == k, col, route)
        route_ref[...] = route


def _router(x2d, sh, sc, g, router_w, base_cnt, tile, rows_per_mod, h_rows, h_prev=None,
            row_off=0):
    n, d = x2d.shape
    ne = router_w.shape[-1]
    assert ne == ROUTE_COLS and n % tile == 0 and row_off % tile == 0
    n_steps = n // tile
    tail = h_rows - (row_off + n)
    assert 0 <= tail < tile
    fill_tail = h_prev is None and tail > 0
    last = n_steps - 1
    clamp = (lambda i: jnp.minimum(i, last)) if fill_tail else (lambda i: i)
    per_row_mod = sh.shape[1] > 1
    if per_row_mod:
        mod_spec = pl.BlockSpec((1, tile, d), lambda i: (0, clamp(i), 0))
    else:
        assert rows_per_mod % tile == 0
        mod_spec = pl.BlockSpec((1, 1, d), lambda i: (clamp(i) // (rows_per_mod // tile), 0, 0))
    in_specs = [
        pl.BlockSpec((tile, d), lambda i: (clamp(i), 0)),
        mod_spec,
        mod_spec,
        _resident((1, d)),
        _resident(router_w.shape),
        _resident((1, ne)),
    ]
    args = [x2d, sh, sc, g.reshape(1, d), router_w, base_cnt]
    n_in = len(args)
    aliases = {}
    body = functools.partial(_router_kernel, n_steps=n_steps)
    if h_prev is not None:
        assert h_prev.shape == (h_rows, d)
        in_specs.append(pl.BlockSpec(memory_space=pl.ANY))
        args.append(h_prev)
        aliases = {n_in: 0}

        def body(*refs):
            _router_kernel(*refs[:n_in], *refs[n_in + 1:], n_steps=n_steps)

    blk_off = row_off // tile
    return pl.pallas_call(
        body,
        grid=(n_steps + int(fill_tail),),
        in_specs=in_specs,
        out_specs=[
            pl.BlockSpec((tile, d), lambda i: (i + blk_off, 0)),
            pl.BlockSpec((tile, ne), lambda i: (clamp(i), 0)),
            pl.BlockSpec((1, ne), lambda i: (0, 0)),
        ],
        out_shape=[
            jax.ShapeDtypeStruct((h_rows, d), F32),
            jax.ShapeDtypeStruct((n, ne), F32),
            jax.ShapeDtypeStruct((1, ne), F32),
        ],
        scratch_shapes=[pltpu.VMEM((tile, tile), BF16), pltpu.VMEM((1, ne), F32)],
        input_output_aliases=aliases,
        compiler_params=_params("arbitrary"),
        name="router",
    )(*args)


def _row_copy(src_hbm, row, dst, dst_row, sem):
    return pltpu.make_async_copy(src_hbm.at[pl.ds(row, 1), :], dst.at[pl.ds(dst_row, 1), :], sem)


def _row_copy_out(src, src_row, dst_hbm, row, sem):
    return pltpu.make_async_copy(src.at[pl.ds(src_row, 1), :], dst_hbm.at[pl.ds(row, 1), :], sem)


def _moe_grouped_kernel(te_ref, na_ref, src_ref, dst_ref, h_hbm, wg_ref, wu_ref, wd_ref, y_hbm,
                        xs, hbuf, obuf, gsem, ssem, fsem, *, tm, rows_per_step, fill_row,
                        fill_tiles):
    del te_ref
    i = pl.program_id(0)
    c = pl.program_id(1)
    n_active = na_ref[0]
    slot = i % 2
    other = (i + 1) % 2

    def send(tile, r):
        return _row_copy_out(obuf.at[(tile + 2) % 2], r, y_hbm, dst_ref[(tile + 1) * tm + r],
                             ssem.at[(tile + 2) % 2])

    @pl.when(jnp.logical_and(i == 0, c == 0))
    def _():
        obuf[1] = jnp.zeros((tm, obuf.shape[-1]), F32)
        fills = [pltpu.make_async_copy(obuf.at[1], y_hbm.at[pl.ds(fill_row + q * tm, tm), :], fsem)
                 for q in range(fill_tiles)]
        for f in fills:
            f.start()
        for f in fills:
            f.wait()

        def body(r, carry):
            _row_copy(h_hbm, src_ref[r], xs.at[0], r, gsem.at[0]).start()
            return carry

        lax.fori_loop(0, tm, body, 0)

    @pl.when(jnp.logical_and(i <= n_active, c == 0))
    def _():
        pltpu.make_async_copy(h_hbm.at[pl.ds(0, tm), :], xs.at[slot], gsem.at[slot]).wait()

    @pl.when(jnp.logical_and(jnp.logical_and(i >= 1, i <= n_active + 1), c == 0))
    def _():
        pltpu.make_async_copy(obuf.at[slot], y_hbm.at[pl.ds(0, tm), :], ssem.at[slot]).wait()

    @pl.when(jnp.logical_and(i < n_active, c == 0))
    def _():
        obuf[slot] = jnp.zeros((tm, obuf.shape[-1]), F32)
        hbuf[...] = xs[slot].astype(BF16)

    @pl.when(jnp.logical_and(i == n_active, c == 0))
    def _():
        def body(r, carry):
            send(i - 1, r).start()
            return carry

        lax.fori_loop(0, tm, body, 0)

    @pl.when(i < n_active)
    def _():
        for k in range(rows_per_step):
            r = c * rows_per_step + k
            _row_copy(h_hbm, src_ref[(i + 1) * tm + r], xs.at[other], r, gsem.at[other]).start()
            send(i - 1, r).start()

        h = hbuf[...]
        a = _dot(h, wg_ref[0].astype(BF16))
        b = _dot(h, wu_ref[0].astype(BF16))
        obuf[slot] += _dot(_silu_mul(a, b).astype(BF16), wd_ref[0].astype(BF16))


def _moe_grouped(h_all, tile_expert, n_active, src, dst, wg, wu, wd, tm, fc, y_rows, fill_row,
                 fill_tiles):
    ne, d, dff = wg.shape
    n_tiles = tile_expert.shape[0]
    nc = dff // fc
    assert dff % fc == 0 and tm % nc == 0
    assert src.shape[0] == n_tiles * tm and dst.shape[0] == n_tiles * tm
    rows_per_step = tm // nc

    def w_col(i, c, te, na, src_, dst_):
        return (te[i], 0, jnp.where(i < na[0], c, nc - 1))

    def w_row(i, c, te, na, src_, dst_):
        return (te[i], jnp.where(i < na[0], c, nc - 1), 0)

    grid_spec = pltpu.PrefetchScalarGridSpec(
        num_scalar_prefetch=4,
        grid=(n_tiles, nc),
        in_specs=[
            pl.BlockSpec(memory_space=pl.ANY),
            pl.BlockSpec((1, d, fc), w_col),
            pl.BlockSpec((1, d, fc), w_col),
            pl.BlockSpec((1, fc, d), w_row),
        ],
        out_specs=pl.BlockSpec(memory_space=pl.ANY),
        scratch_shapes=[
            pltpu.VMEM((2, tm, d), F32),
            pltpu.VMEM((tm, d), BF16),
            pltpu.VMEM((2, tm, d), F32),
            pltpu.SemaphoreType.DMA((2,)),
            pltpu.SemaphoreType.DMA((2,)),
            pltpu.SemaphoreType.DMA,
        ],
    )
    return pl.pallas_call(
        functools.partial(_moe_grouped_kernel, tm=tm, rows_per_step=rows_per_step,
                          fill_row=fill_row, fill_tiles=fill_tiles),
        grid_spec=grid_spec,
        out_shape=jax.ShapeDtypeStruct((y_rows, d), F32),
        compiler_params=_params("arbitrary", "arbitrary"),
        name="moe_grouped",
    )(tile_expert, n_active, src, dst, h_all, wg, wu, wd)


def _combine_kernel(x_ref, gt_ref, route_ref, gf_ref, y0_ref, y1_ref, o_ref):
    route = route_ref[...]
    f = (route[:, COL_PROB:COL_PROB + 1] * y0_ref[...]
         + route[:, COL_PROB + 1:COL_PROB + 2] * y1_ref[...])
    o_ref[...] = _rms(x_ref[...] + gt_ref[0] * f, gf_ref[...])


def _combine(x2d, gt, route, g_final, y_tok, tile, rows_per_mod, tok_off, plane_rows):
    n, d = x2d.shape
    assert n % tile == 0 and tok_off % tile == 0 and plane_rows % tile == 0
    per_row_mod = gt.shape[1] > 1
    if per_row_mod:
        mod_spec = pl.BlockSpec((1, tile, d), lambda j: (0, j, 0))
    else:
        assert rows_per_mod % tile == 0
        mod_spec = pl.BlockSpec((1, 1, d), lambda j: (j // (rows_per_mod // tile), 0, 0))
    first = tok_off // tile
    second = (plane_rows + tok_off) // tile
    return pl.pallas_call(
        _combine_kernel,
        grid=(n // tile,),
        in_specs=[
            pl.BlockSpec((tile, d), lambda j: (j, 0)),
            mod_spec,
            pl.BlockSpec((tile, ROUTE_COLS), lambda j: (j, 0)),
            pl.BlockSpec((1, d), lambda j: (0, 0)),
            pl.BlockSpec((tile, d), lambda j: (first + j, 0)),
            pl.BlockSpec((tile, d), lambda j: (second + j, 0)),
        ],
        out_specs=pl.BlockSpec((tile, d), lambda j: (j, 0)),
        out_shape=jax.ShapeDtypeStruct((n, d), F32),
        compiler_params=_params("arbitrary"),
        name="combine",
    )(x2d, gt, route, g_final.reshape(1, d), y_tok, y_tok)


def _routing_tables(route_all, counts, tm, n_tiles, plane_rows):
    ne = counts.shape[0]
    n = route_all.shape[0]
    e_idx = route_all[:, COL_EXPERT:COL_EXPERT + TOP_K].astype(I32)
    rank = route_all[:, COL_RANK:COL_RANK + TOP_K].astype(I32)
    padded = (counts + tm - 1) // tm * tm
    ends = jnp.cumsum(padded)
    offs = ends - padded
    onehot = e_idx[..., None] == jnp.arange(ne, dtype=I32)
    pos = jnp.sum(jnp.where(onehot, offs, 0), axis=-1) + rank
    n_active = ends[-1] // tm
    starts = jnp.arange(n_tiles, dtype=I32) * tm
    tile_expert = jnp.sum(starts[:, None] >= ends[None, :], axis=1).astype(I32)
    last_expert = jnp.sum((n_active - 1) * tm >= ends).astype(I32)
    tile_expert = jnp.minimum(tile_expert, last_expert)
    n_rows = n_tiles * tm
    inv = jnp.zeros((n_rows,), I32).at[pos.reshape(-1)].set(jnp.arange(n * TOP_K, dtype=I32))
    r = jnp.arange(n_rows, dtype=I32)[:, None]
    real_before = jnp.sum(jnp.clip(r - offs, 0, counts), axis=1)
    is_real = jnp.any(jnp.logical_and(r >= offs, r < offs + counts), axis=1)
    tok = inv // TOP_K
    choice = inv % TOP_K
    spare = n + jnp.minimum(r[:, 0] - real_before, ne * tm - 1)
    src = jnp.where(is_real, tok, 0)
    dst = jnp.where(is_real, choice * plane_rows + tok, spare)
    lead = n + ne * tm + jnp.arange(tm, dtype=I32)
    dst = jnp.concatenate([lead, dst])[:n_rows]
    return src, dst, tile_expert, n_active.astype(I32).reshape(1)


def _moe_layer(xp, xs, mod_p, mod_s, g, router_w, wg, wu, wd, g_final):
    nb, seq, d = xp.shape
    ns = xs.shape[0]
    ne = router_w.shape[-1]
    n_all = nb * seq + ns
    sh_p, sc_p, gt_p = mod_p
    sh_s, sc_s, gt_s = mod_s
    xp2 = xp.reshape(nb * seq, d)

    zero_cnt = jnp.zeros((1, ne), F32)
    h_all, route_p, cnt_p = _router(xp2, sh_p, sc_p, g, router_w, zero_cnt, ROUTER_TILE, seq,
                                    n_all)
    h_all, route_s, cnt_s = _router(xs, sh_s, sc_s, g, router_w, cnt_p, ns, ns, n_all,
                                    h_prev=h_all, row_off=nb * seq)

    tm = MOE_ROW_TILE
    n_tiles = (n_all * TOP_K) // tm + ne + 2
    fill_tiles = ne + 1
    plane_rows = n_all + fill_tiles * tm
    route_all = jnp.concatenate([route_p, route_s], axis=0)
    src, dst, tile_expert, n_active = _routing_tables(
        route_all, cnt_s[0].astype(I32), tm, n_tiles, plane_rows)

    y_tok = _moe_grouped(h_all, tile_expert, n_active, src, dst, wg, wu, wd, tm, MOE_FF_CHUNK,
                         plane_rows + n_all, n_all, fill_tiles)
    yp = _combine(xp2, gt_p, route_p, g_final, y_tok, COMBINE_TILE, seq, 0, plane_rows)
    ys = _combine(xs, gt_s, route_s, g_final, y_tok, ns, ns, nb * seq, plane_rows)
    return yp.reshape(nb, seq, d), ys


def kernel(x_prompt, x_sample, c_prompt, c_sample, state_conv, state_pool, w_ada, b_ada, g_mix,
           g_ffn, w_in, conv_w, pool_w, pool_scale, g_conv_out, g_pool_out, w_out, dense_w_gate,
           dense_w_up, dense_w_down, router_w, moe_w_gate, moe_w_up, moe_w_down, g_final):
    depth = w_ada.shape[0]
    nb, seq, d = x_prompt.shape
    ns = x_sample.shape[0]
    assert x_sample.shape[1] == 1 and depth == 2
    assert sum(DENSE_FF_CHUNKS) == dense_w_gate.shape[-1]

    mod = _ada(jnp.concatenate([c_prompt, c_sample], axis=0), w_ada, b_ada)
    mod = mod.reshape(depth, nb + ns, 6, d)
    mod_p = [jnp.transpose(mod[i, :nb], (1, 0, 2))[:, :, None, :] for i in range(depth)]
    mod_s = [jnp.transpose(mod[i, nb:], (1, 0, 2))[:, None, :, :] for i in range(depth)]

    w_in_b = w_in.astype(BF16)
    w_out_b = w_out.astype(BF16)
    pool_w_b = pool_w.astype(BF16)
    dense_b = [w.astype(BF16) for w in (dense_w_gate, dense_w_up, dense_w_down)]
    cb_t = jnp.transpose(state_conv, (0, 2, 1, 3))
    pb_t = jnp.transpose(state_pool, (0, 2, 1, 3))

    xp = x_prompt
    xs = x_sample.reshape(ns, d)
    conv_p, pool_p, conv_s, pool_s = [], [], [], []
    for i in range(depth):
        sh1, sc1, gt1, sh2, sc2, gt2 = mod_p[i]
        mix_w = (g_mix[i], w_in_b[i], conv_w[i], pool_w_b[i], pool_scale[i], g_conv_out[i],
                 g_pool_out[i], w_out_b[i])
        xp, cs, ps = _mix_prompt(xp, sh1, sc1, gt1, *mix_w, tile=PROMPT_TILE)
        conv_p.append(cs)
        pool_p.append(ps)
        s1, c1, t1, s2, c2, t2 = mod_s[i]
        xs, v_new, u_new = _mix_sample(xs, s1[0], c1[0], t1[0], *mix_w, cb_t[i], pb_t[i])
        conv_s.append(jnp.concatenate([state_conv[i][:, 1:], v_new[:, None, :]], axis=1))
        pool_s.append(jnp.concatenate([state_pool[i][:, 1:], u_new[:, None, :]], axis=1))
        j = i // 2
        if i % 2 == 0:
            wg, wu, wd = (w[j] for w in dense_b)
            xp = _ffn(xp, sh2, sc2, gt2, g_ffn[i], wg, wu, wd, PROMPT_TILE, DENSE_FF_CHUNKS)
            xs = _ffn(xs[None], s2, c2, t2, g_ffn[i], wg, wu, wd, ns, DENSE_FF_CHUNKS)[0]
        else:
            xp, xs = _moe_layer(xp, xs, (sh2, sc2, gt2), (s2, c2, t2), g_ffn[i], router_w[j],
                                moe_w_gate[j], moe_w_up[j], moe_w_down[j], g_final)

    return (xp, xs.reshape(ns, 1, d), jnp.stack(conv_p), jnp.stack(pool_p),
            jnp.stack(conv_s), jnp.stack(pool_s))
```

```python
import functools

import jax
import jax.numpy as jnp
from jax import lax
from jax.experimental import pallas as pl
from jax.experimental.pallas import tpu as pltpu

F32 = jnp.float32
BF16 = jnp.bfloat16
I32 = jnp.int32

EPS = 1e-6
CONV_K = 3
POOL_WINDOWS = (2, 4, 8, 16)
POOL_HIST = max(POOL_WINDOWS) - 1
TOP_K = 2

CONV_PAD = 8
POOL_PAD = 16

ROUTE_COLS = 8
COL_EXPERT, COL_RANK, COL_PROB = 0, 2, 4

VMEM_LIMIT_BYTES = 56 * 1024 * 1024

PROMPT_TILE = 512
ROUTER_TILE = 1024
COMBINE_TILE = 512
MOE_ROW_TILE = 896
MOE_FF_CHUNK = 512
DENSE_FF_CHUNKS = (768, 768, 768, 512)


def _params(*sem):
    return pltpu.CompilerParams(dimension_semantics=sem, vmem_limit_bytes=VMEM_LIMIT_BYTES)


def _resident(shape):
    nd = len(shape)
    return pl.BlockSpec(shape, lambda *_: (0,) * nd, pipeline_mode=pl.Buffered(1))


def _rms(x, g):
    ms = jnp.mean(x * x, axis=-1, keepdims=True)
    return x * lax.rsqrt(ms + EPS) * g


def _mod_norm(x, g, sc, sh):
    ms = jnp.mean(x * x, axis=-1, keepdims=True)
    return x * lax.rsqrt(ms + EPS) * (g * (1.0 + sc)) + sh


def _dot(a, b):
    return jnp.dot(a, b, preferred_element_type=F32)


def _silu_mul(a, b):
    return a * jax.nn.sigmoid(a) * b


def _ada_kernel(c_ref, w_ref, b_ref, o_ref):
    c = c_ref[...]
    a = (c * jax.nn.sigmoid(c)).astype(BF16)
    o_ref[0] = _dot(a, w_ref[0].astype(BF16)) + b_ref[0]


def _ada(c_all, w_ada, b_ada):
    depth, d, n = w_ada.shape
    m = c_all.shape[0]
    tn = 1024
    return pl.pallas_call(
        _ada_kernel,
        grid=(depth, n // tn),
        in_specs=[
            pl.BlockSpec((m, d), lambda i, j: (0, 0)),
            pl.BlockSpec((1, d, tn), lambda i, j: (i, 0, j)),
            pl.BlockSpec((1, 1, tn), lambda i, j: (i, 0, j)),
        ],
        out_specs=pl.BlockSpec((1, m, tn), lambda i, j: (i, 0, j)),
        out_shape=jax.ShapeDtypeStruct((depth, m, n), F32),
        compiler_params=_params("arbitrary", "arbitrary"),
        name="ada",
    )(c_all, w_ada, b_ada.reshape(depth, 1, n))


def _mix_tail(x, gt, bg, y, d_groups, poolw_ref, pscale_ref, gco_ref, gpo_ref, wout_ref):
    cw = gco_ref.shape[-1]
    ya = bg * y
    yb = jnp.concatenate(
        [_dot(d.astype(BF16), poolw_ref[g]) for g, d in enumerate(d_groups)], axis=-1
    ) * pscale_ref[...]
    ma = _rms(ya, gco_ref[...]).astype(BF16)
    mb = _rms(yb, gpo_ref[...]).astype(BF16)
    o = _dot(ma, wout_ref[0:cw, :]) + _dot(mb, wout_ref[cw:, :])
    return x + gt * o


def _mix_prompt_kernel(x_ref, sh_ref, sc_ref, gt_ref, g_ref, win_ref, convw_ref, poolw_ref,
                       pscale_ref, gco_ref, gpo_ref, wout_ref,
                       xo_ref, cs_ref, ps_ref, vbuf, ubuf, *, tile):
    l = pl.program_id(1)
    cw = convw_ref.shape[-1]
    pg = poolw_ref.shape[-1]

    @pl.when(l == 0)
    def _():
        vbuf[0:CONV_PAD, :] = jnp.zeros((CONV_PAD, cw), F32)
        ubuf[0:POOL_PAD, :] = jnp.zeros((POOL_PAD, ubuf.shape[-1]), F32)

    x = x_ref[0]
    h = _mod_norm(x, g_ref[...], sc_ref[0], sh_ref[0]).astype(BF16)
    p = _dot(h, win_ref[...])
    bg = p[:, 0:cw]
    v = p[:, cw:2 * cw] * p[:, 2 * cw:3 * cw]
    u = p[:, 3 * cw:]
    vbuf[CONV_PAD:CONV_PAD + tile, :] = v
    ubuf[POOL_PAD:POOL_PAD + tile, :] = u

    w = convw_ref[...]
    y = w[CONV_K - 1:CONV_K] * v
    for k in range(1, CONV_K):
        y = y + w[CONV_K - 1 - k:CONV_K - k] * vbuf[CONV_PAD - k:CONV_PAD - k + tile, :]

    pos = l * tile + lax.broadcasted_iota(I32, (tile, 1), 0)
    d_groups = []
    for g, win in enumerate(POOL_WINDOWS):
        ug = u[:, g * pg:(g + 1) * pg]
        acc = ug
        for k in range(1, win):
            acc = acc + ubuf[POOL_PAD - k:POOL_PAD - k + tile, g * pg:(g + 1) * pg]
        cnt = jnp.minimum(pos + 1, win).astype(F32)
        d_groups.append(acc / cnt - ug)

    xo_ref[0] = _mix_tail(x, gt_ref[0], bg, y, d_groups, poolw_ref, pscale_ref,
                          gco_ref, gpo_ref, wout_ref)

    cs_ref[0] = vbuf[CONV_PAD + tile - (CONV_K - 1):CONV_PAD + tile, :]
    ps_ref[0] = ubuf[POOL_PAD + tile - POOL_HIST:POOL_PAD + tile, :]
    vbuf[0:CONV_PAD, :] = vbuf[tile:tile + CONV_PAD, :]
    ubuf[0:POOL_PAD, :] = ubuf[tile:tile + POOL_PAD, :]


def _mix_prompt(x, sh, sc, gt, g, w_in, conv_w, pool_w, pool_scale, g_co, g_po, w_out, tile):
    b, seq, d = x.shape
    cw = conv_w.shape[-1]
    pw = pool_scale.shape[-1]
    row = lambda bi, li: (bi, 0, 0)
    return pl.pallas_call(
        functools.partial(_mix_prompt_kernel, tile=tile),
        grid=(b, seq // tile),
        in_specs=[
            pl.BlockSpec((1, tile, d), lambda bi, li: (bi, li, 0)),
            pl.BlockSpec((1, 1, d), row),
            pl.BlockSpec((1, 1, d), row),
            pl.BlockSpec((1, 1, d), row),
            _resident((1, d)),
            _resident(w_in.shape),
            _resident(conv_w.shape),
            _resident(pool_w.shape),
            _resident((1, pw)),
            _resident((1, cw)),
            _resident((1, pw)),
            _resident(w_out.shape),
        ],
        out_specs=[
            pl.BlockSpec((1, tile, d), lambda bi, li: (bi, li, 0)),
            pl.BlockSpec((1, CONV_K - 1, cw), row),
            pl.BlockSpec((1, POOL_HIST, pw), row),
        ],
        out_shape=[
            jax.ShapeDtypeStruct((b, seq, d), F32),
            jax.ShapeDtypeStruct((b, CONV_K - 1, cw), F32),
            jax.ShapeDtypeStruct((b, POOL_HIST, pw), F32),
        ],
        scratch_shapes=[
            pltpu.VMEM((CONV_PAD + tile, cw), F32),
            pltpu.VMEM((POOL_PAD + tile, pw), F32),
        ],
        compiler_params=_params("arbitrary", "arbitrary"),
        name="mix_prompt",
    )(x, sh, sc, gt, g.reshape(1, d), w_in, conv_w, pool_w, pool_scale.reshape(1, pw),
      g_co.reshape(1, cw), g_po.reshape(1, pw), w_out)


def _mix_sample_kernel(x_ref, sh_ref, sc_ref, gt_ref, g_ref, win_ref, convw_ref, poolw_ref,
                       pscale_ref, gco_ref, gpo_ref, wout_ref, cb_ref, pb_ref,
                       xo_ref, v_ref, u_ref):
    cw = convw_ref.shape[-1]
    pg = poolw_ref.shape[-1]
    x = x_ref[...]
    h = _mod_norm(x, g_ref[...], sc_ref[...], sh_ref[...]).astype(BF16)
    p = _dot(h, win_ref[...])
    bg = p[:, 0:cw]
    v = p[:, cw:2 * cw] * p[:, 2 * cw:3 * cw]
    u = p[:, 3 * cw:]
    v_ref[...] = v
    u_ref[...] = u

    w = convw_ref[...]
    y = w[CONV_K - 1:CONV_K] * v
    for k in range(1, CONV_K):
        y = y + w[CONV_K - 1 - k:CONV_K - k] * cb_ref[CONV_K - 1 - k]

    d_groups = []
    for g, win in enumerate(POOL_WINDOWS):
        ug = u[:, g * pg:(g + 1) * pg]
        acc = ug
        for k in range(1, win):
            acc = acc + pb_ref[POOL_HIST - k, :, g * pg:(g + 1) * pg]
        d_groups.append(acc / float(win) - ug)

    xo_ref[...] = _mix_tail(x, gt_ref[...], bg, y, d_groups, poolw_ref, pscale_ref,
                            gco_ref, gpo_ref, wout_ref)


def _mix_sample(x, sh, sc, gt, g, w_in, conv_w, pool_w, pool_scale, g_co, g_po, w_out, cb, pb):
    n, d = x.shape
    cw = conv_w.shape[-1]
    pw = pool_scale.shape[-1]
    return pl.pallas_call(
        _mix_sample_kernel,
        out_shape=[
            jax.ShapeDtypeStruct((n, d), F32),
            jax.ShapeDtypeStruct((n, cw), F32),
            jax.ShapeDtypeStruct((n, pw), F32),
        ],
        compiler_params=pltpu.CompilerParams(vmem_limit_bytes=VMEM_LIMIT_BYTES),
        name="mix_sample",
    )(x, sh, sc, gt, g.reshape(1, d), w_in, conv_w, pool_w, pool_scale.reshape(1, pw),
      g_co.reshape(1, cw), g_po.reshape(1, pw), w_out, cb, pb)


def _ffn_kernel(x_ref, sh_ref, sc_ref, gt_ref, g_ref, wg_ref, wu_ref, wd_ref, o_ref, *, chunks):
    x = x_ref[0]
    h = _mod_norm(x, g_ref[...], sc_ref[0], sh_ref[0]).astype(BF16)
    acc = None
    lo = 0
    for fc in chunks:
        a = _dot(h, wg_ref[:, lo:lo + fc])
        b = _dot(h, wu_ref[:, lo:lo + fc])
        part = _dot(_silu_mul(a, b).astype(BF16), wd_ref[lo:lo + fc, :])
        acc = part if acc is None else acc + part
        lo += fc
    o_ref[0] = x + gt_ref[0] * acc


def _ffn(x, sh, sc, gt, g, wg, wu, wd, tile, chunks):
    b, seq, d = x.shape
    tm = sh.shape[1]
    tmod = tile if tm > 1 else 1
    mod_map = (lambda bi, li: (bi, li, 0)) if tm > 1 else (lambda bi, li: (bi, 0, 0))
    return pl.pallas_call(
        functools.partial(_ffn_kernel, chunks=chunks),
        grid=(b, seq // tile),
        in_specs=[
            pl.BlockSpec((1, tile, d), lambda bi, li: (bi, li, 0)),
            pl.BlockSpec((1, tmod, d), mod_map),
            pl.BlockSpec((1, tmod, d), mod_map),
            pl.BlockSpec((1, tmod, d), mod_map),
            _resident((1, d)),
            _resident(wg.shape),
            _resident(wu.shape),
            _resident(wd.shape),
        ],
        out_specs=pl.BlockSpec((1, tile, d), lambda bi, li: (bi, li, 0)),
        out_shape=jax.ShapeDtypeStruct((b, seq, d), F32),
        compiler_params=_params("arbitrary", "arbitrary"),
        name="ffn",
    )(x, sh, sc, gt, g.reshape(1, d), wg, wu, wd)


def _split_bf16(a):
    hi = a.astype(BF16)
    return hi, (a - hi.astype(F32)).astype(BF16)


def _router_kernel(x_ref, sh_ref, sc_ref, g_ref, rw_ref, base_ref, h_ref, route_ref, cnt_ref,
                   tri, carry, *, n_steps):
    t = x_ref.shape[0]
    step = pl.program_id(0)

    @pl.when(step == 0)
    def _():
        r = lax.broadcasted_iota(I32, (t, t), 0)
        c = lax.broadcasted_iota(I32, (t, t), 1)
        tri[...] = jnp.where(c < r, 1.0, 0.0).astype(BF16)
        carry[...] = base_ref[...]

    @pl.when(step >= n_steps)
    def _():
        h_ref[...] = jnp.zeros_like(h_ref)

    @pl.when(step < n_steps)
    def _():
        h = _mod_norm(x_ref[...], g_ref[...], sc_ref[0], sh_ref[0])
        h_ref[...] = h
        h_hi, h_lo = _split_bf16(h)
        rw_hi, rw_lo = _split_bf16(rw_ref[...])
        logits = _dot(h_hi, rw_hi) + _dot(h_lo, rw_hi) + _dot(h_hi, rw_lo)

        ne = logits.shape[-1]
        idx = lax.broadcasted_iota(I32, logits.shape, 1)
        m1 = jnp.max(logits, axis=-1, keepdims=True)
        i1 = jnp.min(jnp.where(logits == m1, idx, ne), axis=-1, keepdims=True)
        sel1 = idx == i1
        rest = jnp.where(sel1, -jnp.inf, logits)
        m2 = jnp.max(rest, axis=-1, keepdims=True)
        i2 = jnp.min(jnp.where(rest == m2, idx, ne), axis=-1, keepdims=True)
        sel2 = idx == i2
        e = jnp.exp(m2 - m1)
        p1 = 1.0 / (1.0 + e)
        p2 = e / (1.0 + e)

        chosen = jnp.where(sel1, 1.0, jnp.where(sel2, 1.0, 0.0))
        before = _dot(tri[...], chosen.astype(BF16)) + carry[...]
        r1 = jnp.sum(jnp.where(sel1, before, 0.0), axis=-1, keepdims=True)
        r2 = jnp.sum(jnp.where(sel2, before, 0.0), axis=-1, keepdims=True)
        carry[...] += jnp.sum(chosen, axis=0, keepdims=True)
        cnt_ref[...] = carry[...]

        cols = (i1.astype(F32), i2.astype(F32), r1, r2, p1, p2)
        route = jnp.zeros(logits.shape, F32)
        for k, col in enumerate(cols):
            route = jnp.where(idx == k, col, route)
        route_ref[...] = route


def _router(x2d, sh, sc, g, router_w, base_cnt, tile, rows_per_mod, h_rows, h_prev=None,
            row_off=0):
    n, d = x2d.shape
    ne = router_w.shape[-1]
    assert ne == ROUTE_COLS and n % tile == 0 and row_off % tile == 0
    n_steps = n // tile
    tail = h_rows - (row_off + n)
    assert 0 <= tail < tile
    fill_tail = h_prev is None and tail > 0
    last = n_steps - 1
    clamp = (lambda i: jnp.minimum(i, last)) if fill_tail else (lambda i: i)
    per_row_mod = sh.shape[1] > 1
    if per_row_mod:
        mod_spec = pl.BlockSpec((1, tile, d), lambda i: (0, clamp(i), 0))
    else:
        assert rows_per_mod % tile == 0
        mod_spec = pl.BlockSpec((1, 1, d), lambda i: (clamp(i) // (rows_per_mod // tile), 0, 0))
    in_specs = [
        pl.BlockSpec((tile, d), lambda i: (clamp(i), 0)),
        mod_spec,
        mod_spec,
        _resident((1, d)),
        _resident(router_w.shape),
        _resident((1, ne)),
    ]
    args = [x2d, sh, sc, g.reshape(1, d), router_w, base_cnt]
    n_in = len(args)
    aliases = {}
    body = functools.partial(_router_kernel, n_steps=n_steps)
    if h_prev is not None:
        assert h_prev.shape == (h_rows, d)
        in_specs.append(pl.BlockSpec(memory_space=pl.ANY))
        args.append(h_prev)
        aliases = {n_in: 0}

        def body(*refs):
            _router_kernel(*refs[:n_in], *refs[n_in + 1:], n_steps=n_steps)

    blk_off = row_off // tile
    return pl.pallas_call(
        body,
        grid=(n_steps + int(fill_tail),),
        in_specs=in_specs,
        out_specs=[
            pl.BlockSpec((tile, d), lambda i: (i + blk_off, 0)),
            pl.BlockSpec((tile, ne), lambda i: (clamp(i), 0)),
            pl.BlockSpec((1, ne), lambda i: (0, 0)),
        ],
        out_shape=[
            jax.ShapeDtypeStruct((h_rows, d), F32),
            jax.ShapeDtypeStruct((n, ne), F32),
            jax.ShapeDtypeStruct((1, ne), F32),
        ],
        scratch_shapes=[pltpu.VMEM((tile, tile), BF16), pltpu.VMEM((1, ne), F32)],
        input_output_aliases=aliases,
        compiler_params=_params("arbitrary"),
        name="router",
    )(*args)


def _row_copy(src_hbm, row, dst, dst_row, sem):
    return pltpu.make_async_copy(src_hbm.at[pl.ds(row, 1), :], dst.at[pl.ds(dst_row, 1), :], sem)


def _row_copy_out(src, src_row, dst_hbm, row, sem):
    return pltpu.make_async_copy(src.at[pl.ds(src_row, 1), :], dst_hbm.at[pl.ds(row, 1), :], sem)


def _moe_grouped_kernel(te_ref, na_ref, src_ref, dst_ref, h_hbm, wg_ref, wu_ref, wd_ref, y_hbm,
                        xs, hbuf, obuf, gsem, ssem, fsem, *, tm, rows_per_step, fill_row,
                        fill_tiles):
    del te_ref
    i = pl.program_id(0)
    c = pl.program_id(1)
    n_active = na_ref[0]
    slot = i % 2
    other = (i + 1) % 2

    def send(tile, r):
        return _row_copy_out(obuf.at[(tile + 2) % 2], r, y_hbm, dst_ref[(tile + 1) * tm + r],
                             ssem.at[(tile + 2) % 2])

    @pl.when(jnp.logical_and(i == 0, c == 0))
    def _():
        obuf[1] = jnp.zeros((tm, obuf.shape[-1]), F32)
        fills = [pltpu.make_async_copy(obuf.at[1], y_hbm.at[pl.ds(fill_row + q * tm, tm), :], fsem)
                 for q in range(fill_tiles)]
        for f in fills:
            f.start()
        for f in fills:
            f.wait()

        def body(r, carry):
            _row_copy(h_hbm, src_ref[r], xs.at[0], r, gsem.at[0]).start()
            return carry

        lax.fori_loop(0, tm, body, 0)

    @pl.when(jnp.logical_and(i <= n_active, c == 0))
    def _():
        pltpu.make_async_copy(h_hbm.at[pl.ds(0, tm), :], xs.at[slot], gsem.at[slot]).wait()

    @pl.when(jnp.logical_and(jnp.logical_and(i >= 1, i <= n_active + 1), c == 0))
    def _():
        pltpu.make_async_copy(obuf.at[slot], y_hbm.at[pl.ds(0, tm), :], ssem.at[slot]).wait()

    @pl.when(jnp.logical_and(i < n_active, c == 0))
    def _():
        obuf[slot] = jnp.zeros((tm, obuf.shape[-1]), F32)
        hbuf[...] = xs[slot].astype(BF16)

    @pl.when(jnp.logical_and(i == n_active, c == 0))
    def _():
        def body(r, carry):
            send(i - 1, r).start()
            return carry

        lax.fori_loop(0, tm, body, 0)

    @pl.when(i < n_active)
    def _():
        for k in range(rows_per_step):
            r = c * rows_per_step + k
            _row_copy(h_hbm, src_ref[(i + 1) * tm + r], xs.at[other], r, gsem.at[other]).start()
            send(i - 1, r).start()

        h = hbuf[...]
        a = _dot(h, wg_ref[0].astype(BF16))
        b = _dot(h, wu_ref[0].astype(BF16))
        obuf[slot] += _dot(_silu_mul(a, b).astype(BF16), wd_ref[0].astype(BF16))


def _moe_grouped(h_all, tile_expert, n_active, src, dst, wg, wu, wd, tm, fc, y_rows, fill_row,
                 fill_tiles):
    ne, d, dff = wg.shape
    n_tiles = tile_expert.shape[0]
    nc = dff // fc
    assert dff % fc == 0 and tm % nc == 0
    assert src.shape[0] == n_tiles * tm and dst.shape[0] == n_tiles * tm
    rows_per_step = tm // nc

    def w_col(i, c, te, na, src_, dst_):
        return (te[i], 0, jnp.where(i < na[0], c, nc - 1))

    def w_row(i, c, te, na, src_, dst_):
        return (te[i], jnp.where(i < na[0], c, nc - 1), 0)

    grid_spec = pltpu.PrefetchScalarGridSpec(
        num_scalar_prefetch=4,
        grid=(n_tiles, nc),
        in_specs=[
            pl.BlockSpec(memory_space=pl.ANY),
            pl.BlockSpec((1, d, fc), w_col),
            pl.BlockSpec((1, d, fc), w_col),
            pl.BlockSpec((1, fc, d), w_row),
        ],
        out_specs=pl.BlockSpec(memory_space=pl.ANY),
        scratch_shapes=[
            pltpu.VMEM((2, tm, d), F32),
            pltpu.VMEM((tm, d), BF16),
            pltpu.VMEM((2, tm, d), F32),
            pltpu.SemaphoreType.DMA((2,)),
            pltpu.SemaphoreType.DMA((2,)),
            pltpu.SemaphoreType.DMA,
        ],
    )
    return pl.pallas_call(
        functools.partial(_moe_grouped_kernel, tm=tm, rows_per_step=rows_per_step,
                          fill_row=fill_row, fill_tiles=fill_tiles),
        grid_spec=grid_spec,
        out_shape=jax.ShapeDtypeStruct((y_rows, d), F32),
        compiler_params=_params("arbitrary", "arbitrary"),
        name="moe_grouped",
    )(tile_expert, n_active, src, dst, h_all, wg, wu, wd)


def _combine_kernel(x_ref, gt_ref, route_ref, gf_ref, y0_ref, y1_ref, o_ref):
    route = route_ref[...]
    f = (route[:, COL_PROB:COL_PROB + 1] * y0_ref[...]
         + route[:, COL_PROB + 1:COL_PROB + 2] * y1_ref[...])
    o_ref[...] = _rms(x_ref[...] + gt_ref[0] * f, gf_ref[...])


def _combine(x2d, gt, route, g_final, y_tok, tile, rows_per_mod, tok_off, plane_rows):
    n, d = x2d.shape
    assert n % tile == 0 and tok_off % tile == 0 and plane_rows % tile == 0
    per_row_mod = gt.shape[1] > 1
    if per_row_mod:
        mod_spec = pl.BlockSpec((1, tile, d), lambda j: (0, j, 0))
    else:
        assert rows_per_mod % tile == 0
        mod_spec = pl.BlockSpec((1, 1, d), lambda j: (j // (rows_per_mod // tile), 0, 0))
    first = tok_off // tile
    second = (plane_rows + tok_off) // tile
    return pl.pallas_call(
        _combine_kernel,
        grid=(n // tile,),
        in_specs=[
            pl.BlockSpec((tile, d), lambda j: (j, 0)),
            mod_spec,
            pl.BlockSpec((tile, ROUTE_COLS), lambda j: (j, 0)),
            pl.BlockSpec((1, d), lambda j: (0, 0)),
            pl.BlockSpec((tile, d), lambda j: (first + j, 0)),
            pl.BlockSpec((tile, d), lambda j: (second + j, 0)),
        ],
        out_specs=pl.BlockSpec((tile, d), lambda j: (j, 0)),
        out_shape=jax.ShapeDtypeStruct((n, d), F32),
        compiler_params=_params("arbitrary"),
        name="combine",
    )(x2d, gt, route, g_final.reshape(1, d), y_tok, y_tok)


def _routing_tables(route_all, counts, tm, n_tiles, plane_rows):
    ne = counts.shape[0]
    n = route_all.shape[0]
    e_idx = route_all[:, COL_EXPERT:COL_EXPERT + TOP_K].astype(I32)
    rank = route_all[:, COL_RANK:COL_RANK + TOP_K].astype(I32)
    padded = (counts + tm - 1) // tm * tm
    ends = jnp.cumsum(padded)
    offs = ends - padded
    onehot = e_idx[..., None] == jnp.arange(ne, dtype=I32)
    pos = jnp.sum(jnp.where(onehot, offs, 0), axis=-1) + rank
    n_active = ends[-1] // tm
    starts = jnp.arange(n_tiles, dtype=I32) * tm
    tile_expert = jnp.sum(starts[:, None] >= ends[None, :], axis=1).astype(I32)
    last_expert = jnp.sum((n_active - 1) * tm >= ends).astype(I32)
    tile_expert = jnp.minimum(tile_expert, last_expert)
    n_rows = n_tiles * tm
    inv = jnp.zeros((n_rows,), I32).at[pos.reshape(-1)].set(jnp.arange(n * TOP_K, dtype=I32))
    r = jnp.arange(n_rows, dtype=I32)[:, None]
    real_before = jnp.sum(jnp.clip(r - offs, 0, counts), axis=1)
    is_real = jnp.any(jnp.logical_and(r >= offs, r < offs + counts), axis=1)
    tok = inv // TOP_K
    choice = inv % TOP_K
    spare = n + jnp.minimum(r[:, 0] - real_before, ne * tm - 1)
    src = jnp.where(is_real, tok, 0)
    dst = jnp.where(is_real, choice * plane_rows + tok, spare)
    lead = n + ne * tm + jnp.arange(tm, dtype=I32)
    dst = jnp.concatenate([lead, dst])[:n_rows]
    return src, dst, tile_expert, n_active.astype(I32).reshape(1)


def _moe_layer(xp, xs, mod_p, mod_s, g, router_w, wg, wu, wd, g_final):
    nb, seq, d = xp.shape
    ns = xs.shape[0]
    ne = router_w.shape[-1]
    n_all = nb * seq + ns
    sh_p, sc_p, gt_p = mod_p
    sh_s, sc_s, gt_s = mod_s
    xp2 = xp.reshape(nb * seq, d)

    zero_cnt = jnp.zeros((1, ne), F32)
    h_all, route_p, cnt_p = _router(xp2, sh_p, sc_p, g, router_w, zero_cnt, ROUTER_TILE, seq,
                                    n_all)
    h_all, route_s, cnt_s = _router(xs, sh_s, sc_s, g, router_w, cnt_p, ns, ns, n_all,
                                    h_prev=h_all, row_off=nb * seq)

    tm = MOE_ROW_TILE
    n_tiles = (n_all * TOP_K) // tm + ne + 2
    fill_tiles = ne + 1
    plane_rows = n_all + fill_tiles * tm
    route_all = jnp.concatenate([route_p, route_s], axis=0)
    src, dst, tile_expert, n_active = _routing_tables(
        route_all, cnt_s[0].astype(I32), tm, n_tiles, plane_rows)

    y_tok = _moe_grouped(h_all, tile_expert, n_active, src, dst, wg, wu, wd, tm, MOE_FF_CHUNK,
                         plane_rows + n_all, n_all, fill_tiles)
    yp = _combine(xp2, gt_p, route_p, g_final, y_tok, COMBINE_TILE, seq, 0, plane_rows)
    ys = _combine(xs, gt_s, route_s, g_final, y_tok, ns, ns, nb * seq, plane_rows)
    return yp.reshape(nb, seq, d), ys


def kernel(x_prompt, x_sample, c_prompt, c_sample, state_conv, state_pool, w_ada, b_ada, g_mix,
           g_ffn, w_in, conv_w, pool_w, pool_scale, g_conv_out, g_pool_out, w_out, dense_w_gate,
           dense_w_up, dense_w_down, router_w, moe_w_gate, moe_w_up, moe_w_down, g_final):
    depth = w_ada.shape[0]
    nb, seq, d = x_prompt.shape
    ns = x_sample.shape[0]
    assert x_sample.shape[1] == 1 and depth == 2
    assert sum(DENSE_FF_CHUNKS) == dense_w_gate.shape[-1]

    mod = _ada(jnp.concatenate([c_prompt, c_sample], axis=0), w_ada, b_ada)
    mod = mod.reshape(depth, nb + ns, 6, d)
    mod_p = [jnp.transpose(mod[i, :nb], (1, 0, 2))[:, :, None, :] for i in range(depth)]
    mod_s = [jnp.transpose(mod[i, nb:], (1, 0, 2))[:, None, :, :] for i in range(depth)]

    w_in_b = w_in.astype(BF16)
    w_out_b = w_out.astype(BF16)
    pool_w_b = pool_w.astype(BF16)
    dense_b = [w.astype(BF16) for w in (dense_w_gate, dense_w_up, dense_w_down)]
    cb_t = jnp.transpose(state_conv, (0, 2, 1, 3))
    pb_t = jnp.transpose(state_pool, (0, 2, 1, 3))

    xp = x_prompt
    xs = x_sample.reshape(ns, d)
    conv_p, pool_p, conv_s, pool_s = [], [], [], []
    for i in range(depth):
        sh1, sc1, gt1, sh2, sc2, gt2 = mod_p[i]
        mix_w = (g_mix[i], w_in_b[i], conv_w[i], pool_w_b[i], pool_scale[i], g_conv_out[i],
                 g_pool_out[i], w_out_b[i])
        xp, cs, ps = _mix_prompt(xp, sh1, sc1, gt1, *mix_w, tile=PROMPT_TILE)
        conv_p.append(cs)
        pool_p.append(ps)
        s1, c1, t1, s2, c2, t2 = mod_s[i]
        xs, v_new, u_new = _mix_sample(xs, s1[0], c1[0], t1[0], *mix_w, cb_t[i], pb_t[i])
        conv_s.append(jnp.concatenate([state_conv[i][:, 1:], v_new[:, None, :]], axis=1))
        pool_s.append(jnp.concatenate([state_pool[i][:, 1:], u_new[:, None, :]], axis=1))
        j = i // 2
        if i % 2 == 0:
            wg, wu, wd = (w[j] for w in dense_b)
            xp = _ffn(xp, sh2, sc2, gt2, g_ffn[i], wg, wu, wd, PROMPT_TILE, DENSE_FF_CHUNKS)
            xs = _ffn(xs[None], s2, c2, t2, g_ffn[i], wg, wu, wd, ns, DENSE_FF_CHUNKS)[0]
        else:
            xp, xs = _moe_layer(xp, xs, (sh2, sc2, gt2), (s2, c2, t2), g_ffn[i], router_w[j],
                                moe_w_gate[j].astype(BF16), moe_w_up[j].astype(BF16),
                                moe_w_down[j].astype(BF16), g_final)

    return (xp, xs.reshape(ns, 1, d), jnp.stack(conv_p), jnp.stack(pool_p),
            jnp.stack(conv_s), jnp.stack(pool_s))
```

```python
import functools

import jax
import jax.numpy as jnp
from jax import lax
from jax.experimental import pallas as pl
from jax.experimental.pallas import tpu as pltpu

F32 = jnp.float32
BF16 = jnp.bfloat16
I32 = jnp.int32

EPS = 1e-6
CONV_K = 3
POOL_WINDOWS = (2, 4, 8, 16)
POOL_HIST = max(POOL_WINDOWS) - 1
TOP_K = 2

CONV_PAD = 8
POOL_PAD = 16

ROUTE_COLS = 8
COL_EXPERT, COL_RANK, COL_PROB = 0, 2, 4

VMEM_LIMIT_BYTES = 56 * 1024 * 1024

PROMPT_TILE = 512
ROUTER_TILE = 1024
COMBINE_TILE = 512
MOE_ROW_TILE = 896
MOE_FF_CHUNK = 512
ROW_SLOTS = 3
DENSE_FF_CHUNKS = (768, 768, 768, 512)


def _params(*sem):
    return pltpu.CompilerParams(dimension_semantics=sem, vmem_limit_bytes=VMEM_LIMIT_BYTES)


def _resident(shape):
    nd = len(shape)
    return pl.BlockSpec(shape, lambda *_: (0,) * nd, pipeline_mode=pl.Buffered(1))


def _rms(x, g):
    ms = jnp.mean(x * x, axis=-1, keepdims=True)
    return x * lax.rsqrt(ms + EPS) * g


def _mod_norm(x, g, sc, sh):
    ms = jnp.mean(x * x, axis=-1, keepdims=True)
    return x * lax.rsqrt(ms + EPS) * (g * (1.0 + sc)) + sh


def _dot(a, b):
    return jnp.dot(a, b, preferred_element_type=F32)


def _silu_mul(a, b):
    return a * jax.nn.sigmoid(a) * b


def _ada_kernel(c_ref, w_ref, b_ref, o_ref):
    c = c_ref[...]
    a = (c * jax.nn.sigmoid(c)).astype(BF16)
    o_ref[0] = _dot(a, w_ref[0].astype(BF16)) + b_ref[0]


def _ada(c_all, w_ada, b_ada):
    depth, d, n = w_ada.shape
    m = c_all.shape[0]
    tn = 1024
    return pl.pallas_call(
        _ada_kernel,
        grid=(depth, n // tn),
        in_specs=[
            pl.BlockSpec((m, d), lambda i, j: (0, 0)),
            pl.BlockSpec((1, d, tn), lambda i, j: (i, 0, j)),
            pl.BlockSpec((1, 1, tn), lambda i, j: (i, 0, j)),
        ],
        out_specs=pl.BlockSpec((1, m, tn), lambda i, j: (i, 0, j)),
        out_shape=jax.ShapeDtypeStruct((depth, m, n), F32),
        compiler_params=_params("arbitrary", "arbitrary"),
        name="ada",
    )(c_all, w_ada, b_ada.reshape(depth, 1, n))


def _mix_tail(x, gt, bg, y, d_groups, poolw_ref, pscale_ref, gco_ref, gpo_ref, wout_ref):
    cw = gco_ref.shape[-1]
    ya = bg * y
    yb = jnp.concatenate(
        [_dot(d.astype(BF16), poolw_ref[g]) for g, d in enumerate(d_groups)], axis=-1
    ) * pscale_ref[...]
    ma = _rms(ya, gco_ref[...]).astype(BF16)
    mb = _rms(yb, gpo_ref[...]).astype(BF16)
    o = _dot(ma, wout_ref[0:cw, :]) + _dot(mb, wout_ref[cw:, :])
    return x + gt * o


def _mix_prompt_kernel(x_ref, sh_ref, sc_ref, gt_ref, g_ref, win_ref, convw_ref, poolw_ref,
                       pscale_ref, gco_ref, gpo_ref, wout_ref,
                       xo_ref, cs_ref, ps_ref, vbuf, ubuf, *, tile):
    l = pl.program_id(1)
    cw = convw_ref.shape[-1]
    pg = poolw_ref.shape[-1]

    @pl.when(l == 0)
    def _():
        vbuf[0:CONV_PAD, :] = jnp.zeros((CONV_PAD, cw), F32)
        ubuf[0:POOL_PAD, :] = jnp.zeros((POOL_PAD, ubuf.shape[-1]), F32)

    x = x_ref[0]
    h = _mod_norm(x, g_ref[...], sc_ref[0], sh_ref[0]).astype(BF16)
    p = _dot(h, win_ref[...])
    bg = p[:, 0:cw]
    v = p[:, cw:2 * cw] * p[:, 2 * cw:3 * cw]
    u = p[:, 3 * cw:]
    vbuf[CONV_PAD:CONV_PAD + tile, :] = v
    ubuf[POOL_PAD:POOL_PAD + tile, :] = u

    w = convw_ref[...]
    y = w[CONV_K - 1:CONV_K] * v
    for k in range(1, CONV_K):
        y = y + w[CONV_K - 1 - k:CONV_K - k] * vbuf[CONV_PAD - k:CONV_PAD - k + tile, :]

    pos = l * tile + lax.broadcasted_iota(I32, (tile, 1), 0)
    d_groups = []
    for g, win in enumerate(POOL_WINDOWS):
        ug = u[:, g * pg:(g + 1) * pg]
        acc = ug
        for k in range(1, win):
            acc = acc + ubuf[POOL_PAD - k:POOL_PAD - k + tile, g * pg:(g + 1) * pg]
        cnt = jnp.minimum(pos + 1, win).astype(F32)
        d_groups.append(acc / cnt - ug)

    xo_ref[0] = _mix_tail(x, gt_ref[0], bg, y, d_groups, poolw_ref, pscale_ref,
                          gco_ref, gpo_ref, wout_ref)

    cs_ref[0] = vbuf[CONV_PAD + tile - (CONV_K - 1):CONV_PAD + tile, :]
    ps_ref[0] = ubuf[POOL_PAD + tile - POOL_HIST:POOL_PAD + tile, :]
    vbuf[0:CONV_PAD, :] = vbuf[tile:tile + CONV_PAD, :]
    ubuf[0:POOL_PAD, :] = ubuf[tile:tile + POOL_PAD, :]


def _mix_prompt(x, sh, sc, gt, g, w_in, conv_w, pool_w, pool_scale, g_co, g_po, w_out, tile):
    b, seq, d = x.shape
    cw = conv_w.shape[-1]
    pw = pool_scale.shape[-1]
    row = lambda bi, li: (bi, 0, 0)
    return pl.pallas_call(
        functools.partial(_mix_prompt_kernel, tile=tile),
        grid=(b, seq // tile),
        in_specs=[
            pl.BlockSpec((1, tile, d), lambda bi, li: (bi, li, 0)),
            pl.BlockSpec((1, 1, d), row),
            pl.BlockSpec((1, 1, d), row),
            pl.BlockSpec((1, 1, d), row),
            _resident((1, d)),
            _resident(w_in.shape),
            _resident(conv_w.shape),
            _resident(pool_w.shape),
            _resident((1, pw)),
            _resident((1, cw)),
            _resident((1, pw)),
            _resident(w_out.shape),
        ],
        out_specs=[
            pl.BlockSpec((1, tile, d), lambda bi, li: (bi, li, 0)),
            pl.BlockSpec((1, CONV_K - 1, cw), row),
            pl.BlockSpec((1, POOL_HIST, pw), row),
        ],
        out_shape=[
            jax.ShapeDtypeStruct((b, seq, d), F32),
            jax.ShapeDtypeStruct((b, CONV_K - 1, cw), F32),
            jax.ShapeDtypeStruct((b, POOL_HIST, pw), F32),
        ],
        scratch_shapes=[
            pltpu.VMEM((CONV_PAD + tile, cw), F32),
            pltpu.VMEM((POOL_PAD + tile, pw), F32),
        ],
        compiler_params=_params("arbitrary", "arbitrary"),
        name="mix_prompt",
    )(x, sh, sc, gt, g.reshape(1, d), w_in, conv_w, pool_w, pool_scale.reshape(1, pw),
      g_co.reshape(1, cw), g_po.reshape(1, pw), w_out)


def _mix_sample_kernel(x_ref, sh_ref, sc_ref, gt_ref, g_ref, win_ref, convw_ref, poolw_ref,
                       pscale_ref, gco_ref, gpo_ref, wout_ref, cb_ref, pb_ref,
                       xo_ref, v_ref, u_ref):
    cw = convw_ref.shape[-1]
    pg = poolw_ref.shape[-1]
    x = x_ref[...]
    h = _mod_norm(x, g_ref[...], sc_ref[...], sh_ref[...]).astype(BF16)
    p = _dot(h, win_ref[...])
    bg = p[:, 0:cw]
    v = p[:, cw:2 * cw] * p[:, 2 * cw:3 * cw]
    u = p[:, 3 * cw:]
    v_ref[...] = v
    u_ref[...] = u

    w = convw_ref[...]
    y = w[CONV_K - 1:CONV_K] * v
    for k in range(1, CONV_K):
        y = y + w[CONV_K - 1 - k:CONV_K - k] * cb_ref[CONV_K - 1 - k]

    d_groups = []
    for g, win in enumerate(POOL_WINDOWS):
        ug = u[:, g * pg:(g + 1) * pg]
        acc = ug
        for k in range(1, win):
            acc = acc + pb_ref[POOL_HIST - k, :, g * pg:(g + 1) * pg]
        d_groups.append(acc / float(win) - ug)

    xo_ref[...] = _mix_tail(x, gt_ref[...], bg, y, d_groups, poolw_ref, pscale_ref,
                            gco_ref, gpo_ref, wout_ref)


def _mix_sample(x, sh, sc, gt, g, w_in, conv_w, pool_w, pool_scale, g_co, g_po, w_out, cb, pb):
    n, d = x.shape
    cw = conv_w.shape[-1]
    pw = pool_scale.shape[-1]
    return pl.pallas_call(
        _mix_sample_kernel,
        out_shape=[
            jax.ShapeDtypeStruct((n, d), F32),
            jax.ShapeDtypeStruct((n, cw), F32),
            jax.ShapeDtypeStruct((n, pw), F32),
        ],
        compiler_params=pltpu.CompilerParams(vmem_limit_bytes=VMEM_LIMIT_BYTES),
        name="mix_sample",
    )(x, sh, sc, gt, g.reshape(1, d), w_in, conv_w, pool_w, pool_scale.reshape(1, pw),
      g_co.reshape(1, cw), g_po.reshape(1, pw), w_out, cb, pb)


def _ffn_kernel(x_ref, sh_ref, sc_ref, gt_ref, g_ref, wg_ref, wu_ref, wd_ref, o_ref, *, chunks):
    x = x_ref[0]
    h = _mod_norm(x, g_ref[...], sc_ref[0], sh_ref[0]).astype(BF16)
    acc = None
    lo = 0
    for fc in chunks:
        a = _dot(h, wg_ref[:, lo:lo + fc])
        b = _dot(h, wu_ref[:, lo:lo + fc])
        part = _dot(_silu_mul(a, b).astype(BF16), wd_ref[lo:lo + fc, :])
        acc = part if acc is None else acc + part
        lo += fc
    o_ref[0] = x + gt_ref[0] * acc


def _ffn(x, sh, sc, gt, g, wg, wu, wd, tile, chunks):
    b, seq, d = x.shape
    tm = sh.shape[1]
    tmod = tile if tm > 1 else 1
    mod_map = (lambda bi, li: (bi, li, 0)) if tm > 1 else (lambda bi, li: (bi, 0, 0))
    return pl.pallas_call(
        functools.partial(_ffn_kernel, chunks=chunks),
        grid=(b, seq // tile),
        in_specs=[
            pl.BlockSpec((1, tile, d), lambda bi, li: (bi, li, 0)),
            pl.BlockSpec((1, tmod, d), mod_map),
            pl.BlockSpec((1, tmod, d), mod_map),
            pl.BlockSpec((1, tmod, d), mod_map),
            _resident((1, d)),
            _resident(wg.shape),
            _resident(wu.shape),
            _resident(wd.shape),
        ],
        out_specs=pl.BlockSpec((1, tile, d), lambda bi, li: (bi, li, 0)),
        out_shape=jax.ShapeDtypeStruct((b, seq, d), F32),
        compiler_params=_params("arbitrary", "arbitrary"),
        name="ffn",
    )(x, sh, sc, gt, g.reshape(1, d), wg, wu, wd)


def _split_bf16(a):
    hi = a.astype(BF16)
    return hi, (a - hi.astype(F32)).astype(BF16)


def _router_kernel(x_ref, sh_ref, sc_ref, g_ref, rw_ref, base_ref, h_ref, route_ref, cnt_ref,
                   tri, carry, *, n_steps):
    t = x_ref.shape[0]
    step = pl.program_id(0)

    @pl.when(step == 0)
    def _():
        r = lax.broadcasted_iota(I32, (t, t), 0)
        c = lax.broadcasted_iota(I32, (t, t), 1)
        tri[...] = jnp.where(c < r, 1.0, 0.0).astype(BF16)
        carry[...] = base_ref[...]

    @pl.when(step >= n_steps)
    def _():
        h_ref[...] = jnp.zeros_like(h_ref)

    @pl.when(step < n_steps)
    def _():
        h = _mod_norm(x_ref[...], g_ref[...], sc_ref[0], sh_ref[0])
        h_ref[...] = h
        h_hi, h_lo = _split_bf16(h)
        rw_hi, rw_lo = _split_bf16(rw_ref[...])
        logits = _dot(h_hi, rw_hi) + _dot(h_lo, rw_hi) + _dot(h_hi, rw_lo)

        ne = logits.shape[-1]
        idx = lax.broadcasted_iota(I32, logits.shape, 1)
        m1 = jnp.max(logits, axis=-1, keepdims=True)
        i1 = jnp.min(jnp.where(logits == m1, idx, ne), axis=-1, keepdims=True)
        sel1 = idx == i1
        rest = jnp.where(sel1, -jnp.inf, logits)
        m2 = jnp.max(rest, axis=-1, keepdims=True)
        i2 = jnp.min(jnp.where(rest == m2, idx, ne), axis=-1, keepdims=True)
        sel2 = idx == i2
        e = jnp.exp(m2 - m1)
        p1 = 1.0 / (1.0 + e)
        p2 = e / (1.0 + e)

        chosen = jnp.where(sel1, 1.0, jnp.where(sel2, 1.0, 0.0))
        before = _dot(tri[...], chosen.astype(BF16)) + carry[...]
        r1 = jnp.sum(jnp.where(sel1, before, 0.0), axis=-1, keepdims=True)
        r2 = jnp.sum(jnp.where(sel2, before, 0.0), axis=-1, keepdims=True)
        carry[...] += jnp.sum(chosen, axis=0, keepdims=True)
        cnt_ref[...] = carry[...]

        cols = (i1.astype(F32), i2.astype(F32), r1, r2, p1, p2)
        route = jnp.zeros(logits.shape, F32)
        for k, col in enumerate(cols):
            route = jnp.where(idx == k, col, route)
        route_ref[...] = route


def _router(x2d, sh, sc, g, router_w, base_cnt, tile, rows_per_mod, h_rows, h_prev=None,
            row_off=0):
    n, d = x2d.shape
    ne = router_w.shape[-1]
    assert ne == ROUTE_COLS and n % tile == 0 and row_off % tile == 0
    n_steps = n // tile
    tail = h_rows - (row_off + n)
    assert 0 <= tail < tile
    fill_tail = h_prev is None and tail > 0
    last = n_steps - 1
    clamp = (lambda i: jnp.minimum(i, last)) if fill_tail else (lambda i: i)
    per_row_mod = sh.shape[1] > 1
    if per_row_mod:
        mod_spec = pl.BlockSpec((1, tile, d), lambda i: (0, clamp(i), 0))
    else:
        assert rows_per_mod % tile == 0
        mod_spec = pl.BlockSpec((1, 1, d), lambda i: (clamp(i) // (rows_per_mod // tile), 0, 0))
    in_specs = [
        pl.BlockSpec((tile, d), lambda i: (clamp(i), 0)),
        mod_spec,
        mod_spec,
        _resident((1, d)),
        _resident(router_w.shape),
        _resident((1, ne)),
    ]
    args = [x2d, sh, sc, g.reshape(1, d), router_w, base_cnt]
    n_in = len(args)
    aliases = {}
    body = functools.partial(_router_kernel, n_steps=n_steps)
    if h_prev is not None:
        assert h_prev.shape == (h_rows, d)
        in_specs.append(pl.BlockSpec(memory_space=pl.ANY))
        args.append(h_prev)
        aliases = {n_in: 0}

        def body(*refs):
            _router_kernel(*refs[:n_in], *refs[n_in + 1:], n_steps=n_steps)

    blk_off = row_off // tile
    return pl.pallas_call(
        body,
        grid=(n_steps + int(fill_tail),),
        in_specs=in_specs,
        out_specs=[
            pl.BlockSpec((tile, d), lambda i: (i + blk_off, 0)),
            pl.BlockSpec((tile, ne), lambda i: (clamp(i), 0)),
            pl.BlockSpec((1, ne), lambda i: (0, 0)),
        ],
        out_shape=[
            jax.ShapeDtypeStruct((h_rows, d), F32),
            jax.ShapeDtypeStruct((n, ne), F32),
            jax.ShapeDtypeStruct((1, ne), F32),
        ],
        scratch_shapes=[pltpu.VMEM((tile, tile), BF16), pltpu.VMEM((1, ne), F32)],
        input_output_aliases=aliases,
        compiler_params=_params("arbitrary"),
        name="router",
    )(*args)


def _row_copy(src_hbm, row, dst, dst_row, sem):
    return pltpu.make_async_copy(src_hbm.at[pl.ds(row, 1), :], dst.at[pl.ds(dst_row, 1), :], sem)


def _row_copy_out(src, src_row, dst_hbm, row, sem):
    return pltpu.make_async_copy(src.at[pl.ds(src_row, 1), :], dst_hbm.at[pl.ds(row, 1), :], sem)


def _moe_grouped_kernel(te_ref, na_ref, src_ref, dst_ref, h_hbm, wg_ref, wu_ref, wd_ref, y_hbm,
                        xs, hbuf, obuf, gsem, ssem, fsem, *, tm, rows_per_step, fill_row,
                        fill_tiles):
    del te_ref
    i = pl.program_id(0)
    c = pl.program_id(1)
    n_active = na_ref[0]
    ahead = ROW_SLOTS - 1
    slot = i % ROW_SLOTS

    def request(tile, r):
        s = tile % ROW_SLOTS
        return _row_copy(h_hbm, src_ref[tile * tm + r], xs.at[s], r, gsem.at[s])

    def send(tile, r):
        s = (tile + ROW_SLOTS) % ROW_SLOTS
        return _row_copy_out(obuf.at[s], r, y_hbm, dst_ref[(tile + 1) * tm + r], ssem.at[s])

    @pl.when(jnp.logical_and(i == 0, c == 0))
    def _():
        last = ROW_SLOTS - 1
        obuf[last] = jnp.zeros((tm, obuf.shape[-1]), F32)
        fills = [pltpu.make_async_copy(obuf.at[last], y_hbm.at[pl.ds(fill_row + q * tm, tm), :],
                                       fsem) for q in range(fill_tiles)]
        for f in fills:
            f.start()
        for f in fills:
            f.wait()

        def body(r, carry):
            for t in range(ahead):
                request(t, r).start()
            return carry

        lax.fori_loop(0, tm, body, 0)

    @pl.when(jnp.logical_and(i < n_active + ahead, c == 0))
    def _():
        pltpu.make_async_copy(h_hbm.at[pl.ds(0, tm), :], xs.at[slot], gsem.at[slot]).wait()

    @pl.when(jnp.logical_and(jnp.logical_and(i >= ahead, i < n_active + ROW_SLOTS), c == 0))
    def _():
        pltpu.make_async_copy(obuf.at[slot], y_hbm.at[pl.ds(0, tm), :], ssem.at[slot]).wait()

    @pl.when(jnp.logical_and(i < n_active, c == 0))
    def _():
        obuf[slot] = jnp.zeros((tm, obuf.shape[-1]), F32)
        hbuf[...] = xs[slot].astype(BF16)

    @pl.when(jnp.logical_and(i == n_active, c == 0))
    def _():
        def body(r, carry):
            send(i - 1, r).start()
            return carry

        lax.fori_loop(0, tm, body, 0)

    @pl.when(i < n_active)
    def _():
        for k in range(rows_per_step):
            r = c * rows_per_step + k
            request(i + ahead, r).start()
            send(i - 1, r).start()

        h = hbuf[...]
        a = _dot(h, wg_ref[0].astype(BF16))
        b = _dot(h, wu_ref[0].astype(BF16))
        obuf[slot] += _dot(_silu_mul(a, b).astype(BF16), wd_ref[0].astype(BF16))


def _moe_grouped(h_all, tile_expert, n_active, src, dst, wg, wu, wd, tm, fc, y_rows, fill_row,
                 fill_tiles):
    ne, d, dff = wg.shape
    n_tiles = tile_expert.shape[0]
    nc = dff // fc
    assert dff % fc == 0 and tm % nc == 0
    assert src.shape[0] == n_tiles * tm and dst.shape[0] == n_tiles * tm
    rows_per_step = tm // nc

    def w_col(i, c, te, na, src_, dst_):
        return (te[i], 0, jnp.where(i < na[0], c, nc - 1))

    def w_row(i, c, te, na, src_, dst_):
        return (te[i], jnp.where(i < na[0], c, nc - 1), 0)

    grid_spec = pltpu.PrefetchScalarGridSpec(
        num_scalar_prefetch=4,
        grid=(n_tiles, nc),
        in_specs=[
            pl.BlockSpec(memory_space=pl.ANY),
            pl.BlockSpec((1, d, fc), w_col),
            pl.BlockSpec((1, d, fc), w_col),
            pl.BlockSpec((1, fc, d), w_row),
        ],
        out_specs=pl.BlockSpec(memory_space=pl.ANY),
        scratch_shapes=[
            pltpu.VMEM((ROW_SLOTS, tm, d), F32),
            pltpu.VMEM((tm, d), BF16),
            pltpu.VMEM((ROW_SLOTS, tm, d), F32),
            pltpu.SemaphoreType.DMA((ROW_SLOTS,)),
            pltpu.SemaphoreType.DMA((ROW_SLOTS,)),
            pltpu.SemaphoreType.DMA,
        ],
    )
    return pl.pallas_call(
        functools.partial(_moe_grouped_kernel, tm=tm, rows_per_step=rows_per_step,
                          fill_row=fill_row, fill_tiles=fill_tiles),
        grid_spec=grid_spec,
        out_shape=jax.ShapeDtypeStruct((y_rows, d), F32),
        compiler_params=_params("arbitrary", "arbitrary"),
        name="moe_grouped",
    )(tile_expert, n_active, src, dst, h_all, wg, wu, wd)


def _combine_kernel(x_ref, gt_ref, route_ref, gf_ref, y0_ref, y1_ref, o_ref):
    route = route_ref[...]
    f = (route[:, COL_PROB:COL_PROB + 1] * y0_ref[...]
         + route[:, COL_PROB + 1:COL_PROB + 2] * y1_ref[...])
    o_ref[...] = _rms(x_ref[...] + gt_ref[0] * f, gf_ref[...])


def _combine(x2d, gt, route, g_final, y_tok, tile, rows_per_mod, tok_off, plane_rows):
    n, d = x2d.shape
    assert n % tile == 0 and tok_off % tile == 0 and plane_rows % tile == 0
    per_row_mod = gt.shape[1] > 1
    if per_row_mod:
        mod_spec = pl.BlockSpec((1, tile, d), lambda j: (0, j, 0))
    else:
        assert rows_per_mod % tile == 0
        mod_spec = pl.BlockSpec((1, 1, d), lambda j: (j // (rows_per_mod // tile), 0, 0))
    first = tok_off // tile
    second = (plane_rows + tok_off) // tile
    return pl.pallas_call(
        _combine_kernel,
        grid=(n // tile,),
        in_specs=[
            pl.BlockSpec((tile, d), lambda j: (j, 0)),
            mod_spec,
            pl.BlockSpec((tile, ROUTE_COLS), lambda j: (j, 0)),
            pl.BlockSpec((1, d), lambda j: (0, 0)),
            pl.BlockSpec((tile, d), lambda j: (first + j, 0)),
            pl.BlockSpec((tile, d), lambda j: (second + j, 0)),
        ],
        out_specs=pl.BlockSpec((tile, d), lambda j: (j, 0)),
        out_shape=jax.ShapeDtypeStruct((n, d), F32),
        compiler_params=_params("arbitrary"),
        name="combine",
    )(x2d, gt, route, g_final.reshape(1, d), y_tok, y_tok)


def _routing_tables(route_all, counts, tm, n_tiles, plane_rows):
    ne = counts.shape[0]
    n = route_all.shape[0]
    e_idx = route_all[:, COL_EXPERT:COL_EXPERT + TOP_K].astype(I32)
    rank = route_all[:, COL_RANK:COL_RANK + TOP_K].astype(I32)
    padded = (counts + tm - 1) // tm * tm
    ends = jnp.cumsum(padded)
    offs = ends - padded
    onehot = e_idx[..., None] == jnp.arange(ne, dtype=I32)
    pos = jnp.sum(jnp.where(onehot, offs, 0), axis=-1) + rank
    n_active = ends[-1] // tm
    starts = jnp.arange(n_tiles, dtype=I32) * tm
    tile_expert = jnp.sum(starts[:, None] >= ends[None, :], axis=1).astype(I32)
    last_expert = jnp.sum((n_active - 1) * tm >= ends).astype(I32)
    tile_expert = jnp.minimum(tile_expert, last_expert)
    n_rows = n_tiles * tm
    inv = jnp.zeros((n_rows,), I32).at[pos.reshape(-1)].set(jnp.arange(n * TOP_K, dtype=I32))
    r = jnp.arange(n_rows, dtype=I32)[:, None]
    real_before = jnp.sum(jnp.clip(r - offs, 0, counts), axis=1)
    is_real = jnp.any(jnp.logical_and(r >= offs, r < offs + counts), axis=1)
    tok = inv // TOP_K
    choice = inv % TOP_K
    spare = n + jnp.minimum(r[:, 0] - real_before, ne * tm - 1)
    src = jnp.where(is_real, tok, 0)
    dst = jnp.where(is_real, choice * plane_rows + tok, spare)
    lead = n + ne * tm + jnp.arange(tm, dtype=I32)
    dst = jnp.concatenate([lead, dst])[:n_rows]
    return src, dst, tile_expert, n_active.astype(I32).reshape(1)


def _moe_layer(xp, xs, mod_p, mod_s, g, router_w, wg, wu, wd, g_final):
    nb, seq, d = xp.shape
    ns = xs.shape[0]
    ne = router_w.shape[-1]
    n_all = nb * seq + ns
    sh_p, sc_p, gt_p = mod_p
    sh_s, sc_s, gt_s = mod_s
    xp2 = xp.reshape(nb * seq, d)

    zero_cnt = jnp.zeros((1, ne), F32)
    h_all, route_p, cnt_p = _router(xp2, sh_p, sc_p, g, router_w, zero_cnt, ROUTER_TILE, seq,
                                    n_all)
    h_all, route_s, cnt_s = _router(xs, sh_s, sc_s, g, router_w, cnt_p, ns, ns, n_all,
                                    h_prev=h_all, row_off=nb * seq)

    tm = MOE_ROW_TILE
    n_tiles = (n_all * TOP_K) // tm + ne + ROW_SLOTS
    fill_tiles = ne + 1
    plane_rows = n_all + fill_tiles * tm
    route_all = jnp.concatenate([route_p, route_s], axis=0)
    src, dst, tile_expert, n_active = _routing_tables(
        route_all, cnt_s[0].astype(I32), tm, n_tiles, plane_rows)

    y_tok = _moe_grouped(h_all, tile_expert, n_active, src, dst, wg, wu, wd, tm, MOE_FF_CHUNK,
                         plane_rows + n_all, n_all, fill_tiles)
    yp = _combine(xp2, gt_p, route_p, g_final, y_tok, COMBINE_TILE, seq, 0, plane_rows)
    ys = _combine(xs, gt_s, route_s, g_final, y_tok, ns, ns, nb * seq, plane_rows)
    return yp.reshape(nb, seq, d), ys


def kernel(x_prompt, x_sample, c_prompt, c_sample, state_conv, state_pool, w_ada, b_ada, g_mix,
           g_ffn, w_in, conv_w, pool_w, pool_scale, g_conv_out, g_pool_out, w_out, dense_w_gate,
           dense_w_up, dense_w_down, router_w, moe_w_gate, moe_w_up, moe_w_down, g_final):
    depth = w_ada.shape[0]
    nb, seq, d = x_prompt.shape
    ns = x_sample.shape[0]
    assert x_sample.shape[1] == 1 and depth == 2
    assert sum(DENSE_FF_CHUNKS) == dense_w_gate.shape[-1]

    mod = _ada(jnp.concatenate([c_prompt, c_sample], axis=0), w_ada, b_ada)
    mod = mod.reshape(depth, nb + ns, 6, d)
    mod_p = [jnp.transpose(mod[i, :nb], (1, 0, 2))[:, :, None, :] for i in range(depth)]
    mod_s = [jnp.transpose(mod[i, nb:], (1, 0, 2))[:, None, :, :] for i in range(depth)]

    w_in_b = w_in.astype(BF16)
    w_out_b = w_out.astype(BF16)
    pool_w_b = pool_w.astype(BF16)
    dense_b = [w.astype(BF16) for w in (dense_w_gate, dense_w_up, dense_w_down)]
    cb_t = jnp.transpose(state_conv, (0, 2, 1, 3))
    pb_t = jnp.transpose(state_pool, (0, 2, 1, 3))

    xp = x_prompt
    xs = x_sample.reshape(ns, d)
    conv_p, pool_p, conv_s, pool_s = [], [], [], []
    for i in range(depth):
        sh1, sc1, gt1, sh2, sc2, gt2 = mod_p[i]
        mix_w = (g_mix[i], w_in_b[i], conv_w[i], pool_w_b[i], pool_scale[i], g_conv_out[i],
                 g_pool_out[i], w_out_b[i])
        xp, cs, ps = _mix_prompt(xp, sh1, sc1, gt1, *mix_w, tile=PROMPT_TILE)
        conv_p.append(cs)
        pool_p.append(ps)
        s1, c1, t1, s2, c2, t2 = mod_s[i]
        xs, v_new, u_new = _mix_sample(xs, s1[0], c1[0], t1[0], *mix_w, cb_t[i], pb_t[i])
        conv_s.append(jnp.concatenate([state_conv[i][:, 1:], v_new[:, None, :]], axis=1))
        pool_s.append(jnp.concatenate([state_pool[i][:, 1:], u_new[:, None, :]], axis=1))
        j = i // 2
        if i % 2 == 0:
            wg, wu, wd = (w[j] for w in dense_b)
            xp = _ffn(xp, sh2, sc2, gt2, g_ffn[i], wg, wu, wd, PROMPT_TILE, DENSE_FF_CHUNKS)
            xs = _ffn(xs[None], s2, c2, t2, g_ffn[i], wg, wu, wd, ns, DENSE_FF_CHUNKS)[0]
        else:
            xp, xs = _moe_layer(xp, xs, (sh2, sc2, gt2), (s2, c2, t2), g_ffn[i], router_w[j],
                                moe_w_gate[j], moe_w_up[j], moe_w_down[j], g_final)

    return (xp, xs.reshape(ns, 1, d), jnp.stack(conv_p), jnp.stack(pool_p),
            jnp.stack(conv_s), jnp.stack(pool_s))
```

```python
import functools

import jax
import jax.numpy as jnp
from jax import lax
from jax.experimental import pallas as pl
from jax.experimental.pallas import tpu as pltpu

F32 = jnp.float32
BF16 = jnp.bfloat16
I32 = jnp.int32

EPS = 1e-6
CONV_K = 3
POOL_WINDOWS = (2, 4, 8, 16)
POOL_HIST = max(POOL_WINDOWS) - 1
TOP_K = 2

CONV_PAD = 8
POOL_PAD = 16

ROUTE_COLS = 8
COL_EXPERT, COL_RANK, COL_PROB = 0, 2, 4

VMEM_LIMIT_BYTES = 56 * 1024 * 1024

PROMPT_TILE = 512
MIX_TILE = 1024
MIX_ROW_BLOCK = 512
ROUTER_TILE = 1024
COMBINE_TILE = 512
MOE_ROW_TILE = 896
MOE_FF_CHUNK = 512
ROW_SLOTS = 3
DENSE_FF_CHUNKS = (768, 768, 768, 512)


def _params(*sem):
    return pltpu.CompilerParams(dimension_semantics=sem, vmem_limit_bytes=VMEM_LIMIT_BYTES)


def _resident(shape):
    nd = len(shape)
    return pl.BlockSpec(shape, lambda *_: (0,) * nd, pipeline_mode=pl.Buffered(1))


def _rms(x, g):
    ms = jnp.mean(x * x, axis=-1, keepdims=True)
    return x * lax.rsqrt(ms + EPS) * g


def _mod_norm(x, g, sc, sh):
    ms = jnp.mean(x * x, axis=-1, keepdims=True)
    return x * lax.rsqrt(ms + EPS) * (g * (1.0 + sc)) + sh


def _dot(a, b):
    return jnp.dot(a, b, preferred_element_type=F32)


def _silu_mul(a, b):
    return a * jax.nn.sigmoid(a) * b


def _ada_kernel(c_ref, w_ref, b_ref, o_ref):
    c = c_ref[...]
    a = (c * jax.nn.sigmoid(c)).astype(BF16)
    o_ref[0] = _dot(a, w_ref[0].astype(BF16)) + b_ref[0]


def _ada(c_all, w_ada, b_ada):
    depth, d, n = w_ada.shape
    m = c_all.shape[0]
    tn = 1024
    return pl.pallas_call(
        _ada_kernel,
        grid=(depth, n // tn),
        in_specs=[
            pl.BlockSpec((m, d), lambda i, j: (0, 0)),
            pl.BlockSpec((1, d, tn), lambda i, j: (i, 0, j)),
            pl.BlockSpec((1, 1, tn), lambda i, j: (i, 0, j)),
        ],
        out_specs=pl.BlockSpec((1, m, tn), lambda i, j: (i, 0, j)),
        out_shape=jax.ShapeDtypeStruct((depth, m, n), F32),
        compiler_params=_params("arbitrary", "arbitrary"),
        name="ada",
    )(c_all, w_ada, b_ada.reshape(depth, 1, n))


def _mix_tail(x, gt, bg, y, d_groups, poolw_ref, pscale_ref, gco_ref, gpo_ref, wout_ref):
    cw = gco_ref.shape[-1]
    ya = bg * y
    yb = jnp.concatenate(
        [_dot(d.astype(BF16), poolw_ref[g]) for g, d in enumerate(d_groups)], axis=-1
    ) * pscale_ref[...]
    ma = _rms(ya, gco_ref[...]).astype(BF16)
    mb = _rms(yb, gpo_ref[...]).astype(BF16)
    o = _dot(ma, wout_ref[0:cw, :]) + _dot(mb, wout_ref[cw:, :])
    return x + gt * o


def _mix_prompt_kernel(x_ref, sh_ref, sc_ref, gt_ref, g_ref, win_ref, convw_ref, poolw_ref,
                       pscale_ref, gco_ref, gpo_ref, wout_ref,
                       xo_ref, cs_ref, ps_ref, vbuf, ubuf, *, tile, sub):
    l = pl.program_id(1)
    cw = convw_ref.shape[-1]
    pg = poolw_ref.shape[-1]

    @pl.when(l == 0)
    def _():
        vbuf[0:CONV_PAD, :] = jnp.zeros((CONV_PAD, cw), F32)
        ubuf[0:POOL_PAD, :] = jnp.zeros((POOL_PAD, ubuf.shape[-1]), F32)

    gates = []
    for lo in range(0, tile, sub):
        x = x_ref[0, lo:lo + sub, :]
        h = _mod_norm(x, g_ref[...], sc_ref[0], sh_ref[0]).astype(BF16)
        p = _dot(h, win_ref[...])
        gates.append(p[:, 0:cw])
        vbuf[CONV_PAD + lo:CONV_PAD + lo + sub, :] = p[:, cw:2 * cw] * p[:, 2 * cw:3 * cw]
        ubuf[POOL_PAD + lo:POOL_PAD + lo + sub, :] = p[:, 3 * cw:]

    w = convw_ref[...]
    for bg, lo in zip(gates, range(0, tile, sub)):
        y = w[CONV_K - 1:CONV_K] * vbuf[CONV_PAD + lo:CONV_PAD + lo + sub, :]
        for k in range(1, CONV_K):
            y = y + w[CONV_K - 1 - k:CONV_K - k] * vbuf[CONV_PAD + lo - k:CONV_PAD + lo - k + sub, :]

        pos = l * tile + lo + lax.broadcasted_iota(I32, (sub, 1), 0)
        d_groups = []
        for g, win in enumerate(POOL_WINDOWS):
            ug = ubuf[POOL_PAD + lo:POOL_PAD + lo + sub, g * pg:(g + 1) * pg]
            acc = ug
            for k in range(1, win):
                acc = acc + ubuf[POOL_PAD + lo - k:POOL_PAD + lo - k + sub, g * pg:(g + 1) * pg]
            cnt = jnp.minimum(pos + 1, win).astype(F32)
            d_groups.append(acc / cnt - ug)

        xo_ref[0, lo:lo + sub, :] = _mix_tail(x_ref[0, lo:lo + sub, :], gt_ref[0], bg, y, d_groups,
                                              poolw_ref, pscale_ref, gco_ref, gpo_ref, wout_ref)

    cs_ref[0] = vbuf[CONV_PAD + tile - (CONV_K - 1):CONV_PAD + tile, :]
    ps_ref[0] = ubuf[POOL_PAD + tile - POOL_HIST:POOL_PAD + tile, :]
    vbuf[0:CONV_PAD, :] = vbuf[tile:tile + CONV_PAD, :]
    ubuf[0:POOL_PAD, :] = ubuf[tile:tile + POOL_PAD, :]


def _mix_prompt(x, sh, sc, gt, g, w_in, conv_w, pool_w, pool_scale, g_co, g_po, w_out, tile, sub):
    b, seq, d = x.shape
    assert seq % tile == 0 and tile % sub == 0 and sub >= POOL_PAD
    cw = conv_w.shape[-1]
    pw = pool_scale.shape[-1]
    row = lambda bi, li: (bi, 0, 0)
    return pl.pallas_call(
        functools.partial(_mix_prompt_kernel, tile=tile, sub=sub),
        grid=(b, seq // tile),
        in_specs=[
            pl.BlockSpec((1, tile, d), lambda bi, li: (bi, li, 0)),
            pl.BlockSpec((1, 1, d), row),
            pl.BlockSpec((1, 1, d), row),
            pl.BlockSpec((1, 1, d), row),
            _resident((1, d)),
            _resident(w_in.shape),
            _resident(conv_w.shape),
            _resident(pool_w.shape),
            _resident((1, pw)),
            _resident((1, cw)),
            _resident((1, pw)),
            _resident(w_out.shape),
        ],
        out_specs=[
            pl.BlockSpec((1, tile, d), lambda bi, li: (bi, li, 0)),
            pl.BlockSpec((1, CONV_K - 1, cw), row),
            pl.BlockSpec((1, POOL_HIST, pw), row),
        ],
        out_shape=[
            jax.ShapeDtypeStruct((b, seq, d), F32),
            jax.ShapeDtypeStruct((b, CONV_K - 1, cw), F32),
            jax.ShapeDtypeStruct((b, POOL_HIST, pw), F32),
        ],
        scratch_shapes=[
            pltpu.VMEM((CONV_PAD + tile, cw), F32),
            pltpu.VMEM((POOL_PAD + tile, pw), F32),
        ],
        compiler_params=_params("arbitrary", "arbitrary"),
        name="mix_prompt",
    )(x, sh, sc, gt, g.reshape(1, d), w_in, conv_w, pool_w, pool_scale.reshape(1, pw),
      g_co.reshape(1, cw), g_po.reshape(1, pw), w_out)


def _mix_sample_kernel(x_ref, sh_ref, sc_ref, gt_ref, g_ref, win_ref, convw_ref, poolw_ref,
                       pscale_ref, gco_ref, gpo_ref, wout_ref, cb_ref, pb_ref,
                       xo_ref, v_ref, u_ref):
    cw = convw_ref.shape[-1]
    pg = poolw_ref.shape[-1]
    x = x_ref[...]
    h = _mod_norm(x, g_ref[...], sc_ref[...], sh_ref[...]).astype(BF16)
    p = _dot(h, win_ref[...])
    bg = p[:, 0:cw]
    v = p[:, cw:2 * cw] * p[:, 2 * cw:3 * cw]
    u = p[:, 3 * cw:]
    v_ref[...] = v
    u_ref[...] = u

    w = convw_ref[...]
    y = w[CONV_K - 1:CONV_K] * v
    for k in range(1, CONV_K):
        y = y + w[CONV_K - 1 - k:CONV_K - k] * cb_ref[CONV_K - 1 - k]

    d_groups = []
    for g, win in enumerate(POOL_WINDOWS):
        ug = u[:, g * pg:(g + 1) * pg]
        acc = ug
        for k in range(1, win):
            acc = acc + pb_ref[POOL_HIST - k, :, g * pg:(g + 1) * pg]
        d_groups.append(acc / float(win) - ug)

    xo_ref[...] = _mix_tail(x, gt_ref[...], bg, y, d_groups, poolw_ref, pscale_ref,
                            gco_ref, gpo_ref, wout_ref)


def _mix_sample(x, sh, sc, gt, g, w_in, conv_w, pool_w, pool_scale, g_co, g_po, w_out, cb, pb):
    n, d = x.shape
    cw = conv_w.shape[-1]
    pw = pool_scale.shape[-1]
    return pl.pallas_call(
        _mix_sample_kernel,
        out_shape=[
            jax.ShapeDtypeStruct((n, d), F32),
            jax.ShapeDtypeStruct((n, cw), F32),
            jax.ShapeDtypeStruct((n, pw), F32),
        ],
        compiler_params=pltpu.CompilerParams(vmem_limit_bytes=VMEM_LIMIT_BYTES),
        name="mix_sample",
    )(x, sh, sc, gt, g.reshape(1, d), w_in, conv_w, pool_w, pool_scale.reshape(1, pw),
      g_co.reshape(1, cw), g_po.reshape(1, pw), w_out, cb, pb)


def _ffn_kernel(x_ref, sh_ref, sc_ref, gt_ref, g_ref, wg_ref, wu_ref, wd_ref, o_ref, *, chunks):
    x = x_ref[0]
    h = _mod_norm(x, g_ref[...], sc_ref[0], sh_ref[0]).astype(BF16)
    acc = None
    lo = 0
    for fc in chunks:
        a = _dot(h, wg_ref[:, lo:lo + fc])
        b = _dot(h, wu_ref[:, lo:lo + fc])
        part = _dot(_silu_mul(a, b).astype(BF16), wd_ref[lo:lo + fc, :])
        acc = part if acc is None else acc + part
        lo += fc
    o_ref[0] = x + gt_ref[0] * acc


def _ffn(x, sh, sc, gt, g, wg, wu, wd, tile, chunks):
    b, seq, d = x.shape
    tm = sh.shape[1]
    tmod = tile if tm > 1 else 1
    mod_map = (lambda bi, li: (bi, li, 0)) if tm > 1 else (lambda bi, li: (bi, 0, 0))
    return pl.pallas_call(
        functools.partial(_ffn_kernel, chunks=chunks),
        grid=(b, seq // tile),
        in_specs=[
            pl.BlockSpec((1, tile, d), lambda bi, li: (bi, li, 0)),
            pl.BlockSpec((1, tmod, d), mod_map),
            pl.BlockSpec((1, tmod, d), mod_map),
            pl.BlockSpec((1, tmod, d), mod_map),
            _resident((1, d)),
            _resident(wg.shape),
            _resident(wu.shape),
            _resident(wd.shape),
        ],
        out_specs=pl.BlockSpec((1, tile, d), lambda bi, li: (bi, li, 0)),
        out_shape=jax.ShapeDtypeStruct((b, seq, d), F32),
        compiler_params=_params("arbitrary", "arbitrary"),
        name="ffn",
    )(x, sh, sc, gt, g.reshape(1, d), wg, wu, wd)


def _split_bf16(a):
    hi = a.astype(BF16)
    return hi, (a - hi.astype(F32)).astype(BF16)


def _router_kernel(x_ref, sh_ref, sc_ref, g_ref, rw_ref, base_ref, h_ref, route_ref, cnt_ref,
                   tri, carry, *, n_steps):
    t = x_ref.shape[0]
    step = pl.program_id(0)

    @pl.when(step == 0)
    def _():
        r = lax.broadcasted_iota(I32, (t, t), 0)
        c = lax.broadcasted_iota(I32, (t, t), 1)
        tri[...] = jnp.where(c < r, 1.0, 0.0).astype(BF16)
        carry[...] = base_ref[...]

    @pl.when(step >= n_steps)
    def _():
        h_ref[...] = jnp.zeros_like(h_ref)

    @pl.when(step < n_steps)
    def _():
        h = _mod_norm(x_ref[...], g_ref[...], sc_ref[0], sh_ref[0])
        h_ref[...] = h
        h_hi, h_lo = _split_bf16(h)
        rw_hi, rw_lo = _split_bf16(rw_ref[...])
        logits = _dot(h_hi, rw_hi) + _dot(h_lo, rw_hi) + _dot(h_hi, rw_lo)

        ne = logits.shape[-1]
        idx = lax.broadcasted_iota(I32, logits.shape, 1)
        m1 = jnp.max(logits, axis=-1, keepdims=True)
        i1 = jnp.min(jnp.where(logits == m1, idx, ne), axis=-1, keepdims=True)
        sel1 = idx == i1
        rest = jnp.where(sel1, -jnp.inf, logits)
        m2 = jnp.max(rest, axis=-1, keepdims=True)
        i2 = jnp.min(jnp.where(rest == m2, idx, ne), axis=-1, keepdims=True)
        sel2 = idx == i2
        e = jnp.exp(m2 - m1)
        p1 = 1.0 / (1.0 + e)
        p2 = e / (1.0 + e)

        chosen = jnp.where(sel1, 1.0, jnp.where(sel2, 1.0, 0.0))
        before = _dot(tri[...], chosen.astype(BF16)) + carry[...]
        r1 = jnp.sum(jnp.where(sel1, before, 0.0), axis=-1, keepdims=True)
        r2 = jnp.sum(jnp.where(sel2, before, 0.0), axis=-1, keepdims=True)
        carry[...] += jnp.sum(chosen, axis=0, keepdims=True)
        cnt_ref[...] = carry[...]

        cols = (i1.astype(F32), i2.astype(F32), r1, r2, p1, p2)
        route = jnp.zeros(logits.shape, F32)
        for k, col in enumerate(cols):
            route = jnp.where(idx == k, col, route)
        route_ref[...] = route


def _router(x2d, sh, sc, g, router_w, base_cnt, tile, rows_per_mod, h_rows, h_prev=None,
            row_off=0):
    n, d = x2d.shape
    ne = router_w.shape[-1]
    assert ne == ROUTE_COLS and n % tile == 0 and row_off % tile == 0
    n_steps = n // tile
    tail = h_rows - (row_off + n)
    assert 0 <= tail < tile
    fill_tail = h_prev is None and tail > 0
    last = n_steps - 1
    clamp = (lambda i: jnp.minimum(i, last)) if fill_tail else (lambda i: i)
    per_row_mod = sh.shape[1] > 1
    if per_row_mod:
        mod_spec = pl.BlockSpec((1, tile, d), lambda i: (0, clamp(i), 0))
    else:
        assert rows_per_mod % tile == 0
        mod_spec = pl.BlockSpec((1, 1, d), lambda i: (clamp(i) // (rows_per_mod // tile), 0, 0))
    in_specs = [
        pl.BlockSpec((tile, d), lambda i: (clamp(i), 0)),
        mod_spec,
        mod_spec,
        _resident((1, d)),
        _resident(router_w.shape),
        _resident((1, ne)),
    ]
    args = [x2d, sh, sc, g.reshape(1, d), router_w, base_cnt]
    n_in = len(args)
    aliases = {}
    body = functools.partial(_router_kernel, n_steps=n_steps)
    if h_prev is not None:
        assert h_prev.shape == (h_rows, d)
        in_specs.append(pl.BlockSpec(memory_space=pl.ANY))
        args.append(h_prev)
        aliases = {n_in: 0}

        def body(*refs):
            _router_kernel(*refs[:n_in], *refs[n_in + 1:], n_steps=n_steps)

    blk_off = row_off // tile
    return pl.pallas_call(
        body,
        grid=(n_steps + int(fill_tail),),
        in_specs=in_specs,
        out_specs=[
            pl.BlockSpec((tile, d), lambda i: (i + blk_off, 0)),
            pl.BlockSpec((tile, ne), lambda i: (clamp(i), 0)),
            pl.BlockSpec((1, ne), lambda i: (0, 0)),
        ],
        out_shape=[
            jax.ShapeDtypeStruct((h_rows, d), F32),
            jax.ShapeDtypeStruct((n, ne), F32),
            jax.ShapeDtypeStruct((1, ne), F32),
        ],
        scratch_shapes=[pltpu.VMEM((tile, tile), BF16), pltpu.VMEM((1, ne), F32)],
        input_output_aliases=aliases,
        compiler_params=_params("arbitrary"),
        name="router",
    )(*args)


def _row_copy(src_hbm, row, dst, dst_row, sem):
    return pltpu.make_async_copy(src_hbm.at[pl.ds(row, 1), :], dst.at[pl.ds(dst_row, 1), :], sem)


def _row_copy_out(src, src_row, dst_hbm, row, sem):
    return pltpu.make_async_copy(src.at[pl.ds(src_row, 1), :], dst_hbm.at[pl.ds(row, 1), :], sem)


def _moe_grouped_kernel(te_ref, na_ref, src_ref, dst_ref, h_hbm, wg_ref, wu_ref, wd_ref, y_hbm,
                        xs, hbuf, obuf, gsem, ssem, fsem, *, tm, rows_per_step, fill_row,
                        fill_tiles):
    del te_ref
    i = pl.program_id(0)
    c = pl.program_id(1)
    n_active = na_ref[0]
    ahead = ROW_SLOTS - 1
    slot = i % ROW_SLOTS

    def request(tile, r):
        s = tile % ROW_SLOTS
        return _row_copy(h_hbm, src_ref[tile * tm + r], xs.at[s], r, gsem.at[s])

    def send(tile, r):
        s = (tile + ROW_SLOTS) % ROW_SLOTS
        return _row_copy_out(obuf.at[s], r, y_hbm, dst_ref[(tile + 1) * tm + r], ssem.at[s])

    @pl.when(jnp.logical_and(i == 0, c == 0))
    def _():
        last = ROW_SLOTS - 1
        obuf[last] = jnp.zeros((tm, obuf.shape[-1]), F32)
        fills = [pltpu.make_async_copy(obuf.at[last], y_hbm.at[pl.ds(fill_row + q * tm, tm), :],
                                       fsem) for q in range(fill_tiles)]
        for f in fills:
            f.start()
        for f in fills:
            f.wait()

        def body(r, carry):
            for t in range(ahead):
                request(t, r).start()
            return carry

        lax.fori_loop(0, tm, body, 0)

    @pl.when(jnp.logical_and(i < n_active + ahead, c == 0))
    def _():
        pltpu.make_async_copy(h_hbm.at[pl.ds(0, tm), :], xs.at[slot], gsem.at[slot]).wait()

    @pl.when(jnp.logical_and(jnp.logical_and(i >= ahead, i < n_active + ROW_SLOTS), c == 0))
    def _():
        pltpu.make_async_copy(obuf.at[slot], y_hbm.at[pl.ds(0, tm), :], ssem.at[slot]).wait()

    @pl.when(jnp.logical_and(i < n_active, c == 0))
    def _():
        obuf[slot] = jnp.zeros((tm, obuf.shape[-1]), F32)
        hbuf[...] = xs[slot].astype(BF16)

    @pl.when(jnp.logical_and(i == n_active, c == 0))
    def _():
        def body(r, carry):
            send(i - 1, r).start()
            return carry

        lax.fori_loop(0, tm, body, 0)

    @pl.when(i < n_active)
    def _():
        for k in range(rows_per_step):
            r = c * rows_per_step + k
            request(i + ahead, r).start()
            send(i - 1, r).start()

        h = hbuf[...]
        a = _dot(h, wg_ref[0].astype(BF16))
        b = _dot(h, wu_ref[0].astype(BF16))
        obuf[slot] += _dot(_silu_mul(a, b).astype(BF16), wd_ref[0].astype(BF16))


def _moe_grouped(h_all, tile_expert, n_active, src, dst, wg, wu, wd, tm, fc, y_rows, fill_row,
                 fill_tiles):
    ne, d, dff = wg.shape
    n_tiles = tile_expert.shape[0]
    nc = dff // fc
    assert dff % fc == 0 and tm % nc == 0
    assert src.shape[0] == n_tiles * tm and dst.shape[0] == n_tiles * tm
    rows_per_step = tm // nc

    def w_col(i, c, te, na, src_, dst_):
        return (te[i], 0, jnp.where(i < na[0], c, nc - 1))

    def w_row(i, c, te, na, src_, dst_):
        return (te[i], jnp.where(i < na[0], c, nc - 1), 0)

    grid_spec = pltpu.PrefetchScalarGridSpec(
        num_scalar_prefetch=4,
        grid=(n_tiles, nc),
        in_specs=[
            pl.BlockSpec(memory_space=pl.ANY),
            pl.BlockSpec((1, d, fc), w_col),
            pl.BlockSpec((1, d, fc), w_col),
            pl.BlockSpec((1, fc, d), w_row),
        ],
        out_specs=pl.BlockSpec(memory_space=pl.ANY),
        scratch_shapes=[
            pltpu.VMEM((ROW_SLOTS, tm, d), F32),
            pltpu.VMEM((tm, d), BF16),
            pltpu.VMEM((ROW_SLOTS, tm, d), F32),
            pltpu.SemaphoreType.DMA((ROW_SLOTS,)),
            pltpu.SemaphoreType.DMA((ROW_SLOTS,)),
            pltpu.SemaphoreType.DMA,
        ],
    )
    return pl.pallas_call(
        functools.partial(_moe_grouped_kernel, tm=tm, rows_per_step=rows_per_step,
                          fill_row=fill_row, fill_tiles=fill_tiles),
        grid_spec=grid_spec,
        out_shape=jax.ShapeDtypeStruct((y_rows, d), F32),
        compiler_params=_params("arbitrary", "arbitrary"),
        name="moe_grouped",
    )(tile_expert, n_active, src, dst, h_all, wg, wu, wd)


def _combine_kernel(x_ref, gt_ref, route_ref, gf_ref, y0_ref, y1_ref, o_ref):
    route = route_ref[...]
    f = (route[:, COL_PROB:COL_PROB + 1] * y0_ref[...]
         + route[:, COL_PROB + 1:COL_PROB + 2] * y1_ref[...])
    o_ref[...] = _rms(x_ref[...] + gt_ref[0] * f, gf_ref[...])


def _combine(x2d, gt, route, g_final, y_tok, tile, rows_per_mod, tok_off, plane_rows):
    n, d = x2d.shape
    assert n % tile == 0 and tok_off % tile == 0 and plane_rows % tile == 0
    per_row_mod = gt.shape[1] > 1
    if per_row_mod:
        mod_spec = pl.BlockSpec((1, tile, d), lambda j: (0, j, 0))
    else:
        assert rows_per_mod % tile == 0
        mod_spec = pl.BlockSpec((1, 1, d), lambda j: (j // (rows_per_mod // tile), 0, 0))
    first = tok_off // tile
    second = (plane_rows + tok_off) // tile
    return pl.pallas_call(
        _combine_kernel,
        grid=(n // tile,),
        in_specs=[
            pl.BlockSpec((tile, d), lambda j: (j, 0)),
            mod_spec,
            pl.BlockSpec((tile, ROUTE_COLS), lambda j: (j, 0)),
            pl.BlockSpec((1, d), lambda j: (0, 0)),
            pl.BlockSpec((tile, d), lambda j: (first + j, 0)),
            pl.BlockSpec((tile, d), lambda j: (second + j, 0)),
        ],
        out_specs=pl.BlockSpec((tile, d), lambda j: (j, 0)),
        out_shape=jax.ShapeDtypeStruct((n, d), F32),
        compiler_params=_params("arbitrary"),
        name="combine",
    )(x2d, gt, route, g_final.reshape(1, d), y_tok, y_tok)


def _routing_tables(route_all, counts, tm, n_tiles, plane_rows):
    ne = counts.shape[0]
    n = route_all.shape[0]
    e_idx = route_all[:, COL_EXPERT:COL_EXPERT + TOP_K].astype(I32)
    rank = route_all[:, COL_RANK:COL_RANK + TOP_K].astype(I32)
    padded = (counts + tm - 1) // tm * tm
    ends = jnp.cumsum(padded)
    offs = ends - padded
    onehot = e_idx[..., None] == jnp.arange(ne, dtype=I32)
    pos = jnp.sum(jnp.where(onehot, offs, 0), axis=-1) + rank
    n_active = ends[-1] // tm
    starts = jnp.arange(n_tiles, dtype=I32) * tm
    tile_expert = jnp.sum(starts[:, None] >= ends[None, :], axis=1).astype(I32)
    last_expert = jnp.sum((n_active - 1) * tm >= ends).astype(I32)
    tile_expert = jnp.minimum(tile_expert, last_expert)
    n_rows = n_tiles * tm
    inv = jnp.zeros((n_rows,), I32).at[pos.reshape(-1)].set(jnp.arange(n * TOP_K, dtype=I32))
    r = jnp.arange(n_rows, dtype=I32)[:, None]
    real_before = jnp.sum(jnp.clip(r - offs, 0, counts), axis=1)
    is_real = jnp.any(jnp.logical_and(r >= offs, r < offs + counts), axis=1)
    tok = inv // TOP_K
    choice = inv % TOP_K
    spare = n + jnp.minimum(r[:, 0] - real_before, ne * tm - 1)
    src = jnp.where(is_real, tok, 0)
    dst = jnp.where(is_real, choice * plane_rows + tok, spare)
    lead = n + ne * tm + jnp.arange(tm, dtype=I32)
    dst = jnp.concatenate([lead, dst])[:n_rows]
    return src, dst, tile_expert, n_active.astype(I32).reshape(1)


def _moe_layer(xp, xs, mod_p, mod_s, g, router_w, wg, wu, wd, g_final):
    nb, seq, d = xp.shape
    ns = xs.shape[0]
    ne = router_w.shape[-1]
    n_all = nb * seq + ns
    sh_p, sc_p, gt_p = mod_p
    sh_s, sc_s, gt_s = mod_s
    xp2 = xp.reshape(nb * seq, d)

    zero_cnt = jnp.zeros((1, ne), F32)
    h_all, route_p, cnt_p = _router(xp2, sh_p, sc_p, g, router_w, zero_cnt, ROUTER_TILE, seq,
                                    n_all)
    h_all, route_s, cnt_s = _router(xs, sh_s, sc_s, g, router_w, cnt_p, ns, ns, n_all,
                                    h_prev=h_all, row_off=nb * seq)

    tm = MOE_ROW_TILE
    n_tiles = (n_all * TOP_K) // tm + ne + ROW_SLOTS
    fill_tiles = ne + 1
    plane_rows = n_all + fill_tiles * tm
    route_all = jnp.concatenate([route_p, route_s], axis=0)
    src, dst, tile_expert, n_active = _routing_tables(
        route_all, cnt_s[0].astype(I32), tm, n_tiles, plane_rows)

    y_tok = _moe_grouped(h_all, tile_expert, n_active, src, dst, wg, wu, wd, tm, MOE_FF_CHUNK,
                         plane_rows + n_all, n_all, fill_tiles)
    yp = _combine(xp2, gt_p, route_p, g_final, y_tok, COMBINE_TILE, seq, 0, plane_rows)
    ys = _combine(xs, gt_s, route_s, g_final, y_tok, ns, ns, nb * seq, plane_rows)
    return yp.reshape(nb, seq, d), ys


def kernel(x_prompt, x_sample, c_prompt, c_sample, state_conv, state_pool, w_ada, b_ada, g_mix,
           g_ffn, w_in, conv_w, pool_w, pool_scale, g_conv_out, g_pool_out, w_out, dense_w_gate,
           dense_w_up, dense_w_down, router_w, moe_w_gate, moe_w_up, moe_w_down, g_final):
    depth = w_ada.shape[0]
    nb, seq, d = x_prompt.shape
    ns = x_sample.shape[0]
    assert x_sample.shape[1] == 1 and depth == 2
    assert sum(DENSE_FF_CHUNKS) == dense_w_gate.shape[-1]

    mod = _ada(jnp.concatenate([c_prompt, c_sample], axis=0), w_ada, b_ada)
    mod = mod.reshape(depth, nb + ns, 6, d)
    mod_p = [jnp.transpose(mod[i, :nb], (1, 0, 2))[:, :, None, :] for i in range(depth)]
    mod_s = [jnp.transpose(mod[i, nb:], (1, 0, 2))[:, None, :, :] for i in range(depth)]

    w_in_b = w_in.astype(BF16)
    w_out_b = w_out.astype(BF16)
    pool_w_b = pool_w.astype(BF16)
    dense_b = [w.astype(BF16) for w in (dense_w_gate, dense_w_up, dense_w_down)]
    cb_t = jnp.transpose(state_conv, (0, 2, 1, 3))
    pb_t = jnp.transpose(state_pool, (0, 2, 1, 3))

    xp = x_prompt
    xs = x_sample.reshape(ns, d)
    conv_p, pool_p, conv_s, pool_s = [], [], [], []
    for i in range(depth):
        sh1, sc1, gt1, sh2, sc2, gt2 = mod_p[i]
        mix_w = (g_mix[i], w_in_b[i], conv_w[i], pool_w_b[i], pool_scale[i], g_conv_out[i],
                 g_pool_out[i], w_out_b[i])
        xp, cs, ps = _mix_prompt(xp, sh1, sc1, gt1, *mix_w, tile=MIX_TILE, sub=MIX_ROW_BLOCK)
        conv_p.append(cs)
        pool_p.append(ps)
        s1, c1, t1, s2, c2, t2 = mod_s[i]
        xs, v_new, u_new = _mix_sample(xs, s1[0], c1[0], t1[0], *mix_w, cb_t[i], pb_t[i])
        conv_s.append(jnp.concatenate([state_conv[i][:, 1:], v_new[:, None, :]], axis=1))
        pool_s.append(jnp.concatenate([state_pool[i][:, 1:], u_new[:, None, :]], axis=1))
        j = i // 2
        if i % 2 == 0:
            wg, wu, wd = (w[j] for w in dense_b)
            xp = _ffn(xp, sh2, sc2, gt2, g_ffn[i], wg, wu, wd, PROMPT_TILE, DENSE_FF_CHUNKS)
            xs = _ffn(xs[None], s2, c2, t2, g_ffn[i], wg, wu, wd, ns, DENSE_FF_CHUNKS)[0]
        else:
            xp, xs = _moe_layer(xp, xs, (sh2, sc2, gt2), (s2, c2, t2), g_ffn[i], router_w[j],
                                moe_w_gate[j], moe_w_up[j], moe_w_down[j], g_final)

    return (xp, xs.reshape(ns, 1, d), jnp.stack(conv_p), jnp.stack(pool_p),
            jnp.stack(conv_s), jnp.stack(pool_s))
```

```python
import functools

import jax
import jax.numpy as jnp
from jax import lax
from jax.experimental import pallas as pl
from jax.experimental.pallas import tpu as pltpu

F32 = jnp.float32
BF16 = jnp.bfloat16
I32 = jnp.int32

EPS = 1e-6
CONV_K = 3
POOL_WINDOWS = (2, 4, 8, 16)
POOL_HIST = max(POOL_WINDOWS) - 1
TOP_K = 2

CONV_PAD = 8
POOL_PAD = 16

ROUTE_COLS = 8
COL_EXPERT, COL_RANK, COL_PROB = 0, 2, 4

VMEM_LIMIT_BYTES = 56 * 1024 * 1024

PROMPT_TILE = 512
MIX_TILE = 1024
MIX_ROW_BLOCK = 512
ROUTER_TILE = 1024
COMBINE_TILE = 512
MOE_ROW_TILE = 896
MOE_FF_CHUNK = 512
ROW_SLOTS = 3
DENSE_FF_CHUNKS = (768, 768, 768, 512)


def _params(*sem):
    return pltpu.CompilerParams(dimension_semantics=sem, vmem_limit_bytes=VMEM_LIMIT_BYTES)


def _resident(shape):
    nd = len(shape)
    return pl.BlockSpec(shape, lambda *_: (0,) * nd, pipeline_mode=pl.Buffered(1))


def _rms(x, g):
    ms = jnp.mean(x * x, axis=-1, keepdims=True)
    return x * lax.rsqrt(ms + EPS) * g


def _mod_norm(x, g, sc, sh):
    ms = jnp.mean(x * x, axis=-1, keepdims=True)
    return x * lax.rsqrt(ms + EPS) * (g * (1.0 + sc)) + sh


def _dot(a, b):
    return jnp.dot(a, b, preferred_element_type=F32)


def _silu_mul(a, b):
    return a * jax.nn.sigmoid(a) * b


def _ada_kernel(c_ref, w_ref, b_ref, o_ref):
    c = c_ref[...]
    a = (c * jax.nn.sigmoid(c)).astype(BF16)
    o_ref[0] = _dot(a, w_ref[0].astype(BF16)) + b_ref[0]


def _ada(c_all, w_ada, b_ada):
    depth, d, n = w_ada.shape
    m = c_all.shape[0]
    tn = 1024
    return pl.pallas_call(
        _ada_kernel,
        grid=(depth, n // tn),
        in_specs=[
            pl.BlockSpec((m, d), lambda i, j: (0, 0)),
            pl.BlockSpec((1, d, tn), lambda i, j: (i, 0, j)),
            pl.BlockSpec((1, 1, tn), lambda i, j: (i, 0, j)),
        ],
        out_specs=pl.BlockSpec((1, m, tn), lambda i, j: (i, 0, j)),
        out_shape=jax.ShapeDtypeStruct((depth, m, n), F32),
        compiler_params=_params("arbitrary", "arbitrary"),
        name="ada",
    )(c_all, w_ada, b_ada.reshape(depth, 1, n))


def _mix_tail(x, gt, bg, y, d_groups, poolw_ref, pscale_ref, gco_ref, gpo_ref, wout_ref):
    cw = gco_ref.shape[-1]
    ya = bg * y
    yb = jnp.concatenate(
        [_dot(d.astype(BF16), poolw_ref[g]) for g, d in enumerate(d_groups)], axis=-1
    ) * pscale_ref[...]
    ma = _rms(ya, gco_ref[...]).astype(BF16)
    mb = _rms(yb, gpo_ref[...]).astype(BF16)
    o = _dot(ma, wout_ref[0:cw, :]) + _dot(mb, wout_ref[cw:, :])
    return x + gt * o


def _mix_prompt_kernel(x_ref, sh_ref, sc_ref, gt_ref, g_ref, win_ref, convw_ref, poolw_ref,
                       pscale_ref, gco_ref, gpo_ref, wout_ref,
                       xo_ref, cs_ref, ps_ref, vbuf, ubuf, *, tile, sub):
    l = pl.program_id(1)
    cw = convw_ref.shape[-1]
    pg = poolw_ref.shape[-1]

    @pl.when(l == 0)
    def _():
        vbuf[0:CONV_PAD, :] = jnp.zeros((CONV_PAD, cw), F32)
        ubuf[0:POOL_PAD, :] = jnp.zeros((POOL_PAD, ubuf.shape[-1]), F32)

    gates = []
    for lo in range(0, tile, sub):
        x = x_ref[0, lo:lo + sub, :]
        h = _mod_norm(x, g_ref[...], sc_ref[0], sh_ref[0]).astype(BF16)
        p = _dot(h, win_ref[...])
        gates.append(p[:, 0:cw])
        vbuf[CONV_PAD + lo:CONV_PAD + lo + sub, :] = p[:, cw:2 * cw] * p[:, 2 * cw:3 * cw]
        ubuf[POOL_PAD + lo:POOL_PAD + lo + sub, :] = p[:, 3 * cw:]

    w = convw_ref[...]
    for bg, lo in zip(gates, range(0, tile, sub)):
        y = w[CONV_K - 1:CONV_K] * vbuf[CONV_PAD + lo:CONV_PAD + lo + sub, :]
        for k in range(1, CONV_K):
            y = y + w[CONV_K - 1 - k:CONV_K - k] * vbuf[CONV_PAD + lo - k:CONV_PAD + lo - k + sub, :]

        pos = l * tile + lo + lax.broadcasted_iota(I32, (sub, 1), 0)
        d_groups = []
        for g, win in enumerate(POOL_WINDOWS):
            ug = ubuf[POOL_PAD + lo:POOL_PAD + lo + sub, g * pg:(g + 1) * pg]
            acc = ug
            for k in range(1, win):
                acc = acc + ubuf[POOL_PAD + lo - k:POOL_PAD + lo - k + sub, g * pg:(g + 1) * pg]
            cnt = jnp.minimum(pos + 1, win).astype(F32)
            d_groups.append(acc / cnt - ug)

        xo_ref[0, lo:lo + sub, :] = _mix_tail(x_ref[0, lo:lo + sub, :], gt_ref[0], bg, y, d_groups,
                                              poolw_ref, pscale_ref, gco_ref, gpo_ref, wout_ref)

    cs_ref[0] = vbuf[CONV_PAD + tile - (CONV_K - 1):CONV_PAD + tile, :]
    ps_ref[0] = ubuf[POOL_PAD + tile - POOL_HIST:POOL_PAD + tile, :]
    vbuf[0:CONV_PAD, :] = vbuf[tile:tile + CONV_PAD, :]
    ubuf[0:POOL_PAD, :] = ubuf[tile:tile + POOL_PAD, :]


def _mix_prompt(x, sh, sc, gt, g, w_in, conv_w, pool_w, pool_scale, g_co, g_po, w_out, tile, sub):
    b, seq, d = x.shape
    assert seq % tile == 0 and tile % sub == 0 and sub >= POOL_PAD
    cw = conv_w.shape[-1]
    pw = pool_scale.shape[-1]
    row = lambda bi, li: (bi, 0, 0)
    return pl.pallas_call(
        functools.partial(_mix_prompt_kernel, tile=tile, sub=sub),
        grid=(b, seq // tile),
        in_specs=[
            pl.BlockSpec((1, tile, d), lambda bi, li: (bi, li, 0)),
            pl.BlockSpec((1, 1, d), row),
            pl.BlockSpec((1, 1, d), row),
            pl.BlockSpec((1, 1, d), row),
            _resident((1, d)),
            _resident(w_in.shape),
            _resident(conv_w.shape),
            _resident(pool_w.shape),
            _resident((1, pw)),
            _resident((1, cw)),
            _resident((1, pw)),
            _resident(w_out.shape),
        ],
        out_specs=[
            pl.BlockSpec((1, tile, d), lambda bi, li: (bi, li, 0)),
            pl.BlockSpec((1, CONV_K - 1, cw), row),
            pl.BlockSpec((1, POOL_HIST, pw), row),
        ],
        out_shape=[
            jax.ShapeDtypeStruct((b, seq, d), F32),
            jax.ShapeDtypeStruct((b, CONV_K - 1, cw), F32),
            jax.ShapeDtypeStruct((b, POOL_HIST, pw), F32),
        ],
        scratch_shapes=[
            pltpu.VMEM((CONV_PAD + tile, cw), F32),
            pltpu.VMEM((POOL_PAD + tile, pw), F32),
        ],
        compiler_params=_params("arbitrary", "arbitrary"),
        name="mix_prompt",
    )(x, sh, sc, gt, g.reshape(1, d), w_in, conv_w, pool_w, pool_scale.reshape(1, pw),
      g_co.reshape(1, cw), g_po.reshape(1, pw), w_out)


def _mix_sample_kernel(x_ref, sh_ref, sc_ref, gt_ref, g_ref, win_ref, convw_ref, poolw_ref,
                       pscale_ref, gco_ref, gpo_ref, wout_ref, cb_ref, pb_ref,
                       xo_ref, v_ref, u_ref):
    cw = convw_ref.shape[-1]
    pg = poolw_ref.shape[-1]
    x = x_ref[...]
    h = _mod_norm(x, g_ref[...], sc_ref[...], sh_ref[...]).astype(BF16)
    p = _dot(h, win_ref[...])
    bg = p[:, 0:cw]
    v = p[:, cw:2 * cw] * p[:, 2 * cw:3 * cw]
    u = p[:, 3 * cw:]
    v_ref[...] = v
    u_ref[...] = u

    w = convw_ref[...]
    y = w[CONV_K - 1:CONV_K] * v
    for k in range(1, CONV_K):
        y = y + w[CONV_K - 1 - k:CONV_K - k] * cb_ref[CONV_K - 1 - k]

    d_groups = []
    for g, win in enumerate(POOL_WINDOWS):
        ug = u[:, g * pg:(g + 1) * pg]
        acc = ug
        for k in range(1, win):
            acc = acc + pb_ref[POOL_HIST - k, :, g * pg:(g + 1) * pg]
        d_groups.append(acc / float(win) - ug)

    xo_ref[...] = _mix_tail(x, gt_ref[...], bg, y, d_groups, poolw_ref, pscale_ref,
                            gco_ref, gpo_ref, wout_ref)


def _mix_sample(x, sh, sc, gt, g, w_in, conv_w, pool_w, pool_scale, g_co, g_po, w_out, cb, pb):
    n, d = x.shape
    cw = conv_w.shape[-1]
    pw = pool_scale.shape[-1]
    return pl.pallas_call(
        _mix_sample_kernel,
        out_shape=[
            jax.ShapeDtypeStruct((n, d), F32),
            jax.ShapeDtypeStruct((n, cw), F32),
            jax.ShapeDtypeStruct((n, pw), F32),
        ],
        compiler_params=pltpu.CompilerParams(vmem_limit_bytes=VMEM_LIMIT_BYTES),
        name="mix_sample",
    )(x, sh, sc, gt, g.reshape(1, d), w_in, conv_w, pool_w, pool_scale.reshape(1, pw),
      g_co.reshape(1, cw), g_po.reshape(1, pw), w_out, cb, pb)


def _ffn_kernel(x_ref, sh_ref, sc_ref, gt_ref, g_ref, wg_ref, wu_ref, wd_ref, o_ref, *, chunks):
    x = x_ref[0]
    h = _mod_norm(x, g_ref[...], sc_ref[0], sh_ref[0]).astype(BF16)
    acc = None
    lo = 0
    for fc in chunks:
        a = _dot(h, wg_ref[:, lo:lo + fc])
        b = _dot(h, wu_ref[:, lo:lo + fc])
        part = _dot(_silu_mul(a, b).astype(BF16), wd_ref[lo:lo + fc, :])
        acc = part if acc is None else acc + part
        lo += fc
    o_ref[0] = x + gt_ref[0] * acc


def _ffn(x, sh, sc, gt, g, wg, wu, wd, tile, chunks):
    b, seq, d = x.shape
    tm = sh.shape[1]
    tmod = tile if tm > 1 else 1
    mod_map = (lambda bi, li: (bi, li, 0)) if tm > 1 else (lambda bi, li: (bi, 0, 0))
    return pl.pallas_call(
        functools.partial(_ffn_kernel, chunks=chunks),
        grid=(b, seq // tile),
        in_specs=[
            pl.BlockSpec((1, tile, d), lambda bi, li: (bi, li, 0)),
            pl.BlockSpec((1, tmod, d), mod_map),
            pl.BlockSpec((1, tmod, d), mod_map),
            pl.BlockSpec((1, tmod, d), mod_map),
            _resident((1, d)),
            _resident(wg.shape),
            _resident(wu.shape),
            _resident(wd.shape),
        ],
        out_specs=pl.BlockSpec((1, tile, d), lambda bi, li: (bi, li, 0)),
        out_shape=jax.ShapeDtypeStruct((b, seq, d), F32),
        compiler_params=_params("arbitrary", "arbitrary"),
        name="ffn",
    )(x, sh, sc, gt, g.reshape(1, d), wg, wu, wd)


def _split_bf16(a):
    hi = a.astype(BF16)
    return hi, (a - hi.astype(F32)).astype(BF16)


def _router_kernel(x_ref, sh_ref, sc_ref, g_ref, rw_ref, base_ref, h_ref, route_ref, cnt_ref,
                   tri, carry, *, n_steps):
    t = x_ref.shape[0]
    step = pl.program_id(0)

    @pl.when(step == 0)
    def _():
        r = lax.broadcasted_iota(I32, (t, t), 0)
        c = lax.broadcasted_iota(I32, (t, t), 1)
        tri[...] = jnp.where(c < r, 1.0, 0.0).astype(BF16)
        carry[...] = base_ref[...]

    @pl.when(step >= n_steps)
    def _():
        h_ref[...] = jnp.zeros_like(h_ref)

    @pl.when(step < n_steps)
    def _():
        h = _mod_norm(x_ref[...], g_ref[...], sc_ref[0], sh_ref[0])
        h_ref[...] = h
        h_hi, h_lo = _split_bf16(h)
        rw_hi, rw_lo = _split_bf16(rw_ref[...])
        logits = _dot(h_hi, rw_hi) + _dot(h_lo, rw_hi) + _dot(h_hi, rw_lo)

        ne = logits.shape[-1]
        idx = lax.broadcasted_iota(I32, logits.shape, 1)
        m1 = jnp.max(logits, axis=-1, keepdims=True)
        i1 = jnp.min(jnp.where(logits == m1, idx, ne), axis=-1, keepdims=True)
        sel1 = idx == i1
        rest = jnp.where(sel1, -jnp.inf, logits)
        m2 = jnp.max(rest, axis=-1, keepdims=True)
        i2 = jnp.min(jnp.where(rest == m2, idx, ne), axis=-1, keepdims=True)
        sel2 = idx == i2
        e = jnp.exp(m2 - m1)
        p1 = 1.0 / (1.0 + e)
        p2 = e / (1.0 + e)

        chosen = jnp.where(sel1, 1.0, jnp.where(sel2, 1.0, 0.0))
        before = _dot(tri[...], chosen.astype(BF16)) + carry[...]
        r1 = jnp.sum(jnp.where(sel1, before, 0.0), axis=-1, keepdims=True)
        r2 = jnp.sum(jnp.where(sel2, before, 0.0), axis=-1, keepdims=True)
        carry[...] += jnp.sum(chosen, axis=0, keepdims=True)
        cnt_ref[...] = carry[...]

        cols = (i1.astype(F32), i2.astype(F32), r1, r2, p1, p2)
        route = jnp.zeros(logits.shape, F32)
        for k, col in enumerate(cols):
            route = jnp.where(idx == k, col, route)
        route_ref[...] = route


def _router(x2d, sh, sc, g, router_w, base_cnt, tile, rows_per_mod, h_rows, h_prev=None,
            row_off=0):
    n, d = x2d.shape
    ne = router_w.shape[-1]
    assert ne == ROUTE_COLS and n % tile == 0 and row_off % tile == 0
    n_steps = n // tile
    tail = h_rows - (row_off + n)
    assert 0 <= tail < tile
    fill_tail = h_prev is None and tail > 0
    last = n_steps - 1
    clamp = (lambda i: jnp.minimum(i, last)) if fill_tail else (lambda i: i)
    per_row_mod = sh.shape[1] > 1
    if per_row_mod:
        mod_spec = pl.BlockSpec((1, tile, d), lambda i: (0, clamp(i), 0))
    else:
        assert rows_per_mod % tile == 0
        mod_spec = pl.BlockSpec((1, 1, d), lambda i: (clamp(i) // (rows_per_mod // tile), 0, 0))
    in_specs = [
        pl.BlockSpec((tile, d), lambda i: (clamp(i), 0)),
        mod_spec,
        mod_spec,
        _resident((1, d)),
        _resident(router_w.shape),
        _resident((1, ne)),
    ]
    args = [x2d, sh, sc, g.reshape(1, d), router_w, base_cnt]
    n_in = len(args)
    aliases = {}
    body = functools.partial(_router_kernel, n_steps=n_steps)
    if h_prev is not None:
        assert h_prev.shape == (h_rows, d)
        in_specs.append(pl.BlockSpec(memory_space=pl.ANY))
        args.append(h_prev)
        aliases = {n_in: 0}

        def body(*refs):
            _router_kernel(*refs[:n_in], *refs[n_in + 1:], n_steps=n_steps)

    blk_off = row_off // tile
    return pl.pallas_call(
        body,
        grid=(n_steps + int(fill_tail),),
        in_specs=in_specs,
        out_specs=[
            pl.BlockSpec((tile, d), lambda i: (i + blk_off, 0)),
            pl.BlockSpec((tile, ne), lambda i: (clamp(i), 0)),
            pl.BlockSpec((1, ne), lambda i: (0, 0)),
        ],
        out_shape=[
            jax.ShapeDtypeStruct((h_rows, d), F32),
            jax.ShapeDtypeStruct((n, ne), F32),
            jax.ShapeDtypeStruct((1, ne), F32),
        ],
        scratch_shapes=[pltpu.VMEM((tile, tile), BF16), pltpu.VMEM((1, ne), F32)],
        input_output_aliases=aliases,
        compiler_params=_params("arbitrary"),
        name="router",
    )(*args)


def _row_copy(src_hbm, row, dst, dst_row, sem):
    return pltpu.make_async_copy(src_hbm.at[pl.ds(row, 1), :], dst.at[pl.ds(dst_row, 1), :], sem)


def _row_copy_out(src, src_row, dst_hbm, row, sem):
    return pltpu.make_async_copy(src.at[pl.ds(src_row, 1), :], dst_hbm.at[pl.ds(row, 1), :], sem)


def _moe_grouped_kernel(te_ref, na_ref, src_ref, dst_ref, h_hbm, wg_ref, wu_ref, wd_ref, y_hbm,
                        xs, hbuf, obuf, gsem, ssem, fsem, *, tm, rows_per_step, fill_row,
                        fill_tiles):
    del te_ref
    i = pl.program_id(0)
    c = pl.program_id(1)
    n_active = na_ref[0]
    ahead = ROW_SLOTS - 1
    slot = i % ROW_SLOTS

    def request(tile, r):
        s = tile % ROW_SLOTS
        return _row_copy(h_hbm, src_ref[tile * tm + r], xs.at[s], r, gsem.at[s])

    def send(tile, r):
        s = (tile + ROW_SLOTS) % ROW_SLOTS
        return _row_copy_out(obuf.at[s], r, y_hbm, dst_ref[(tile + 1) * tm + r], ssem.at[s])

    @pl.when(jnp.logical_and(i == 0, c == 0))
    def _():
        last = ROW_SLOTS - 1
        obuf[last] = jnp.zeros((tm, obuf.shape[-1]), F32)
        fills = [pltpu.make_async_copy(obuf.at[last], y_hbm.at[pl.ds(fill_row + q * tm, tm), :],
                                       fsem) for q in range(fill_tiles)]
        for f in fills:
            f.start()
        for f in fills:
            f.wait()

        def body(r, carry):
            for t in range(ahead):
                request(t, r).start()
            return carry

        lax.fori_loop(0, tm, body, 0)

    @pl.when(jnp.logical_and(i < n_active + ahead, c == 0))
    def _():
        pltpu.make_async_copy(h_hbm.at[pl.ds(0, tm), :], xs.at[slot], gsem.at[slot]).wait()

    @pl.when(jnp.logical_and(jnp.logical_and(i >= ahead, i < n_active + ROW_SLOTS), c == 0))
    def _():
        pltpu.make_async_copy(obuf.at[slot], y_hbm.at[pl.ds(0, tm), :], ssem.at[slot]).wait()

    @pl.when(jnp.logical_and(i < n_active, c == 0))
    def _():
        obuf[slot] = jnp.zeros((tm, obuf.shape[-1]), F32)
        hbuf[...] = xs[slot].astype(BF16)

    @pl.when(jnp.logical_and(i == n_active, c == 0))
    def _():
        def body(r, carry):
            send(i - 1, r).start()
            return carry

        lax.fori_loop(0, tm, body, 0)

    @pl.when(i < n_active)
    def _():
        for k in range(rows_per_step):
            r = c * rows_per_step + k
            request(i + ahead, r).start()
            send(i - 1, r).start()

        h = hbuf[...]
        a = _dot(h, wg_ref[0].astype(BF16))
        b = _dot(h, wu_ref[0].astype(BF16))
        obuf[slot] += _dot(_silu_mul(a, b).astype(BF16), wd_ref[0].astype(BF16))


def _moe_grouped(h_all, tile_expert, n_active, src, dst, wg, wu, wd, tm, fc, y_rows, fill_row,
                 fill_tiles):
    ne, d, dff = wg.shape
    n_tiles = tile_expert.shape[0]
    nc = dff // fc
    assert dff % fc == 0 and tm % nc == 0
    assert src.shape[0] == n_tiles * tm and dst.shape[0] == n_tiles * tm
    rows_per_step = tm // nc

    def w_col(i, c, te, na, src_, dst_):
        return (te[i], 0, jnp.where(i < na[0], c, nc - 1))

    def w_row(i, c, te, na, src_, dst_):
        return (te[i], jnp.where(i < na[0], c, nc - 1), 0)

    grid_spec = pltpu.PrefetchScalarGridSpec(
        num_scalar_prefetch=4,
        grid=(n_tiles, nc),
        in_specs=[
            pl.BlockSpec(memory_space=pl.ANY),
            pl.BlockSpec((1, d, fc), w_col),
            pl.BlockSpec((1, d, fc), w_col),
            pl.BlockSpec((1, fc, d), w_row),
        ],
        out_specs=pl.BlockSpec(memory_space=pl.ANY),
        scratch_shapes=[
            pltpu.VMEM((ROW_SLOTS, tm, d), F32),
            pltpu.VMEM((tm, d), BF16),
            pltpu.VMEM((ROW_SLOTS, tm, d), F32),
            pltpu.SemaphoreType.DMA((ROW_SLOTS,)),
            pltpu.SemaphoreType.DMA((ROW_SLOTS,)),
            pltpu.SemaphoreType.DMA,
        ],
    )
    return pl.pallas_call(
        functools.partial(_moe_grouped_kernel, tm=tm, rows_per_step=rows_per_step,
                          fill_row=fill_row, fill_tiles=fill_tiles),
        grid_spec=grid_spec,
        out_shape=jax.ShapeDtypeStruct((y_rows, d), F32),
        compiler_params=_params("arbitrary", "arbitrary"),
        name="moe_grouped",
    )(tile_expert, n_active, src, dst, h_all, wg, wu, wd)


def _combine_kernel(x_ref, gt_ref, route_ref, gf_ref, y0_ref, y1_ref, o_ref):
    route = route_ref[...]
    f = (route[:, COL_PROB:COL_PROB + 1] * y0_ref[...]
         + route[:, COL_PROB + 1:COL_PROB + 2] * y1_ref[...])
    o_ref[...] = _rms(x_ref[...] + gt_ref[0] * f, gf_ref[...])


def _combine(x2d, gt, route, g_final, y_tok, tile, rows_per_mod, tok_off, plane_rows):
    n, d = x2d.shape
    assert n % tile == 0 and tok_off % tile == 0 and plane_rows % tile == 0
    per_row_mod = gt.shape[1] > 1
    if per_row_mod:
        mod_spec = pl.BlockSpec((1, tile, d), lambda j: (0, j, 0))
    else:
        assert rows_per_mod % tile == 0
        mod_spec = pl.BlockSpec((1, 1, d), lambda j: (j // (rows_per_mod // tile), 0, 0))
    first = tok_off // tile
    second = (plane_rows + tok_off) // tile
    return pl.pallas_call(
        _combine_kernel,
        grid=(n // tile,),
        in_specs=[
            pl.BlockSpec((tile, d), lambda j: (j, 0)),
            mod_spec,
            pl.BlockSpec((tile, ROUTE_COLS), lambda j: (j, 0)),
            pl.BlockSpec((1, d), lambda j: (0, 0)),
            pl.BlockSpec((tile, d), lambda j: (first + j, 0)),
            pl.BlockSpec((tile, d), lambda j: (second + j, 0)),
        ],
        out_specs=pl.BlockSpec((tile, d), lambda j: (j, 0)),
        out_shape=jax.ShapeDtypeStruct((n, d), F32),
        compiler_params=_params("arbitrary"),
        name="combine",
    )(x2d, gt, route, g_final.reshape(1, d), y_tok, y_tok)


def _invert_kernel(pos_ref, inv_ref):
    def clear(r, carry):
        inv_ref[r] = 0
        return carry

    lax.fori_loop(0, inv_ref.shape[0], clear, 0, unroll=32)

    def place(a, carry):
        inv_ref[pos_ref[a]] = a
        return carry

    lax.fori_loop(0, pos_ref.shape[0], place, 0, unroll=32)


def _invert(pos_flat, n_rows):
    assert pos_flat.shape[0] % 8 == 0 and n_rows % 8 == 0
    smem = pl.BlockSpec(memory_space=pltpu.SMEM)
    return pl.pallas_call(
        _invert_kernel,
        in_specs=[smem],
        out_specs=smem,
        out_shape=jax.ShapeDtypeStruct((n_rows,), I32),
        name="invert",
    )(pos_flat)


def _routing_tables(route_all, counts, tm, n_tiles, plane_rows):
    ne = counts.shape[0]
    n = route_all.shape[0]
    e_idx = route_all[:, COL_EXPERT:COL_EXPERT + TOP_K].astype(I32)
    rank = route_all[:, COL_RANK:COL_RANK + TOP_K].astype(I32)
    padded = (counts + tm - 1) // tm * tm
    ends = jnp.cumsum(padded)
    offs = ends - padded
    onehot = e_idx[..., None] == jnp.arange(ne, dtype=I32)
    pos = jnp.sum(jnp.where(onehot, offs, 0), axis=-1) + rank
    n_active = ends[-1] // tm
    starts = jnp.arange(n_tiles, dtype=I32) * tm
    tile_expert = jnp.sum(starts[:, None] >= ends[None, :], axis=1).astype(I32)
    last_expert = jnp.sum((n_active - 1) * tm >= ends).astype(I32)
    tile_expert = jnp.minimum(tile_expert, last_expert)
    n_rows = n_tiles * tm
    inv = _invert(pos.reshape(-1), n_rows)
    r = jnp.arange(n_rows, dtype=I32)[:, None]
    real_before = jnp.sum(jnp.clip(r - offs, 0, counts), axis=1)
    is_real = jnp.any(jnp.logical_and(r >= offs, r < offs + counts), axis=1)
    tok = inv // TOP_K
    choice = inv % TOP_K
    spare = n + jnp.minimum(r[:, 0] - real_before, ne * tm - 1)
    src = jnp.where(is_real, tok, 0)
    dst = jnp.where(is_real, choice * plane_rows + tok, spare)
    lead = n + ne * tm + jnp.arange(tm, dtype=I32)
    dst = jnp.concatenate([lead, dst])[:n_rows]
    return src, dst, tile_expert, n_active.astype(I32).reshape(1)


def _moe_layer(xp, xs, mod_p, mod_s, g, router_w, wg, wu, wd, g_final):
    nb, seq, d = xp.shape
    ns = xs.shape[0]
    ne = router_w.shape[-1]
    n_all = nb * seq + ns
    sh_p, sc_p, gt_p = mod_p
    sh_s, sc_s, gt_s = mod_s
    xp2 = xp.reshape(nb * seq, d)

    zero_cnt = jnp.zeros((1, ne), F32)
    h_all, route_p, cnt_p = _router(xp2, sh_p, sc_p, g, router_w, zero_cnt, ROUTER_TILE, seq,
                                    n_all)
    h_all, route_s, cnt_s = _router(xs, sh_s, sc_s, g, router_w, cnt_p, ns, ns, n_all,
                                    h_prev=h_all, row_off=nb * seq)

    tm = MOE_ROW_TILE
    n_tiles = (n_all * TOP_K) // tm + ne + ROW_SLOTS
    fill_tiles = ne + 1
    plane_rows = n_all + fill_tiles * tm
    route_all = jnp.concatenate([route_p, route_s], axis=0)
    src, dst, tile_expert, n_active = _routing_tables(
        route_all, cnt_s[0].astype(I32), tm, n_tiles, plane_rows)

    y_tok = _moe_grouped(h_all, tile_expert, n_active, src, dst, wg, wu, wd, tm, MOE_FF_CHUNK,
                         plane_rows + n_all, n_all, fill_tiles)
    yp = _combine(xp2, gt_p, route_p, g_final, y_tok, COMBINE_TILE, seq, 0, plane_rows)
    ys = _combine(xs, gt_s, route_s, g_final, y_tok, ns, ns, nb * seq, plane_rows)
    return yp.reshape(nb, seq, d), ys


def kernel(x_prompt, x_sample, c_prompt, c_sample, state_conv, state_pool, w_ada, b_ada, g_mix,
           g_ffn, w_in, conv_w, pool_w, pool_scale, g_conv_out, g_pool_out, w_out, dense_w_gate,
           dense_w_up, dense_w_down, router_w, moe_w_gate, moe_w_up, moe_w_down, g_final):
    depth = w_ada.shape[0]
    nb, seq, d = x_prompt.shape
    ns = x_sample.shape[0]
    assert x_sample.shape[1] == 1 and depth == 2
    assert sum(DENSE_FF_CHUNKS) == dense_w_gate.shape[-1]

    mod = _ada(jnp.concatenate([c_prompt, c_sample], axis=0), w_ada, b_ada)
    mod = mod.reshape(depth, nb + ns, 6, d)
    mod_p = [jnp.transpose(mod[i, :nb], (1, 0, 2))[:, :, None, :] for i in range(depth)]
    mod_s = [jnp.transpose(mod[i, nb:], (1, 0, 2))[:, None, :, :] for i in range(depth)]

    w_in_b = w_in.astype(BF16)
    w_out_b = w_out.astype(BF16)
    pool_w_b = pool_w.astype(BF16)
    dense_b = [w.astype(BF16) for w in (dense_w_gate, dense_w_up, dense_w_down)]
    cb_t = jnp.transpose(state_conv, (0, 2, 1, 3))
    pb_t = jnp.transpose(state_pool, (0, 2, 1, 3))

    xp = x_prompt
    xs = x_sample.reshape(ns, d)
    conv_p, pool_p, conv_s, pool_s = [], [], [], []
    for i in range(depth):
        sh1, sc1, gt1, sh2, sc2, gt2 = mod_p[i]
        mix_w = (g_mix[i], w_in_b[i], conv_w[i], pool_w_b[i], pool_scale[i], g_conv_out[i],
                 g_pool_out[i], w_out_b[i])
        xp, cs, ps = _mix_prompt(xp, sh1, sc1, gt1, *mix_w, tile=MIX_TILE, sub=MIX_ROW_BLOCK)
        conv_p.append(cs)
        pool_p.append(ps)
        s1, c1, t1, s2, c2, t2 = mod_s[i]
        xs, v_new, u_new = _mix_sample(xs, s1[0], c1[0], t1[0], *mix_w, cb_t[i], pb_t[i])
        conv_s.append(jnp.concatenate([state_conv[i][:, 1:], v_new[:, None, :]], axis=1))
        pool_s.append(jnp.concatenate([state_pool[i][:, 1:], u_new[:, None, :]], axis=1))
        j = i // 2
        if i % 2 == 0:
            wg, wu, wd = (w[j] for w in dense_b)
            xp = _ffn(xp, sh2, sc2, gt2, g_ffn[i], wg, wu, wd, PROMPT_TILE, DENSE_FF_CHUNKS)
            xs = _ffn(xs[None], s2, c2, t2, g_ffn[i], wg, wu, wd, ns, DENSE_FF_CHUNKS)[0]
        else:
            xp, xs = _moe_layer(xp, xs, (sh2, sc2, gt2), (s2, c2, t2), g_ffn[i], router_w[j],
                                moe_w_gate[j], moe_w_up[j], moe_w_down[j], g_final)

    return (xp, xs.reshape(ns, 1, d), jnp.stack(conv_p), jnp.stack(pool_p),
            jnp.stack(conv_s), jnp.stack(pool_s))
```

```python
import functools

import jax
import jax.numpy as jnp
from jax import lax
from jax.experimental import pallas as pl
from jax.experimental.pallas import tpu as pltpu

F32 = jnp.float32
BF16 = jnp.bfloat16
I32 = jnp.int32

EPS = 1e-6
CONV_K = 3
POOL_WINDOWS = (2, 4, 8, 16)
POOL_HIST = max(POOL_WINDOWS) - 1
TOP_K = 2

CONV_PAD = 8
POOL_PAD = 16

LANES = 128

ROUTE_ROWS = 8
ROW_EXPERT, ROW_RANK, ROW_PROB = 0, 2, 4

VMEM_LIMIT_BYTES = 56 * 1024 * 1024

PROMPT_TILE = 512
MIX_TILE = 1024
MIX_ROW_BLOCK = 512
ROUTER_TILE = 1024
COMBINE_TILE = 512
MOE_ROW_TILE = 896
MOE_FF_CHUNK = 512
ROW_SLOTS = 3
DENSE_FF_CHUNKS = (768, 768, 768, 512)


def _params(*sem):
    return pltpu.CompilerParams(dimension_semantics=sem, vmem_limit_bytes=VMEM_LIMIT_BYTES)


def _resident(shape):
    nd = len(shape)
    return pl.BlockSpec(shape, lambda *_: (0,) * nd, pipeline_mode=pl.Buffered(1))


def _rms(x, g):
    ms = jnp.mean(x * x, axis=-1, keepdims=True)
    return x * lax.rsqrt(ms + EPS) * g


def _mod_norm(x, g, sc, sh):
    ms = jnp.mean(x * x, axis=-1, keepdims=True)
    return x * lax.rsqrt(ms + EPS) * (g * (1.0 + sc)) + sh


def _dot(a, b):
    return jnp.dot(a, b, preferred_element_type=F32)


def _dot_nt(a, b):
    return lax.dot_general(a, b, (((1,), (1,)), ((), ())), preferred_element_type=F32)


def _silu_mul(a, b):
    return a * jax.nn.sigmoid(a) * b


def _ada_kernel(c_ref, w_ref, b_ref, o_ref):
    c = c_ref[...]
    a = (c * jax.nn.sigmoid(c)).astype(BF16)
    o_ref[0] = _dot(a, w_ref[0].astype(BF16)) + b_ref[0]


def _ada(c_all, w_ada, b_ada):
    depth, d, n = w_ada.shape
    m = c_all.shape[0]
    tn = 1024
    return pl.pallas_call(
        _ada_kernel,
        grid=(depth, n // tn),
        in_specs=[
            pl.BlockSpec((m, d), lambda i, j: (0, 0)),
            pl.BlockSpec((1, d, tn), lambda i, j: (i, 0, j)),
            pl.BlockSpec((1, 1, tn), lambda i, j: (i, 0, j)),
        ],
        out_specs=pl.BlockSpec((1, m, tn), lambda i, j: (i, 0, j)),
        out_shape=jax.ShapeDtypeStruct((depth, m, n), F32),
        compiler_params=_params("arbitrary", "arbitrary"),
        name="ada",
    )(c_all, w_ada, b_ada.reshape(depth, 1, n))


def _mix_tail(x, gt, bg, y, d_groups, poolw_ref, pscale_ref, gco_ref, gpo_ref, wout_ref):
    cw = gco_ref.shape[-1]
    ya = bg * y
    yb = jnp.concatenate(
        [_dot(d.astype(BF16), poolw_ref[g]) for g, d in enumerate(d_groups)], axis=-1
    ) * pscale_ref[...]
    ma = _rms(ya, gco_ref[...]).astype(BF16)
    mb = _rms(yb, gpo_ref[...]).astype(BF16)
    o = _dot(ma, wout_ref[0:cw, :]) + _dot(mb, wout_ref[cw:, :])
    return x + gt * o


def _mix_prompt_kernel(x_ref, sh_ref, sc_ref, gt_ref, g_ref, win_ref, convw_ref, poolw_ref,
                       pscale_ref, gco_ref, gpo_ref, wout_ref,
                       xo_ref, cs_ref, ps_ref, vbuf, ubuf, *, tile, sub):
    l = pl.program_id(1)
    cw = convw_ref.shape[-1]
    pg = poolw_ref.shape[-1]

    @pl.when(l == 0)
    def _():
        vbuf[0:CONV_PAD, :] = jnp.zeros((CONV_PAD, cw), F32)
        ubuf[0:POOL_PAD, :] = jnp.zeros((POOL_PAD, ubuf.shape[-1]), F32)

    gates = []
    for lo in range(0, tile, sub):
        x = x_ref[0, lo:lo + sub, :]
        h = _mod_norm(x, g_ref[...], sc_ref[0], sh_ref[0]).astype(BF16)
        p = _dot(h, win_ref[...])
        gates.append(p[:, 0:cw])
        vbuf[CONV_PAD + lo:CONV_PAD + lo + sub, :] = p[:, cw:2 * cw] * p[:, 2 * cw:3 * cw]
        ubuf[POOL_PAD + lo:POOL_PAD + lo + sub, :] = p[:, 3 * cw:]

    w = convw_ref[...]
    for bg, lo in zip(gates, range(0, tile, sub)):
        y = w[CONV_K - 1:CONV_K] * vbuf[CONV_PAD + lo:CONV_PAD + lo + sub, :]
        for k in range(1, CONV_K):
            y = y + w[CONV_K - 1 - k:CONV_K - k] * vbuf[CONV_PAD + lo - k:CONV_PAD + lo - k + sub, :]

        pos = l * tile + lo + lax.broadcasted_iota(I32, (sub, 1), 0)
        d_groups = []
        for g, win in enumerate(POOL_WINDOWS):
            ug = ubuf[POOL_PAD + lo:POOL_PAD + lo + sub, g * pg:(g + 1) * pg]
            acc = ug
            for k in range(1, win):
                acc = acc + ubuf[POOL_PAD + lo - k:POOL_PAD + lo - k + sub, g * pg:(g + 1) * pg]
            cnt = jnp.minimum(pos + 1, win).astype(F32)
            d_groups.append(acc / cnt - ug)

        xo_ref[0, lo:lo + sub, :] = _mix_tail(x_ref[0, lo:lo + sub, :], gt_ref[0], bg, y, d_groups,
                                              poolw_ref, pscale_ref, gco_ref, gpo_ref, wout_ref)

    cs_ref[0] = vbuf[CONV_PAD + tile - (CONV_K - 1):CONV_PAD + tile, :]
    ps_ref[0] = ubuf[POOL_PAD + tile - POOL_HIST:POOL_PAD + tile, :]
    vbuf[0:CONV_PAD, :] = vbuf[tile:tile + CONV_PAD, :]
    ubuf[0:POOL_PAD, :] = ubuf[tile:tile + POOL_PAD, :]


def _mix_prompt(x, sh, sc, gt, g, w_in, conv_w, pool_w, pool_scale, g_co, g_po, w_out, tile, sub):
    b, seq, d = x.shape
    assert seq % tile == 0 and tile % sub == 0 and sub >= POOL_PAD
    cw = conv_w.shape[-1]
    pw = pool_scale.shape[-1]
    row = lambda bi, li: (bi, 0, 0)
    return pl.pallas_call(
        functools.partial(_mix_prompt_kernel, tile=tile, sub=sub),
        grid=(b, seq // tile),
        in_specs=[
            pl.BlockSpec((1, tile, d), lambda bi, li: (bi, li, 0)),
            pl.BlockSpec((1, 1, d), row),
            pl.BlockSpec((1, 1, d), row),
            pl.BlockSpec((1, 1, d), row),
            _resident((1, d)),
            _resident(w_in.shape),
            _resident(conv_w.shape),
            _resident(pool_w.shape),
            _resident((1, pw)),
            _resident((1, cw)),
            _resident((1, pw)),
            _resident(w_out.shape),
        ],
        out_specs=[
            pl.BlockSpec((1, tile, d), lambda bi, li: (bi, li, 0)),
            pl.BlockSpec((1, CONV_K - 1, cw), row),
            pl.BlockSpec((1, POOL_HIST, pw), row),
        ],
        out_shape=[
            jax.ShapeDtypeStruct((b, seq, d), F32),
            jax.ShapeDtypeStruct((b, CONV_K - 1, cw), F32),
            jax.ShapeDtypeStruct((b, POOL_HIST, pw), F32),
        ],
        scratch_shapes=[
            pltpu.VMEM((CONV_PAD + tile, cw), F32),
            pltpu.VMEM((POOL_PAD + tile, pw), F32),
        ],
        compiler_params=_params("arbitrary", "arbitrary"),
        name="mix_prompt",
    )(x, sh, sc, gt, g.reshape(1, d), w_in, conv_w, pool_w, pool_scale.reshape(1, pw),
      g_co.reshape(1, cw), g_po.reshape(1, pw), w_out)


def _mix_sample_kernel(x_ref, sh_ref, sc_ref, gt_ref, g_ref, win_ref, convw_ref, poolw_ref,
                       pscale_ref, gco_ref, gpo_ref, wout_ref, cb_ref, pb_ref,
                       xo_ref, v_ref, u_ref):
    cw = convw_ref.shape[-1]
    pg = poolw_ref.shape[-1]
    x = x_ref[...]
    h = _mod_norm(x, g_ref[...], sc_ref[...], sh_ref[...]).astype(BF16)
    p = _dot(h, win_ref[...])
    bg = p[:, 0:cw]
    v = p[:, cw:2 * cw] * p[:, 2 * cw:3 * cw]
    u = p[:, 3 * cw:]
    v_ref[...] = v
    u_ref[...] = u

    w = convw_ref[...]
    y = w[CONV_K - 1:CONV_K] * v
    for k in range(1, CONV_K):
        y = y + w[CONV_K - 1 - k:CONV_K - k] * cb_ref[CONV_K - 1 - k]

    d_groups = []
    for g, win in enumerate(POOL_WINDOWS):
        ug = u[:, g * pg:(g + 1) * pg]
        acc = ug
        for k in range(1, win):
            acc = acc + pb_ref[POOL_HIST - k, :, g * pg:(g + 1) * pg]
        d_groups.append(acc / float(win) - ug)

    xo_ref[...] = _mix_tail(x, gt_ref[...], bg, y, d_groups, poolw_ref, pscale_ref,
                            gco_ref, gpo_ref, wout_ref)


def _mix_sample(x, sh, sc, gt, g, w_in, conv_w, pool_w, pool_scale, g_co, g_po, w_out, cb, pb):
    n, d = x.shape
    cw = conv_w.shape[-1]
    pw = pool_scale.shape[-1]
    return pl.pallas_call(
        _mix_sample_kernel,
        out_shape=[
            jax.ShapeDtypeStruct((n, d), F32),
            jax.ShapeDtypeStruct((n, cw), F32),
            jax.ShapeDtypeStruct((n, pw), F32),
        ],
        compiler_params=pltpu.CompilerParams(vmem_limit_bytes=VMEM_LIMIT_BYTES),
        name="mix_sample",
    )(x, sh, sc, gt, g.reshape(1, d), w_in, conv_w, pool_w, pool_scale.reshape(1, pw),
      g_co.reshape(1, cw), g_po.reshape(1, pw), w_out, cb, pb)


def _ffn_kernel(x_ref, sh_ref, sc_ref, gt_ref, g_ref, wg_ref, wu_ref, wd_ref, o_ref, *, chunks):
    x = x_ref[0]
    h = _mod_norm(x, g_ref[...], sc_ref[0], sh_ref[0]).astype(BF16)
    acc = None
    lo = 0
    for fc in chunks:
        a = _dot(h, wg_ref[:, lo:lo + fc])
        b = _dot(h, wu_ref[:, lo:lo + fc])
        part = _dot(_silu_mul(a, b).astype(BF16), wd_ref[lo:lo + fc, :])
        acc = part if acc is None else acc + part
        lo += fc
    o_ref[0] = x + gt_ref[0] * acc


def _ffn(x, sh, sc, gt, g, wg, wu, wd, tile, chunks):
    b, seq, d = x.shape
    tm = sh.shape[1]
    tmod = tile if tm > 1 else 1
    mod_map = (lambda bi, li: (bi, li, 0)) if tm > 1 else (lambda bi, li: (bi, 0, 0))
    return pl.pallas_call(
        functools.partial(_ffn_kernel, chunks=chunks),
        grid=(b, seq // tile),
        in_specs=[
            pl.BlockSpec((1, tile, d), lambda bi, li: (bi, li, 0)),
            pl.BlockSpec((1, tmod, d), mod_map),
            pl.BlockSpec((1, tmod, d), mod_map),
            pl.BlockSpec((1, tmod, d), mod_map),
            _resident((1, d)),
            _resident(wg.shape),
            _resident(wu.shape),
            _resident(wd.shape),
        ],
        out_specs=pl.BlockSpec((1, tile, d), lambda bi, li: (bi, li, 0)),
        out_shape=jax.ShapeDtypeStruct((b, seq, d), F32),
        compiler_params=_params("arbitrary", "arbitrary"),
        name="ffn",
    )(x, sh, sc, gt, g.reshape(1, d), wg, wu, wd)


def _split_bf16(a):
    hi = a.astype(BF16)
    return hi, (a - hi.astype(F32)).astype(BF16)


def _router_kernel(x_ref, sh_ref, sc_ref, g_ref, rw_ref, base_ref, h_ref, route_ref, cnt_ref,
                   tri, carry, *, n_steps):
    t = x_ref.shape[0]
    step = pl.program_id(0)

    @pl.when(step == 0)
    def _():
        r = lax.broadcasted_iota(I32, (t, t), 0)
        c = lax.broadcasted_iota(I32, (t, t), 1)
        tri[...] = jnp.where(r < c, 1.0, 0.0).astype(BF16)
        carry[...] = base_ref[...]

    @pl.when(step >= n_steps)
    def _():
        h_ref[...] = jnp.zeros_like(h_ref)

    @pl.when(step < n_steps)
    def _():
        h = _mod_norm(x_ref[...], g_ref[...], sc_ref[0], sh_ref[0])
        h_ref[...] = h
        h_hi, h_lo = _split_bf16(h)
        rw_hi, rw_lo = _split_bf16(rw_ref[...])
        logits = _dot_nt(rw_hi, h_hi) + _dot_nt(rw_hi, h_lo) + _dot_nt(rw_lo, h_hi)

        ne = logits.shape[0]
        idx = lax.broadcasted_iota(I32, logits.shape, 0)
        m1 = jnp.max(logits, axis=0, keepdims=True)
        i1 = jnp.min(jnp.where(logits == m1, idx, ne), axis=0, keepdims=True)
        sel1 = idx == i1
        rest = jnp.where(sel1, -jnp.inf, logits)
        m2 = jnp.max(rest, axis=0, keepdims=True)
        i2 = jnp.min(jnp.where(rest == m2, idx, ne), axis=0, keepdims=True)
        sel2 = idx == i2
        e = jnp.exp(m2 - m1)
        p1 = 1.0 / (1.0 + e)
        p2 = e / (1.0 + e)

        chosen = jnp.where(sel1, 1.0, jnp.where(sel2, 1.0, 0.0))
        before = _dot(chosen.astype(BF16), tri[...]) + carry[:, 0:1]
        r1 = jnp.sum(jnp.where(sel1, before, 0.0), axis=0, keepdims=True)
        r2 = jnp.sum(jnp.where(sel2, before, 0.0), axis=0, keepdims=True)
        carry[...] += jnp.sum(chosen, axis=1, keepdims=True)
        cnt_ref[...] = carry[...]

        rows = (i1.astype(F32), i2.astype(F32), r1, r2, p1, p2)
        route = jnp.zeros(logits.shape, F32)
        for k, row in enumerate(rows):
            route = jnp.where(idx == k, row, route)
        route_ref[...] = route


def _router(x2d, sh, sc, g, router_w, base_cnt, tile, rows_per_mod, h_rows, h_prev=None,
            row_off=0):
    n, d = x2d.shape
    ne = router_w.shape[-1]
    assert ne == ROUTE_ROWS and n % tile == 0 and row_off % tile == 0 and tile % LANES == 0
    assert base_cnt.shape == (ne, LANES)
    n_steps = n // tile
    tail = h_rows - (row_off + n)
    assert 0 <= tail < tile
    fill_tail = h_prev is None and tail > 0
    last = n_steps - 1
    clamp = (lambda i: jnp.minimum(i, last)) if fill_tail else (lambda i: i)
    per_row_mod = sh.shape[1] > 1
    if per_row_mod:
        mod_spec = pl.BlockSpec((1, tile, d), lambda i: (0, clamp(i), 0))
    else:
        assert rows_per_mod % tile == 0
        mod_spec = pl.BlockSpec((1, 1, d), lambda i: (clamp(i) // (rows_per_mod // tile), 0, 0))
    in_specs = [
        pl.BlockSpec((tile, d), lambda i: (clamp(i), 0)),
        mod_spec,
        mod_spec,
        _resident((1, d)),
        _resident((ne, d)),
        _resident((ne, LANES)),
    ]
    args = [x2d, sh, sc, g.reshape(1, d), router_w.T, base_cnt]
    n_in = len(args)
    aliases = {}
    body = functools.partial(_router_kernel, n_steps=n_steps)
    if h_prev is not None:
        assert h_prev.shape == (h_rows, d)
        in_specs.append(pl.BlockSpec(memory_space=pl.ANY))
        args.append(h_prev)
        aliases = {n_in: 0}

        def body(*refs):
            _router_kernel(*refs[:n_in], *refs[n_in + 1:], n_steps=n_steps)

    blk_off = row_off // tile
    return pl.pallas_call(
        body,
        grid=(n_steps + int(fill_tail),),
        in_specs=in_specs,
        out_specs=[
            pl.BlockSpec((tile, d), lambda i: (i + blk_off, 0)),
            pl.BlockSpec((ne, tile), lambda i: (0, clamp(i))),
            pl.BlockSpec((ne, LANES), lambda i: (0, 0)),
        ],
        out_shape=[
            jax.ShapeDtypeStruct((h_rows, d), F32),
            jax.ShapeDtypeStruct((ne, n), F32),
            jax.ShapeDtypeStruct((ne, LANES), F32),
        ],
        scratch_shapes=[pltpu.VMEM((tile, tile), BF16), pltpu.VMEM((ne, LANES), F32)],
        input_output_aliases=aliases,
        compiler_params=_params("arbitrary"),
        name="router",
    )(*args)


def _row_copy(src_hbm, row, dst, dst_row, sem):
    return pltpu.make_async_copy(src_hbm.at[pl.ds(row, 1), :], dst.at[pl.ds(dst_row, 1), :], sem)


def _row_copy_out(src, src_row, dst_hbm, row, sem):
    return pltpu.make_async_copy(src.at[pl.ds(src_row, 1), :], dst_hbm.at[pl.ds(row, 1), :], sem)


def _moe_grouped_kernel(te_ref, na_ref, src_ref, dst_ref, h_hbm, wg_ref, wu_ref, wd_ref, y_hbm,
                        xs, hbuf, obuf, gsem, ssem, fsem, *, tm, rows_per_step, fill_row,
                        fill_tiles):
    del te_ref
    i = pl.program_id(0)
    c = pl.program_id(1)
    n_active = na_ref[0]
    ahead = ROW_SLOTS - 1
    slot = i % ROW_SLOTS

    def request(tile, r):
        s = tile % ROW_SLOTS
        return _row_copy(h_hbm, src_ref[tile * tm + r], xs.at[s], r, gsem.at[s])

    def send(tile, r):
        s = (tile + ROW_SLOTS) % ROW_SLOTS
        return _row_copy_out(obuf.at[s], r, y_hbm, dst_ref[(tile + 1) * tm + r], ssem.at[s])

    @pl.when(jnp.logical_and(i == 0, c == 0))
    def _():
        last = ROW_SLOTS - 1
        obuf[last] = jnp.zeros((tm, obuf.shape[-1]), F32)
        fills = [pltpu.make_async_copy(obuf.at[last], y_hbm.at[pl.ds(fill_row + q * tm, tm), :],
                                       fsem) for q in range(fill_tiles)]
        for f in fills:
            f.start()
        for f in fills:
            f.wait()

        def body(r, carry):
            for t in range(ahead):
                request(t, r).start()
            return carry

        lax.fori_loop(0, tm, body, 0)

    @pl.when(jnp.logical_and(i < n_active + ahead, c == 0))
    def _():
        pltpu.make_async_copy(h_hbm.at[pl.ds(0, tm), :], xs.at[slot], gsem.at[slot]).wait()

    @pl.when(jnp.logical_and(jnp.logical_and(i >= ahead, i < n_active + ROW_SLOTS), c == 0))
    def _():
        pltpu.make_async_copy(obuf.at[slot], y_hbm.at[pl.ds(0, tm), :], ssem.at[slot]).wait()

    @pl.when(jnp.logical_and(i < n_active, c == 0))
    def _():
        obuf[slot] = jnp.zeros((tm, obuf.shape[-1]), F32)
        hbuf[...] = xs[slot].astype(BF16)

    @pl.when(jnp.logical_and(i == n_active, c == 0))
    def _():
        def body(r, carry):
            send(i - 1, r).start()
            return carry

        lax.fori_loop(0, tm, body, 0)

    @pl.when(i < n_active)
    def _():
        for k in range(rows_per_step):
            r = c * rows_per_step + k
            request(i + ahead, r).start()
            send(i - 1, r).start()

        h = hbuf[...]
        a = _dot(h, wg_ref[0].astype(BF16))
        b = _dot(h, wu_ref[0].astype(BF16))
        obuf[slot] += _dot(_silu_mul(a, b).astype(BF16), wd_ref[0].astype(BF16))


def _moe_grouped(h_all, tile_expert, n_active, src, dst, wg, wu, wd, tm, fc, y_rows, fill_row,
                 fill_tiles):
    ne, d, dff = wg.shape
    n_tiles = tile_expert.shape[0]
    nc = dff // fc
    assert dff % fc == 0 and tm % nc == 0
    assert src.shape[0] == n_tiles * tm and dst.shape[0] == n_tiles * tm
    rows_per_step = tm // nc

    def w_col(i, c, te, na, src_, dst_):
        return (te[i], 0, jnp.where(i < na[0], c, nc - 1))

    def w_row(i, c, te, na, src_, dst_):
        return (te[i], jnp.where(i < na[0], c, nc - 1), 0)

    grid_spec = pltpu.PrefetchScalarGridSpec(
        num_scalar_prefetch=4,
        grid=(n_tiles, nc),
        in_specs=[
            pl.BlockSpec(memory_space=pl.ANY),
            pl.BlockSpec((1, d, fc), w_col),
            pl.BlockSpec((1, d, fc), w_col),
            pl.BlockSpec((1, fc, d), w_row),
        ],
        out_specs=pl.BlockSpec(memory_space=pl.ANY),
        scratch_shapes=[
            pltpu.VMEM((ROW_SLOTS, tm, d), F32),
            pltpu.VMEM((tm, d), BF16),
            pltpu.VMEM((ROW_SLOTS, tm, d), F32),
            pltpu.SemaphoreType.DMA((ROW_SLOTS,)),
            pltpu.SemaphoreType.DMA((ROW_SLOTS,)),
            pltpu.SemaphoreType.DMA,
        ],
    )
    return pl.pallas_call(
        functools.partial(_moe_grouped_kernel, tm=tm, rows_per_step=rows_per_step,
                          fill_row=fill_row, fill_tiles=fill_tiles),
        grid_spec=grid_spec,
        out_shape=jax.ShapeDtypeStruct((y_rows, d), F32),
        compiler_params=_params("arbitrary", "arbitrary"),
        name="moe_grouped",
    )(tile_expert, n_active, src, dst, h_all, wg, wu, wd)


def _combine_kernel(x_ref, gt_ref, route_ref, gf_ref, y0_ref, y1_ref, o_ref):
    route = route_ref[...]
    pad = jnp.zeros((LANES - route.shape[0], route.shape[1]), F32)
    cols = jnp.concatenate([route, pad], axis=0).T
    f = (cols[:, ROW_PROB:ROW_PROB + 1] * y0_ref[...]
         + cols[:, ROW_PROB + 1:ROW_PROB + 2] * y1_ref[...])
    o_ref[...] = _rms(x_ref[...] + gt_ref[0] * f, gf_ref[...])


def _combine(x2d, gt, route, g_final, y_tok, tile, rows_per_mod, tok_off, plane_rows):
    n, d = x2d.shape
    assert n % tile == 0 and tok_off % tile == 0 and plane_rows % tile == 0
    per_row_mod = gt.shape[1] > 1
    if per_row_mod:
        mod_spec = pl.BlockSpec((1, tile, d), lambda j: (0, j, 0))
    else:
        assert rows_per_mod % tile == 0
        mod_spec = pl.BlockSpec((1, 1, d), lambda j: (j // (rows_per_mod // tile), 0, 0))
    first = tok_off // tile
    second = (plane_rows + tok_off) // tile
    return pl.pallas_call(
        _combine_kernel,
        grid=(n // tile,),
        in_specs=[
            pl.BlockSpec((tile, d), lambda j: (j, 0)),
            mod_spec,
            pl.BlockSpec((ROUTE_ROWS, tile), lambda j: (0, j)),
            pl.BlockSpec((1, d), lambda j: (0, 0)),
            pl.BlockSpec((tile, d), lambda j: (first + j, 0)),
            pl.BlockSpec((tile, d), lambda j: (second + j, 0)),
        ],
        out_specs=pl.BlockSpec((tile, d), lambda j: (j, 0)),
        out_shape=jax.ShapeDtypeStruct((n, d), F32),
        compiler_params=_params("arbitrary"),
        name="combine",
    )(x2d, gt, route, g_final.reshape(1, d), y_tok, y_tok)


def _invert_kernel(pos_ref, inv_ref):
    def clear(r, carry):
        inv_ref[r] = 0
        return carry

    lax.fori_loop(0, inv_ref.shape[0], clear, 0, unroll=32)

    def place(a, carry):
        inv_ref[pos_ref[a]] = a
        return carry

    lax.fori_loop(0, pos_ref.shape[0], place, 0, unroll=32)


def _invert(pos_flat, n_rows):
    assert pos_flat.shape[0] % 32 == 0 and n_rows % 32 == 0
    smem = pl.BlockSpec(memory_space=pltpu.SMEM)
    return pl.pallas_call(
        _invert_kernel,
        in_specs=[smem],
        out_specs=smem,
        out_shape=jax.ShapeDtypeStruct((n_rows,), I32),
        name="invert",
    )(pos_flat)


def _routing_tables(route_all, counts, tm, n_tiles, plane_rows):
    ne = counts.shape[0]
    n = route_all.shape[1]
    e_idx = route_all[ROW_EXPERT:ROW_EXPERT + TOP_K].astype(I32)
    rank = route_all[ROW_RANK:ROW_RANK + TOP_K].astype(I32)
    padded = (counts + tm - 1) // tm * tm
    ends = jnp.cumsum(padded)
    offs = ends - padded
    onehot = e_idx[..., None] == jnp.arange(ne, dtype=I32)
    pos = jnp.sum(jnp.where(onehot, offs, 0), axis=-1) + rank
    n_active = ends[-1] // tm
    starts = jnp.arange(n_tiles, dtype=I32) * tm
    tile_expert = jnp.sum(starts[:, None] >= ends[None, :], axis=1).astype(I32)
    last_expert = jnp.sum((n_active - 1) * tm >= ends).astype(I32)
    tile_expert = jnp.minimum(tile_expert, last_expert)
    n_rows = n_tiles * tm
    inv = _invert(pos.reshape(-1), n_rows)
    r = jnp.arange(n_rows, dtype=I32)[:, None]
    real_before = jnp.sum(jnp.clip(r - offs, 0, counts), axis=1)
    is_real = jnp.any(jnp.logical_and(r >= offs, r < offs + counts), axis=1)
    tok = inv % n
    choice = inv // n
    spare = n + jnp.minimum(r[:, 0] - real_before, ne * tm - 1)
    src = jnp.where(is_real, tok, 0)
    dst = jnp.where(is_real, choice * plane_rows + tok, spare)
    lead = n + ne * tm + jnp.arange(tm, dtype=I32)
    dst = jnp.concatenate([lead, dst])[:n_rows]
    return src, dst, tile_expert, n_active.astype(I32).reshape(1)


def _moe_layer(xp, xs, mod_p, mod_s, g, router_w, wg, wu, wd, g_final):
    nb, seq, d = xp.shape
    ns = xs.shape[0]
    ne = router_w.shape[-1]
    n_all = nb * seq + ns
    sh_p, sc_p, gt_p = mod_p
    sh_s, sc_s, gt_s = mod_s
    xp2 = xp.reshape(nb * seq, d)

    zero_cnt = jnp.zeros((ne, LANES), F32)
    h_all, route_p, cnt_p = _router(xp2, sh_p, sc_p, g, router_w, zero_cnt, ROUTER_TILE, seq,
                                    n_all)
    h_all, route_s, cnt_s = _router(xs, sh_s, sc_s, g, router_w, cnt_p, ns, ns, n_all,
                                    h_prev=h_all, row_off=nb * seq)

    tm = MOE_ROW_TILE
    n_tiles = (n_all * TOP_K) // tm + ne + ROW_SLOTS
    fill_tiles = ne + 1
    plane_rows = n_all + fill_tiles * tm
    route_all = jnp.concatenate([route_p, route_s], axis=1)
    src, dst, tile_expert, n_active = _routing_tables(
        route_all, cnt_s[:, 0].astype(I32), tm, n_tiles, plane_rows)

    y_tok = _moe_grouped(h_all, tile_expert, n_active, src, dst, wg, wu, wd, tm, MOE_FF_CHUNK,
                         plane_rows + n_all, n_all, fill_tiles)
    yp = _combine(xp2, gt_p, route_p, g_final, y_tok, COMBINE_TILE, seq, 0, plane_rows)
    ys = _combine(xs, gt_s, route_s, g_final, y_tok, ns, ns, nb * seq, plane_rows)
    return yp.reshape(nb, seq, d), ys


def kernel(x_prompt, x_sample, c_prompt, c_sample, state_conv, state_pool, w_ada, b_ada, g_mix,
           g_ffn, w_in, conv_w, pool_w, pool_scale, g_conv_out, g_pool_out, w_out, dense_w_gate,
           dense_w_up, dense_w_down, router_w, moe_w_gate, moe_w_up, moe_w_down, g_final):
    depth = w_ada.shape[0]
    nb, seq, d = x_prompt.shape
    ns = x_sample.shape[0]
    assert x_sample.shape[1] == 1 and depth == 2
    assert sum(DENSE_FF_CHUNKS) == dense_w_gate.shape[-1]

    mod = _ada(jnp.concatenate([c_prompt, c_sample], axis=0), w_ada, b_ada)
    mod = mod.reshape(depth, nb + ns, 6, d)
    mod_p = [jnp.transpose(mod[i, :nb], (1, 0, 2))[:, :, None, :] for i in range(depth)]
    mod_s = [jnp.transpose(mod[i, nb:], (1, 0, 2))[:, None, :, :] for i in range(depth)]

    w_in_b = w_in.astype(BF16)
    w_out_b = w_out.astype(BF16)
    pool_w_b = pool_w.astype(BF16)
    dense_b = [w.astype(BF16) for w in (dense_w_gate, dense_w_up, dense_w_down)]
    cb_t = jnp.transpose(state_conv, (0, 2, 1, 3))
    pb_t = jnp.transpose(state_pool, (0, 2, 1, 3))

    xp = x_prompt
    xs = x_sample.reshape(ns, d)
    conv_p, pool_p, conv_s, pool_s = [], [], [], []
    for i in range(depth):
        sh1, sc1, gt1, sh2, sc2, gt2 = mod_p[i]
        mix_w = (g_mix[i], w_in_b[i], conv_w[i], pool_w_b[i], pool_scale[i], g_conv_out[i],
                 g_pool_out[i], w_out_b[i])
        xp, cs, ps = _mix_prompt(xp, sh1, sc1, gt1, *mix_w, tile=MIX_TILE, sub=MIX_ROW_BLOCK)
        conv_p.append(cs)
        pool_p.append(ps)
        s1, c1, t1, s2, c2, t2 = mod_s[i]
        xs, v_new, u_new = _mix_sample(xs, s1[0], c1[0], t1[0], *mix_w, cb_t[i], pb_t[i])
        conv_s.append(jnp.concatenate([state_conv[i][:, 1:], v_new[:, None, :]], axis=1))
        pool_s.append(jnp.concatenate([state_pool[i][:, 1:], u_new[:, None, :]], axis=1))
        j = i // 2
        if i % 2 == 0:
            wg, wu, wd = (w[j] for w in dense_b)
            xp = _ffn(xp, sh2, sc2, gt2, g_ffn[i], wg, wu, wd, PROMPT_TILE, DENSE_FF_CHUNKS)
            xs = _ffn(xs[None], s2, c2, t2, g_ffn[i], wg, wu, wd, ns, DENSE_FF_CHUNKS)[0]
        else:
            xp, xs = _moe_layer(xp, xs, (sh2, sc2, gt2), (s2, c2, t2), g_ffn[i], router_w[j],
                                moe_w_gate[j], moe_w_up[j], moe_w_down[j], g_final)

    return (xp, xs.reshape(ns, 1, d), jnp.stack(conv_p), jnp.stack(pool_p),
            jnp.stack(conv_s), jnp.stack(pool_s))
```

```python
import functools

import jax
import jax.numpy as jnp
from jax import lax
from jax.experimental import pallas as pl
from jax.experimental.pallas import tpu as pltpu

F32 = jnp.float32
BF16 = jnp.bfloat16
I32 = jnp.int32

EPS = 1e-6
CONV_K = 3
POOL_WINDOWS = (2, 4, 8, 16)
POOL_HIST = max(POOL_WINDOWS) - 1
TOP_K = 2

CONV_PAD = 8
POOL_PAD = 16

LANES = 128

ROUTE_ROWS = 8
ROW_EXPERT, ROW_RANK, ROW_PROB = 0, 2, 4

VMEM_LIMIT_BYTES = 56 * 1024 * 1024

PROMPT_TILE = 512
MIX_TILE = 1024
MIX_ROW_BLOCK = 512
ROUTER_TILE = 1024
COMBINE_TILE = 512
MOE_ROW_TILE = 896
MOE_FF_CHUNK = 512
ROW_SLOTS = 3
DENSE_FF_CHUNKS = (768, 768, 768, 512)


def _params(*sem):
    return pltpu.CompilerParams(dimension_semantics=sem, vmem_limit_bytes=VMEM_LIMIT_BYTES)


def _resident(shape):
    nd = len(shape)
    return pl.BlockSpec(shape, lambda *_: (0,) * nd, pipeline_mode=pl.Buffered(1))


def _rms(x, g):
    ms = jnp.mean(x * x, axis=-1, keepdims=True)
    return x * lax.rsqrt(ms + EPS) * g


def _mod_norm(x, g, sc, sh):
    ms = jnp.mean(x * x, axis=-1, keepdims=True)
    return x * lax.rsqrt(ms + EPS) * (g * (1.0 + sc)) + sh


def _dot(a, b):
    return jnp.dot(a, b, preferred_element_type=F32)


def _dot_nt(a, b):
    return lax.dot_general(a, b, (((1,), (1,)), ((), ())), preferred_element_type=F32)


def _silu_mul(a, b):
    return a * jax.nn.sigmoid(a) * b


def _ada_kernel(c_ref, w_ref, b_ref, o_ref):
    c = c_ref[...]
    a = (c * jax.nn.sigmoid(c)).astype(BF16)
    o_ref[0] = _dot(a, w_ref[0].astype(BF16)) + b_ref[0]


def _ada(c_all, w_ada, b_ada):
    depth, d, n = w_ada.shape
    m = c_all.shape[0]
    tn = 1024
    return pl.pallas_call(
        _ada_kernel,
        grid=(depth, n // tn),
        in_specs=[
            pl.BlockSpec((m, d), lambda i, j: (0, 0)),
            pl.BlockSpec((1, d, tn), lambda i, j: (i, 0, j)),
            pl.BlockSpec((1, 1, tn), lambda i, j: (i, 0, j)),
        ],
        out_specs=pl.BlockSpec((1, m, tn), lambda i, j: (i, 0, j)),
        out_shape=jax.ShapeDtypeStruct((depth, m, n), F32),
        compiler_params=_params("arbitrary", "arbitrary"),
        name="ada",
    )(c_all, w_ada, b_ada.reshape(depth, 1, n))


def _mix_tail(x, gt, bg, y, d_groups, poolw_ref, pscale_ref, gco_ref, gpo_ref, wout_ref):
    cw = gco_ref.shape[-1]
    ya = bg * y
    yb = jnp.concatenate(
        [_dot(d.astype(BF16), poolw_ref[g]) for g, d in enumerate(d_groups)], axis=-1
    ) * pscale_ref[...]
    ma = _rms(ya, gco_ref[...]).astype(BF16)
    mb = _rms(yb, gpo_ref[...]).astype(BF16)
    o = _dot(ma, wout_ref[0:cw, :]) + _dot(mb, wout_ref[cw:, :])
    return x + gt * o


def _mix_prompt_kernel(x_ref, sh_ref, sc_ref, gt_ref, g_ref, win_ref, convw_ref, poolw_ref,
                       pscale_ref, gco_ref, gpo_ref, wout_ref,
                       xo_ref, cs_ref, ps_ref, vbuf, ubuf, *, tile, sub):
    l = pl.program_id(1)
    cw = convw_ref.shape[-1]
    pg = poolw_ref.shape[-1]

    @pl.when(l == 0)
    def _():
        vbuf[0:CONV_PAD, :] = jnp.zeros((CONV_PAD, cw), F32)
        ubuf[0:POOL_PAD, :] = jnp.zeros((POOL_PAD, ubuf.shape[-1]), F32)

    gates = []
    for lo in range(0, tile, sub):
        x = x_ref[0, lo:lo + sub, :]
        h = _mod_norm(x, g_ref[...], sc_ref[0], sh_ref[0]).astype(BF16)
        p = _dot(h, win_ref[...])
        gates.append(p[:, 0:cw])
        vbuf[CONV_PAD + lo:CONV_PAD + lo + sub, :] = p[:, cw:2 * cw] * p[:, 2 * cw:3 * cw]
        ubuf[POOL_PAD + lo:POOL_PAD + lo + sub, :] = p[:, 3 * cw:]

    w = convw_ref[...]
    for bg, lo in zip(gates, range(0, tile, sub)):
        y = w[CONV_K - 1:CONV_K] * vbuf[CONV_PAD + lo:CONV_PAD + lo + sub, :]
        for k in range(1, CONV_K):
            y = y + w[CONV_K - 1 - k:CONV_K - k] * vbuf[CONV_PAD + lo - k:CONV_PAD + lo - k + sub, :]

        pos = l * tile + lo + lax.broadcasted_iota(I32, (sub, 1), 0)
        d_groups = []
        for g, win in enumerate(POOL_WINDOWS):
            ug = ubuf[POOL_PAD + lo:POOL_PAD + lo + sub, g * pg:(g + 1) * pg]
            acc = ug
            for k in range(1, win):
                acc = acc + ubuf[POOL_PAD + lo - k:POOL_PAD + lo - k + sub, g * pg:(g + 1) * pg]
            cnt = jnp.minimum(pos + 1, win).astype(F32)
            d_groups.append(acc / cnt - ug)

        xo_ref[0, lo:lo + sub, :] = _mix_tail(x_ref[0, lo:lo + sub, :], gt_ref[0], bg, y, d_groups,
                                              poolw_ref, pscale_ref, gco_ref, gpo_ref, wout_ref)

    cs_ref[0] = vbuf[CONV_PAD + tile - (CONV_K - 1):CONV_PAD + tile, :]
    ps_ref[0] = ubuf[POOL_PAD + tile - POOL_HIST:POOL_PAD + tile, :]
    vbuf[0:CONV_PAD, :] = vbuf[tile:tile + CONV_PAD, :]
    ubuf[0:POOL_PAD, :] = ubuf[tile:tile + POOL_PAD, :]


def _mix_prompt(x, sh, sc, gt, g, w_in, conv_w, pool_w, pool_scale, g_co, g_po, w_out, tile, sub):
    b, seq, d = x.shape
    assert seq % tile == 0 and tile % sub == 0 and sub >= POOL_PAD
    cw = conv_w.shape[-1]
    pw = pool_scale.shape[-1]
    row = lambda bi, li: (bi, 0, 0)
    return pl.pallas_call(
        functools.partial(_mix_prompt_kernel, tile=tile, sub=sub),
        grid=(b, seq // tile),
        in_specs=[
            pl.BlockSpec((1, tile, d), lambda bi, li: (bi, li, 0)),
            pl.BlockSpec((1, 1, d), row),
            pl.BlockSpec((1, 1, d), row),
            pl.BlockSpec((1, 1, d), row),
            _resident((1, d)),
            _resident(w_in.shape),
            _resident(conv_w.shape),
            _resident(pool_w.shape),
            _resident((1, pw)),
            _resident((1, cw)),
            _resident((1, pw)),
            _resident(w_out.shape),
        ],
        out_specs=[
            pl.BlockSpec((1, tile, d), lambda bi, li: (bi, li, 0)),
            pl.BlockSpec((1, CONV_K - 1, cw), row),
            pl.BlockSpec((1, POOL_HIST, pw), row),
        ],
        out_shape=[
            jax.ShapeDtypeStruct((b, seq, d), F32),
            jax.ShapeDtypeStruct((b, CONV_K - 1, cw), F32),
            jax.ShapeDtypeStruct((b, POOL_HIST, pw), F32),
        ],
        scratch_shapes=[
            pltpu.VMEM((CONV_PAD + tile, cw), F32),
            pltpu.VMEM((POOL_PAD + tile, pw), F32),
        ],
        compiler_params=_params("arbitrary", "arbitrary"),
        name="mix_prompt",
    )(x, sh, sc, gt, g.reshape(1, d), w_in, conv_w, pool_w, pool_scale.reshape(1, pw),
      g_co.reshape(1, cw), g_po.reshape(1, pw), w_out)


def _mix_sample_kernel(x_ref, sh_ref, sc_ref, gt_ref, g_ref, win_ref, convw_ref, poolw_ref,
                       pscale_ref, gco_ref, gpo_ref, wout_ref, cb_ref, pb_ref,
                       xo_ref, v_ref, u_ref):
    cw = convw_ref.shape[-1]
    pg = poolw_ref.shape[-1]
    x = x_ref[...]
    h = _mod_norm(x, g_ref[...], sc_ref[...], sh_ref[...]).astype(BF16)
    p = _dot(h, win_ref[...])
    bg = p[:, 0:cw]
    v = p[:, cw:2 * cw] * p[:, 2 * cw:3 * cw]
    u = p[:, 3 * cw:]
    v_ref[...] = v
    u_ref[...] = u

    w = convw_ref[...]
    y = w[CONV_K - 1:CONV_K] * v
    for k in range(1, CONV_K):
        y = y + w[CONV_K - 1 - k:CONV_K - k] * cb_ref[CONV_K - 1 - k]

    d_groups = []
    for g, win in enumerate(POOL_WINDOWS):
        ug = u[:, g * pg:(g + 1) * pg]
        acc = ug
        for k in range(1, win):
            acc = acc + pb_ref[POOL_HIST - k, :, g * pg:(g + 1) * pg]
        d_groups.append(acc / float(win) - ug)

    xo_ref[...] = _mix_tail(x, gt_ref[...], bg, y, d_groups, poolw_ref, pscale_ref,
                            gco_ref, gpo_ref, wout_ref)


def _mix_sample(x, sh, sc, gt, g, w_in, conv_w, pool_w, pool_scale, g_co, g_po, w_out, cb, pb):
    n, d = x.shape
    cw = conv_w.shape[-1]
    pw = pool_scale.shape[-1]
    return pl.pallas_call(
        _mix_sample_kernel,
        out_shape=[
            jax.ShapeDtypeStruct((n, d), F32),
            jax.ShapeDtypeStruct((n, cw), F32),
            jax.ShapeDtypeStruct((n, pw), F32),
        ],
        compiler_params=pltpu.CompilerParams(vmem_limit_bytes=VMEM_LIMIT_BYTES),
        name="mix_sample",
    )(x, sh, sc, gt, g.reshape(1, d), w_in, conv_w, pool_w, pool_scale.reshape(1, pw),
      g_co.reshape(1, cw), g_po.reshape(1, pw), w_out, cb, pb)


def _ffn_kernel(x_ref, sh_ref, sc_ref, gt_ref, g_ref, wg_ref, wu_ref, wd_ref, o_ref, *, chunks):
    x = x_ref[0]
    h = _mod_norm(x, g_ref[...], sc_ref[0], sh_ref[0]).astype(BF16)
    acc = None
    lo = 0
    for fc in chunks:
        a = _dot(h, wg_ref[:, lo:lo + fc])
        b = _dot(h, wu_ref[:, lo:lo + fc])
        part = _dot(_silu_mul(a, b).astype(BF16), wd_ref[lo:lo + fc, :])
        acc = part if acc is None else acc + part
        lo += fc
    o_ref[0] = x + gt_ref[0] * acc


def _ffn(x, sh, sc, gt, g, wg, wu, wd, tile, chunks):
    b, seq, d = x.shape
    tm = sh.shape[1]
    tmod = tile if tm > 1 else 1
    mod_map = (lambda bi, li: (bi, li, 0)) if tm > 1 else (lambda bi, li: (bi, 0, 0))
    return pl.pallas_call(
        functools.partial(_ffn_kernel, chunks=chunks),
        grid=(b, seq // tile),
        in_specs=[
            pl.BlockSpec((1, tile, d), lambda bi, li: (bi, li, 0)),
            pl.BlockSpec((1, tmod, d), mod_map),
            pl.BlockSpec((1, tmod, d), mod_map),
            pl.BlockSpec((1, tmod, d), mod_map),
            _resident((1, d)),
            _resident(wg.shape),
            _resident(wu.shape),
            _resident(wd.shape),
        ],
        out_specs=pl.BlockSpec((1, tile, d), lambda bi, li: (bi, li, 0)),
        out_shape=jax.ShapeDtypeStruct((b, seq, d), F32),
        compiler_params=_params("arbitrary", "arbitrary"),
        name="ffn",
    )(x, sh, sc, gt, g.reshape(1, d), wg, wu, wd)


def _split_bf16(a):
    hi = a.astype(BF16)
    return hi, (a - hi.astype(F32)).astype(BF16)


def _router_kernel(x_ref, sh_ref, sc_ref, g_ref, rw_ref, base_ref, h_ref, route_ref, cnt_ref,
                   tri, carry, *, n_steps):
    t = x_ref.shape[0]
    step = pl.program_id(0)

    @pl.when(step == 0)
    def _():
        r = lax.broadcasted_iota(I32, (t, t), 0)
        c = lax.broadcasted_iota(I32, (t, t), 1)
        tri[...] = jnp.where(r < c, 1.0, 0.0).astype(BF16)
        carry[...] = base_ref[...]

    @pl.when(step >= n_steps)
    def _():
        h_ref[...] = jnp.zeros_like(h_ref)

    @pl.when(step < n_steps)
    def _():
        h = _mod_norm(x_ref[...], g_ref[...], sc_ref[0], sh_ref[0])
        h_ref[...] = h
        h_hi, h_lo = _split_bf16(h)
        rw_hi, rw_lo = _split_bf16(rw_ref[...])
        logits = _dot_nt(rw_hi, h_hi) + _dot_nt(rw_hi, h_lo) + _dot_nt(rw_lo, h_hi)

        ne = logits.shape[0]
        idx = lax.broadcasted_iota(I32, logits.shape, 0)
        m1 = jnp.max(logits, axis=0, keepdims=True)
        i1 = jnp.min(jnp.where(logits == m1, idx, ne), axis=0, keepdims=True)
        sel1 = idx == i1
        rest = jnp.where(sel1, -jnp.inf, logits)
        m2 = jnp.max(rest, axis=0, keepdims=True)
        i2 = jnp.min(jnp.where(rest == m2, idx, ne), axis=0, keepdims=True)
        sel2 = idx == i2
        e = jnp.exp(m2 - m1)
        p1 = 1.0 / (1.0 + e)
        p2 = e / (1.0 + e)

        chosen = jnp.where(sel1, 1.0, jnp.where(sel2, 1.0, 0.0))
        before = _dot(chosen.astype(BF16), tri[...]) + carry[:, 0:1]
        r1 = jnp.sum(jnp.where(sel1, before, 0.0), axis=0, keepdims=True)
        r2 = jnp.sum(jnp.where(sel2, before, 0.0), axis=0, keepdims=True)
        carry[...] += jnp.sum(chosen, axis=1, keepdims=True)
        cnt_ref[...] = carry[...]

        rows = (i1.astype(F32), i2.astype(F32), r1, r2, p1, p2)
        route = jnp.zeros(logits.shape, F32)
        for k, row in enumerate(rows):
            route = jnp.where(idx == k, row, route)
        route_ref[...] = route


def _router(x2d, sh, sc, g, router_w, base_cnt, tile, rows_per_mod, h_rows, h_prev=None,
            row_off=0):
    n, d = x2d.shape
    ne = router_w.shape[-1]
    assert ne == ROUTE_ROWS and n % tile == 0 and row_off % tile == 0 and tile % LANES == 0
    assert base_cnt.shape == (ne, LANES)
    n_steps = n // tile
    tail = h_rows - (row_off + n)
    assert 0 <= tail < tile
    fill_tail = h_prev is None and tail > 0
    last = n_steps - 1
    clamp = (lambda i: jnp.minimum(i, last)) if fill_tail else (lambda i: i)
    per_row_mod = sh.shape[1] > 1
    if per_row_mod:
        mod_spec = pl.BlockSpec((1, tile, d), lambda i: (0, clamp(i), 0))
    else:
        assert rows_per_mod % tile == 0
        mod_spec = pl.BlockSpec((1, 1, d), lambda i: (clamp(i) // (rows_per_mod // tile), 0, 0))
    in_specs = [
        pl.BlockSpec((tile, d), lambda i: (clamp(i), 0)),
        mod_spec,
        mod_spec,
        _resident((1, d)),
        _resident((ne, d)),
        _resident((ne, LANES)),
    ]
    args = [x2d, sh, sc, g.reshape(1, d), router_w.T, base_cnt]
    n_in = len(args)
    aliases = {}
    body = functools.partial(_router_kernel, n_steps=n_steps)
    if h_prev is not None:
        assert h_prev.shape == (h_rows, d)
        in_specs.append(pl.BlockSpec(memory_space=pl.ANY))
        args.append(h_prev)
        aliases = {n_in: 0}

        def body(*refs):
            _router_kernel(*refs[:n_in], *refs[n_in + 1:], n_steps=n_steps)

    blk_off = row_off // tile
    return pl.pallas_call(
        body,
        grid=(n_steps + int(fill_tail),),
        in_specs=in_specs,
        out_specs=[
            pl.BlockSpec((tile, d), lambda i: (i + blk_off, 0)),
            pl.BlockSpec((ne, tile), lambda i: (0, clamp(i))),
            pl.BlockSpec((ne, LANES), lambda i: (0, 0)),
        ],
        out_shape=[
            jax.ShapeDtypeStruct((h_rows, d), F32),
            jax.ShapeDtypeStruct((ne, n), F32),
            jax.ShapeDtypeStruct((ne, LANES), F32),
        ],
        scratch_shapes=[pltpu.VMEM((tile, tile), BF16), pltpu.VMEM((ne, LANES), F32)],
        input_output_aliases=aliases,
        compiler_params=_params("arbitrary"),
        name="router",
    )(*args)


def _row_copy(src_hbm, row, dst, dst_row, sem):
    return pltpu.make_async_copy(src_hbm.at[pl.ds(row, 1), :], dst.at[pl.ds(dst_row, 1), :], sem)


def _row_copy_out(src, src_row, dst_hbm, row, sem):
    return pltpu.make_async_copy(src.at[pl.ds(src_row, 1), :], dst_hbm.at[pl.ds(row, 1), :], sem)


def _moe_grouped_kernel(te_ref, na_ref, src_ref, dst_ref, h_hbm, wg_hbm, wu_hbm, wd_hbm, y_hbm,
                        xs, hbuf, obuf, wgb, wub, wdb, gsem, ssem, fsem, wsem, *, tm, fc, nc,
                        fill_row, fill_tiles):
    i = pl.program_id(0)
    n_active = na_ref[0]
    ahead = ROW_SLOTS - 1
    slot = i % ROW_SLOTS
    rows_per_chunk = tm // nc

    def request(tile, r):
        s = tile % ROW_SLOTS
        return _row_copy(h_hbm, src_ref[tile * tm + r], xs.at[s], r, gsem.at[s])

    def send(tile, r):
        s = (tile + ROW_SLOTS) % ROW_SLOTS
        return _row_copy_out(obuf.at[s], r, y_hbm, dst_ref[(tile + 1) * tm + r], ssem.at[s])

    def weight_copies(tile, c, s):
        e = te_ref[tile]
        cols = pl.ds(c * fc, fc)
        return (pltpu.make_async_copy(wg_hbm.at[e, :, cols], wgb.at[s], wsem.at[s]),
                pltpu.make_async_copy(wu_hbm.at[e, :, cols], wub.at[s], wsem.at[s]),
                pltpu.make_async_copy(wd_hbm.at[e, cols, :], wdb.at[s], wsem.at[s]))

    @pl.when(i == 0)
    def _():
        last = ROW_SLOTS - 1
        obuf[last] = jnp.zeros((tm, obuf.shape[-1]), F32)
        fills = [pltpu.make_async_copy(obuf.at[last], y_hbm.at[pl.ds(fill_row + q * tm, tm), :],
                                       fsem) for q in range(fill_tiles)]
        for f in fills:
            f.start()
        for f in fills:
            f.wait()
        for cp in weight_copies(0, 0, 0):
            cp.start()

        def body(r, carry):
            for t in range(ahead):
                request(t, r).start()
            return carry

        lax.fori_loop(0, tm, body, 0)

    @pl.when(i < n_active + ahead)
    def _():
        pltpu.make_async_copy(h_hbm.at[pl.ds(0, tm), :], xs.at[slot], gsem.at[slot]).wait()

    @pl.when(jnp.logical_and(i >= ahead, i < n_active + ROW_SLOTS))
    def _():
        pltpu.make_async_copy(obuf.at[slot], y_hbm.at[pl.ds(0, tm), :], ssem.at[slot]).wait()

    @pl.when(i == n_active)
    def _():
        for cp in weight_copies(i, 0, (i * nc) % 2):
            cp.wait()

        def body(r, carry):
            send(i - 1, r).start()
            return carry

        lax.fori_loop(0, tm, body, 0)

    @pl.when(i < n_active)
    def _():
        obuf[slot] = jnp.zeros((tm, obuf.shape[-1]), F32)
        hbuf[...] = xs[slot].astype(BF16)
        part = None
        for c in range(nc):
            s = (i * nc + c) % 2
            for cp in weight_copies(i, c, s):
                cp.wait()
            nxt = weight_copies(i, c + 1, 1 - s) if c + 1 < nc else weight_copies(i + 1, 0, 1 - s)
            for cp in nxt:
                cp.start()
            if part is not None:
                obuf[slot] += part
            for k in range(rows_per_chunk):
                r = c * rows_per_chunk + k
                request(i + ahead, r).start()
                send(i - 1, r).start()
            h = hbuf[...]
            a = _dot(h, wgb[s].astype(BF16))
            b = _dot(h, wub[s].astype(BF16))
            part = _dot(_silu_mul(a, b).astype(BF16), wdb[s].astype(BF16))
        obuf[slot] += part


def _moe_grouped(h_all, tile_expert, n_active, src, dst, wg, wu, wd, tm, fc, y_rows, fill_row,
                 fill_tiles):
    ne, d, dff = wg.shape
    n_tiles = tile_expert.shape[0]
    nc = dff // fc
    assert dff % fc == 0 and tm % nc == 0
    assert src.shape[0] == n_tiles * tm and dst.shape[0] == n_tiles * tm
    hbm = pl.BlockSpec(memory_space=pl.ANY)
    grid_spec = pltpu.PrefetchScalarGridSpec(
        num_scalar_prefetch=4,
        grid=(n_tiles,),
        in_specs=[hbm, hbm, hbm, hbm],
        out_specs=hbm,
        scratch_shapes=[
            pltpu.VMEM((ROW_SLOTS, tm, d), F32),
            pltpu.VMEM((tm, d), BF16),
            pltpu.VMEM((ROW_SLOTS, tm, d), F32),
            pltpu.VMEM((2, d, fc), wg.dtype),
            pltpu.VMEM((2, d, fc), wu.dtype),
            pltpu.VMEM((2, fc, d), wd.dtype),
            pltpu.SemaphoreType.DMA((ROW_SLOTS,)),
            pltpu.SemaphoreType.DMA((ROW_SLOTS,)),
            pltpu.SemaphoreType.DMA,
            pltpu.SemaphoreType.DMA((2,)),
        ],
    )
    return pl.pallas_call(
        functools.partial(_moe_grouped_kernel, tm=tm, fc=fc, nc=nc, fill_row=fill_row,
                          fill_tiles=fill_tiles),
        grid_spec=grid_spec,
        out_shape=jax.ShapeDtypeStruct((y_rows, d), F32),
        compiler_params=_params("arbitrary"),
        name="moe_grouped",
    )(tile_expert, n_active, src, dst, h_all, wg, wu, wd)


def _combine_kernel(x_ref, gt_ref, route_ref, gf_ref, y0_ref, y1_ref, o_ref):
    route = route_ref[...]
    pad = jnp.zeros((LANES - route.shape[0], route.shape[1]), F32)
    cols = jnp.concatenate([route, pad], axis=0).T
    f = (cols[:, ROW_PROB:ROW_PROB + 1] * y0_ref[...]
         + cols[:, ROW_PROB + 1:ROW_PROB + 2] * y1_ref[...])
    o_ref[...] = _rms(x_ref[...] + gt_ref[0] * f, gf_ref[...])


def _combine(x2d, gt, route, g_final, y_tok, tile, rows_per_mod, tok_off, plane_rows):
    n, d = x2d.shape
    assert n % tile == 0 and tok_off % tile == 0 and plane_rows % tile == 0
    per_row_mod = gt.shape[1] > 1
    if per_row_mod:
        mod_spec = pl.BlockSpec((1, tile, d), lambda j: (0, j, 0))
    else:
        assert rows_per_mod % tile == 0
        mod_spec = pl.BlockSpec((1, 1, d), lambda j: (j // (rows_per_mod // tile), 0, 0))
    first = tok_off // tile
    second = (plane_rows + tok_off) // tile
    return pl.pallas_call(
        _combine_kernel,
        grid=(n // tile,),
        in_specs=[
            pl.BlockSpec((tile, d), lambda j: (j, 0)),
            mod_spec,
            pl.BlockSpec((ROUTE_ROWS, tile), lambda j: (0, j)),
            pl.BlockSpec((1, d), lambda j: (0, 0)),
            pl.BlockSpec((tile, d), lambda j: (first + j, 0)),
            pl.BlockSpec((tile, d), lambda j: (second + j, 0)),
        ],
        out_specs=pl.BlockSpec((tile, d), lambda j: (j, 0)),
        out_shape=jax.ShapeDtypeStruct((n, d), F32),
        compiler_params=_params("arbitrary"),
        name="combine",
    )(x2d, gt, route, g_final.reshape(1, d), y_tok, y_tok)


def _invert_kernel(pos_ref, inv_ref):
    def clear(r, carry):
        inv_ref[r] = 0
        return carry

    lax.fori_loop(0, inv_ref.shape[0], clear, 0, unroll=32)

    def place(a, carry):
        inv_ref[pos_ref[a]] = a
        return carry

    lax.fori_loop(0, pos_ref.shape[0], place, 0, unroll=32)


def _invert(pos_flat, n_rows):
    assert pos_flat.shape[0] % 32 == 0 and n_rows % 32 == 0
    smem = pl.BlockSpec(memory_space=pltpu.SMEM)
    return pl.pallas_call(
        _invert_kernel,
        in_specs=[smem],
        out_specs=smem,
        out_shape=jax.ShapeDtypeStruct((n_rows,), I32),
        name="invert",
    )(pos_flat)


def _routing_tables(route_all, counts, tm, n_tiles, plane_rows):
    ne = counts.shape[0]
    n = route_all.shape[1]
    e_idx = route_all[ROW_EXPERT:ROW_EXPERT + TOP_K].astype(I32)
    rank = route_all[ROW_RANK:ROW_RANK + TOP_K].astype(I32)
    padded = (counts + tm - 1) // tm * tm
    ends = jnp.cumsum(padded)
    offs = ends - padded
    onehot = e_idx[..., None] == jnp.arange(ne, dtype=I32)
    pos = jnp.sum(jnp.where(onehot, offs, 0), axis=-1) + rank
    n_active = ends[-1] // tm
    starts = jnp.arange(n_tiles, dtype=I32) * tm
    tile_expert = jnp.sum(starts[:, None] >= ends[None, :], axis=1).astype(I32)
    last_expert = jnp.sum((n_active - 1) * tm >= ends).astype(I32)
    tile_expert = jnp.minimum(tile_expert, last_expert)
    n_rows = n_tiles * tm
    inv = _invert(pos.reshape(-1), n_rows)
    r = jnp.arange(n_rows, dtype=I32)[:, None]
    real_before = jnp.sum(jnp.clip(r - offs, 0, counts), axis=1)
    is_real = jnp.any(jnp.logical_and(r >= offs, r < offs + counts), axis=1)
    tok = inv % n
    choice = inv // n
    spare = n + jnp.minimum(r[:, 0] - real_before, ne * tm - 1)
    src = jnp.where(is_real, tok, 0)
    dst = jnp.where(is_real, choice * plane_rows + tok, spare)
    lead = n + ne * tm + jnp.arange(tm, dtype=I32)
    dst = jnp.concatenate([lead, dst])[:n_rows]
    return src, dst, tile_expert, n_active.astype(I32).reshape(1)


def _moe_layer(xp, xs, mod_p, mod_s, g, router_w, wg, wu, wd, g_final):
    nb, seq, d = xp.shape
    ns = xs.shape[0]
    ne = router_w.shape[-1]
    n_all = nb * seq + ns
    sh_p, sc_p, gt_p = mod_p
    sh_s, sc_s, gt_s = mod_s
    xp2 = xp.reshape(nb * seq, d)

    zero_cnt = jnp.zeros((ne, LANES), F32)
    h_all, route_p, cnt_p = _router(xp2, sh_p, sc_p, g, router_w, zero_cnt, ROUTER_TILE, seq,
                                    n_all)
    h_all, route_s, cnt_s = _router(xs, sh_s, sc_s, g, router_w, cnt_p, ns, ns, n_all,
                                    h_prev=h_all, row_off=nb * seq)

    tm = MOE_ROW_TILE
    n_tiles = (n_all * TOP_K) // tm + ne + ROW_SLOTS
    fill_tiles = ne + 1
    plane_rows = n_all + fill_tiles * tm
    route_all = jnp.concatenate([route_p, route_s], axis=1)
    src, dst, tile_expert, n_active = _routing_tables(
        route_all, cnt_s[:, 0].astype(I32), tm, n_tiles, plane_rows)

    y_tok = _moe_grouped(h_all, tile_expert, n_active, src, dst, wg, wu, wd, tm, MOE_FF_CHUNK,
                         plane_rows + n_all, n_all, fill_tiles)
    yp = _combine(xp2, gt_p, route_p, g_final, y_tok, COMBINE_TILE, seq, 0, plane_rows)
    ys = _combine(xs, gt_s, route_s, g_final, y_tok, ns, ns, nb * seq, plane_rows)
    return yp.reshape(nb, seq, d), ys


def kernel(x_prompt, x_sample, c_prompt, c_sample, state_conv, state_pool, w_ada, b_ada, g_mix,
           g_ffn, w_in, conv_w, pool_w, pool_scale, g_conv_out, g_pool_out, w_out, dense_w_gate,
           dense_w_up, dense_w_down, router_w, moe_w_gate, moe_w_up, moe_w_down, g_final):
    depth = w_ada.shape[0]
    nb, seq, d = x_prompt.shape
    ns = x_sample.shape[0]
    assert x_sample.shape[1] == 1 and depth == 2
    assert sum(DENSE_FF_CHUNKS) == dense_w_gate.shape[-1]

    mod = _ada(jnp.concatenate([c_prompt, c_sample], axis=0), w_ada, b_ada)
    mod = mod.reshape(depth, nb + ns, 6, d)
    mod_p = [jnp.transpose(mod[i, :nb], (1, 0, 2))[:, :, None, :] for i in range(depth)]
    mod_s = [jnp.transpose(mod[i, nb:], (1, 0, 2))[:, None, :, :] for i in range(depth)]

    w_in_b = w_in.astype(BF16)
    w_out_b = w_out.astype(BF16)
    pool_w_b = pool_w.astype(BF16)
    dense_b = [w.astype(BF16) for w in (dense_w_gate, dense_w_up, dense_w_down)]
    cb_t = jnp.transpose(state_conv, (0, 2, 1, 3))
    pb_t = jnp.transpose(state_pool, (0, 2, 1, 3))

    xp = x_prompt
    xs = x_sample.reshape(ns, d)
    conv_p, pool_p, conv_s, pool_s = [], [], [], []
    for i in range(depth):
        sh1, sc1, gt1, sh2, sc2, gt2 = mod_p[i]
        mix_w = (g_mix[i], w_in_b[i], conv_w[i], pool_w_b[i], pool_scale[i], g_conv_out[i],
                 g_pool_out[i], w_out_b[i])
        xp, cs, ps = _mix_prompt(xp, sh1, sc1, gt1, *mix_w, tile=MIX_TILE, sub=MIX_ROW_BLOCK)
        conv_p.append(cs)
        pool_p.append(ps)
        s1, c1, t1, s2, c2, t2 = mod_s[i]
        xs, v_new, u_new = _mix_sample(xs, s1[0], c1[0], t1[0], *mix_w, cb_t[i], pb_t[i])
        conv_s.append(jnp.concatenate([state_conv[i][:, 1:], v_new[:, None, :]], axis=1))
        pool_s.append(jnp.concatenate([state_pool[i][:, 1:], u_new[:, None, :]], axis=1))
        j = i // 2
        if i % 2 == 0:
            wg, wu, wd = (w[j] for w in dense_b)
            xp = _ffn(xp, sh2, sc2, gt2, g_ffn[i], wg, wu, wd, PROMPT_TILE, DENSE_FF_CHUNKS)
            xs = _ffn(xs[None], s2, c2, t2, g_ffn[i], wg, wu, wd, ns, DENSE_FF_CHUNKS)[0]
        else:
            xp, xs = _moe_layer(xp, xs, (sh2, sc2, gt2), (s2, c2, t2), g_ffn[i], router_w[j],
                                moe_w_gate[j], moe_w_up[j], moe_w_down[j], g_final)

    return (xp, xs.reshape(ns, 1, d), jnp.stack(conv_p), jnp.stack(pool_p),
            jnp.stack(conv_s), jnp.stack(pool_s))
```

```python
import functools

import jax
import jax.numpy as jnp
from jax import lax
from jax.experimental import pallas as pl
from jax.experimental.pallas import tpu as pltpu

F32 = jnp.float32
BF16 = jnp.bfloat16
I32 = jnp.int32

EPS = 1e-6
CONV_K = 3
POOL_WINDOWS = (2, 4, 8, 16)
POOL_HIST = max(POOL_WINDOWS) - 1
TOP_K = 2

CONV_PAD = 8
POOL_PAD = 16

LANES = 128

ROUTE_ROWS = 8
ROW_EXPERT, ROW_RANK, ROW_PROB = 0, 2, 4

VMEM_LIMIT_BYTES = 56 * 1024 * 1024

PROMPT_TILE = 512
MIX_TILE = 1024
MIX_ROW_BLOCK = 512
ROUTER_TILE = 1024
COMBINE_TILE = 512
MOE_ROW_TILE = 1024
MOE_FF_CHUNK = 512
FILL_ROWS = 128
ROW_SLOTS = 3
DENSE_FF_CHUNKS = (768, 768, 768, 512)


def _params(*sem):
    return pltpu.CompilerParams(dimension_semantics=sem, vmem_limit_bytes=VMEM_LIMIT_BYTES)


def _resident(shape):
    nd = len(shape)
    return pl.BlockSpec(shape, lambda *_: (0,) * nd, pipeline_mode=pl.Buffered(1))


def _rms(x, g):
    ms = jnp.mean(x * x, axis=-1, keepdims=True)
    return x * lax.rsqrt(ms + EPS) * g


def _mod_norm(x, g, sc, sh):
    ms = jnp.mean(x * x, axis=-1, keepdims=True)
    return x * lax.rsqrt(ms + EPS) * (g * (1.0 + sc)) + sh


def _dot(a, b):
    return jnp.dot(a, b, preferred_element_type=F32)


def _dot_nt(a, b):
    return lax.dot_general(a, b, (((1,), (1,)), ((), ())), preferred_element_type=F32)


def _silu_mul(a, b):
    return a * jax.nn.sigmoid(a) * b


def _ada_kernel(c_ref, w_ref, b_ref, o_ref):
    c = c_ref[...]
    a = (c * jax.nn.sigmoid(c)).astype(BF16)
    o_ref[0] = _dot(a, w_ref[0].astype(BF16)) + b_ref[0]


def _ada(c_all, w_ada, b_ada):
    depth, d, n = w_ada.shape
    m = c_all.shape[0]
    tn = 1024
    return pl.pallas_call(
        _ada_kernel,
        grid=(depth, n // tn),
        in_specs=[
            pl.BlockSpec((m, d), lambda i, j: (0, 0)),
            pl.BlockSpec((1, d, tn), lambda i, j: (i, 0, j)),
            pl.BlockSpec((1, 1, tn), lambda i, j: (i, 0, j)),
        ],
        out_specs=pl.BlockSpec((1, m, tn), lambda i, j: (i, 0, j)),
        out_shape=jax.ShapeDtypeStruct((depth, m, n), F32),
        compiler_params=_params("arbitrary", "arbitrary"),
        name="ada",
    )(c_all, w_ada, b_ada.reshape(depth, 1, n))


def _mix_tail(x, gt, bg, y, d_groups, poolw_ref, pscale_ref, gco_ref, gpo_ref, wout_ref):
    cw = gco_ref.shape[-1]
    ya = bg * y
    yb = jnp.concatenate(
        [_dot(d.astype(BF16), poolw_ref[g]) for g, d in enumerate(d_groups)], axis=-1
    ) * pscale_ref[...]
    ma = _rms(ya, gco_ref[...]).astype(BF16)
    mb = _rms(yb, gpo_ref[...]).astype(BF16)
    o = _dot(ma, wout_ref[0:cw, :]) + _dot(mb, wout_ref[cw:, :])
    return x + gt * o


def _mix_prompt_kernel(x_ref, sh_ref, sc_ref, gt_ref, g_ref, win_ref, convw_ref, poolw_ref,
                       pscale_ref, gco_ref, gpo_ref, wout_ref,
                       xo_ref, cs_ref, ps_ref, vbuf, ubuf, *, tile, sub):
    l = pl.program_id(1)
    cw = convw_ref.shape[-1]
    pg = poolw_ref.shape[-1]

    @pl.when(l == 0)
    def _():
        vbuf[0:CONV_PAD, :] = jnp.zeros((CONV_PAD, cw), F32)
        ubuf[0:POOL_PAD, :] = jnp.zeros((POOL_PAD, ubuf.shape[-1]), F32)

    gates = []
    for lo in range(0, tile, sub):
        x = x_ref[0, lo:lo + sub, :]
        h = _mod_norm(x, g_ref[...], sc_ref[0], sh_ref[0]).astype(BF16)
        p = _dot(h, win_ref[...])
        gates.append(p[:, 0:cw])
        vbuf[CONV_PAD + lo:CONV_PAD + lo + sub, :] = p[:, cw:2 * cw] * p[:, 2 * cw:3 * cw]
        ubuf[POOL_PAD + lo:POOL_PAD + lo + sub, :] = p[:, 3 * cw:]

    w = convw_ref[...]
    for bg, lo in zip(gates, range(0, tile, sub)):
        y = w[CONV_K - 1:CONV_K] * vbuf[CONV_PAD + lo:CONV_PAD + lo + sub, :]
        for k in range(1, CONV_K):
            y = y + w[CONV_K - 1 - k:CONV_K - k] * vbuf[CONV_PAD + lo - k:CONV_PAD + lo - k + sub, :]

        pos = l * tile + lo + lax.broadcasted_iota(I32, (sub, 1), 0)
        d_groups = []
        for g, win in enumerate(POOL_WINDOWS):
            ug = ubuf[POOL_PAD + lo:POOL_PAD + lo + sub, g * pg:(g + 1) * pg]
            acc = ug
            for k in range(1, win):
                acc = acc + ubuf[POOL_PAD + lo - k:POOL_PAD + lo - k + sub, g * pg:(g + 1) * pg]
            cnt = jnp.minimum(pos + 1, win).astype(F32)
            d_groups.append(acc / cnt - ug)

        xo_ref[0, lo:lo + sub, :] = _mix_tail(x_ref[0, lo:lo + sub, :], gt_ref[0], bg, y, d_groups,
                                              poolw_ref, pscale_ref, gco_ref, gpo_ref, wout_ref)

    cs_ref[0] = vbuf[CONV_PAD + tile - (CONV_K - 1):CONV_PAD + tile, :]
    ps_ref[0] = ubuf[POOL_PAD + tile - POOL_HIST:POOL_PAD + tile, :]
    vbuf[0:CONV_PAD, :] = vbuf[tile:tile + CONV_PAD, :]
    ubuf[0:POOL_PAD, :] = ubuf[tile:tile + POOL_PAD, :]


def _mix_prompt(x, sh, sc, gt, g, w_in, conv_w, pool_w, pool_scale, g_co, g_po, w_out, tile, sub):
    b, seq, d = x.shape
    assert seq % tile == 0 and tile % sub == 0 and sub >= POOL_PAD
    cw = conv_w.shape[-1]
    pw = pool_scale.shape[-1]
    row = lambda bi, li: (bi, 0, 0)
    return pl.pallas_call(
        functools.partial(_mix_prompt_kernel, tile=tile, sub=sub),
        grid=(b, seq // tile),
        in_specs=[
            pl.BlockSpec((1, tile, d), lambda bi, li: (bi, li, 0)),
            pl.BlockSpec((1, 1, d), row),
            pl.BlockSpec((1, 1, d), row),
            pl.BlockSpec((1, 1, d), row),
            _resident((1, d)),
            _resident(w_in.shape),
            _resident(conv_w.shape),
            _resident(pool_w.shape),
            _resident((1, pw)),
            _resident((1, cw)),
            _resident((1, pw)),
            _resident(w_out.shape),
        ],
        out_specs=[
            pl.BlockSpec((1, tile, d), lambda bi, li: (bi, li, 0)),
            pl.BlockSpec((1, CONV_K - 1, cw), row),
            pl.BlockSpec((1, POOL_HIST, pw), row),
        ],
        out_shape=[
            jax.ShapeDtypeStruct((b, seq, d), F32),
            jax.ShapeDtypeStruct((b, CONV_K - 1, cw), F32),
            jax.ShapeDtypeStruct((b, POOL_HIST, pw), F32),
        ],
        scratch_shapes=[
            pltpu.VMEM((CONV_PAD + tile, cw), F32),
            pltpu.VMEM((POOL_PAD + tile, pw), F32),
        ],
        compiler_params=_params("arbitrary", "arbitrary"),
        name="mix_prompt",
    )(x, sh, sc, gt, g.reshape(1, d), w_in, conv_w, pool_w, pool_scale.reshape(1, pw),
      g_co.reshape(1, cw), g_po.reshape(1, pw), w_out)


def _mix_sample_kernel(x_ref, sh_ref, sc_ref, gt_ref, g_ref, win_ref, convw_ref, poolw_ref,
                       pscale_ref, gco_ref, gpo_ref, wout_ref, cb_ref, pb_ref,
                       xo_ref, v_ref, u_ref):
    cw = convw_ref.shape[-1]
    pg = poolw_ref.shape[-1]
    x = x_ref[...]
    h = _mod_norm(x, g_ref[...], sc_ref[...], sh_ref[...]).astype(BF16)
    p = _dot(h, win_ref[...])
    bg = p[:, 0:cw]
    v = p[:, cw:2 * cw] * p[:, 2 * cw:3 * cw]
    u = p[:, 3 * cw:]
    v_ref[...] = v
    u_ref[...] = u

    w = convw_ref[...]
    y = w[CONV_K - 1:CONV_K] * v
    for k in range(1, CONV_K):
        y = y + w[CONV_K - 1 - k:CONV_K - k] * cb_ref[CONV_K - 1 - k]

    d_groups = []
    for g, win in enumerate(POOL_WINDOWS):
        ug = u[:, g * pg:(g + 1) * pg]
        acc = ug
        for k in range(1, win):
            acc = acc + pb_ref[POOL_HIST - k, :, g * pg:(g + 1) * pg]
        d_groups.append(acc / float(win) - ug)

    xo_ref[...] = _mix_tail(x, gt_ref[...], bg, y, d_groups, poolw_ref, pscale_ref,
                            gco_ref, gpo_ref, wout_ref)


def _mix_sample(x, sh, sc, gt, g, w_in, conv_w, pool_w, pool_scale, g_co, g_po, w_out, cb, pb):
    n, d = x.shape
    cw = conv_w.shape[-1]
    pw = pool_scale.shape[-1]
    return pl.pallas_call(
        _mix_sample_kernel,
        out_shape=[
            jax.ShapeDtypeStruct((n, d), F32),
            jax.ShapeDtypeStruct((n, cw), F32),
            jax.ShapeDtypeStruct((n, pw), F32),
        ],
        compiler_params=pltpu.CompilerParams(vmem_limit_bytes=VMEM_LIMIT_BYTES),
        name="mix_sample",
    )(x, sh, sc, gt, g.reshape(1, d), w_in, conv_w, pool_w, pool_scale.reshape(1, pw),
      g_co.reshape(1, cw), g_po.reshape(1, pw), w_out, cb, pb)


def _ffn_kernel(x_ref, sh_ref, sc_ref, gt_ref, g_ref, wg_ref, wu_ref, wd_ref, o_ref, *, chunks):
    x = x_ref[0]
    h = _mod_norm(x, g_ref[...], sc_ref[0], sh_ref[0]).astype(BF16)
    acc = None
    lo = 0
    for fc in chunks:
        a = _dot(h, wg_ref[:, lo:lo + fc])
        b = _dot(h, wu_ref[:, lo:lo + fc])
        part = _dot(_silu_mul(a, b).astype(BF16), wd_ref[lo:lo + fc, :])
        acc = part if acc is None else acc + part
        lo += fc
    o_ref[0] = x + gt_ref[0] * acc


def _ffn(x, sh, sc, gt, g, wg, wu, wd, tile, chunks):
    b, seq, d = x.shape
    tm = sh.shape[1]
    tmod = tile if tm > 1 else 1
    mod_map = (lambda bi, li: (bi, li, 0)) if tm > 1 else (lambda bi, li: (bi, 0, 0))
    return pl.pallas_call(
        functools.partial(_ffn_kernel, chunks=chunks),
        grid=(b, seq // tile),
        in_specs=[
            pl.BlockSpec((1, tile, d), lambda bi, li: (bi, li, 0)),
            pl.BlockSpec((1, tmod, d), mod_map),
            pl.BlockSpec((1, tmod, d), mod_map),
            pl.BlockSpec((1, tmod, d), mod_map),
            _resident((1, d)),
            _resident(wg.shape),
            _resident(wu.shape),
            _resident(wd.shape),
        ],
        out_specs=pl.BlockSpec((1, tile, d), lambda bi, li: (bi, li, 0)),
        out_shape=jax.ShapeDtypeStruct((b, seq, d), F32),
        compiler_params=_params("arbitrary", "arbitrary"),
        name="ffn",
    )(x, sh, sc, gt, g.reshape(1, d), wg, wu, wd)


def _split_bf16(a):
    hi = a.astype(BF16)
    return hi, (a - hi.astype(F32)).astype(BF16)


def _router_kernel(x_ref, sh_ref, sc_ref, g_ref, rw_ref, base_ref, h_ref, route_ref, cnt_ref,
                   tri, carry, *, n_steps):
    t = x_ref.shape[0]
    step = pl.program_id(0)

    @pl.when(step == 0)
    def _():
        r = lax.broadcasted_iota(I32, (t, t), 0)
        c = lax.broadcasted_iota(I32, (t, t), 1)
        tri[...] = jnp.where(r < c, 1.0, 0.0).astype(BF16)
        carry[...] = base_ref[...]

    @pl.when(step >= n_steps)
    def _():
        h_ref[...] = jnp.zeros_like(h_ref)

    @pl.when(step < n_steps)
    def _():
        h = _mod_norm(x_ref[...], g_ref[...], sc_ref[0], sh_ref[0])
        h_ref[...] = h
        h_hi, h_lo = _split_bf16(h)
        rw_hi, rw_lo = _split_bf16(rw_ref[...])
        logits = _dot_nt(rw_hi, h_hi) + _dot_nt(rw_hi, h_lo) + _dot_nt(rw_lo, h_hi)

        ne = logits.shape[0]
        idx = lax.broadcasted_iota(I32, logits.shape, 0)
        m1 = jnp.max(logits, axis=0, keepdims=True)
        i1 = jnp.min(jnp.where(logits == m1, idx, ne), axis=0, keepdims=True)
        sel1 = idx == i1
        rest = jnp.where(sel1, -jnp.inf, logits)
        m2 = jnp.max(rest, axis=0, keepdims=True)
        i2 = jnp.min(jnp.where(rest == m2, idx, ne), axis=0, keepdims=True)
        sel2 = idx == i2
        e = jnp.exp(m2 - m1)
        p1 = 1.0 / (1.0 + e)
        p2 = e / (1.0 + e)

        chosen = jnp.where(sel1, 1.0, jnp.where(sel2, 1.0, 0.0))
        before = _dot(chosen.astype(BF16), tri[...]) + carry[:, 0:1]
        r1 = jnp.sum(jnp.where(sel1, before, 0.0), axis=0, keepdims=True)
        r2 = jnp.sum(jnp.where(sel2, before, 0.0), axis=0, keepdims=True)
        carry[...] += jnp.sum(chosen, axis=1, keepdims=True)
        cnt_ref[...] = carry[...]

        rows = (i1.astype(F32), i2.astype(F32), r1, r2, p1, p2)
        route = jnp.zeros(logits.shape, F32)
        for k, row in enumerate(rows):
            route = jnp.where(idx == k, row, route)
        route_ref[...] = route


def _router(x2d, sh, sc, g, router_w, base_cnt, tile, rows_per_mod, h_rows, h_prev=None,
            row_off=0):
    n, d = x2d.shape
    ne = router_w.shape[-1]
    assert ne == ROUTE_ROWS and n % tile == 0 and row_off % tile == 0 and tile % LANES == 0
    assert base_cnt.shape == (ne, LANES)
    n_steps = n // tile
    tail = h_rows - (row_off + n)
    assert 0 <= tail < tile
    fill_tail = h_prev is None and tail > 0
    last = n_steps - 1
    clamp = (lambda i: jnp.minimum(i, last)) if fill_tail else (lambda i: i)
    per_row_mod = sh.shape[1] > 1
    if per_row_mod:
        mod_spec = pl.BlockSpec((1, tile, d), lambda i: (0, clamp(i), 0))
    else:
        assert rows_per_mod % tile == 0
        mod_spec = pl.BlockSpec((1, 1, d), lambda i: (clamp(i) // (rows_per_mod // tile), 0, 0))
    in_specs = [
        pl.BlockSpec((tile, d), lambda i: (clamp(i), 0)),
        mod_spec,
        mod_spec,
        _resident((1, d)),
        _resident((ne, d)),
        _resident((ne, LANES)),
    ]
    args = [x2d, sh, sc, g.reshape(1, d), router_w.T, base_cnt]
    n_in = len(args)
    aliases = {}
    body = functools.partial(_router_kernel, n_steps=n_steps)
    if h_prev is not None:
        assert h_prev.shape == (h_rows, d)
        in_specs.append(pl.BlockSpec(memory_space=pl.ANY))
        args.append(h_prev)
        aliases = {n_in: 0}

        def body(*refs):
            _router_kernel(*refs[:n_in], *refs[n_in + 1:], n_steps=n_steps)

    blk_off = row_off // tile
    return pl.pallas_call(
        body,
        grid=(n_steps + int(fill_tail),),
        in_specs=in_specs,
        out_specs=[
            pl.BlockSpec((tile, d), lambda i: (i + blk_off, 0)),
            pl.BlockSpec((ne, tile), lambda i: (0, clamp(i))),
            pl.BlockSpec((ne, LANES), lambda i: (0, 0)),
        ],
        out_shape=[
            jax.ShapeDtypeStruct((h_rows, d), F32),
            jax.ShapeDtypeStruct((ne, n), F32),
            jax.ShapeDtypeStruct((ne, LANES), F32),
        ],
        scratch_shapes=[pltpu.VMEM((tile, tile), BF16), pltpu.VMEM((ne, LANES), F32)],
        input_output_aliases=aliases,
        compiler_params=_params("arbitrary"),
        name="router",
    )(*args)


def _row_copy(src_hbm, row, dst, dst_row, sem):
    return pltpu.make_async_copy(src_hbm.at[pl.ds(row, 1), :], dst.at[pl.ds(dst_row, 1), :], sem)


def _row_copy_out(src, src_row, dst_hbm, row, sem):
    return pltpu.make_async_copy(src.at[pl.ds(src_row, 1), :], dst_hbm.at[pl.ds(row, 1), :], sem)


def _moe_grouped_kernel(te_ref, na_ref, src_ref, dst_ref, h_hbm, wg_ref, wu_ref, wd_ref, y_hbm,
                        xs, hbuf, obuf, gsem, ssem, fsem, *, tm, nc, rows_per_step, fill_row,
                        fill_blocks):
    del te_ref
    i = pl.program_id(0)
    c = pl.program_id(1)
    n_active = na_ref[0]
    ahead = ROW_SLOTS - 1
    slot = i % ROW_SLOTS

    def request(tile, r):
        s = tile % ROW_SLOTS
        return _row_copy(h_hbm, src_ref[tile * tm + r], xs.at[s], r, gsem.at[s])

    def send(tile, r):
        s = (tile + ROW_SLOTS) % ROW_SLOTS
        return _row_copy_out(obuf.at[s], r, y_hbm, dst_ref[(tile + 1) * tm + r], ssem.at[s])

    @pl.when(jnp.logical_and(i == 0, c == 0))
    def _():
        last = ROW_SLOTS - 1
        obuf[last] = jnp.zeros((tm, obuf.shape[-1]), F32)
        fills = [pltpu.make_async_copy(obuf.at[last, pl.ds(0, FILL_ROWS), :],
                                       y_hbm.at[pl.ds(fill_row + q * FILL_ROWS, FILL_ROWS), :],
                                       fsem) for q in range(fill_blocks)]
        for f in fills:
            f.start()
        for f in fills:
            f.wait()

        def body(r, carry):
            for t in range(ahead):
                request(t, r).start()
            return carry

        lax.fori_loop(0, tm, body, 0)

    @pl.when(jnp.logical_and(i < n_active + ahead, c == 0))
    def _():
        pltpu.make_async_copy(h_hbm.at[pl.ds(0, tm), :], xs.at[slot], gsem.at[slot]).wait()

    @pl.when(jnp.logical_and(jnp.logical_and(i >= ahead, i < n_active + ROW_SLOTS), c == 0))
    def _():
        pltpu.make_async_copy(obuf.at[slot], y_hbm.at[pl.ds(0, tm), :], ssem.at[slot]).wait()

    @pl.when(jnp.logical_and(i < n_active, c == 0))
    def _():
        obuf[slot] = jnp.zeros((tm, obuf.shape[-1]), F32)
        hbuf[...] = xs[slot].astype(BF16)

    @pl.when(jnp.logical_and(i == n_active, c == 0))
    def _():
        def body(r, carry):
            send(i - 1, r).start()
            return carry

        lax.fori_loop(0, tm, body, 0)

    @pl.when(jnp.logical_and(i < n_active, c == 0))
    def _():
        for r in range(rows_per_step * nc, tm):
            request(i + ahead, r).start()
            send(i - 1, r).start()

    @pl.when(i < n_active)
    def _():
        for k in range(rows_per_step):
            r = c * rows_per_step + k
            request(i + ahead, r).start()
            send(i - 1, r).start()

        h = hbuf[...]
        a = _dot(h, wg_ref[0].astype(BF16))
        b = _dot(h, wu_ref[0].astype(BF16))
        obuf[slot] += _dot(_silu_mul(a, b).astype(BF16), wd_ref[0].astype(BF16))


def _moe_grouped(h_all, tile_expert, n_active, src, dst, wg, wu, wd, tm, fc, y_rows, fill_row,
                 fill_rows):
    ne, d, dff = wg.shape
    n_tiles = tile_expert.shape[0]
    nc = dff // fc
    assert dff % fc == 0
    assert src.shape[0] == n_tiles * tm and dst.shape[0] == n_tiles * tm
    assert fill_rows % FILL_ROWS == 0 and tm >= FILL_ROWS
    rows_per_step = tm // nc

    def w_col(i, c, te, na, src_, dst_):
        return (te[i], 0, jnp.where(i < na[0], c, nc - 1))

    def w_row(i, c, te, na, src_, dst_):
        return (te[i], jnp.where(i < na[0], c, nc - 1), 0)

    grid_spec = pltpu.PrefetchScalarGridSpec(
        num_scalar_prefetch=4,
        grid=(n_tiles, nc),
        in_specs=[
            pl.BlockSpec(memory_space=pl.ANY),
            pl.BlockSpec((1, d, fc), w_col),
            pl.BlockSpec((1, d, fc), w_col),
            pl.BlockSpec((1, fc, d), w_row),
        ],
        out_specs=pl.BlockSpec(memory_space=pl.ANY),
        scratch_shapes=[
            pltpu.VMEM((ROW_SLOTS, tm, d), F32),
            pltpu.VMEM((tm, d), BF16),
            pltpu.VMEM((ROW_SLOTS, tm, d), F32),
            pltpu.SemaphoreType.DMA((ROW_SLOTS,)),
            pltpu.SemaphoreType.DMA((ROW_SLOTS,)),
            pltpu.SemaphoreType.DMA,
        ],
    )
    return pl.pallas_call(
        functools.partial(_moe_grouped_kernel, tm=tm, nc=nc, rows_per_step=rows_per_step,
                          fill_row=fill_row, fill_blocks=fill_rows // FILL_ROWS),
        grid_spec=grid_spec,
        out_shape=jax.ShapeDtypeStruct((y_rows, d), F32),
        compiler_params=_params("arbitrary", "arbitrary"),
        name="moe_grouped",
    )(tile_expert, n_active, src, dst, h_all, wg, wu, wd)


def _combine_kernel(x_ref, gt_ref, route_ref, gf_ref, y0_ref, y1_ref, o_ref):
    route = route_ref[...]
    pad = jnp.zeros((LANES - route.shape[0], route.shape[1]), F32)
    cols = jnp.concatenate([route, pad], axis=0).T
    f = (cols[:, ROW_PROB:ROW_PROB + 1] * y0_ref[...]
         + cols[:, ROW_PROB + 1:ROW_PROB + 2] * y1_ref[...])
    o_ref[...] = _rms(x_ref[...] + gt_ref[0] * f, gf_ref[...])


def _combine(x2d, gt, route, g_final, y_tok, tile, rows_per_mod, tok_off, plane_rows):
    n, d = x2d.shape
    assert n % tile == 0 and tok_off % tile == 0 and plane_rows % tile == 0
    per_row_mod = gt.shape[1] > 1
    if per_row_mod:
        mod_spec = pl.BlockSpec((1, tile, d), lambda j: (0, j, 0))
    else:
        assert rows_per_mod % tile == 0
        mod_spec = pl.BlockSpec((1, 1, d), lambda j: (j // (rows_per_mod // tile), 0, 0))
    first = tok_off // tile
    second = (plane_rows + tok_off) // tile
    return pl.pallas_call(
        _combine_kernel,
        grid=(n // tile,),
        in_specs=[
            pl.BlockSpec((tile, d), lambda j: (j, 0)),
            mod_spec,
            pl.BlockSpec((ROUTE_ROWS, tile), lambda j: (0, j)),
            pl.BlockSpec((1, d), lambda j: (0, 0)),
            pl.BlockSpec((tile, d), lambda j: (first + j, 0)),
            pl.BlockSpec((tile, d), lambda j: (second + j, 0)),
        ],
        out_specs=pl.BlockSpec((tile, d), lambda j: (j, 0)),
        out_shape=jax.ShapeDtypeStruct((n, d), F32),
        compiler_params=_params("arbitrary"),
        name="combine",
    )(x2d, gt, route, g_final.reshape(1, d), y_tok, y_tok)


def _invert_kernel(pos_ref, inv_ref):
    def clear(r, carry):
        inv_ref[r] = 0
        return carry

    lax.fori_loop(0, inv_ref.shape[0], clear, 0, unroll=32)

    def place(a, carry):
        inv_ref[pos_ref[a]] = a
        return carry

    lax.fori_loop(0, pos_ref.shape[0], place, 0, unroll=32)


def _invert(pos_flat, n_rows):
    assert pos_flat.shape[0] % 32 == 0 and n_rows % 32 == 0
    smem = pl.BlockSpec(memory_space=pltpu.SMEM)
    return pl.pallas_call(
        _invert_kernel,
        in_specs=[smem],
        out_specs=smem,
        out_shape=jax.ShapeDtypeStruct((n_rows,), I32),
        name="invert",
    )(pos_flat)


def _routing_tables(route_all, counts, tm, n_tiles, plane_rows):
    ne = counts.shape[0]
    n = route_all.shape[1]
    e_idx = route_all[ROW_EXPERT:ROW_EXPERT + TOP_K].astype(I32)
    rank = route_all[ROW_RANK:ROW_RANK + TOP_K].astype(I32)
    padded = (counts + tm - 1) // tm * tm
    ends = jnp.cumsum(padded)
    offs = ends - padded
    onehot = e_idx[..., None] == jnp.arange(ne, dtype=I32)
    pos = jnp.sum(jnp.where(onehot, offs, 0), axis=-1) + rank
    n_active = ends[-1] // tm
    starts = jnp.arange(n_tiles, dtype=I32) * tm
    tile_expert = jnp.sum(starts[:, None] >= ends[None, :], axis=1).astype(I32)
    last_expert = jnp.sum((n_active - 1) * tm >= ends).astype(I32)
    tile_expert = jnp.minimum(tile_expert, last_expert)
    n_rows = n_tiles * tm
    inv = _invert(pos.reshape(-1), n_rows)
    r = jnp.arange(n_rows, dtype=I32)[:, None]
    real_before = jnp.sum(jnp.clip(r - offs, 0, counts), axis=1)
    is_real = jnp.any(jnp.logical_and(r >= offs, r < offs + counts), axis=1)
    tok = inv % n
    choice = inv // n
    spare = n + jnp.minimum(r[:, 0] - real_before, ne * tm - 1)
    src = jnp.where(is_real, tok, 0)
    dst = jnp.where(is_real, choice * plane_rows + tok, spare)
    lead = n + ne * tm + jnp.arange(tm, dtype=I32)
    dst = jnp.concatenate([lead, dst])[:n_rows]
    return src, dst, tile_expert, n_active.astype(I32).reshape(1)


def _moe_layer(xp, xs, mod_p, mod_s, g, router_w, wg, wu, wd, g_final):
    nb, seq, d = xp.shape
    ns = xs.shape[0]
    ne = router_w.shape[-1]
    n_all = nb * seq + ns
    sh_p, sc_p, gt_p = mod_p
    sh_s, sc_s, gt_s = mod_s
    xp2 = xp.reshape(nb * seq, d)

    zero_cnt = jnp.zeros((ne, LANES), F32)
    h_all, route_p, cnt_p = _router(xp2, sh_p, sc_p, g, router_w, zero_cnt, ROUTER_TILE, seq,
                                    n_all)
    h_all, route_s, cnt_s = _router(xs, sh_s, sc_s, g, router_w, cnt_p, ns, ns, n_all,
                                    h_prev=h_all, row_off=nb * seq)

    tm = MOE_ROW_TILE
    n_tiles = (n_all * TOP_K) // tm + ne + ROW_SLOTS
    plane_rows = -(-(n_all + (ne + 1) * tm) // COMBINE_TILE) * COMBINE_TILE
    route_all = jnp.concatenate([route_p, route_s], axis=1)
    src, dst, tile_expert, n_active = _routing_tables(
        route_all, cnt_s[:, 0].astype(I32), tm, n_tiles, plane_rows)

    y_tok = _moe_grouped(h_all, tile_expert, n_active, src, dst, wg, wu, wd, tm, MOE_FF_CHUNK,
                         plane_rows + n_all, n_all, plane_rows - n_all)
    yp = _combine(xp2, gt_p, route_p, g_final, y_tok, COMBINE_TILE, seq, 0, plane_rows)
    ys = _combine(xs, gt_s, route_s, g_final, y_tok, ns, ns, nb * seq, plane_rows)
    return yp.reshape(nb, seq, d), ys


def kernel(x_prompt, x_sample, c_prompt, c_sample, state_conv, state_pool, w_ada, b_ada, g_mix,
           g_ffn, w_in, conv_w, pool_w, pool_scale, g_conv_out, g_pool_out, w_out, dense_w_gate,
           dense_w_up, dense_w_down, router_w, moe_w_gate, moe_w_up, moe_w_down, g_final):
    depth = w_ada.shape[0]
    nb, seq, d = x_prompt.shape
    ns = x_sample.shape[0]
    assert x_sample.shape[1] == 1 and depth == 2
    assert sum(DENSE_FF_CHUNKS) == dense_w_gate.shape[-1]

    mod = _ada(jnp.concatenate([c_prompt, c_sample], axis=0), w_ada, b_ada)
    mod = mod.reshape(depth, nb + ns, 6, d)
    mod_p = [jnp.transpose(mod[i, :nb], (1, 0, 2))[:, :, None, :] for i in range(depth)]
    mod_s = [jnp.transpose(mod[i, nb:], (1, 0, 2))[:, None, :, :] for i in range(depth)]

    w_in_b = w_in.astype(BF16)
    w_out_b = w_out.astype(BF16)
    pool_w_b = pool_w.astype(BF16)
    dense_b = [w.astype(BF16) for w in (dense_w_gate, dense_w_up, dense_w_down)]
    cb_t = jnp.transpose(state_conv, (0, 2, 1, 3))
    pb_t = jnp.transpose(state_pool, (0, 2, 1, 3))

    xp = x_prompt
    xs = x_sample.reshape(ns, d)
    conv_p, pool_p, conv_s, pool_s = [], [], [], []
    for i in range(depth):
        sh1, sc1, gt1, sh2, sc2, gt2 = mod_p[i]
        mix_w = (g_mix[i], w_in_b[i], conv_w[i], pool_w_b[i], pool_scale[i], g_conv_out[i],
                 g_pool_out[i], w_out_b[i])
        xp, cs, ps = _mix_prompt(xp, sh1, sc1, gt1, *mix_w, tile=MIX_TILE, sub=MIX_ROW_BLOCK)
        conv_p.append(cs)
        pool_p.append(ps)
        s1, c1, t1, s2, c2, t2 = mod_s[i]
        xs, v_new, u_new = _mix_sample(xs, s1[0], c1[0], t1[0], *mix_w, cb_t[i], pb_t[i])
        conv_s.append(jnp.concatenate([state_conv[i][:, 1:], v_new[:, None, :]], axis=1))
        pool_s.append(jnp.concatenate([state_pool[i][:, 1:], u_new[:, None, :]], axis=1))
        j = i // 2
        if i % 2 == 0:
            wg, wu, wd = (w[j] for w in dense_b)
            xp = _ffn(xp, sh2, sc2, gt2, g_ffn[i], wg, wu, wd, PROMPT_TILE, DENSE_FF_CHUNKS)
            xs = _ffn(xs[None], s2, c2, t2, g_ffn[i], wg, wu, wd, ns, DENSE_FF_CHUNKS)[0]
        else:
            xp, xs = _moe_layer(xp, xs, (sh2, sc2, gt2), (s2, c2, t2), g_ffn[i], router_w[j],
                                moe_w_gate[j], moe_w_up[j], moe_w_down[j], g_final)

    return (xp, xs.reshape(ns, 1, d), jnp.stack(conv_p), jnp.stack(pool_p),
            jnp.stack(conv_s), jnp.stack(pool_s))
```

```python
import functools

import jax
import jax.numpy as jnp
from jax import lax
from jax.experimental import pallas as pl
from jax.experimental.pallas import tpu as pltpu

F32 = jnp.float32
BF16 = jnp.bfloat16
I32 = jnp.int32

EPS = 1e-6
CONV_K = 3
POOL_WINDOWS = (2, 4, 8, 16)
POOL_HIST = max(POOL_WINDOWS) - 1
TOP_K = 2

CONV_PAD = 8
POOL_PAD = 16

LANES = 128

ROUTE_ROWS = 8
ROW_EXPERT, ROW_RANK, ROW_PROB = 0, 2, 4

VMEM_LIMIT_BYTES = 56 * 1024 * 1024

PROMPT_TILE = 512
MIX_TILE = 2048
MIX_ROW_BLOCK = 512
ROUTER_TILE = 1024
COMBINE_TILE = 512
MOE_ROW_TILE = 1024
MOE_FF_CHUNK = 512
FILL_ROWS = 128
ROW_SLOTS = 3
DENSE_FF_CHUNKS = (768, 768, 768, 512)


def _params(*sem):
    return pltpu.CompilerParams(dimension_semantics=sem, vmem_limit_bytes=VMEM_LIMIT_BYTES)


def _resident(shape):
    nd = len(shape)
    return pl.BlockSpec(shape, lambda *_: (0,) * nd, pipeline_mode=pl.Buffered(1))


def _rms(x, g):
    ms = jnp.mean(x * x, axis=-1, keepdims=True)
    return x * lax.rsqrt(ms + EPS) * g


def _mod_norm(x, g, sc, sh):
    ms = jnp.mean(x * x, axis=-1, keepdims=True)
    return x * lax.rsqrt(ms + EPS) * (g * (1.0 + sc)) + sh


def _dot(a, b):
    return jnp.dot(a, b, preferred_element_type=F32)


def _dot_nt(a, b):
    return lax.dot_general(a, b, (((1,), (1,)), ((), ())), preferred_element_type=F32)


def _silu_mul(a, b):
    return a * jax.nn.sigmoid(a) * b


def _ada_kernel(c_ref, w_ref, b_ref, o_ref):
    c = c_ref[...]
    a = (c * jax.nn.sigmoid(c)).astype(BF16)
    o_ref[0] = _dot(a, w_ref[0].astype(BF16)) + b_ref[0]


def _ada(c_all, w_ada, b_ada):
    depth, d, n = w_ada.shape
    m = c_all.shape[0]
    tn = 1024
    return pl.pallas_call(
        _ada_kernel,
        grid=(depth, n // tn),
        in_specs=[
            pl.BlockSpec((m, d), lambda i, j: (0, 0)),
            pl.BlockSpec((1, d, tn), lambda i, j: (i, 0, j)),
            pl.BlockSpec((1, 1, tn), lambda i, j: (i, 0, j)),
        ],
        out_specs=pl.BlockSpec((1, m, tn), lambda i, j: (i, 0, j)),
        out_shape=jax.ShapeDtypeStruct((depth, m, n), F32),
        compiler_params=_params("arbitrary", "arbitrary"),
        name="ada",
    )(c_all, w_ada, b_ada.reshape(depth, 1, n))


def _mix_tail(x, gt, bg, y, d_groups, poolw_ref, pscale_ref, gco_ref, gpo_ref, wout_ref):
    cw = gco_ref.shape[-1]
    ya = bg * y
    yb = jnp.concatenate(
        [_dot(d.astype(BF16), poolw_ref[g]) for g, d in enumerate(d_groups)], axis=-1
    ) * pscale_ref[...]
    ma = _rms(ya, gco_ref[...]).astype(BF16)
    mb = _rms(yb, gpo_ref[...]).astype(BF16)
    o = _dot(ma, wout_ref[0:cw, :]) + _dot(mb, wout_ref[cw:, :])
    return x + gt * o


def _mix_prompt_kernel(x_ref, sh_ref, sc_ref, gt_ref, g_ref, win_ref, convw_ref, poolw_ref,
                       pscale_ref, gco_ref, gpo_ref, wout_ref,
                       xo_ref, cs_ref, ps_ref, vbuf, ubuf, *, tile, sub):
    l = pl.program_id(1)
    cw = convw_ref.shape[-1]
    pg = poolw_ref.shape[-1]

    @pl.when(l == 0)
    def _():
        vbuf[0:CONV_PAD, :] = jnp.zeros((CONV_PAD, cw), F32)
        ubuf[0:POOL_PAD, :] = jnp.zeros((POOL_PAD, ubuf.shape[-1]), F32)

    gates = []
    for lo in range(0, tile, sub):
        x = x_ref[0, lo:lo + sub, :]
        h = _mod_norm(x, g_ref[...], sc_ref[0], sh_ref[0]).astype(BF16)
        p = _dot(h, win_ref[...])
        gates.append(p[:, 0:cw])
        vbuf[CONV_PAD + lo:CONV_PAD + lo + sub, :] = p[:, cw:2 * cw] * p[:, 2 * cw:3 * cw]
        ubuf[POOL_PAD + lo:POOL_PAD + lo + sub, :] = p[:, 3 * cw:]

    w = convw_ref[...]
    for bg, lo in zip(gates, range(0, tile, sub)):
        y = w[CONV_K - 1:CONV_K] * vbuf[CONV_PAD + lo:CONV_PAD + lo + sub, :]
        for k in range(1, CONV_K):
            y = y + w[CONV_K - 1 - k:CONV_K - k] * vbuf[CONV_PAD + lo - k:CONV_PAD + lo - k + sub, :]

        pos = l * tile + lo + lax.broadcasted_iota(I32, (sub, 1), 0)
        d_groups = []
        for g, win in enumerate(POOL_WINDOWS):
            ug = ubuf[POOL_PAD + lo:POOL_PAD + lo + sub, g * pg:(g + 1) * pg]
            acc = ug
            for k in range(1, win):
                acc = acc + ubuf[POOL_PAD + lo - k:POOL_PAD + lo - k + sub, g * pg:(g + 1) * pg]
            cnt = jnp.minimum(pos + 1, win).astype(F32)
            d_groups.append(acc / cnt - ug)

        xo_ref[0, lo:lo + sub, :] = _mix_tail(x_ref[0, lo:lo + sub, :], gt_ref[0], bg, y, d_groups,
                                              poolw_ref, pscale_ref, gco_ref, gpo_ref, wout_ref)

    cs_ref[0] = vbuf[CONV_PAD + tile - (CONV_K - 1):CONV_PAD + tile, :]
    ps_ref[0] = ubuf[POOL_PAD + tile - POOL_HIST:POOL_PAD + tile, :]
    vbuf[0:CONV_PAD, :] = vbuf[tile:tile + CONV_PAD, :]
    ubuf[0:POOL_PAD, :] = ubuf[tile:tile + POOL_PAD, :]


def _mix_prompt(x, sh, sc, gt, g, w_in, conv_w, pool_w, pool_scale, g_co, g_po, w_out, tile, sub):
    b, seq, d = x.shape
    assert seq % tile == 0 and tile % sub == 0 and sub >= POOL_PAD
    cw = conv_w.shape[-1]
    pw = pool_scale.shape[-1]
    row = lambda bi, li: (bi, 0, 0)
    return pl.pallas_call(
        functools.partial(_mix_prompt_kernel, tile=tile, sub=sub),
        grid=(b, seq // tile),
        in_specs=[
            pl.BlockSpec((1, tile, d), lambda bi, li: (bi, li, 0)),
            pl.BlockSpec((1, 1, d), row),
            pl.BlockSpec((1, 1, d), row),
            pl.BlockSpec((1, 1, d), row),
            _resident((1, d)),
            _resident(w_in.shape),
            _resident(conv_w.shape),
            _resident(pool_w.shape),
            _resident((1, pw)),
            _resident((1, cw)),
            _resident((1, pw)),
            _resident(w_out.shape),
        ],
        out_specs=[
            pl.BlockSpec((1, tile, d), lambda bi, li: (bi, li, 0)),
            pl.BlockSpec((1, CONV_K - 1, cw), row),
            pl.BlockSpec((1, POOL_HIST, pw), row),
        ],
        out_shape=[
            jax.ShapeDtypeStruct((b, seq, d), F32),
            jax.ShapeDtypeStruct((b, CONV_K - 1, cw), F32),
            jax.ShapeDtypeStruct((b, POOL_HIST, pw), F32),
        ],
        scratch_shapes=[
            pltpu.VMEM((CONV_PAD + tile, cw), F32),
            pltpu.VMEM((POOL_PAD + tile, pw), F32),
        ],
        compiler_params=_params("arbitrary", "arbitrary"),
        name="mix_prompt",
    )(x, sh, sc, gt, g.reshape(1, d), w_in, conv_w, pool_w, pool_scale.reshape(1, pw),
      g_co.reshape(1, cw), g_po.reshape(1, pw), w_out)


def _mix_sample_kernel(x_ref, sh_ref, sc_ref, gt_ref, g_ref, win_ref, convw_ref, poolw_ref,
                       pscale_ref, gco_ref, gpo_ref, wout_ref, cb_ref, pb_ref,
                       xo_ref, v_ref, u_ref):
    cw = convw_ref.shape[-1]
    pg = poolw_ref.shape[-1]
    x = x_ref[...]
    h = _mod_norm(x, g_ref[...], sc_ref[...], sh_ref[...]).astype(BF16)
    p = _dot(h, win_ref[...])
    bg = p[:, 0:cw]
    v = p[:, cw:2 * cw] * p[:, 2 * cw:3 * cw]
    u = p[:, 3 * cw:]
    v_ref[...] = v
    u_ref[...] = u

    w = convw_ref[...]
    y = w[CONV_K - 1:CONV_K] * v
    for k in range(1, CONV_K):
        y = y + w[CONV_K - 1 - k:CONV_K - k] * cb_ref[CONV_K - 1 - k]

    d_groups = []
    for g, win in enumerate(POOL_WINDOWS):
        ug = u[:, g * pg:(g + 1) * pg]
        acc = ug
        for k in range(1, win):
            acc = acc + pb_ref[POOL_HIST - k, :, g * pg:(g + 1) * pg]
        d_groups.append(acc / float(win) - ug)

    xo_ref[...] = _mix_tail(x, gt_ref[...], bg, y, d_groups, poolw_ref, pscale_ref,
                            gco_ref, gpo_ref, wout_ref)


def _mix_sample(x, sh, sc, gt, g, w_in, conv_w, pool_w, pool_scale, g_co, g_po, w_out, cb, pb):
    n, d = x.shape
    cw = conv_w.shape[-1]
    pw = pool_scale.shape[-1]
    return pl.pallas_call(
        _mix_sample_kernel,
        out_shape=[
            jax.ShapeDtypeStruct((n, d), F32),
            jax.ShapeDtypeStruct((n, cw), F32),
            jax.ShapeDtypeStruct((n, pw), F32),
        ],
        compiler_params=pltpu.CompilerParams(vmem_limit_bytes=VMEM_LIMIT_BYTES),
        name="mix_sample",
    )(x, sh, sc, gt, g.reshape(1, d), w_in, conv_w, pool_w, pool_scale.reshape(1, pw),
      g_co.reshape(1, cw), g_po.reshape(1, pw), w_out, cb, pb)


def _ffn_kernel(x_ref, sh_ref, sc_ref, gt_ref, g_ref, wg_ref, wu_ref, wd_ref, o_ref, *, chunks):
    x = x_ref[0]
    h = _mod_norm(x, g_ref[...], sc_ref[0], sh_ref[0]).astype(BF16)
    acc = None
    lo = 0
    for fc in chunks:
        a = _dot(h, wg_ref[:, lo:lo + fc])
        b = _dot(h, wu_ref[:, lo:lo + fc])
        part = _dot(_silu_mul(a, b).astype(BF16), wd_ref[lo:lo + fc, :])
        acc = part if acc is None else acc + part
        lo += fc
    o_ref[0] = x + gt_ref[0] * acc


def _ffn(x, sh, sc, gt, g, wg, wu, wd, tile, chunks):
    b, seq, d = x.shape
    tm = sh.shape[1]
    tmod = tile if tm > 1 else 1
    mod_map = (lambda bi, li: (bi, li, 0)) if tm > 1 else (lambda bi, li: (bi, 0, 0))
    return pl.pallas_call(
        functools.partial(_ffn_kernel, chunks=chunks),
        grid=(b, seq // tile),
        in_specs=[
            pl.BlockSpec((1, tile, d), lambda bi, li: (bi, li, 0)),
            pl.BlockSpec((1, tmod, d), mod_map),
            pl.BlockSpec((1, tmod, d), mod_map),
            pl.BlockSpec((1, tmod, d), mod_map),
            _resident((1, d)),
            _resident(wg.shape),
            _resident(wu.shape),
            _resident(wd.shape),
        ],
        out_specs=pl.BlockSpec((1, tile, d), lambda bi, li: (bi, li, 0)),
        out_shape=jax.ShapeDtypeStruct((b, seq, d), F32),
        compiler_params=_params("arbitrary", "arbitrary"),
        name="ffn",
    )(x, sh, sc, gt, g.reshape(1, d), wg, wu, wd)


def _split_bf16(a):
    hi = a.astype(BF16)
    return hi, (a - hi.astype(F32)).astype(BF16)


def _router_kernel(x_ref, sh_ref, sc_ref, g_ref, rw_ref, base_ref, h_ref, route_ref, cnt_ref,
                   tri, carry, *, n_steps):
    t = x_ref.shape[0]
    step = pl.program_id(0)

    @pl.when(step == 0)
    def _():
        r = lax.broadcasted_iota(I32, (t, t), 0)
        c = lax.broadcasted_iota(I32, (t, t), 1)
        tri[...] = jnp.where(r < c, 1.0, 0.0).astype(BF16)
        carry[...] = base_ref[...]

    @pl.when(step >= n_steps)
    def _():
        h_ref[...] = jnp.zeros_like(h_ref)

    @pl.when(step < n_steps)
    def _():
        h = _mod_norm(x_ref[...], g_ref[...], sc_ref[0], sh_ref[0])
        h_ref[...] = h
        h_hi, h_lo = _split_bf16(h)
        rw_hi, rw_lo = _split_bf16(rw_ref[...])
        logits = _dot_nt(rw_hi, h_hi) + _dot_nt(rw_hi, h_lo) + _dot_nt(rw_lo, h_hi)

        ne = logits.shape[0]
        idx = lax.broadcasted_iota(I32, logits.shape, 0)
        m1 = jnp.max(logits, axis=0, keepdims=True)
        i1 = jnp.min(jnp.where(logits == m1, idx, ne), axis=0, keepdims=True)
        sel1 = idx == i1
        rest = jnp.where(sel1, -jnp.inf, logits)
        m2 = jnp.max(rest, axis=0, keepdims=True)
        i2 = jnp.min(jnp.where(rest == m2, idx, ne), axis=0, keepdims=True)
        sel2 = idx == i2
        e = jnp.exp(m2 - m1)
        p1 = 1.0 / (1.0 + e)
        p2 = e / (1.0 + e)

        chosen = jnp.where(sel1, 1.0, jnp.where(sel2, 1.0, 0.0))
        before = _dot(chosen.astype(BF16), tri[...]) + carry[:, 0:1]
        r1 = jnp.sum(jnp.where(sel1, before, 0.0), axis=0, keepdims=True)
        r2 = jnp.sum(jnp.where(sel2, before, 0.0), axis=0, keepdims=True)
        carry[...] += jnp.sum(chosen, axis=1, keepdims=True)
        cnt_ref[...] = carry[...]

        rows = (i1.astype(F32), i2.astype(F32), r1, r2, p1, p2)
        route = jnp.zeros(logits.shape, F32)
        for k, row in enumerate(rows):
            route = jnp.where(idx == k, row, route)
        route_ref[...] = route


def _router(x2d, sh, sc, g, router_w, base_cnt, tile, rows_per_mod, h_rows, h_prev=None,
            row_off=0):
    n, d = x2d.shape
    ne = router_w.shape[-1]
    assert ne == ROUTE_ROWS and n % tile == 0 and row_off % tile == 0 and tile % LANES == 0
    assert base_cnt.shape == (ne, LANES)
    n_steps = n // tile
    tail = h_rows - (row_off + n)
    assert 0 <= tail < tile
    fill_tail = h_prev is None and tail > 0
    last = n_steps - 1
    clamp = (lambda i: jnp.minimum(i, last)) if fill_tail else (lambda i: i)
    per_row_mod = sh.shape[1] > 1
    if per_row_mod:
        mod_spec = pl.BlockSpec((1, tile, d), lambda i: (0, clamp(i), 0))
    else:
        assert rows_per_mod % tile == 0
        mod_spec = pl.BlockSpec((1, 1, d), lambda i: (clamp(i) // (rows_per_mod // tile), 0, 0))
    in_specs = [
        pl.BlockSpec((tile, d), lambda i: (clamp(i), 0)),
        mod_spec,
        mod_spec,
        _resident((1, d)),
        _resident((ne, d)),
        _resident((ne, LANES)),
    ]
    args = [x2d, sh, sc, g.reshape(1, d), router_w.T, base_cnt]
    n_in = len(args)
    aliases = {}
    body = functools.partial(_router_kernel, n_steps=n_steps)
    if h_prev is not None:
        assert h_prev.shape == (h_rows, d)
        in_specs.append(pl.BlockSpec(memory_space=pl.ANY))
        args.append(h_prev)
        aliases = {n_in: 0}

        def body(*refs):
            _router_kernel(*refs[:n_in], *refs[n_in + 1:], n_steps=n_steps)

    blk_off = row_off // tile
    return pl.pallas_call(
        body,
        grid=(n_steps + int(fill_tail),),
        in_specs=in_specs,
        out_specs=[
            pl.BlockSpec((tile, d), lambda i: (i + blk_off, 0)),
            pl.BlockSpec((ne, tile), lambda i: (0, clamp(i))),
            pl.BlockSpec((ne, LANES), lambda i: (0, 0)),
        ],
        out_shape=[
            jax.ShapeDtypeStruct((h_rows, d), F32),
            jax.ShapeDtypeStruct((ne, n), F32),
            jax.ShapeDtypeStruct((ne, LANES), F32),
        ],
        scratch_shapes=[pltpu.VMEM((tile, tile), BF16), pltpu.VMEM((ne, LANES), F32)],
        input_output_aliases=aliases,
        compiler_params=_params("arbitrary"),
        name="router",
    )(*args)


def _row_copy(src_hbm, row, dst, dst_row, sem):
    return pltpu.make_async_copy(src_hbm.at[pl.ds(row, 1), :], dst.at[pl.ds(dst_row, 1), :], sem)


def _row_copy_out(src, src_row, dst_hbm, row, sem):
    return pltpu.make_async_copy(src.at[pl.ds(src_row, 1), :], dst_hbm.at[pl.ds(row, 1), :], sem)


def _moe_grouped_kernel(te_ref, na_ref, src_ref, dst_ref, h_hbm, wg_ref, wu_ref, wd_ref, y_hbm,
                        xs, hbuf, obuf, gsem, ssem, fsem, *, tm, nc, rows_per_step, fill_row,
                        fill_blocks):
    del te_ref
    i = pl.program_id(0)
    c = pl.program_id(1)
    n_active = na_ref[0]
    ahead = ROW_SLOTS - 1
    slot = i % ROW_SLOTS

    def request(tile, r):
        s = tile % ROW_SLOTS
        return _row_copy(h_hbm, src_ref[tile * tm + r], xs.at[s], r, gsem.at[s])

    def send(tile, r):
        s = (tile + ROW_SLOTS) % ROW_SLOTS
        return _row_copy_out(obuf.at[s], r, y_hbm, dst_ref[(tile + 1) * tm + r], ssem.at[s])

    @pl.when(jnp.logical_and(i == 0, c == 0))
    def _():
        last = ROW_SLOTS - 1
        obuf[last] = jnp.zeros((tm, obuf.shape[-1]), F32)
        fills = [pltpu.make_async_copy(obuf.at[last, pl.ds(0, FILL_ROWS), :],
                                       y_hbm.at[pl.ds(fill_row + q * FILL_ROWS, FILL_ROWS), :],
                                       fsem) for q in range(fill_blocks)]
        for f in fills:
            f.start()
        for f in fills:
            f.wait()

        def body(r, carry):
            for t in range(ahead):
                request(t, r).start()
            return carry

        lax.fori_loop(0, tm, body, 0)

    @pl.when(jnp.logical_and(i < n_active + ahead, c == 0))
    def _():
        pltpu.make_async_copy(h_hbm.at[pl.ds(0, tm), :], xs.at[slot], gsem.at[slot]).wait()

    @pl.when(jnp.logical_and(jnp.logical_and(i >= ahead, i < n_active + ROW_SLOTS), c == 0))
    def _():
        pltpu.make_async_copy(obuf.at[slot], y_hbm.at[pl.ds(0, tm), :], ssem.at[slot]).wait()

    @pl.when(jnp.logical_and(i < n_active, c == 0))
    def _():
        obuf[slot] = jnp.zeros((tm, obuf.shape[-1]), F32)
        hbuf[...] = xs[slot].astype(BF16)

    @pl.when(jnp.logical_and(i == n_active, c == 0))
    def _():
        def body(r, carry):
            send(i - 1, r).start()
            return carry

        lax.fori_loop(0, tm, body, 0)

    @pl.when(jnp.logical_and(i < n_active, c == 0))
    def _():
        for r in range(rows_per_step * nc, tm):
            request(i + ahead, r).start()
            send(i - 1, r).start()

    @pl.when(i < n_active)
    def _():
        for k in range(rows_per_step):
            r = c * rows_per_step + k
            request(i + ahead, r).start()
            send(i - 1, r).start()

        h = hbuf[...]
        a = _dot(h, wg_ref[0].astype(BF16))
        b = _dot(h, wu_ref[0].astype(BF16))
        obuf[slot] += _dot(_silu_mul(a, b).astype(BF16), wd_ref[0].astype(BF16))


def _moe_grouped(h_all, tile_expert, n_active, src, dst, wg, wu, wd, tm, fc, y_rows, fill_row,
                 fill_rows):
    ne, d, dff = wg.shape
    n_tiles = tile_expert.shape[0]
    nc = dff // fc
    assert dff % fc == 0
    assert src.shape[0] == n_tiles * tm and dst.shape[0] == n_tiles * tm
    assert fill_rows % FILL_ROWS == 0 and tm >= FILL_ROWS
    rows_per_step = tm // nc

    def w_col(i, c, te, na, src_, dst_):
        return (te[i], 0, jnp.where(i < na[0], c, nc - 1))

    def w_row(i, c, te, na, src_, dst_):
        return (te[i], jnp.where(i < na[0], c, nc - 1), 0)

    grid_spec = pltpu.PrefetchScalarGridSpec(
        num_scalar_prefetch=4,
        grid=(n_tiles, nc),
        in_specs=[
            pl.BlockSpec(memory_space=pl.ANY),
            pl.BlockSpec((1, d, fc), w_col),
            pl.BlockSpec((1, d, fc), w_col),
            pl.BlockSpec((1, fc, d), w_row),
        ],
        out_specs=pl.BlockSpec(memory_space=pl.ANY),
        scratch_shapes=[
            pltpu.VMEM((ROW_SLOTS, tm, d), F32),
            pltpu.VMEM((tm, d), BF16),
            pltpu.VMEM((ROW_SLOTS, tm, d), F32),
            pltpu.SemaphoreType.DMA((ROW_SLOTS,)),
            pltpu.SemaphoreType.DMA((ROW_SLOTS,)),
            pltpu.SemaphoreType.DMA,
        ],
    )
    return pl.pallas_call(
        functools.partial(_moe_grouped_kernel, tm=tm, nc=nc, rows_per_step=rows_per_step,
                          fill_row=fill_row, fill_blocks=fill_rows // FILL_ROWS),
        grid_spec=grid_spec,
        out_shape=jax.ShapeDtypeStruct((y_rows, d), F32),
        compiler_params=_params("arbitrary", "arbitrary"),
        name="moe_grouped",
    )(tile_expert, n_active, src, dst, h_all, wg, wu, wd)


def _combine_kernel(x_ref, gt_ref, route_ref, gf_ref, y0_ref, y1_ref, o_ref):
    route = route_ref[...]
    pad = jnp.zeros((LANES - route.shape[0], route.shape[1]), F32)
    cols = jnp.concatenate([route, pad], axis=0).T
    f = (cols[:, ROW_PROB:ROW_PROB + 1] * y0_ref[...]
         + cols[:, ROW_PROB + 1:ROW_PROB + 2] * y1_ref[...])
    o_ref[...] = _rms(x_ref[...] + gt_ref[0] * f, gf_ref[...])


def _combine(x2d, gt, route, g_final, y_tok, tile, rows_per_mod, tok_off, plane_rows):
    n, d = x2d.shape
    assert n % tile == 0 and tok_off % tile == 0 and plane_rows % tile == 0
    per_row_mod = gt.shape[1] > 1
    if per_row_mod:
        mod_spec = pl.BlockSpec((1, tile, d), lambda j: (0, j, 0))
    else:
        assert rows_per_mod % tile == 0
        mod_spec = pl.BlockSpec((1, 1, d), lambda j: (j // (rows_per_mod // tile), 0, 0))
    first = tok_off // tile
    second = (plane_rows + tok_off) // tile
    return pl.pallas_call(
        _combine_kernel,
        grid=(n // tile,),
        in_specs=[
            pl.BlockSpec((tile, d), lambda j: (j, 0)),
            mod_spec,
            pl.BlockSpec((ROUTE_ROWS, tile), lambda j: (0, j)),
            pl.BlockSpec((1, d), lambda j: (0, 0)),
            pl.BlockSpec((tile, d), lambda j: (first + j, 0)),
            pl.BlockSpec((tile, d), lambda j: (second + j, 0)),
        ],
        out_specs=pl.BlockSpec((tile, d), lambda j: (j, 0)),
        out_shape=jax.ShapeDtypeStruct((n, d), F32),
        compiler_params=_params("arbitrary"),
        name="combine",
    )(x2d, gt, route, g_final.reshape(1, d), y_tok, y_tok)


def _invert_kernel(pos_ref, inv_ref):
    def clear(r, carry):
        inv_ref[r] = 0
        return carry

    lax.fori_loop(0, inv_ref.shape[0], clear, 0, unroll=32)

    def place(a, carry):
        inv_ref[pos_ref[a]] = a
        return carry

    lax.fori_loop(0, pos_ref.shape[0], place, 0, unroll=32)


def _invert(pos_flat, n_rows):
    assert pos_flat.shape[0] % 32 == 0 and n_rows % 32 == 0
    smem = pl.BlockSpec(memory_space=pltpu.SMEM)
    return pl.pallas_call(
        _invert_kernel,
        in_specs=[smem],
        out_specs=smem,
        out_shape=jax.ShapeDtypeStruct((n_rows,), I32),
        name="invert",
    )(pos_flat)


def _routing_tables(route_all, counts, tm, n_tiles, plane_rows):
    ne = counts.shape[0]
    n = route_all.shape[1]
    e_idx = route_all[ROW_EXPERT:ROW_EXPERT + TOP_K].astype(I32)
    rank = route_all[ROW_RANK:ROW_RANK + TOP_K].astype(I32)
    padded = (counts + tm - 1) // tm * tm
    ends = jnp.cumsum(padded)
    offs = ends - padded
    onehot = e_idx[..., None] == jnp.arange(ne, dtype=I32)
    pos = jnp.sum(jnp.where(onehot, offs, 0), axis=-1) + rank
    n_active = ends[-1] // tm
    starts = jnp.arange(n_tiles, dtype=I32) * tm
    tile_expert = jnp.sum(starts[:, None] >= ends[None, :], axis=1).astype(I32)
    last_expert = jnp.sum((n_active - 1) * tm >= ends).astype(I32)
    tile_expert = jnp.minimum(tile_expert, last_expert)
    n_rows = n_tiles * tm
    inv = _invert(pos.reshape(-1), n_rows)
    r = jnp.arange(n_rows, dtype=I32)[:, None]
    real_before = jnp.sum(jnp.clip(r - offs, 0, counts), axis=1)
    is_real = jnp.any(jnp.logical_and(r >= offs, r < offs + counts), axis=1)
    tok = inv % n
    choice = inv // n
    spare = n + jnp.minimum(r[:, 0] - real_before, ne * tm - 1)
    src = jnp.where(is_real, tok, 0)
    dst = jnp.where(is_real, choice * plane_rows + tok, spare)
    lead = n + ne * tm + jnp.arange(tm, dtype=I32)
    dst = jnp.concatenate([lead, dst])[:n_rows]
    return src, dst, tile_expert, n_active.astype(I32).reshape(1)


def _moe_layer(xp, xs, mod_p, mod_s, g, router_w, wg, wu, wd, g_final):
    nb, seq, d = xp.shape
    ns = xs.shape[0]
    ne = router_w.shape[-1]
    n_all = nb * seq + ns
    sh_p, sc_p, gt_p = mod_p
    sh_s, sc_s, gt_s = mod_s
    xp2 = xp.reshape(nb * seq, d)

    zero_cnt = jnp.zeros((ne, LANES), F32)
    h_all, route_p, cnt_p = _router(xp2, sh_p, sc_p, g, router_w, zero_cnt, ROUTER_TILE, seq,
                                    n_all)
    h_all, route_s, cnt_s = _router(xs, sh_s, sc_s, g, router_w, cnt_p, ns, ns, n_all,
                                    h_prev=h_all, row_off=nb * seq)

    tm = MOE_ROW_TILE
    n_tiles = (n_all * TOP_K) // tm + ne + ROW_SLOTS
    plane_rows = -(-(n_all + (ne + 1) * tm) // COMBINE_TILE) * COMBINE_TILE
    route_all = jnp.concatenate([route_p, route_s], axis=1)
    src, dst, tile_expert, n_active = _routing_tables(
        route_all, cnt_s[:, 0].astype(I32), tm, n_tiles, plane_rows)

    y_tok = _moe_grouped(h_all, tile_expert, n_active, src, dst, wg, wu, wd, tm, MOE_FF_CHUNK,
                         plane_rows + n_all, n_all, plane_rows - n_all)
    yp = _combine(xp2, gt_p, route_p, g_final, y_tok, COMBINE_TILE, seq, 0, plane_rows)
    ys = _combine(xs, gt_s, route_s, g_final, y_tok, ns, ns, nb * seq, plane_rows)
    return yp.reshape(nb, seq, d), ys


def kernel(x_prompt, x_sample, c_prompt, c_sample, state_conv, state_pool, w_ada, b_ada, g_mix,
           g_ffn, w_in, conv_w, pool_w, pool_scale, g_conv_out, g_pool_out, w_out, dense_w_gate,
           dense_w_up, dense_w_down, router_w, moe_w_gate, moe_w_up, moe_w_down, g_final):
    depth = w_ada.shape[0]
    nb, seq, d = x_prompt.shape
    ns = x_sample.shape[0]
    assert x_sample.shape[1] == 1 and depth == 2
    assert sum(DENSE_FF_CHUNKS) == dense_w_gate.shape[-1]

    mod = _ada(jnp.concatenate([c_prompt, c_sample], axis=0), w_ada, b_ada)
    mod = mod.reshape(depth, nb + ns, 6, d)
    mod_p = [jnp.transpose(mod[i, :nb], (1, 0, 2))[:, :, None, :] for i in range(depth)]
    mod_s = [jnp.transpose(mod[i, nb:], (1, 0, 2))[:, None, :, :] for i in range(depth)]

    w_in_b = w_in.astype(BF16)
    w_out_b = w_out.astype(BF16)
    pool_w_b = pool_w.astype(BF16)
    dense_b = [w.astype(BF16) for w in (dense_w_gate, dense_w_up, dense_w_down)]
    cb_t = jnp.transpose(state_conv, (0, 2, 1, 3))
    pb_t = jnp.transpose(state_pool, (0, 2, 1, 3))

    xp = x_prompt
    xs = x_sample.reshape(ns, d)
    conv_p, pool_p, conv_s, pool_s = [], [], [], []
    for i in range(depth):
        sh1, sc1, gt1, sh2, sc2, gt2 = mod_p[i]
        mix_w = (g_mix[i], w_in_b[i], conv_w[i], pool_w_b[i], pool_scale[i], g_conv_out[i],
                 g_pool_out[i], w_out_b[i])
        xp, cs, ps = _mix_prompt(xp, sh1, sc1, gt1, *mix_w, tile=MIX_TILE, sub=MIX_ROW_BLOCK)
        conv_p.append(cs)
        pool_p.append(ps)
        s1, c1, t1, s2, c2, t2 = mod_s[i]
        xs, v_new, u_new = _mix_sample(xs, s1[0], c1[0], t1[0], *mix_w, cb_t[i], pb_t[i])
        conv_s.append(jnp.concatenate([state_conv[i][:, 1:], v_new[:, None, :]], axis=1))
        pool_s.append(jnp.concatenate([state_pool[i][:, 1:], u_new[:, None, :]], axis=1))
        j = i // 2
        if i % 2 == 0:
            wg, wu, wd = (w[j] for w in dense_b)
            xp = _ffn(xp, sh2, sc2, gt2, g_ffn[i], wg, wu, wd, PROMPT_TILE, DENSE_FF_CHUNKS)
            xs = _ffn(xs[None], s2, c2, t2, g_ffn[i], wg, wu, wd, ns, DENSE_FF_CHUNKS)[0]
        else:
            xp, xs = _moe_layer(xp, xs, (sh2, sc2, gt2), (s2, c2, t2), g_ffn[i], router_w[j],
                                moe_w_gate[j], moe_w_up[j], moe_w_down[j], g_final)

    return (xp, xs.reshape(ns, 1, d), jnp.stack(conv_p), jnp.stack(pool_p),
            jnp.stack(conv_s), jnp.stack(pool_s))
```

```python
import functools

import jax
import jax.numpy as jnp
from jax import lax
from jax.experimental import pallas as pl
from jax.experimental.pallas import tpu as pltpu

F32 = jnp.float32
BF16 = jnp.bfloat16
I32 = jnp.int32

EPS = 1e-6
CONV_K = 3
POOL_WINDOWS = (2, 4, 8, 16)
POOL_HIST = max(POOL_WINDOWS) - 1
TOP_K = 2

CONV_PAD = 8
POOL_PAD = 16

LANES = 128

ROUTE_ROWS = 8
ROW_EXPERT, ROW_RANK, ROW_PROB = 0, 2, 4

VMEM_LIMIT_BYTES = 56 * 1024 * 1024

PROMPT_TILE = 512
MIX_TILE = 2048
MIX_ROW_BLOCK = 512
ROUTER_TILE = 1024
COMBINE_TILE = 512
MOE_ROW_TILE = 1024
MOE_FF_CHUNK = 512
FILL_ROWS = 128
ROW_SLOTS = 3
DENSE_FF_CHUNKS = (768, 768, 768, 512)


def _params(*sem):
    return pltpu.CompilerParams(dimension_semantics=sem, vmem_limit_bytes=VMEM_LIMIT_BYTES)


def _resident(shape):
    nd = len(shape)
    return pl.BlockSpec(shape, lambda *_: (0,) * nd, pipeline_mode=pl.Buffered(1))


def _rms(x, g):
    ms = jnp.mean(x * x, axis=-1, keepdims=True)
    return x * lax.rsqrt(ms + EPS) * g


def _mod_norm(x, g, sc, sh):
    ms = jnp.mean(x * x, axis=-1, keepdims=True)
    return x * lax.rsqrt(ms + EPS) * (g * (1.0 + sc)) + sh


def _dot(a, b):
    return jnp.dot(a, b, preferred_element_type=F32)


def _dot_nt(a, b):
    return lax.dot_general(a, b, (((1,), (1,)), ((), ())), preferred_element_type=F32)


def _silu_mul(a, b):
    return a * jax.nn.sigmoid(a) * b


def _ada_kernel(c_ref, w_ref, b_ref, o_ref):
    c = c_ref[...]
    a = (c * jax.nn.sigmoid(c)).astype(BF16)
    o_ref[0] = _dot(a, w_ref[0].astype(BF16)) + b_ref[0]


def _ada(c_all, w_ada, b_ada):
    depth, d, n = w_ada.shape
    m = c_all.shape[0]
    tn = 1024
    return pl.pallas_call(
        _ada_kernel,
        grid=(depth, n // tn),
        in_specs=[
            pl.BlockSpec((m, d), lambda i, j: (0, 0)),
            pl.BlockSpec((1, d, tn), lambda i, j: (i, 0, j)),
            pl.BlockSpec((1, 1, tn), lambda i, j: (i, 0, j)),
        ],
        out_specs=pl.BlockSpec((1, m, tn), lambda i, j: (i, 0, j)),
        out_shape=jax.ShapeDtypeStruct((depth, m, n), F32),
        compiler_params=_params("arbitrary", "arbitrary"),
        name="ada",
    )(c_all, w_ada, b_ada.reshape(depth, 1, n))


def _mix_tail(x, gt, bg, y, d_groups, poolw_ref, pscale_ref, gco_ref, gpo_ref, wout_ref):
    cw = gco_ref.shape[-1]
    ya = bg * y
    yb = jnp.concatenate(
        [_dot(d.astype(BF16), poolw_ref[g]) for g, d in enumerate(d_groups)], axis=-1
    ) * pscale_ref[...]
    ma = _rms(ya, gco_ref[...]).astype(BF16)
    mb = _rms(yb, gpo_ref[...]).astype(BF16)
    o = _dot(ma, wout_ref[0:cw, :]) + _dot(mb, wout_ref[cw:, :])
    return x + gt * o


def _mix_prompt_kernel(x_ref, sh_ref, sc_ref, gt_ref, g_ref, win_ref, convw_ref, poolw_ref,
                       pscale_ref, gco_ref, gpo_ref, wout_ref,
                       xo_ref, cs_ref, ps_ref, vbuf, ubuf, *, tile, sub):
    l = pl.program_id(1)
    cw = convw_ref.shape[-1]
    pg = poolw_ref.shape[-1]

    @pl.when(l == 0)
    def _():
        vbuf[0:CONV_PAD, :] = jnp.zeros((CONV_PAD, cw), F32)
        ubuf[0:POOL_PAD, :] = jnp.zeros((POOL_PAD, ubuf.shape[-1]), F32)

    gates = []
    for lo in range(0, tile, sub):
        x = x_ref[0, lo:lo + sub, :]
        h = _mod_norm(x, g_ref[...], sc_ref[0], sh_ref[0]).astype(BF16)
        p = _dot(h, win_ref[...])
        gates.append(p[:, 0:cw])
        vbuf[CONV_PAD + lo:CONV_PAD + lo + sub, :] = p[:, cw:2 * cw] * p[:, 2 * cw:3 * cw]
        ubuf[POOL_PAD + lo:POOL_PAD + lo + sub, :] = p[:, 3 * cw:]

    w = convw_ref[...]
    for bg, lo in zip(gates, range(0, tile, sub)):
        y = w[CONV_K - 1:CONV_K] * vbuf[CONV_PAD + lo:CONV_PAD + lo + sub, :]
        for k in range(1, CONV_K):
            y = y + w[CONV_K - 1 - k:CONV_K - k] * vbuf[CONV_PAD + lo - k:CONV_PAD + lo - k + sub, :]

        pos = l * tile + lo + lax.broadcasted_iota(I32, (sub, 1), 0)
        d_groups = []
        for g, win in enumerate(POOL_WINDOWS):
            ug = ubuf[POOL_PAD + lo:POOL_PAD + lo + sub, g * pg:(g + 1) * pg]
            acc = ug
            for k in range(1, win):
                acc = acc + ubuf[POOL_PAD + lo - k:POOL_PAD + lo - k + sub, g * pg:(g + 1) * pg]
            cnt = jnp.minimum(pos + 1, win).astype(F32)
            d_groups.append(acc / cnt - ug)

        xo_ref[0, lo:lo + sub, :] = _mix_tail(x_ref[0, lo:lo + sub, :], gt_ref[0], bg, y, d_groups,
                                              poolw_ref, pscale_ref, gco_ref, gpo_ref, wout_ref)

    cs_ref[0] = vbuf[CONV_PAD + tile - (CONV_K - 1):CONV_PAD + tile, :]
    ps_ref[0] = ubuf[POOL_PAD + tile - POOL_HIST:POOL_PAD + tile, :]
    vbuf[0:CONV_PAD, :] = vbuf[tile:tile + CONV_PAD, :]
    ubuf[0:POOL_PAD, :] = ubuf[tile:tile + POOL_PAD, :]


def _mix_prompt(x, sh, sc, gt, g, w_in, conv_w, pool_w, pool_scale, g_co, g_po, w_out, tile, sub):
    b, seq, d = x.shape
    assert seq % tile == 0 and tile % sub == 0 and sub >= POOL_PAD
    cw = conv_w.shape[-1]
    pw = pool_scale.shape[-1]
    row = lambda bi, li: (bi, 0, 0)
    return pl.pallas_call(
        functools.partial(_mix_prompt_kernel, tile=tile, sub=sub),
        grid=(b, seq // tile),
        in_specs=[
            pl.BlockSpec((1, tile, d), lambda bi, li: (bi, li, 0)),
            pl.BlockSpec((1, 1, d), row),
            pl.BlockSpec((1, 1, d), row),
            pl.BlockSpec((1, 1, d), row),
            _resident((1, d)),
            _resident(w_in.shape),
            _resident(conv_w.shape),
            _resident(pool_w.shape),
            _resident((1, pw)),
            _resident((1, cw)),
            _resident((1, pw)),
            _resident(w_out.shape),
        ],
        out_specs=[
            pl.BlockSpec((1, tile, d), lambda bi, li: (bi, li, 0)),
            pl.BlockSpec((1, CONV_K - 1, cw), row),
            pl.BlockSpec((1, POOL_HIST, pw), row),
        ],
        out_shape=[
            jax.ShapeDtypeStruct((b, seq, d), F32),
            jax.ShapeDtypeStruct((b, CONV_K - 1, cw), F32),
            jax.ShapeDtypeStruct((b, POOL_HIST, pw), F32),
        ],
        scratch_shapes=[
            pltpu.VMEM((CONV_PAD + tile, cw), F32),
            pltpu.VMEM((POOL_PAD + tile, pw), F32),
        ],
        compiler_params=_params("arbitrary", "arbitrary"),
        name="mix_prompt",
    )(x, sh, sc, gt, g.reshape(1, d), w_in, conv_w, pool_w, pool_scale.reshape(1, pw),
      g_co.reshape(1, cw), g_po.reshape(1, pw), w_out)


def _mix_sample_kernel(x_ref, sh_ref, sc_ref, gt_ref, g_ref, win_ref, convw_ref, poolw_ref,
                       pscale_ref, gco_ref, gpo_ref, wout_ref, cb_ref, pb_ref,
                       xo_ref, v_ref, u_ref):
    cw = convw_ref.shape[-1]
    pg = poolw_ref.shape[-1]
    x = x_ref[...]
    h = _mod_norm(x, g_ref[...], sc_ref[...], sh_ref[...]).astype(BF16)
    p = _dot(h, win_ref[...])
    bg = p[:, 0:cw]
    v = p[:, cw:2 * cw] * p[:, 2 * cw:3 * cw]
    u = p[:, 3 * cw:]
    v_ref[...] = v
    u_ref[...] = u

    w = convw_ref[...]
    y = w[CONV_K - 1:CONV_K] * v
    for k in range(1, CONV_K):
        y = y + w[CONV_K - 1 - k:CONV_K - k] * cb_ref[CONV_K - 1 - k]

    d_groups = []
    for g, win in enumerate(POOL_WINDOWS):
        ug = u[:, g * pg:(g + 1) * pg]
        acc = ug
        for k in range(1, win):
            acc = acc + pb_ref[POOL_HIST - k, :, g * pg:(g + 1) * pg]
        d_groups.append(acc / float(win) - ug)

    xo_ref[...] = _mix_tail(x, gt_ref[...], bg, y, d_groups, poolw_ref, pscale_ref,
                            gco_ref, gpo_ref, wout_ref)


def _mix_sample(x, sh, sc, gt, g, w_in, conv_w, pool_w, pool_scale, g_co, g_po, w_out, cb, pb):
    n, d = x.shape
    cw = conv_w.shape[-1]
    pw = pool_scale.shape[-1]
    return pl.pallas_call(
        _mix_sample_kernel,
        out_shape=[
            jax.ShapeDtypeStruct((n, d), F32),
            jax.ShapeDtypeStruct((n, cw), F32),
            jax.ShapeDtypeStruct((n, pw), F32),
        ],
        compiler_params=pltpu.CompilerParams(vmem_limit_bytes=VMEM_LIMIT_BYTES),
        name="mix_sample",
    )(x, sh, sc, gt, g.reshape(1, d), w_in, conv_w, pool_w, pool_scale.reshape(1, pw),
      g_co.reshape(1, cw), g_po.reshape(1, pw), w_out, cb, pb)


def _ffn_kernel(x_ref, sh_ref, sc_ref, gt_ref, g_ref, wg_ref, wu_ref, wd_ref, o_ref, *, chunks):
    x = x_ref[0]
    h = _mod_norm(x, g_ref[...], sc_ref[0], sh_ref[0]).astype(BF16)
    acc = None
    lo = 0
    for fc in chunks:
        a = _dot(h, wg_ref[:, lo:lo + fc])
        b = _dot(h, wu_ref[:, lo:lo + fc])
        part = _dot(_silu_mul(a, b).astype(BF16), wd_ref[lo:lo + fc, :])
        acc = part if acc is None else acc + part
        lo += fc
    o_ref[0] = x + gt_ref[0] * acc


def _ffn(x, sh, sc, gt, g, wg, wu, wd, tile, chunks):
    b, seq, d = x.shape
    tm = sh.shape[1]
    tmod = tile if tm > 1 else 1
    mod_map = (lambda bi, li: (bi, li, 0)) if tm > 1 else (lambda bi, li: (bi, 0, 0))
    return pl.pallas_call(
        functools.partial(_ffn_kernel, chunks=chunks),
        grid=(b, seq // tile),
        in_specs=[
            pl.BlockSpec((1, tile, d), lambda bi, li: (bi, li, 0)),
            pl.BlockSpec((1, tmod, d), mod_map),
            pl.BlockSpec((1, tmod, d), mod_map),
            pl.BlockSpec((1, tmod, d), mod_map),
            _resident((1, d)),
            _resident(wg.shape),
            _resident(wu.shape),
            _resident(wd.shape),
        ],
        out_specs=pl.BlockSpec((1, tile, d), lambda bi, li: (bi, li, 0)),
        out_shape=jax.ShapeDtypeStruct((b, seq, d), F32),
        compiler_params=_params("arbitrary", "arbitrary"),
        name="ffn",
    )(x, sh, sc, gt, g.reshape(1, d), wg, wu, wd)


def _split_bf16(a):
    hi = a.astype(BF16)
    return hi, (a - hi.astype(F32)).astype(BF16)


def _router_kernel(x_ref, sh_ref, sc_ref, g_ref, rw_ref, base_ref, h_ref, route_ref, cnt_ref,
                   tri, carry, *, n_steps):
    t = x_ref.shape[0]
    step = pl.program_id(0)

    @pl.when(step == 0)
    def _():
        r = lax.broadcasted_iota(I32, (t, t), 0)
        c = lax.broadcasted_iota(I32, (t, t), 1)
        tri[...] = jnp.where(r < c, 1.0, 0.0).astype(BF16)
        carry[...] = base_ref[...]

    @pl.when(step >= n_steps)
    def _():
        h_ref[...] = jnp.zeros_like(h_ref)

    @pl.when(step < n_steps)
    def _():
        h = _mod_norm(x_ref[...], g_ref[...], sc_ref[0], sh_ref[0])
        h_ref[...] = h
        h_hi, h_lo = _split_bf16(h)
        rw_hi, rw_lo = _split_bf16(rw_ref[...])
        logits = _dot_nt(rw_hi, h_hi) + _dot_nt(rw_hi, h_lo) + _dot_nt(rw_lo, h_hi)

        ne = logits.shape[0]
        idx = lax.broadcasted_iota(I32, logits.shape, 0)
        m1 = jnp.max(logits, axis=0, keepdims=True)
        i1 = jnp.min(jnp.where(logits == m1, idx, ne), axis=0, keepdims=True)
        sel1 = idx == i1
        rest = jnp.where(sel1, -jnp.inf, logits)
        m2 = jnp.max(rest, axis=0, keepdims=True)
        i2 = jnp.min(jnp.where(rest == m2, idx, ne), axis=0, keepdims=True)
        sel2 = idx == i2
        e = jnp.exp(m2 - m1)
        p1 = 1.0 / (1.0 + e)
        p2 = e / (1.0 + e)

        chosen = jnp.where(sel1, 1.0, jnp.where(sel2, 1.0, 0.0))
        before = _dot(chosen.astype(BF16), tri[...]) + carry[:, 0:1]
        r1 = jnp.sum(jnp.where(sel1, before, 0.0), axis=0, keepdims=True)
        r2 = jnp.sum(jnp.where(sel2, before, 0.0), axis=0, keepdims=True)
        carry[...] += jnp.sum(chosen, axis=1, keepdims=True)
        cnt_ref[...] = carry[...]

        rows = (i1.astype(F32), i2.astype(F32), r1, r2, p1, p2)
        route = jnp.zeros(logits.shape, F32)
        for k, row in enumerate(rows):
            route = jnp.where(idx == k, row, route)
        route_ref[...] = route


def _router(x2d, sh, sc, g, router_w, base_cnt, tile, rows_per_mod, h_rows, h_prev=None,
            row_off=0):
    n, d = x2d.shape
    ne = router_w.shape[-1]
    assert ne == ROUTE_ROWS and n % tile == 0 and row_off % tile == 0 and tile % LANES == 0
    assert base_cnt.shape == (ne, LANES)
    n_steps = n // tile
    tail = h_rows - (row_off + n)
    assert 0 <= tail < tile
    fill_tail = h_prev is None and tail > 0
    last = n_steps - 1
    clamp = (lambda i: jnp.minimum(i, last)) if fill_tail else (lambda i: i)
    per_row_mod = sh.shape[1] > 1
    if per_row_mod:
        mod_spec = pl.BlockSpec((1, tile, d), lambda i: (0, clamp(i), 0))
    else:
        assert rows_per_mod % tile == 0
        mod_spec = pl.BlockSpec((1, 1, d), lambda i: (clamp(i) // (rows_per_mod // tile), 0, 0))
    in_specs = [
        pl.BlockSpec((tile, d), lambda i: (clamp(i), 0)),
        mod_spec,
        mod_spec,
        _resident((1, d)),
        _resident((ne, d)),
        _resident((ne, LANES)),
    ]
    args = [x2d, sh, sc, g.reshape(1, d), router_w.T, base_cnt]
    n_in = len(args)
    aliases = {}
    body = functools.partial(_router_kernel, n_steps=n_steps)
    if h_prev is not None:
        assert h_prev.shape == (h_rows, d)
        in_specs.append(pl.BlockSpec(memory_space=pl.ANY))
        args.append(h_prev)
        aliases = {n_in: 0}

        def body(*refs):
            _router_kernel(*refs[:n_in], *refs[n_in + 1:], n_steps=n_steps)

    blk_off = row_off // tile
    return pl.pallas_call(
        body,
        grid=(n_steps + int(fill_tail),),
        in_specs=in_specs,
        out_specs=[
            pl.BlockSpec((tile, d), lambda i: (i + blk_off, 0)),
            pl.BlockSpec((ne, tile), lambda i: (0, clamp(i))),
            pl.BlockSpec((ne, LANES), lambda i: (0, 0)),
        ],
        out_shape=[
            jax.ShapeDtypeStruct((h_rows, d), F32),
            jax.ShapeDtypeStruct((ne, n), F32),
            jax.ShapeDtypeStruct((ne, LANES), F32),
        ],
        scratch_shapes=[pltpu.VMEM((tile, tile), BF16), pltpu.VMEM((ne, LANES), F32)],
        input_output_aliases=aliases,
        compiler_params=_params("arbitrary"),
        name="router",
    )(*args)


def _row_copy(src_hbm, row, dst, dst_row, sem):
    return pltpu.make_async_copy(src_hbm.at[pl.ds(row, 1), :], dst.at[pl.ds(dst_row, 1), :], sem)


def _row_copy_out(src, src_row, dst_hbm, row, sem):
    return pltpu.make_async_copy(src.at[pl.ds(src_row, 1), :], dst_hbm.at[pl.ds(row, 1), :], sem)


def _moe_grouped_kernel(te_ref, na_ref, src_ref, dst_ref, h_hbm, wg_ref, wu_ref, wd_ref, y_hbm,
                        xs, hbuf, obuf, gsem, ssem, fsem, *, tm, nc, rows_per_step, fill_row,
                        fill_blocks):
    del te_ref
    i = pl.program_id(0)
    c = pl.program_id(1)
    n_active = na_ref[0]
    ahead = ROW_SLOTS - 1
    slot = i % ROW_SLOTS

    def request(tile, r):
        s = tile % ROW_SLOTS
        return _row_copy(h_hbm, src_ref[tile * tm + r], xs.at[s], r, gsem.at[s])

    def send(tile, r):
        s = (tile + ROW_SLOTS) % ROW_SLOTS
        return _row_copy_out(obuf.at[s], r, y_hbm, dst_ref[(tile + 1) * tm + r], ssem.at[s])

    @pl.when(jnp.logical_and(i == 0, c == 0))
    def _():
        last = ROW_SLOTS - 1
        obuf[last] = jnp.zeros((tm, obuf.shape[-1]), F32)
        fills = [pltpu.make_async_copy(obuf.at[last, pl.ds(0, FILL_ROWS), :],
                                       y_hbm.at[pl.ds(fill_row + q * FILL_ROWS, FILL_ROWS), :],
                                       fsem) for q in range(fill_blocks)]
        for f in fills:
            f.start()
        for f in fills:
            f.wait()

        def body(r, carry):
            for t in range(ahead):
                request(t, r).start()
            return carry

        lax.fori_loop(0, tm, body, 0)

    @pl.when(jnp.logical_and(i < n_active + ahead, c == 0))
    def _():
        pltpu.make_async_copy(h_hbm.at[pl.ds(0, tm), :], xs.at[slot], gsem.at[slot]).wait()

    @pl.when(jnp.logical_and(jnp.logical_and(i >= ahead, i < n_active + ROW_SLOTS), c == 0))
    def _():
        pltpu.make_async_copy(obuf.at[slot], y_hbm.at[pl.ds(0, tm), :], ssem.at[slot]).wait()

    @pl.when(jnp.logical_and(i < n_active, c == 0))
    def _():
        obuf[slot] = jnp.zeros((tm, obuf.shape[-1]), F32)
        hbuf[...] = xs[slot].astype(BF16)

    @pl.when(jnp.logical_and(i == n_active, c == 0))
    def _():
        def body(r, carry):
            send(i - 1, r).start()
            return carry

        lax.fori_loop(0, tm, body, 0)

    @pl.when(jnp.logical_and(i < n_active, c == 0))
    def _():
        for r in range(rows_per_step * nc, tm):
            request(i + ahead, r).start()
            send(i - 1, r).start()

    @pl.when(i < n_active)
    def _():
        for k in range(rows_per_step):
            r = c * rows_per_step + k
            request(i + ahead, r).start()
            send(i - 1, r).start()

        h = hbuf[...]
        a = _dot(h, wg_ref[0].astype(BF16))
        b = _dot(h, wu_ref[0].astype(BF16))
        obuf[slot] += _dot(_silu_mul(a, b).astype(BF16), wd_ref[0].astype(BF16))


def _moe_grouped(h_all, tile_expert, n_active, src, dst, wg, wu, wd, tm, fc, y_rows, fill_row,
                 fill_rows):
    ne, d, dff = wg.shape
    n_tiles = tile_expert.shape[0]
    nc = dff // fc
    assert dff % fc == 0
    assert src.shape[0] == n_tiles * tm and dst.shape[0] == n_tiles * tm
    assert fill_rows % FILL_ROWS == 0 and tm >= FILL_ROWS
    rows_per_step = tm // nc

    def w_col(i, c, te, na, src_, dst_):
        return (te[i], 0, jnp.where(i < na[0], c, nc - 1))

    def w_row(i, c, te, na, src_, dst_):
        return (te[i], jnp.where(i < na[0], c, nc - 1), 0)

    grid_spec = pltpu.PrefetchScalarGridSpec(
        num_scalar_prefetch=4,
        grid=(n_tiles, nc),
        in_specs=[
            pl.BlockSpec(memory_space=pl.ANY),
            pl.BlockSpec((1, d, fc), w_col),
            pl.BlockSpec((1, d, fc), w_col),
            pl.BlockSpec((1, fc, d), w_row),
        ],
        out_specs=pl.BlockSpec(memory_space=pl.ANY),
        scratch_shapes=[
            pltpu.VMEM((ROW_SLOTS, tm, d), F32),
            pltpu.VMEM((tm, d), BF16),
            pltpu.VMEM((ROW_SLOTS, tm, d), F32),
            pltpu.SemaphoreType.DMA((ROW_SLOTS,)),
            pltpu.SemaphoreType.DMA((ROW_SLOTS,)),
            pltpu.SemaphoreType.DMA,
        ],
    )
    return pl.pallas_call(
        functools.partial(_moe_grouped_kernel, tm=tm, nc=nc, rows_per_step=rows_per_step,
                          fill_row=fill_row, fill_blocks=fill_rows // FILL_ROWS),
        grid_spec=grid_spec,
        out_shape=jax.ShapeDtypeStruct((y_rows, d), F32),
        compiler_params=_params("arbitrary", "arbitrary"),
        name="moe_grouped",
    )(tile_expert, n_active, src, dst, h_all, wg, wu, wd)


def _combine_kernel(x_ref, gt_ref, route_ref, gf_ref, y0_ref, y1_ref, o_ref):
    route = route_ref[...]
    pad = jnp.zeros((LANES - route.shape[0], route.shape[1]), F32)
    cols = jnp.concatenate([route, pad], axis=0).T
    f = (cols[:, ROW_PROB:ROW_PROB + 1] * y0_ref[...]
         + cols[:, ROW_PROB + 1:ROW_PROB + 2] * y1_ref[...])
    o_ref[...] = _rms(x_ref[...] + gt_ref[0] * f, gf_ref[...])


def _combine(x2d, gt, route, g_final, y_tok, tile, rows_per_mod, tok_off, plane_rows):
    n, d = x2d.shape
    assert n % tile == 0 and tok_off % tile == 0 and plane_rows % tile == 0
    per_row_mod = gt.shape[1] > 1
    if per_row_mod:
        mod_spec = pl.BlockSpec((1, tile, d), lambda j: (0, j, 0))
    else:
        assert rows_per_mod % tile == 0
        mod_spec = pl.BlockSpec((1, 1, d), lambda j: (j // (rows_per_mod // tile), 0, 0))
    first = tok_off // tile
    second = (plane_rows + tok_off) // tile
    return pl.pallas_call(
        _combine_kernel,
        grid=(n // tile,),
        in_specs=[
            pl.BlockSpec((tile, d), lambda j: (j, 0)),
            mod_spec,
            pl.BlockSpec((ROUTE_ROWS, tile), lambda j: (0, j)),
            pl.BlockSpec((1, d), lambda j: (0, 0)),
            pl.BlockSpec((tile, d), lambda j: (first + j, 0)),
            pl.BlockSpec((tile, d), lambda j: (second + j, 0)),
        ],
        out_specs=pl.BlockSpec((tile, d), lambda j: (j, 0)),
        out_shape=jax.ShapeDtypeStruct((n, d), F32),
        compiler_params=_params("arbitrary"),
        name="combine",
    )(x2d, gt, route, g_final.reshape(1, d), y_tok, y_tok)


def _invert_kernel(perm_ref, inv_ref):
    def place(a, carry):
        inv_ref[perm_ref[a]] = a
        return carry

    lax.fori_loop(0, perm_ref.shape[0], place, 0, unroll=32)


def _invert(perm):
    assert perm.shape[0] % 32 == 0
    smem = pl.BlockSpec(memory_space=pltpu.SMEM)
    return pl.pallas_call(
        _invert_kernel,
        in_specs=[smem],
        out_specs=smem,
        out_shape=jax.ShapeDtypeStruct(perm.shape, I32),
        name="invert",
    )(perm)


def _routing_tables(route_all, counts, tm, n_tiles, plane_rows):
    ne = counts.shape[0]
    n = route_all.shape[1]
    e_idx = route_all[ROW_EXPERT:ROW_EXPERT + TOP_K].astype(I32)
    rank = route_all[ROW_RANK:ROW_RANK + TOP_K].astype(I32)
    padded = (counts + tm - 1) // tm * tm
    ends = jnp.cumsum(padded)
    offs = ends - padded
    onehot = e_idx[..., None] == jnp.arange(ne, dtype=I32)
    pos = jnp.sum(jnp.where(onehot, offs, 0), axis=-1) + rank
    n_active = ends[-1] // tm
    starts = jnp.arange(n_tiles, dtype=I32) * tm
    tile_expert = jnp.sum(starts[:, None] >= ends[None, :], axis=1).astype(I32)
    last_expert = jnp.sum((n_active - 1) * tm >= ends).astype(I32)
    tile_expert = jnp.minimum(tile_expert, last_expert)
    n_rows = n_tiles * tm
    n_real = n * TOP_K
    pad_ends = jnp.cumsum(padded - counts)
    j = jnp.arange(n_rows - n_real, dtype=I32)[:, None]
    in_expert = jnp.logical_and(j >= pad_ends - (padded - counts), j < pad_ends)
    pad_pos = jnp.where(j[:, 0] < pad_ends[-1],
                        jnp.sum(jnp.where(in_expert, offs + counts + j - pad_ends + padded - counts, 0),
                                axis=1),
                        ends[-1] + j[:, 0] - pad_ends[-1])
    inv = _invert(jnp.concatenate([pos.reshape(-1), pad_pos]))
    is_real = inv < n_real
    tok = inv % n
    choice = inv // n
    spare = n + jnp.minimum(inv - n_real, ne * tm - 1)
    src = jnp.where(is_real, tok, 0)
    dst = jnp.where(is_real, choice * plane_rows + tok, spare)
    lead = n + ne * tm + jnp.arange(tm, dtype=I32)
    dst = jnp.concatenate([lead, dst])[:n_rows]
    return src, dst, tile_expert, n_active.astype(I32).reshape(1)


def _moe_layer(xp, xs, mod_p, mod_s, g, router_w, wg, wu, wd, g_final):
    nb, seq, d = xp.shape
    ns = xs.shape[0]
    ne = router_w.shape[-1]
    n_all = nb * seq + ns
    sh_p, sc_p, gt_p = mod_p
    sh_s, sc_s, gt_s = mod_s
    xp2 = xp.reshape(nb * seq, d)

    zero_cnt = jnp.zeros((ne, LANES), F32)
    h_all, route_p, cnt_p = _router(xp2, sh_p, sc_p, g, router_w, zero_cnt, ROUTER_TILE, seq,
                                    n_all)
    h_all, route_s, cnt_s = _router(xs, sh_s, sc_s, g, router_w, cnt_p, ns, ns, n_all,
                                    h_prev=h_all, row_off=nb * seq)

    tm = MOE_ROW_TILE
    n_tiles = (n_all * TOP_K) // tm + ne + ROW_SLOTS
    plane_rows = -(-(n_all + (ne + 1) * tm) // COMBINE_TILE) * COMBINE_TILE
    route_all = jnp.concatenate([route_p, route_s], axis=1)
    src, dst, tile_expert, n_active = _routing_tables(
        route_all, cnt_s[:, 0].astype(I32), tm, n_tiles, plane_rows)

    y_tok = _moe_grouped(h_all, tile_expert, n_active, src, dst, wg, wu, wd, tm, MOE_FF_CHUNK,
                         plane_rows + n_all, n_all, plane_rows - n_all)
    yp = _combine(xp2, gt_p, route_p, g_final, y_tok, COMBINE_TILE, seq, 0, plane_rows)
    ys = _combine(xs, gt_s, route_s, g_final, y_tok, ns, ns, nb * seq, plane_rows)
    return yp.reshape(nb, seq, d), ys


def kernel(x_prompt, x_sample, c_prompt, c_sample, state_conv, state_pool, w_ada, b_ada, g_mix,
           g_ffn, w_in, conv_w, pool_w, pool_scale, g_conv_out, g_pool_out, w_out, dense_w_gate,
           dense_w_up, dense_w_down, router_w, moe_w_gate, moe_w_up, moe_w_down, g_final):
    depth = w_ada.shape[0]
    nb, seq, d = x_prompt.shape
    ns = x_sample.shape[0]
    assert x_sample.shape[1] == 1 and depth == 2
    assert sum(DENSE_FF_CHUNKS) == dense_w_gate.shape[-1]

    mod = _ada(jnp.concatenate([c_prompt, c_sample], axis=0), w_ada, b_ada)
    mod = mod.reshape(depth, nb + ns, 6, d)
    mod_p = [jnp.transpose(mod[i, :nb], (1, 0, 2))[:, :, None, :] for i in range(depth)]
    mod_s = [jnp.transpose(mod[i, nb:], (1, 0, 2))[:, None, :, :] for i in range(depth)]

    w_in_b = w_in.astype(BF16)
    w_out_b = w_out.astype(BF16)
    pool_w_b = pool_w.astype(BF16)
    dense_b = [w.astype(BF16) for w in (dense_w_gate, dense_w_up, dense_w_down)]
    cb_t = jnp.transpose(state_conv, (0, 2, 1, 3))
    pb_t = jnp.transpose(state_pool, (0, 2, 1, 3))

    xp = x_prompt
    xs = x_sample.reshape(ns, d)
    conv_p, pool_p, conv_s, pool_s = [], [], [], []
    for i in range(depth):
        sh1, sc1, gt1, sh2, sc2, gt2 = mod_p[i]
        mix_w = (g_mix[i], w_in_b[i], conv_w[i], pool_w_b[i], pool_scale[i], g_conv_out[i],
                 g_pool_out[i], w_out_b[i])
        xp, cs, ps = _mix_prompt(xp, sh1, sc1, gt1, *mix_w, tile=MIX_TILE, sub=MIX_ROW_BLOCK)
        conv_p.append(cs)
        pool_p.append(ps)
        s1, c1, t1, s2, c2, t2 = mod_s[i]
        xs, v_new, u_new = _mix_sample(xs, s1[0], c1[0], t1[0], *mix_w, cb_t[i], pb_t[i])
        conv_s.append(jnp.concatenate([state_conv[i][:, 1:], v_new[:, None, :]], axis=1))
        pool_s.append(jnp.concatenate([state_pool[i][:, 1:], u_new[:, None, :]], axis=1))
        j = i // 2
        if i % 2 == 0:
            wg, wu, wd = (w[j] for w in dense_b)
            xp = _ffn(xp, sh2, sc2, gt2, g_ffn[i], wg, wu, wd, PROMPT_TILE, DENSE_FF_CHUNKS)
            xs = _ffn(xs[None], s2, c2, t2, g_ffn[i], wg, wu, wd, ns, DENSE_FF_CHUNKS)[0]
        else:
            xp, xs = _moe_layer(xp, xs, (sh2, sc2, gt2), (s2, c2, t2), g_ffn[i], router_w[j],
                                moe_w_gate[j], moe_w_up[j], moe_w_down[j], g_final)

    return (xp, xs.reshape(ns, 1, d), jnp.stack(conv_p), jnp.stack(pool_p),
            jnp.stack(conv_s), jnp.stack(pool_s))
```

```python
import functools

import jax
import jax.numpy as jnp
from jax import lax
from jax.experimental import pallas as pl
from jax.experimental.pallas import tpu as pltpu

F32 = jnp.float32
BF16 = jnp.bfloat16
I32 = jnp.int32

EPS = 1e-6
CONV_K = 3
POOL_WINDOWS = (2, 4, 8, 16)
POOL_HIST = max(POOL_WINDOWS) - 1
TOP_K = 2

CONV_PAD = 8
POOL_PAD = 16

LANES = 128

ROUTE_ROWS = 8
ROW_EXPERT, ROW_RANK, ROW_PROB = 0, 2, 4

VMEM_LIMIT_BYTES = 56 * 1024 * 1024

ADA_COL_TILE = 2048
PROMPT_TILE = 512
MIX_TILE = 2048
MIX_ROW_BLOCK = 512
ROUTER_TILE = 1024
COMBINE_TILE = 1024
MOE_ROW_TILE = 1024
MOE_FF_CHUNK = 512
FILL_ROWS = 128
ROW_SLOTS = 2
DENSE_FF_CHUNKS = (768, 768, 768, 512)


def _params(*sem):
    return pltpu.CompilerParams(dimension_semantics=sem, vmem_limit_bytes=VMEM_LIMIT_BYTES)


def _resident(shape):
    nd = len(shape)
    return pl.BlockSpec(shape, lambda *_: (0,) * nd, pipeline_mode=pl.Buffered(1))


def _rms(x, g):
    ms = jnp.mean(x * x, axis=-1, keepdims=True)
    return x * lax.rsqrt(ms + EPS) * g


def _mod_norm(x, g, sc, sh):
    ms = jnp.mean(x * x, axis=-1, keepdims=True)
    return x * lax.rsqrt(ms + EPS) * (g * (1.0 + sc)) + sh


def _dot(a, b):
    return jnp.dot(a, b, preferred_element_type=F32)


def _dot_nt(a, b):
    return lax.dot_general(a, b, (((1,), (1,)), ((), ())), preferred_element_type=F32)


def _silu_mul(a, b):
    return a * jax.nn.sigmoid(a) * b


def _ada_kernel(c_ref, w_ref, b_ref, o_ref):
    c = c_ref[...]
    a = (c * jax.nn.sigmoid(c)).astype(BF16)
    o_ref[0] = _dot(a, w_ref[0].astype(BF16)) + b_ref[0]


def _ada(c_all, w_ada, b_ada):
    depth, d, n = w_ada.shape
    m = c_all.shape[0]
    tn = ADA_COL_TILE
    return pl.pallas_call(
        _ada_kernel,
        grid=(depth, n // tn),
        in_specs=[
            pl.BlockSpec((m, d), lambda i, j: (0, 0)),
            pl.BlockSpec((1, d, tn), lambda i, j: (i, 0, j)),
            pl.BlockSpec((1, 1, tn), lambda i, j: (i, 0, j)),
        ],
        out_specs=pl.BlockSpec((1, m, tn), lambda i, j: (i, 0, j)),
        out_shape=jax.ShapeDtypeStruct((depth, m, n), F32),
        compiler_params=_params("arbitrary", "arbitrary"),
        name="ada",
    )(c_all, w_ada, b_ada.reshape(depth, 1, n))


def _mix_tail(x, gt, bg, y, d_groups, poolw_ref, pscale_ref, gco_ref, gpo_ref, wout_ref):
    cw = gco_ref.shape[-1]
    ya = bg * y
    yb = jnp.concatenate(
        [_dot(d.astype(BF16), poolw_ref[g]) for g, d in enumerate(d_groups)], axis=-1
    ) * pscale_ref[...]
    ma = _rms(ya, gco_ref[...]).astype(BF16)
    mb = _rms(yb, gpo_ref[...]).astype(BF16)
    o = _dot(ma, wout_ref[0:cw, :]) + _dot(mb, wout_ref[cw:, :])
    return x + gt * o


def _mix_prompt_kernel(x_ref, sh_ref, sc_ref, gt_ref, g_ref, win_ref, convw_ref, poolw_ref,
                       pscale_ref, gco_ref, gpo_ref, wout_ref,
                       xo_ref, cs_ref, ps_ref, vbuf, ubuf, *, tile, sub):
    l = pl.program_id(1)
    cw = convw_ref.shape[-1]
    pg = poolw_ref.shape[-1]

    @pl.when(l == 0)
    def _():
        vbuf[0:CONV_PAD, :] = jnp.zeros((CONV_PAD, cw), F32)
        ubuf[0:POOL_PAD, :] = jnp.zeros((POOL_PAD, ubuf.shape[-1]), F32)

    gates = []
    for lo in range(0, tile, sub):
        x = x_ref[0, lo:lo + sub, :]
        h = _mod_norm(x, g_ref[...], sc_ref[0], sh_ref[0]).astype(BF16)
        p = _dot(h, win_ref[...])
        gates.append(p[:, 0:cw])
        vbuf[CONV_PAD + lo:CONV_PAD + lo + sub, :] = p[:, cw:2 * cw] * p[:, 2 * cw:3 * cw]
        ubuf[POOL_PAD + lo:POOL_PAD + lo + sub, :] = p[:, 3 * cw:]

    w = convw_ref[...]
    for bg, lo in zip(gates, range(0, tile, sub)):
        y = w[CONV_K - 1:CONV_K] * vbuf[CONV_PAD + lo:CONV_PAD + lo + sub, :]
        for k in range(1, CONV_K):
            y = y + w[CONV_K - 1 - k:CONV_K - k] * vbuf[CONV_PAD + lo - k:CONV_PAD + lo - k + sub, :]

        pos = l * tile + lo + lax.broadcasted_iota(I32, (sub, 1), 0)
        d_groups = []
        for g, win in enumerate(POOL_WINDOWS):
            ug = ubuf[POOL_PAD + lo:POOL_PAD + lo + sub, g * pg:(g + 1) * pg]
            acc = ug
            for k in range(1, win):
                acc = acc + ubuf[POOL_PAD + lo - k:POOL_PAD + lo - k + sub, g * pg:(g + 1) * pg]
            cnt = jnp.minimum(pos + 1, win).astype(F32)
            d_groups.append(acc / cnt - ug)

        xo_ref[0, lo:lo + sub, :] = _mix_tail(x_ref[0, lo:lo + sub, :], gt_ref[0], bg, y, d_groups,
                                              poolw_ref, pscale_ref, gco_ref, gpo_ref, wout_ref)

    cs_ref[0] = vbuf[CONV_PAD + tile - (CONV_K - 1):CONV_PAD + tile, :]
    ps_ref[0] = ubuf[POOL_PAD + tile - POOL_HIST:POOL_PAD + tile, :]
    vbuf[0:CONV_PAD, :] = vbuf[tile:tile + CONV_PAD, :]
    ubuf[0:POOL_PAD, :] = ubuf[tile:tile + POOL_PAD, :]


def _mix_prompt(x, sh, sc, gt, g, w_in, conv_w, pool_w, pool_scale, g_co, g_po, w_out, tile, sub):
    b, seq, d = x.shape
    assert seq % tile == 0 and tile % sub == 0 and sub >= POOL_PAD
    cw = conv_w.shape[-1]
    pw = pool_scale.shape[-1]
    row = lambda bi, li: (bi, 0, 0)
    return pl.pallas_call(
        functools.partial(_mix_prompt_kernel, tile=tile, sub=sub),
        grid=(b, seq // tile),
        in_specs=[
            pl.BlockSpec((1, tile, d), lambda bi, li: (bi, li, 0)),
            pl.BlockSpec((1, 1, d), row),
            pl.BlockSpec((1, 1, d), row),
            pl.BlockSpec((1, 1, d), row),
            _resident((1, d)),
            _resident(w_in.shape),
            _resident(conv_w.shape),
            _resident(pool_w.shape),
            _resident((1, pw)),
            _resident((1, cw)),
            _resident((1, pw)),
            _resident(w_out.shape),
        ],
        out_specs=[
            pl.BlockSpec((1, tile, d), lambda bi, li: (bi, li, 0)),
            pl.BlockSpec((1, CONV_K - 1, cw), row),
            pl.BlockSpec((1, POOL_HIST, pw), row),
        ],
        out_shape=[
            jax.ShapeDtypeStruct((b, seq, d), F32),
            jax.ShapeDtypeStruct((b, CONV_K - 1, cw), F32),
            jax.ShapeDtypeStruct((b, POOL_HIST, pw), F32),
        ],
        scratch_shapes=[
            pltpu.VMEM((CONV_PAD + tile, cw), F32),
            pltpu.VMEM((POOL_PAD + tile, pw), F32),
        ],
        compiler_params=_params("arbitrary", "arbitrary"),
        name="mix_prompt",
    )(x, sh, sc, gt, g.reshape(1, d), w_in, conv_w, pool_w, pool_scale.reshape(1, pw),
      g_co.reshape(1, cw), g_po.reshape(1, pw), w_out)


def _mix_sample_kernel(x_ref, sh_ref, sc_ref, gt_ref, g_ref, win_ref, convw_ref, poolw_ref,
                       pscale_ref, gco_ref, gpo_ref, wout_ref, cb_ref, pb_ref,
                       xo_ref, v_ref, u_ref):
    cw = convw_ref.shape[-1]
    pg = poolw_ref.shape[-1]
    x = x_ref[...]
    h = _mod_norm(x, g_ref[...], sc_ref[...], sh_ref[...]).astype(BF16)
    p = _dot(h, win_ref[...])
    bg = p[:, 0:cw]
    v = p[:, cw:2 * cw] * p[:, 2 * cw:3 * cw]
    u = p[:, 3 * cw:]
    v_ref[...] = v
    u_ref[...] = u

    w = convw_ref[...]
    y = w[CONV_K - 1:CONV_K] * v
    for k in range(1, CONV_K):
        y = y + w[CONV_K - 1 - k:CONV_K - k] * cb_ref[CONV_K - 1 - k]

    d_groups = []
    for g, win in enumerate(POOL_WINDOWS):
        ug = u[:, g * pg:(g + 1) * pg]
        acc = ug
        for k in range(1, win):
            acc = acc + pb_ref[POOL_HIST - k, :, g * pg:(g + 1) * pg]
        d_groups.append(acc / float(win) - ug)

    xo_ref[...] = _mix_tail(x, gt_ref[...], bg, y, d_groups, poolw_ref, pscale_ref,
                            gco_ref, gpo_ref, wout_ref)


def _mix_sample(x, sh, sc, gt, g, w_in, conv_w, pool_w, pool_scale, g_co, g_po, w_out, cb, pb):
    n, d = x.shape
    cw = conv_w.shape[-1]
    pw = pool_scale.shape[-1]
    return pl.pallas_call(
        _mix_sample_kernel,
        out_shape=[
            jax.ShapeDtypeStruct((n, d), F32),
            jax.ShapeDtypeStruct((n, cw), F32),
            jax.ShapeDtypeStruct((n, pw), F32),
        ],
        compiler_params=pltpu.CompilerParams(vmem_limit_bytes=VMEM_LIMIT_BYTES),
        name="mix_sample",
    )(x, sh, sc, gt, g.reshape(1, d), w_in, conv_w, pool_w, pool_scale.reshape(1, pw),
      g_co.reshape(1, cw), g_po.reshape(1, pw), w_out, cb, pb)


def _ffn_kernel(x_ref, sh_ref, sc_ref, gt_ref, g_ref, wg_ref, wu_ref, wd_ref, o_ref, *, chunks):
    x = x_ref[0]
    h = _mod_norm(x, g_ref[...], sc_ref[0], sh_ref[0]).astype(BF16)
    acc = None
    lo = 0
    for fc in chunks:
        a = _dot(h, wg_ref[:, lo:lo + fc])
        b = _dot(h, wu_ref[:, lo:lo + fc])
        part = _dot(_silu_mul(a, b).astype(BF16), wd_ref[lo:lo + fc, :])
        acc = part if acc is None else acc + part
        lo += fc
    o_ref[0] = x + gt_ref[0] * acc


def _ffn(x, sh, sc, gt, g, wg, wu, wd, tile, chunks):
    b, seq, d = x.shape
    tm = sh.shape[1]
    tmod = tile if tm > 1 else 1
    mod_map = (lambda bi, li: (bi, li, 0)) if tm > 1 else (lambda bi, li: (bi, 0, 0))
    return pl.pallas_call(
        functools.partial(_ffn_kernel, chunks=chunks),
        grid=(b, seq // tile),
        in_specs=[
            pl.BlockSpec((1, tile, d), lambda bi, li: (bi, li, 0)),
            pl.BlockSpec((1, tmod, d), mod_map),
            pl.BlockSpec((1, tmod, d), mod_map),
            pl.BlockSpec((1, tmod, d), mod_map),
            _resident((1, d)),
            _resident(wg.shape),
            _resident(wu.shape),
            _resident(wd.shape),
        ],
        out_specs=pl.BlockSpec((1, tile, d), lambda bi, li: (bi, li, 0)),
        out_shape=jax.ShapeDtypeStruct((b, seq, d), F32),
        compiler_params=_params("arbitrary", "arbitrary"),
        name="ffn",
    )(x, sh, sc, gt, g.reshape(1, d), wg, wu, wd)


def _split_bf16(a):
    hi = a.astype(BF16)
    return hi, (a - hi.astype(F32)).astype(BF16)


def _router_kernel(x_ref, sh_ref, sc_ref, g_ref, rw_ref, base_ref, h_ref, route_ref, cnt_ref,
                   tri, carry, *, n_steps):
    t = x_ref.shape[0]
    step = pl.program_id(0)

    @pl.when(step == 0)
    def _():
        r = lax.broadcasted_iota(I32, (t, t), 0)
        c = lax.broadcasted_iota(I32, (t, t), 1)
        tri[...] = jnp.where(r < c, 1.0, 0.0).astype(BF16)
        carry[...] = base_ref[...]

    @pl.when(step >= n_steps)
    def _():
        h_ref[...] = jnp.zeros_like(h_ref)

    @pl.when(step < n_steps)
    def _():
        h = _mod_norm(x_ref[...], g_ref[...], sc_ref[0], sh_ref[0])
        h_ref[...] = h
        h_hi, h_lo = _split_bf16(h)
        rw_hi, rw_lo = _split_bf16(rw_ref[...])
        logits = _dot_nt(rw_hi, h_hi) + _dot_nt(rw_hi, h_lo) + _dot_nt(rw_lo, h_hi)

        ne = logits.shape[0]
        idx = lax.broadcasted_iota(I32, logits.shape, 0)
        m1 = jnp.max(logits, axis=0, keepdims=True)
        i1 = jnp.min(jnp.where(logits == m1, idx, ne), axis=0, keepdims=True)
        sel1 = idx == i1
        rest = jnp.where(sel1, -jnp.inf, logits)
        m2 = jnp.max(rest, axis=0, keepdims=True)
        i2 = jnp.min(jnp.where(rest == m2, idx, ne), axis=0, keepdims=True)
        sel2 = idx == i2
        e = jnp.exp(m2 - m1)
        p1 = 1.0 / (1.0 + e)
        p2 = e / (1.0 + e)

        chosen = jnp.where(sel1, 1.0, jnp.where(sel2, 1.0, 0.0))
        before = _dot(chosen.astype(BF16), tri[...]) + carry[:, 0:1]
        r1 = jnp.sum(jnp.where(sel1, before, 0.0), axis=0, keepdims=True)
        r2 = jnp.sum(jnp.where(sel2, before, 0.0), axis=0, keepdims=True)
        carry[...] += jnp.sum(chosen, axis=1, keepdims=True)
        cnt_ref[...] = carry[...]

        rows = (i1.astype(F32), i2.astype(F32), r1, r2, p1, p2)
        route = jnp.zeros(logits.shape, F32)
        for k, row in enumerate(rows):
            route = jnp.where(idx == k, row, route)
        route_ref[...] = route


def _router(x2d, sh, sc, g, router_w, base_cnt, tile, rows_per_mod, h_rows, h_prev=None,
            row_off=0):
    n, d = x2d.shape
    ne = router_w.shape[-1]
    assert ne == ROUTE_ROWS and n % tile == 0 and row_off % tile == 0 and tile % LANES == 0
    assert base_cnt.shape == (ne, LANES)
    n_steps = n // tile
    tail = h_rows - (row_off + n)
    assert 0 <= tail < tile
    fill_tail = h_prev is None and tail > 0
    last = n_steps - 1
    clamp = (lambda i: jnp.minimum(i, last)) if fill_tail else (lambda i: i)
    per_row_mod = sh.shape[1] > 1
    if per_row_mod:
        mod_spec = pl.BlockSpec((1, tile, d), lambda i: (0, clamp(i), 0))
    else:
        assert rows_per_mod % tile == 0
        mod_spec = pl.BlockSpec((1, 1, d), lambda i: (clamp(i) // (rows_per_mod // tile), 0, 0))
    in_specs = [
        pl.BlockSpec((tile, d), lambda i: (clamp(i), 0)),
        mod_spec,
        mod_spec,
        _resident((1, d)),
        _resident((ne, d)),
        _resident((ne, LANES)),
    ]
    args = [x2d, sh, sc, g.reshape(1, d), router_w.T, base_cnt]
    n_in = len(args)
    aliases = {}
    body = functools.partial(_router_kernel, n_steps=n_steps)
    if h_prev is not None:
        assert h_prev.shape == (h_rows, d)
        in_specs.append(pl.BlockSpec(memory_space=pl.ANY))
        args.append(h_prev)
        aliases = {n_in: 0}

        def body(*refs):
            _router_kernel(*refs[:n_in], *refs[n_in + 1:], n_steps=n_steps)

    blk_off = row_off // tile
    return pl.pallas_call(
        body,
        grid=(n_steps + int(fill_tail),),
        in_specs=in_specs,
        out_specs=[
            pl.BlockSpec((tile, d), lambda i: (i + blk_off, 0)),
            pl.BlockSpec((ne, tile), lambda i: (0, clamp(i))),
            pl.BlockSpec((ne, LANES), lambda i: (0, 0)),
        ],
        out_shape=[
            jax.ShapeDtypeStruct((h_rows, d), F32),
            jax.ShapeDtypeStruct((ne, n), F32),
            jax.ShapeDtypeStruct((ne, LANES), F32),
        ],
        scratch_shapes=[pltpu.VMEM((tile, tile), BF16), pltpu.VMEM((ne, LANES), F32)],
        input_output_aliases=aliases,
        compiler_params=_params("arbitrary"),
        name="router",
    )(*args)


def _row_copy(src_hbm, row, dst, dst_row, sem):
    return pltpu.make_async_copy(src_hbm.at[pl.ds(row, 1), :], dst.at[pl.ds(dst_row, 1), :], sem)


def _row_copy_out(src, src_row, dst_hbm, row, sem):
    return pltpu.make_async_copy(src.at[pl.ds(src_row, 1), :], dst_hbm.at[pl.ds(row, 1), :], sem)


def _moe_grouped_kernel(te_ref, na_ref, src_ref, dst_ref, h_hbm, wg_ref, wu_ref, wd_ref, y_hbm,
                        xs, hbuf, obuf, gsem, ssem, fsem, *, tm, nc, rows_per_step, fill_row,
                        fill_blocks):
    del te_ref
    i = pl.program_id(0)
    c = pl.program_id(1)
    n_active = na_ref[0]
    ahead = ROW_SLOTS - 1
    slot = i % ROW_SLOTS

    def request(tile, r):
        s = tile % ROW_SLOTS
        return _row_copy(h_hbm, src_ref[tile * tm + r], xs.at[s], r, gsem.at[s])

    def send(tile, r):
        s = (tile + ROW_SLOTS) % ROW_SLOTS
        return _row_copy_out(obuf.at[s], r, y_hbm, dst_ref[(tile + 1) * tm + r], ssem.at[s])

    @pl.when(jnp.logical_and(i == 0, c == 0))
    def _():
        last = ROW_SLOTS - 1
        obuf[last] = jnp.zeros((tm, obuf.shape[-1]), F32)
        fills = [pltpu.make_async_copy(obuf.at[last, pl.ds(0, FILL_ROWS), :],
                                       y_hbm.at[pl.ds(fill_row + q * FILL_ROWS, FILL_ROWS), :],
                                       fsem) for q in range(fill_blocks)]
        for f in fills:
            f.start()
        for f in fills:
            f.wait()

        def body(r, carry):
            for t in range(ahead):
                request(t, r).start()
            return carry

        lax.fori_loop(0, tm, body, 0)

    @pl.when(jnp.logical_and(i < n_active + ahead, c == 0))
    def _():
        pltpu.make_async_copy(h_hbm.at[pl.ds(0, tm), :], xs.at[slot], gsem.at[slot]).wait()

    @pl.when(jnp.logical_and(jnp.logical_and(i >= ahead, i < n_active + ROW_SLOTS), c == 0))
    def _():
        pltpu.make_async_copy(obuf.at[slot], y_hbm.at[pl.ds(0, tm), :], ssem.at[slot]).wait()

    @pl.when(jnp.logical_and(i < n_active, c == 0))
    def _():
        obuf[slot] = jnp.zeros((tm, obuf.shape[-1]), F32)
        hbuf[...] = xs[slot].astype(BF16)

    @pl.when(jnp.logical_and(i == n_active, c == 0))
    def _():
        def body(r, carry):
            send(i - 1, r).start()
            return carry

        lax.fori_loop(0, tm, body, 0)

    @pl.when(jnp.logical_and(i < n_active, c == 0))
    def _():
        for r in range(rows_per_step * nc, tm):
            request(i + ahead, r).start()
            send(i - 1, r).start()

    @pl.when(i < n_active)
    def _():
        for k in range(rows_per_step):
            r = c * rows_per_step + k
            request(i + ahead, r).start()
            send(i - 1, r).start()

        h = hbuf[...]
        a = _dot(h, wg_ref[0].astype(BF16))
        b = _dot(h, wu_ref[0].astype(BF16))
        obuf[slot] += _dot(_silu_mul(a, b).astype(BF16), wd_ref[0].astype(BF16))


def _moe_grouped(h_all, tile_expert, n_active, src, dst, wg, wu, wd, tm, fc, y_rows, fill_row,
                 fill_rows):
    ne, d, dff = wg.shape
    n_tiles = tile_expert.shape[0]
    nc = dff // fc
    assert dff % fc == 0
    assert src.shape[0] == n_tiles * tm and dst.shape[0] == n_tiles * tm
    assert fill_rows % FILL_ROWS == 0 and tm >= FILL_ROWS
    rows_per_step = tm // nc

    def w_col(i, c, te, na, src_, dst_):
        return (te[i], 0, jnp.where(i < na[0], c, nc - 1))

    def w_row(i, c, te, na, src_, dst_):
        return (te[i], jnp.where(i < na[0], c, nc - 1), 0)

    grid_spec = pltpu.PrefetchScalarGridSpec(
        num_scalar_prefetch=4,
        grid=(n_tiles, nc),
        in_specs=[
            pl.BlockSpec(memory_space=pl.ANY),
            pl.BlockSpec((1, d, fc), w_col),
            pl.BlockSpec((1, d, fc), w_col),
            pl.BlockSpec((1, fc, d), w_row),
        ],
        out_specs=pl.BlockSpec(memory_space=pl.ANY),
        scratch_shapes=[
            pltpu.VMEM((ROW_SLOTS, tm, d), F32),
            pltpu.VMEM((tm, d), BF16),
            pltpu.VMEM((ROW_SLOTS, tm, d), F32),
            pltpu.SemaphoreType.DMA((ROW_SLOTS,)),
            pltpu.SemaphoreType.DMA((ROW_SLOTS,)),
            pltpu.SemaphoreType.DMA,
        ],
    )
    return pl.pallas_call(
        functools.partial(_moe_grouped_kernel, tm=tm, nc=nc, rows_per_step=rows_per_step,
                          fill_row=fill_row, fill_blocks=fill_rows // FILL_ROWS),
        grid_spec=grid_spec,
        out_shape=jax.ShapeDtypeStruct((y_rows, d), F32),
        compiler_params=_params("arbitrary", "arbitrary"),
        name="moe_grouped",
    )(tile_expert, n_active, src, dst, h_all, wg, wu, wd)


def _combine_kernel(x_ref, gt_ref, route_ref, gf_ref, y0_ref, y1_ref, o_ref):
    route = route_ref[...]
    pad = jnp.zeros((LANES - route.shape[0], route.shape[1]), F32)
    cols = jnp.concatenate([route, pad], axis=0).T
    f = (cols[:, ROW_PROB:ROW_PROB + 1] * y0_ref[...]
         + cols[:, ROW_PROB + 1:ROW_PROB + 2] * y1_ref[...])
    o_ref[...] = _rms(x_ref[...] + gt_ref[0] * f, gf_ref[...])


def _combine(x2d, gt, route, g_final, y_tok, tile, rows_per_mod, tok_off, plane_rows):
    n, d = x2d.shape
    assert n % tile == 0 and tok_off % tile == 0 and plane_rows % tile == 0
    per_row_mod = gt.shape[1] > 1
    if per_row_mod:
        mod_spec = pl.BlockSpec((1, tile, d), lambda j: (0, j, 0))
    else:
        assert rows_per_mod % tile == 0
        mod_spec = pl.BlockSpec((1, 1, d), lambda j: (j // (rows_per_mod // tile), 0, 0))
    first = tok_off // tile
    second = (plane_rows + tok_off) // tile
    return pl.pallas_call(
        _combine_kernel,
        grid=(n // tile,),
        in_specs=[
            pl.BlockSpec((tile, d), lambda j: (j, 0)),
            mod_spec,
            pl.BlockSpec((ROUTE_ROWS, tile), lambda j: (0, j)),
            pl.BlockSpec((1, d), lambda j: (0, 0)),
            pl.BlockSpec((tile, d), lambda j: (first + j, 0)),
            pl.BlockSpec((tile, d), lambda j: (second + j, 0)),
        ],
        out_specs=pl.BlockSpec((tile, d), lambda j: (j, 0)),
        out_shape=jax.ShapeDtypeStruct((n, d), F32),
        compiler_params=_params("arbitrary"),
        name="combine",
    )(x2d, gt, route, g_final.reshape(1, d), y_tok, y_tok)


def _invert_kernel(perm_ref, inv_ref):
    def place(a, carry):
        inv_ref[perm_ref[a]] = a
        return carry

    lax.fori_loop(0, perm_ref.shape[0], place, 0, unroll=32)


def _invert(perm):
    assert perm.shape[0] % 32 == 0
    smem = pl.BlockSpec(memory_space=pltpu.SMEM)
    return pl.pallas_call(
        _invert_kernel,
        in_specs=[smem],
        out_specs=smem,
        out_shape=jax.ShapeDtypeStruct(perm.shape, I32),
        name="invert",
    )(perm)


def _routing_tables(route_all, counts, tm, n_tiles, plane_rows):
    ne = counts.shape[0]
    n = route_all.shape[1]
    e_idx = route_all[ROW_EXPERT:ROW_EXPERT + TOP_K].astype(I32)
    rank = route_all[ROW_RANK:ROW_RANK + TOP_K].astype(I32)
    padded = (counts + tm - 1) // tm * tm
    ends = jnp.cumsum(padded)
    offs = ends - padded
    onehot = e_idx[..., None] == jnp.arange(ne, dtype=I32)
    pos = jnp.sum(jnp.where(onehot, offs, 0), axis=-1) + rank
    n_active = ends[-1] // tm
    starts = jnp.arange(n_tiles, dtype=I32) * tm
    tile_expert = jnp.sum(starts[:, None] >= ends[None, :], axis=1).astype(I32)
    last_expert = jnp.sum((n_active - 1) * tm >= ends).astype(I32)
    tile_expert = jnp.minimum(tile_expert, last_expert)
    n_rows = n_tiles * tm
    n_real = n * TOP_K
    pad_ends = jnp.cumsum(padded - counts)
    j = jnp.arange(n_rows - n_real, dtype=I32)[:, None]
    in_expert = jnp.logical_and(j >= pad_ends - (padded - counts), j < pad_ends)
    pad_pos = jnp.where(j[:, 0] < pad_ends[-1],
                        jnp.sum(jnp.where(in_expert, offs + counts + j - pad_ends + padded - counts, 0),
                                axis=1),
                        ends[-1] + j[:, 0] - pad_ends[-1])
    inv = _invert(jnp.concatenate([pos.reshape(-1), pad_pos]))
    is_real = inv < n_real
    tok = inv % n
    choice = inv // n
    spare = n + jnp.minimum(inv - n_real, ne * tm - 1)
    src = jnp.where(is_real, tok, 0)
    dst = jnp.where(is_real, choice * plane_rows + tok, spare)
    lead = n + ne * tm + jnp.arange(tm, dtype=I32)
    dst = jnp.concatenate([lead, dst])[:n_rows]
    return src, dst, tile_expert, n_active.astype(I32).reshape(1)


def _moe_layer(xp, xs, mod_p, mod_s, g, router_w, wg, wu, wd, g_final):
    nb, seq, d = xp.shape
    ns = xs.shape[0]
    ne = router_w.shape[-1]
    n_all = nb * seq + ns
    sh_p, sc_p, gt_p = mod_p
    sh_s, sc_s, gt_s = mod_s
    xp2 = xp.reshape(nb * seq, d)

    zero_cnt = jnp.zeros((ne, LANES), F32)
    h_all, route_p, cnt_p = _router(xp2, sh_p, sc_p, g, router_w, zero_cnt, ROUTER_TILE, seq,
                                    n_all)
    h_all, route_s, cnt_s = _router(xs, sh_s, sc_s, g, router_w, cnt_p, ns, ns, n_all,
                                    h_prev=h_all, row_off=nb * seq)

    tm = MOE_ROW_TILE
    n_tiles = (n_all * TOP_K) // tm + ne + ROW_SLOTS
    plane_rows = -(-(n_all + (ne + 1) * tm) // COMBINE_TILE) * COMBINE_TILE
    route_all = jnp.concatenate([route_p, route_s], axis=1)
    src, dst, tile_expert, n_active = _routing_tables(
        route_all, cnt_s[:, 0].astype(I32), tm, n_tiles, plane_rows)

    y_tok = _moe_grouped(h_all, tile_expert, n_active, src, dst, wg, wu, wd, tm, MOE_FF_CHUNK,
                         plane_rows + n_all, n_all, plane_rows - n_all)
    yp = _combine(xp2, gt_p, route_p, g_final, y_tok, COMBINE_TILE, seq, 0, plane_rows)
    ys = _combine(xs, gt_s, route_s, g_final, y_tok, ns, ns, nb * seq, plane_rows)
    return yp.reshape(nb, seq, d), ys


def kernel(x_prompt, x_sample, c_prompt, c_sample, state_conv, state_pool, w_ada, b_ada, g_mix,
           g_ffn, w_in, conv_w, pool_w, pool_scale, g_conv_out, g_pool_out, w_out, dense_w_gate,
           dense_w_up, dense_w_down, router_w, moe_w_gate, moe_w_up, moe_w_down, g_final):
    depth = w_ada.shape[0]
    nb, seq, d = x_prompt.shape
    ns = x_sample.shape[0]
    assert x_sample.shape[1] == 1 and depth == 2
    assert sum(DENSE_FF_CHUNKS) == dense_w_gate.shape[-1]

    mod = _ada(jnp.concatenate([c_prompt, c_sample], axis=0), w_ada, b_ada)
    mod = mod.reshape(depth, nb + ns, 6, d)
    mod_p = [jnp.transpose(mod[i, :nb], (1, 0, 2))[:, :, None, :] for i in range(depth)]
    mod_s = [jnp.transpose(mod[i, nb:], (1, 0, 2))[:, None, :, :] for i in range(depth)]

    w_in_b = w_in.astype(BF16)
    w_out_b = w_out.astype(BF16)
    pool_w_b = pool_w.astype(BF16)
    dense_b = [w.astype(BF16) for w in (dense_w_gate, dense_w_up, dense_w_down)]
    cb_t = jnp.transpose(state_conv, (0, 2, 1, 3))
    pb_t = jnp.transpose(state_pool, (0, 2, 1, 3))

    xp = x_prompt
    xs = x_sample.reshape(ns, d)
    conv_p, pool_p, conv_s, pool_s = [], [], [], []
    for i in range(depth):
        sh1, sc1, gt1, sh2, sc2, gt2 = mod_p[i]
        mix_w = (g_mix[i], w_in_b[i], conv_w[i], pool_w_b[i], pool_scale[i], g_conv_out[i],
                 g_pool_out[i], w_out_b[i])
        xp, cs, ps = _mix_prompt(xp, sh1, sc1, gt1, *mix_w, tile=MIX_TILE, sub=MIX_ROW_BLOCK)
        conv_p.append(cs)
        pool_p.append(ps)
        s1, c1, t1, s2, c2, t2 = mod_s[i]
        xs, v_new, u_new = _mix_sample(xs, s1[0], c1[0], t1[0], *mix_w, cb_t[i], pb_t[i])
        conv_s.append(jnp.concatenate([state_conv[i][:, 1:], v_new[:, None, :]], axis=1))
        pool_s.append(jnp.concatenate([state_pool[i][:, 1:], u_new[:, None, :]], axis=1))
        j = i // 2
        if i % 2 == 0:
            wg, wu, wd = (w[j] for w in dense_b)
            xp = _ffn(xp, sh2, sc2, gt2, g_ffn[i], wg, wu, wd, PROMPT_TILE, DENSE_FF_CHUNKS)
            xs = _ffn(xs[None], s2, c2, t2, g_ffn[i], wg, wu, wd, ns, DENSE_FF_CHUNKS)[0]
        else:
            xp, xs = _moe_layer(xp, xs, (sh2, sc2, gt2), (s2, c2, t2), g_ffn[i], router_w[j],
                                moe_w_gate[j], moe_w_up[j], moe_w_down[j], g_final)

    return (xp, xs.reshape(ns, 1, d), jnp.stack(conv_p), jnp.stack(pool_p),
            jnp.stack(conv_s), jnp.stack(pool_s))
```

```python
import functools

import jax
import jax.numpy as jnp
from jax import lax
from jax.experimental import pallas as pl
from jax.experimental.pallas import tpu as pltpu

F32 = jnp.float32
BF16 = jnp.bfloat16
I32 = jnp.int32

EPS = 1e-6
CONV_K = 3
POOL_WINDOWS = (2, 4, 8, 16)
POOL_HIST = max(POOL_WINDOWS) - 1
TOP_K = 2

CONV_PAD = 8
POOL_PAD = 16

LANES = 128

ROUTE_ROWS = 8
ROW_EXPERT, ROW_RANK, ROW_PROB = 0, 2, 4

VMEM_LIMIT_BYTES = 56 * 1024 * 1024

PROMPT_TILE = 512
MIX_TILE = 2048
MIX_ROW_BLOCK = 512
ROUTER_TILE = 1024
COMBINE_TILE = 512
MOE_ROW_TILE = 1024
MOE_FF_CHUNK = 512
FILL_ROWS = 128
ROW_SLOTS = 4
DENSE_FF_CHUNKS = (768, 768, 768, 512)


def _params(*sem):
    return pltpu.CompilerParams(dimension_semantics=sem, vmem_limit_bytes=VMEM_LIMIT_BYTES)


def _resident(shape):
    nd = len(shape)
    return pl.BlockSpec(shape, lambda *_: (0,) * nd, pipeline_mode=pl.Buffered(1))


def _rms(x, g):
    ms = jnp.mean(x * x, axis=-1, keepdims=True)
    return x * lax.rsqrt(ms + EPS) * g


def _mod_norm(x, g, sc, sh):
    ms = jnp.mean(x * x, axis=-1, keepdims=True)
    return x * lax.rsqrt(ms + EPS) * (g * (1.0 + sc)) + sh


def _dot(a, b):
    return jnp.dot(a, b, preferred_element_type=F32)


def _dot_nt(a, b):
    return lax.dot_general(a, b, (((1,), (1,)), ((), ())), preferred_element_type=F32)


def _silu_mul(a, b):
    return a * jax.nn.sigmoid(a) * b


def _ada_kernel(c_ref, w_ref, b_ref, o_ref):
    c = c_ref[...]
    a = (c * jax.nn.sigmoid(c)).astype(BF16)
    o_ref[0] = _dot(a, w_ref[0].astype(BF16)) + b_ref[0]


def _ada(c_all, w_ada, b_ada):
    depth, d, n = w_ada.shape
    m = c_all.shape[0]
    tn = 1024
    return pl.pallas_call(
        _ada_kernel,
        grid=(depth, n // tn),
        in_specs=[
            pl.BlockSpec((m, d), lambda i, j: (0, 0)),
            pl.BlockSpec((1, d, tn), lambda i, j: (i, 0, j)),
            pl.BlockSpec((1, 1, tn), lambda i, j: (i, 0, j)),
        ],
        out_specs=pl.BlockSpec((1, m, tn), lambda i, j: (i, 0, j)),
        out_shape=jax.ShapeDtypeStruct((depth, m, n), F32),
        compiler_params=_params("arbitrary", "arbitrary"),
        name="ada",
    )(c_all, w_ada, b_ada.reshape(depth, 1, n))


def _mix_tail(x, gt, bg, y, d_groups, poolw_ref, pscale_ref, gco_ref, gpo_ref, wout_ref):
    cw = gco_ref.shape[-1]
    ya = bg * y
    yb = jnp.concatenate(
        [_dot(d.astype(BF16), poolw_ref[g]) for g, d in enumerate(d_groups)], axis=-1
    ) * pscale_ref[...]
    ma = _rms(ya, gco_ref[...]).astype(BF16)
    mb = _rms(yb, gpo_ref[...]).astype(BF16)
    o = _dot(ma, wout_ref[0:cw, :]) + _dot(mb, wout_ref[cw:, :])
    return x + gt * o


def _mix_prompt_kernel(x_ref, sh_ref, sc_ref, gt_ref, g_ref, win_ref, convw_ref, poolw_ref,
                       pscale_ref, gco_ref, gpo_ref, wout_ref,
                       xo_ref, cs_ref, ps_ref, vbuf, ubuf, *, tile, sub):
    l = pl.program_id(1)
    cw = convw_ref.shape[-1]
    pg = poolw_ref.shape[-1]

    @pl.when(l == 0)
    def _():
        vbuf[0:CONV_PAD, :] = jnp.zeros((CONV_PAD, cw), F32)
        ubuf[0:POOL_PAD, :] = jnp.zeros((POOL_PAD, ubuf.shape[-1]), F32)

    gates = []
    for lo in range(0, tile, sub):
        x = x_ref[0, lo:lo + sub, :]
        h = _mod_norm(x, g_ref[...], sc_ref[0], sh_ref[0]).astype(BF16)
        p = _dot(h, win_ref[...])
        gates.append(p[:, 0:cw])
        vbuf[CONV_PAD + lo:CONV_PAD + lo + sub, :] = p[:, cw:2 * cw] * p[:, 2 * cw:3 * cw]
        ubuf[POOL_PAD + lo:POOL_PAD + lo + sub, :] = p[:, 3 * cw:]

    w = convw_ref[...]
    for bg, lo in zip(gates, range(0, tile, sub)):
        y = w[CONV_K - 1:CONV_K] * vbuf[CONV_PAD + lo:CONV_PAD + lo + sub, :]
        for k in range(1, CONV_K):
            y = y + w[CONV_K - 1 - k:CONV_K - k] * vbuf[CONV_PAD + lo - k:CONV_PAD + lo - k + sub, :]

        pos = l * tile + lo + lax.broadcasted_iota(I32, (sub, 1), 0)
        d_groups = []
        for g, win in enumerate(POOL_WINDOWS):
            ug = ubuf[POOL_PAD + lo:POOL_PAD + lo + sub, g * pg:(g + 1) * pg]
            acc = ug
            for k in range(1, win):
                acc = acc + ubuf[POOL_PAD + lo - k:POOL_PAD + lo - k + sub, g * pg:(g + 1) * pg]
            cnt = jnp.minimum(pos + 1, win).astype(F32)
            d_groups.append(acc / cnt - ug)

        xo_ref[0, lo:lo + sub, :] = _mix_tail(x_ref[0, lo:lo + sub, :], gt_ref[0], bg, y, d_groups,
                                              poolw_ref, pscale_ref, gco_ref, gpo_ref, wout_ref)

    cs_ref[0] = vbuf[CONV_PAD + tile - (CONV_K - 1):CONV_PAD + tile, :]
    ps_ref[0] = ubuf[POOL_PAD + tile - POOL_HIST:POOL_PAD + tile, :]
    vbuf[0:CONV_PAD, :] = vbuf[tile:tile + CONV_PAD, :]
    ubuf[0:POOL_PAD, :] = ubuf[tile:tile + POOL_PAD, :]


def _mix_prompt(x, sh, sc, gt, g, w_in, conv_w, pool_w, pool_scale, g_co, g_po, w_out, tile, sub):
    b, seq, d = x.shape
    assert seq % tile == 0 and tile % sub == 0 and sub >= POOL_PAD
    cw = conv_w.shape[-1]
    pw = pool_scale.shape[-1]
    row = lambda bi, li: (bi, 0, 0)
    return pl.pallas_call(
        functools.partial(_mix_prompt_kernel, tile=tile, sub=sub),
        grid=(b, seq // tile),
        in_specs=[
            pl.BlockSpec((1, tile, d), lambda bi, li: (bi, li, 0)),
            pl.BlockSpec((1, 1, d), row),
            pl.BlockSpec((1, 1, d), row),
            pl.BlockSpec((1, 1, d), row),
            _resident((1, d)),
            _resident(w_in.shape),
            _resident(conv_w.shape),
            _resident(pool_w.shape),
            _resident((1, pw)),
            _resident((1, cw)),
            _resident((1, pw)),
            _resident(w_out.shape),
        ],
        out_specs=[
            pl.BlockSpec((1, tile, d), lambda bi, li: (bi, li, 0)),
            pl.BlockSpec((1, CONV_K - 1, cw), row),
            pl.BlockSpec((1, POOL_HIST, pw), row),
        ],
        out_shape=[
            jax.ShapeDtypeStruct((b, seq, d), F32),
            jax.ShapeDtypeStruct((b, CONV_K - 1, cw), F32),
            jax.ShapeDtypeStruct((b, POOL_HIST, pw), F32),
        ],
        scratch_shapes=[
            pltpu.VMEM((CONV_PAD + tile, cw), F32),
            pltpu.VMEM((POOL_PAD + tile, pw), F32),
        ],
        compiler_params=_params("arbitrary", "arbitrary"),
        name="mix_prompt",
    )(x, sh, sc, gt, g.reshape(1, d), w_in, conv_w, pool_w, pool_scale.reshape(1, pw),
      g_co.reshape(1, cw), g_po.reshape(1, pw), w_out)


def _mix_sample_kernel(x_ref, sh_ref, sc_ref, gt_ref, g_ref, win_ref, convw_ref, poolw_ref,
                       pscale_ref, gco_ref, gpo_ref, wout_ref, cb_ref, pb_ref,
                       xo_ref, v_ref, u_ref):
    cw = convw_ref.shape[-1]
    pg = poolw_ref.shape[-1]
    x = x_ref[...]
    h = _mod_norm(x, g_ref[...], sc_ref[...], sh_ref[...]).astype(BF16)
    p = _dot(h, win_ref[...])
    bg = p[:, 0:cw]
    v = p[:, cw:2 * cw] * p[:, 2 * cw:3 * cw]
    u = p[:, 3 * cw:]
    v_ref[...] = v
    u_ref[...] = u

    w = convw_ref[...]
    y = w[CONV_K - 1:CONV_K] * v
    for k in range(1, CONV_K):
        y = y + w[CONV_K - 1 - k:CONV_K - k] * cb_ref[CONV_K - 1 - k]

    d_groups = []
    for g, win in enumerate(POOL_WINDOWS):
        ug = u[:, g * pg:(g + 1) * pg]
        acc = ug
        for k in range(1, win):
            acc = acc + pb_ref[POOL_HIST - k, :, g * pg:(g + 1) * pg]
        d_groups.append(acc / float(win) - ug)

    xo_ref[...] = _mix_tail(x, gt_ref[...], bg, y, d_groups, poolw_ref, pscale_ref,
                            gco_ref, gpo_ref, wout_ref)


def _mix_sample(x, sh, sc, gt, g, w_in, conv_w, pool_w, pool_scale, g_co, g_po, w_out, cb, pb):
    n, d = x.shape
    cw = conv_w.shape[-1]
    pw = pool_scale.shape[-1]
    return pl.pallas_call(
        _mix_sample_kernel,
        out_shape=[
            jax.ShapeDtypeStruct((n, d), F32),
            jax.ShapeDtypeStruct((n, cw), F32),
            jax.ShapeDtypeStruct((n, pw), F32),
        ],
        compiler_params=pltpu.CompilerParams(vmem_limit_bytes=VMEM_LIMIT_BYTES),
        name="mix_sample",
    )(x, sh, sc, gt, g.reshape(1, d), w_in, conv_w, pool_w, pool_scale.reshape(1, pw),
      g_co.reshape(1, cw), g_po.reshape(1, pw), w_out, cb, pb)


def _ffn_kernel(x_ref, sh_ref, sc_ref, gt_ref, g_ref, wg_ref, wu_ref, wd_ref, o_ref, *, chunks):
    x = x_ref[0]
    h = _mod_norm(x, g_ref[...], sc_ref[0], sh_ref[0]).astype(BF16)
    acc = None
    lo = 0
    for fc in chunks:
        a = _dot(h, wg_ref[:, lo:lo + fc])
        b = _dot(h, wu_ref[:, lo:lo + fc])
        part = _dot(_silu_mul(a, b).astype(BF16), wd_ref[lo:lo + fc, :])
        acc = part if acc is None else acc + part
        lo += fc
    o_ref[0] = x + gt_ref[0] * acc


def _ffn(x, sh, sc, gt, g, wg, wu, wd, tile, chunks):
    b, seq, d = x.shape
    tm = sh.shape[1]
    tmod = tile if tm > 1 else 1
    mod_map = (lambda bi, li: (bi, li, 0)) if tm > 1 else (lambda bi, li: (bi, 0, 0))
    return pl.pallas_call(
        functools.partial(_ffn_kernel, chunks=chunks),
        grid=(b, seq // tile),
        in_specs=[
            pl.BlockSpec((1, tile, d), lambda bi, li: (bi, li, 0)),
            pl.BlockSpec((1, tmod, d), mod_map),
            pl.BlockSpec((1, tmod, d), mod_map),
            pl.BlockSpec((1, tmod, d), mod_map),
            _resident((1, d)),
            _resident(wg.shape),
            _resident(wu.shape),
            _resident(wd.shape),
        ],
        out_specs=pl.BlockSpec((1, tile, d), lambda bi, li: (bi, li, 0)),
        out_shape=jax.ShapeDtypeStruct((b, seq, d), F32),
        compiler_params=_params("arbitrary", "arbitrary"),
        name="ffn",
    )(x, sh, sc, gt, g.reshape(1, d), wg, wu, wd)


def _split_bf16(a):
    hi = a.astype(BF16)
    return hi, (a - hi.astype(F32)).astype(BF16)


def _router_kernel(x_ref, sh_ref, sc_ref, g_ref, rw_ref, base_ref, h_ref, route_ref, cnt_ref,
                   tri, carry, *, n_steps):
    t = x_ref.shape[0]
    step = pl.program_id(0)

    @pl.when(step == 0)
    def _():
        r = lax.broadcasted_iota(I32, (t, t), 0)
        c = lax.broadcasted_iota(I32, (t, t), 1)
        tri[...] = jnp.where(r < c, 1.0, 0.0).astype(BF16)
        carry[...] = base_ref[...]

    @pl.when(step >= n_steps)
    def _():
        h_ref[...] = jnp.zeros_like(h_ref)

    @pl.when(step < n_steps)
    def _():
        h = _mod_norm(x_ref[...], g_ref[...], sc_ref[0], sh_ref[0])
        h_ref[...] = h
        h_hi, h_lo = _split_bf16(h)
        rw_hi, rw_lo = _split_bf16(rw_ref[...])
        logits = _dot_nt(rw_hi, h_hi) + _dot_nt(rw_hi, h_lo) + _dot_nt(rw_lo, h_hi)

        ne = logits.shape[0]
        idx = lax.broadcasted_iota(I32, logits.shape, 0)
        m1 = jnp.max(logits, axis=0, keepdims=True)
        i1 = jnp.min(jnp.where(logits == m1, idx, ne), axis=0, keepdims=True)
        sel1 = idx == i1
        rest = jnp.where(sel1, -jnp.inf, logits)
        m2 = jnp.max(rest, axis=0, keepdims=True)
        i2 = jnp.min(jnp.where(rest == m2, idx, ne), axis=0, keepdims=True)
        sel2 = idx == i2
        e = jnp.exp(m2 - m1)
        p1 = 1.0 / (1.0 + e)
        p2 = e / (1.0 + e)

        chosen = jnp.where(sel1, 1.0, jnp.where(sel2, 1.0, 0.0))
        before = _dot(chosen.astype(BF16), tri[...]) + carry[:, 0:1]
        r1 = jnp.sum(jnp.where(sel1, before, 0.0), axis=0, keepdims=True)
        r2 = jnp.sum(jnp.where(sel2, before, 0.0), axis=0, keepdims=True)
        carry[...] += jnp.sum(chosen, axis=1, keepdims=True)
        cnt_ref[...] = carry[...]

        rows = (i1.astype(F32), i2.astype(F32), r1, r2, p1, p2)
        route = jnp.zeros(logits.shape, F32)
        for k, row in enumerate(rows):
            route = jnp.where(idx == k, row, route)
        route_ref[...] = route


def _router(x2d, sh, sc, g, router_w, base_cnt, tile, rows_per_mod, h_rows, h_prev=None,
            row_off=0):
    n, d = x2d.shape
    ne = router_w.shape[-1]
    assert ne == ROUTE_ROWS and n % tile == 0 and row_off % tile == 0 and tile % LANES == 0
    assert base_cnt.shape == (ne, LANES)
    n_steps = n // tile
    tail = h_rows - (row_off + n)
    assert 0 <= tail < tile
    fill_tail = h_prev is None and tail > 0
    last = n_steps - 1
    clamp = (lambda i: jnp.minimum(i, last)) if fill_tail else (lambda i: i)
    per_row_mod = sh.shape[1] > 1
    if per_row_mod:
        mod_spec = pl.BlockSpec((1, tile, d), lambda i: (0, clamp(i), 0))
    else:
        assert rows_per_mod % tile == 0
        mod_spec = pl.BlockSpec((1, 1, d), lambda i: (clamp(i) // (rows_per_mod // tile), 0, 0))
    in_specs = [
        pl.BlockSpec((tile, d), lambda i: (clamp(i), 0)),
        mod_spec,
        mod_spec,
        _resident((1, d)),
        _resident((ne, d)),
        _resident((ne, LANES)),
    ]
    args = [x2d, sh, sc, g.reshape(1, d), router_w.T, base_cnt]
    n_in = len(args)
    aliases = {}
    body = functools.partial(_router_kernel, n_steps=n_steps)
    if h_prev is not None:
        assert h_prev.shape == (h_rows, d)
        in_specs.append(pl.BlockSpec(memory_space=pl.ANY))
        args.append(h_prev)
        aliases = {n_in: 0}

        def body(*refs):
            _router_kernel(*refs[:n_in], *refs[n_in + 1:], n_steps=n_steps)

    blk_off = row_off // tile
    return pl.pallas_call(
        body,
        grid=(n_steps + int(fill_tail),),
        in_specs=in_specs,
        out_specs=[
            pl.BlockSpec((tile, d), lambda i: (i + blk_off, 0)),
            pl.BlockSpec((ne, tile), lambda i: (0, clamp(i))),
            pl.BlockSpec((ne, LANES), lambda i: (0, 0)),
        ],
        out_shape=[
            jax.ShapeDtypeStruct((h_rows, d), F32),
            jax.ShapeDtypeStruct((ne, n), F32),
            jax.ShapeDtypeStruct((ne, LANES), F32),
        ],
        scratch_shapes=[pltpu.VMEM((tile, tile), BF16), pltpu.VMEM((ne, LANES), F32)],
        input_output_aliases=aliases,
        compiler_params=_params("arbitrary"),
        name="router",
    )(*args)


def _row_copy(src_hbm, row, dst, dst_row, sem):
    return pltpu.make_async_copy(src_hbm.at[pl.ds(row, 1), :], dst.at[pl.ds(dst_row, 1), :], sem)


def _row_copy_out(src, src_row, dst_hbm, row, sem):
    return pltpu.make_async_copy(src.at[pl.ds(src_row, 1), :], dst_hbm.at[pl.ds(row, 1), :], sem)


def _moe_grouped_kernel(te_ref, na_ref, src_ref, dst_ref, h_hbm, wg_ref, wu_ref, wd_ref, y_hbm,
                        xs, hbuf, obuf, gsem, ssem, fsem, *, tm, nc, rows_per_step, fill_row,
                        fill_blocks):
    del te_ref
    i = pl.program_id(0)
    c = pl.program_id(1)
    n_active = na_ref[0]
    ahead = ROW_SLOTS - 1
    slot = i % ROW_SLOTS

    def request(tile, r):
        s = tile % ROW_SLOTS
        return _row_copy(h_hbm, src_ref[tile * tm + r], xs.at[s], r, gsem.at[s])

    def send(tile, r):
        s = (tile + ROW_SLOTS) % ROW_SLOTS
        return _row_copy_out(obuf.at[s], r, y_hbm, dst_ref[(tile + 1) * tm + r], ssem.at[s])

    @pl.when(jnp.logical_and(i == 0, c == 0))
    def _():
        last = ROW_SLOTS - 1
        obuf[last] = jnp.zeros((tm, obuf.shape[-1]), F32)
        fills = [pltpu.make_async_copy(obuf.at[last, pl.ds(0, FILL_ROWS), :],
                                       y_hbm.at[pl.ds(fill_row + q * FILL_ROWS, FILL_ROWS), :],
                                       fsem) for q in range(fill_blocks)]
        for f in fills:
            f.start()
        for f in fills:
            f.wait()

        def body(r, carry):
            for t in range(ahead):
                request(t, r).start()
            return carry

        lax.fori_loop(0, tm, body, 0)

    @pl.when(jnp.logical_and(i < n_active + ahead, c == 0))
    def _():
        pltpu.make_async_copy(h_hbm.at[pl.ds(0, tm), :], xs.at[slot], gsem.at[slot]).wait()

    @pl.when(jnp.logical_and(jnp.logical_and(i >= ahead, i < n_active + ROW_SLOTS), c == 0))
    def _():
        pltpu.make_async_copy(obuf.at[slot], y_hbm.at[pl.ds(0, tm), :], ssem.at[slot]).wait()

    @pl.when(jnp.logical_and(i < n_active, c == 0))
    def _():
        obuf[slot] = jnp.zeros((tm, obuf.shape[-1]), F32)
        hbuf[...] = xs[slot].astype(BF16)

    @pl.when(jnp.logical_and(i == n_active, c == 0))
    def _():
        def body(r, carry):
            send(i - 1, r).start()
            return carry

        lax.fori_loop(0, tm, body, 0)

    @pl.when(jnp.logical_and(i < n_active, c == 0))
    def _():
        for r in range(rows_per_step * nc, tm):
            request(i + ahead, r).start()
            send(i - 1, r).start()

    @pl.when(i < n_active)
    def _():
        for k in range(rows_per_step):
            r = c * rows_per_step + k
            request(i + ahead, r).start()
            send(i - 1, r).start()

        h = hbuf[...]
        a = _dot(h, wg_ref[0].astype(BF16))
        b = _dot(h, wu_ref[0].astype(BF16))
        obuf[slot] += _dot(_silu_mul(a, b).astype(BF16), wd_ref[0].astype(BF16))


def _moe_grouped(h_all, tile_expert, n_active, src, dst, wg, wu, wd, tm, fc, y_rows, fill_row,
                 fill_rows):
    ne, d, dff = wg.shape
    n_tiles = tile_expert.shape[0]
    nc = dff // fc
    assert dff % fc == 0
    assert src.shape[0] == n_tiles * tm and dst.shape[0] == n_tiles * tm
    assert fill_rows % FILL_ROWS == 0 and tm >= FILL_ROWS
    rows_per_step = tm // nc

    def w_col(i, c, te, na, src_, dst_):
        return (te[i], 0, jnp.where(i < na[0], c, nc - 1))

    def w_row(i, c, te, na, src_, dst_):
        return (te[i], jnp.where(i < na[0], c, nc - 1), 0)

    grid_spec = pltpu.PrefetchScalarGridSpec(
        num_scalar_prefetch=4,
        grid=(n_tiles, nc),
        in_specs=[
            pl.BlockSpec(memory_space=pl.ANY),
            pl.BlockSpec((1, d, fc), w_col),
            pl.BlockSpec((1, d, fc), w_col),
            pl.BlockSpec((1, fc, d), w_row),
        ],
        out_specs=pl.BlockSpec(memory_space=pl.ANY),
        scratch_shapes=[
            pltpu.VMEM((ROW_SLOTS, tm, d), F32),
            pltpu.VMEM((tm, d), BF16),
            pltpu.VMEM((ROW_SLOTS, tm, d), F32),
            pltpu.SemaphoreType.DMA((ROW_SLOTS,)),
            pltpu.SemaphoreType.DMA((ROW_SLOTS,)),
            pltpu.SemaphoreType.DMA,
        ],
    )
    return pl.pallas_call(
        functools.partial(_moe_grouped_kernel, tm=tm, nc=nc, rows_per_step=rows_per_step,
                          fill_row=fill_row, fill_blocks=fill_rows // FILL_ROWS),
        grid_spec=grid_spec,
        out_shape=jax.ShapeDtypeStruct((y_rows, d), F32),
        compiler_params=_params("arbitrary", "arbitrary"),
        name="moe_grouped",
    )(tile_expert, n_active, src, dst, h_all, wg, wu, wd)


def _combine_kernel(x_ref, gt_ref, route_ref, gf_ref, y0_ref, y1_ref, o_ref):
    route = route_ref[...]
    pad = jnp.zeros((LANES - route.shape[0], route.shape[1]), F32)
    cols = jnp.concatenate([route, pad], axis=0).T
    f = (cols[:, ROW_PROB:ROW_PROB + 1] * y0_ref[...]
         + cols[:, ROW_PROB + 1:ROW_PROB + 2] * y1_ref[...])
    o_ref[...] = _rms(x_ref[...] + gt_ref[0] * f, gf_ref[...])


def _combine(x2d, gt, route, g_final, y_tok, tile, rows_per_mod, tok_off, plane_rows):
    n, d = x2d.shape
    assert n % tile == 0 and tok_off % tile == 0 and plane_rows % tile == 0
    per_row_mod = gt.shape[1] > 1
    if per_row_mod:
        mod_spec = pl.BlockSpec((1, tile, d), lambda j: (0, j, 0))
    else:
        assert rows_per_mod % tile == 0
        mod_spec = pl.BlockSpec((1, 1, d), lambda j: (j // (rows_per_mod // tile), 0, 0))
    first = tok_off // tile
    second = (plane_rows + tok_off) // tile
    return pl.pallas_call(
        _combine_kernel,
        grid=(n // tile,),
        in_specs=[
            pl.BlockSpec((tile, d), lambda j: (j, 0)),
            mod_spec,
            pl.BlockSpec((ROUTE_ROWS, tile), lambda j: (0, j)),
            pl.BlockSpec((1, d), lambda j: (0, 0)),
            pl.BlockSpec((tile, d), lambda j: (first + j, 0)),
            pl.BlockSpec((tile, d), lambda j: (second + j, 0)),
        ],
        out_specs=pl.BlockSpec((tile, d), lambda j: (j, 0)),
        out_shape=jax.ShapeDtypeStruct((n, d), F32),
        compiler_params=_params("arbitrary"),
        name="combine",
    )(x2d, gt, route, g_final.reshape(1, d), y_tok, y_tok)


def _invert_kernel(perm_ref, inv_ref):
    def place(a, carry):
        inv_ref[perm_ref[a]] = a
        return carry

    lax.fori_loop(0, perm_ref.shape[0], place, 0, unroll=32)


def _invert(perm):
    assert perm.shape[0] % 32 == 0
    smem = pl.BlockSpec(memory_space=pltpu.SMEM)
    return pl.pallas_call(
        _invert_kernel,
        in_specs=[smem],
        out_specs=smem,
        out_shape=jax.ShapeDtypeStruct(perm.shape, I32),
        name="invert",
    )(perm)


def _routing_tables(route_all, counts, tm, n_tiles, plane_rows):
    ne = counts.shape[0]
    n = route_all.shape[1]
    e_idx = route_all[ROW_EXPERT:ROW_EXPERT + TOP_K].astype(I32)
    rank = route_all[ROW_RANK:ROW_RANK + TOP_K].astype(I32)
    padded = (counts + tm - 1) // tm * tm
    ends = jnp.cumsum(padded)
    offs = ends - padded
    onehot = e_idx[..., None] == jnp.arange(ne, dtype=I32)
    pos = jnp.sum(jnp.where(onehot, offs, 0), axis=-1) + rank
    n_active = ends[-1] // tm
    starts = jnp.arange(n_tiles, dtype=I32) * tm
    tile_expert = jnp.sum(starts[:, None] >= ends[None, :], axis=1).astype(I32)
    last_expert = jnp.sum((n_active - 1) * tm >= ends).astype(I32)
    tile_expert = jnp.minimum(tile_expert, last_expert)
    n_rows = n_tiles * tm
    n_real = n * TOP_K
    pad_ends = jnp.cumsum(padded - counts)
    j = jnp.arange(n_rows - n_real, dtype=I32)[:, None]
    in_expert = jnp.logical_and(j >= pad_ends - (padded - counts), j < pad_ends)
    pad_pos = jnp.where(j[:, 0] < pad_ends[-1],
                        jnp.sum(jnp.where(in_expert, offs + counts + j - pad_ends + padded - counts, 0),
                                axis=1),
                        ends[-1] + j[:, 0] - pad_ends[-1])
    inv = _invert(jnp.concatenate([pos.reshape(-1), pad_pos]))
    is_real = inv < n_real
    tok = inv % n
    choice = inv // n
    spare = n + jnp.minimum(inv - n_real, ne * tm - 1)
    src = jnp.where(is_real, tok, 0)
    dst = jnp.where(is_real, choice * plane_rows + tok, spare)
    lead = n + ne * tm + jnp.arange(tm, dtype=I32)
    dst = jnp.concatenate([lead, dst])[:n_rows]
    return src, dst, tile_expert, n_active.astype(I32).reshape(1)


def _moe_layer(xp, xs, mod_p, mod_s, g, router_w, wg, wu, wd, g_final):
    nb, seq, d = xp.shape
    ns = xs.shape[0]
    ne = router_w.shape[-1]
    n_all = nb * seq + ns
    sh_p, sc_p, gt_p = mod_p
    sh_s, sc_s, gt_s = mod_s
    xp2 = xp.reshape(nb * seq, d)

    zero_cnt = jnp.zeros((ne, LANES), F32)
    h_all, route_p, cnt_p = _router(xp2, sh_p, sc_p, g, router_w, zero_cnt, ROUTER_TILE, seq,
                                    n_all)
    h_all, route_s, cnt_s = _router(xs, sh_s, sc_s, g, router_w, cnt_p, ns, ns, n_all,
                                    h_prev=h_all, row_off=nb * seq)

    tm = MOE_ROW_TILE
    n_tiles = (n_all * TOP_K) // tm + ne + ROW_SLOTS
    plane_rows = -(-(n_all + (ne + 1) * tm) // COMBINE_TILE) * COMBINE_TILE
    route_all = jnp.concatenate([route_p, route_s], axis=1)
    src, dst, tile_expert, n_active = _routing_tables(
        route_all, cnt_s[:, 0].astype(I32), tm, n_tiles, plane_rows)

    y_tok = _moe_grouped(h_all, tile_expert, n_active, src, dst, wg, wu, wd, tm, MOE_FF_CHUNK,
                         plane_rows + n_all, n_all, plane_rows - n_all)
    yp = _combine(xp2, gt_p, route_p, g_final, y_tok, COMBINE_TILE, seq, 0, plane_rows)
    ys = _combine(xs, gt_s, route_s, g_final, y_tok, ns, ns, nb * seq, plane_rows)
    return yp.reshape(nb, seq, d), ys


def kernel(x_prompt, x_sample, c_prompt, c_sample, state_conv, state_pool, w_ada, b_ada, g_mix,
           g_ffn, w_in, conv_w, pool_w, pool_scale, g_conv_out, g_pool_out, w_out, dense_w_gate,
           dense_w_up, dense_w_down, router_w, moe_w_gate, moe_w_up, moe_w_down, g_final):
    depth = w_ada.shape[0]
    nb, seq, d = x_prompt.shape
    ns = x_sample.shape[0]
    assert x_sample.shape[1] == 1 and depth == 2
    assert sum(DENSE_FF_CHUNKS) == dense_w_gate.shape[-1]

    mod = _ada(jnp.concatenate([c_prompt, c_sample], axis=0), w_ada, b_ada)
    mod = mod.reshape(depth, nb + ns, 6, d)
    mod_p = [jnp.transpose(mod[i, :nb], (1, 0, 2))[:, :, None, :] for i in range(depth)]
    mod_s = [jnp.transpose(mod[i, nb:], (1, 0, 2))[:, None, :, :] for i in range(depth)]

    w_in_b = w_in.astype(BF16)
    w_out_b = w_out.astype(BF16)
    pool_w_b = pool_w.astype(BF16)
    dense_b = [w.astype(BF16) for w in (dense_w_gate, dense_w_up, dense_w_down)]
    cb_t = jnp.transpose(state_conv, (0, 2, 1, 3))
    pb_t = jnp.transpose(state_pool, (0, 2, 1, 3))

    xp = x_prompt
    xs = x_sample.reshape(ns, d)
    conv_p, pool_p, conv_s, pool_s = [], [], [], []
    for i in range(depth):
        sh1, sc1, gt1, sh2, sc2, gt2 = mod_p[i]
        mix_w = (g_mix[i], w_in_b[i], conv_w[i], pool_w_b[i], pool_scale[i], g_conv_out[i],
                 g_pool_out[i], w_out_b[i])
        xp, cs, ps = _mix_prompt(xp, sh1, sc1, gt1, *mix_w, tile=MIX_TILE, sub=MIX_ROW_BLOCK)
        conv_p.append(cs)
        pool_p.append(ps)
        s1, c1, t1, s2, c2, t2 = mod_s[i]
        xs, v_new, u_new = _mix_sample(xs, s1[0], c1[0], t1[0], *mix_w, cb_t[i], pb_t[i])
        conv_s.append(jnp.concatenate([state_conv[i][:, 1:], v_new[:, None, :]], axis=1))
        pool_s.append(jnp.concatenate([state_pool[i][:, 1:], u_new[:, None, :]], axis=1))
        j = i // 2
        if i % 2 == 0:
            wg, wu, wd = (w[j] for w in dense_b)
            xp = _ffn(xp, sh2, sc2, gt2, g_ffn[i], wg, wu, wd, PROMPT_TILE, DENSE_FF_CHUNKS)
            xs = _ffn(xs[None], s2, c2, t2, g_ffn[i], wg, wu, wd, ns, DENSE_FF_CHUNKS)[0]
        else:
            xp, xs = _moe_layer(xp, xs, (sh2, sc2, gt2), (s2, c2, t2), g_ffn[i], router_w[j],
                                moe_w_gate[j], moe_w_up[j], moe_w_down[j], g_final)

    return (xp, xs.reshape(ns, 1, d), jnp.stack(conv_p), jnp.stack(pool_p),
            jnp.stack(conv_s), jnp.stack(pool_s))
```

```python
import functools

import jax
import jax.numpy as jnp
from jax import lax
from jax.experimental import pallas as pl
from jax.experimental.pallas import tpu as pltpu

F32 = jnp.float32
BF16 = jnp.bfloat16
I32 = jnp.int32

EPS = 1e-6
CONV_K = 3
POOL_WINDOWS = (2, 4, 8, 16)
POOL_HIST = max(POOL_WINDOWS) - 1
TOP_K = 2

CONV_PAD = 8
POOL_PAD = 16

LANES = 128
MXU_ROWS = 256

ROUTE_ROWS = 8
ROW_EXPERT, ROW_RANK, ROW_PROB = 0, 2, 4

VMEM_LIMIT_BYTES = 56 * 1024 * 1024

PROMPT_TILE = 512
MIX_TILE = 2048
MIX_ROW_BLOCK = 512
ROUTER_TILE = 1024
COMBINE_TILE = 512
MOE_ROW_TILE = 1024
MOE_FF_CHUNK = 512
FILL_ROWS = 128
ROW_SLOTS = 3
DENSE_FF_CHUNKS = (768, 768, 768, 512)


def _params(*sem):
    return pltpu.CompilerParams(dimension_semantics=sem, vmem_limit_bytes=VMEM_LIMIT_BYTES)


def _resident(shape):
    nd = len(shape)
    return pl.BlockSpec(shape, lambda *_: (0,) * nd, pipeline_mode=pl.Buffered(1))


def _rms(x, g):
    ms = jnp.mean(x * x, axis=-1, keepdims=True)
    return x * lax.rsqrt(ms + EPS) * g


def _mod_norm(x, g, sc, sh):
    ms = jnp.mean(x * x, axis=-1, keepdims=True)
    return x * lax.rsqrt(ms + EPS) * (g * (1.0 + sc)) + sh


def _dot(a, b):
    return jnp.dot(a, b, preferred_element_type=F32)


def _dot_nt(a, b):
    return lax.dot_general(a, b, (((1,), (1,)), ((), ())), preferred_element_type=F32)


def _silu_mul(a, b):
    return a * jax.nn.sigmoid(a) * b


def _ada_kernel(c_ref, w_ref, b_ref, o_ref):
    c = c_ref[...]
    a = (c * jax.nn.sigmoid(c)).astype(BF16)
    o_ref[0] = _dot(a, w_ref[0].astype(BF16)) + b_ref[0]


def _ada(c_all, w_ada, b_ada):
    depth, d, n = w_ada.shape
    m = c_all.shape[0]
    tn = 1024
    return pl.pallas_call(
        _ada_kernel,
        grid=(depth, n // tn),
        in_specs=[
            pl.BlockSpec((m, d), lambda i, j: (0, 0)),
            pl.BlockSpec((1, d, tn), lambda i, j: (i, 0, j)),
            pl.BlockSpec((1, 1, tn), lambda i, j: (i, 0, j)),
        ],
        out_specs=pl.BlockSpec((1, m, tn), lambda i, j: (i, 0, j)),
        out_shape=jax.ShapeDtypeStruct((depth, m, n), F32),
        compiler_params=_params("arbitrary", "arbitrary"),
        name="ada",
    )(c_all, w_ada, b_ada.reshape(depth, 1, n))


def _mix_tail(x, gt, bg, y, d_groups, poolw_ref, pscale_ref, gco_ref, gpo_ref, wout_ref):
    cw = gco_ref.shape[-1]
    ya = bg * y
    yb = jnp.concatenate(
        [_dot(d.astype(BF16), poolw_ref[g]) for g, d in enumerate(d_groups)], axis=-1
    ) * pscale_ref[...]
    ma = _rms(ya, gco_ref[...]).astype(BF16)
    mb = _rms(yb, gpo_ref[...]).astype(BF16)
    o = _dot(ma, wout_ref[0:cw, :]) + _dot(mb, wout_ref[cw:, :])
    return x + gt * o


def _mix_prompt_kernel(x_ref, sh_ref, sc_ref, gt_ref, g_ref, win_ref, convw_ref, poolw_ref,
                       pscale_ref, gco_ref, gpo_ref, wout_ref,
                       xo_ref, cs_ref, ps_ref, vbuf, ubuf, *, tile, sub):
    l = pl.program_id(1)
    cw = convw_ref.shape[-1]
    pg = poolw_ref.shape[-1]

    @pl.when(l == 0)
    def _():
        vbuf[0:CONV_PAD, :] = jnp.zeros((CONV_PAD, cw), F32)
        ubuf[0:POOL_PAD, :] = jnp.zeros((POOL_PAD, ubuf.shape[-1]), F32)

    gates = []
    for lo in range(0, tile, sub):
        x = x_ref[0, lo:lo + sub, :]
        h = _mod_norm(x, g_ref[...], sc_ref[0], sh_ref[0]).astype(BF16)
        p = _dot(h, win_ref[...])
        gates.append(p[:, 0:cw])
        vbuf[CONV_PAD + lo:CONV_PAD + lo + sub, :] = p[:, cw:2 * cw] * p[:, 2 * cw:3 * cw]
        ubuf[POOL_PAD + lo:POOL_PAD + lo + sub, :] = p[:, 3 * cw:]

    w = convw_ref[...]
    for bg, lo in zip(gates, range(0, tile, sub)):
        y = w[CONV_K - 1:CONV_K] * vbuf[CONV_PAD + lo:CONV_PAD + lo + sub, :]
        for k in range(1, CONV_K):
            y = y + w[CONV_K - 1 - k:CONV_K - k] * vbuf[CONV_PAD + lo - k:CONV_PAD + lo - k + sub, :]

        pos = l * tile + lo + lax.broadcasted_iota(I32, (sub, 1), 0)
        d_groups = []
        for g, win in enumerate(POOL_WINDOWS):
            ug = ubuf[POOL_PAD + lo:POOL_PAD + lo + sub, g * pg:(g + 1) * pg]
            acc = ug
            for k in range(1, win):
                acc = acc + ubuf[POOL_PAD + lo - k:POOL_PAD + lo - k + sub, g * pg:(g + 1) * pg]
            cnt = jnp.minimum(pos + 1, win).astype(F32)
            d_groups.append(acc / cnt - ug)

        xo_ref[0, lo:lo + sub, :] = _mix_tail(x_ref[0, lo:lo + sub, :], gt_ref[0], bg, y, d_groups,
                                              poolw_ref, pscale_ref, gco_ref, gpo_ref, wout_ref)

    cs_ref[0] = vbuf[CONV_PAD + tile - (CONV_K - 1):CONV_PAD + tile, :]
    ps_ref[0] = ubuf[POOL_PAD + tile - POOL_HIST:POOL_PAD + tile, :]
    vbuf[0:CONV_PAD, :] = vbuf[tile:tile + CONV_PAD, :]
    ubuf[0:POOL_PAD, :] = ubuf[tile:tile + POOL_PAD, :]


def _mix_prompt(x, sh, sc, gt, g, w_in, conv_w, pool_w, pool_scale, g_co, g_po, w_out, tile, sub):
    b, seq, d = x.shape
    assert seq % tile == 0 and tile % sub == 0 and sub >= POOL_PAD
    cw = conv_w.shape[-1]
    pw = pool_scale.shape[-1]
    row = lambda bi, li: (bi, 0, 0)
    return pl.pallas_call(
        functools.partial(_mix_prompt_kernel, tile=tile, sub=sub),
        grid=(b, seq // tile),
        in_specs=[
            pl.BlockSpec((1, tile, d), lambda bi, li: (bi, li, 0)),
            pl.BlockSpec((1, 1, d), row),
            pl.BlockSpec((1, 1, d), row),
            pl.BlockSpec((1, 1, d), row),
            _resident((1, d)),
            _resident(w_in.shape),
            _resident(conv_w.shape),
            _resident(pool_w.shape),
            _resident((1, pw)),
            _resident((1, cw)),
            _resident((1, pw)),
            _resident(w_out.shape),
        ],
        out_specs=[
            pl.BlockSpec((1, tile, d), lambda bi, li: (bi, li, 0)),
            pl.BlockSpec((1, CONV_K - 1, cw), row),
            pl.BlockSpec((1, POOL_HIST, pw), row),
        ],
        out_shape=[
            jax.ShapeDtypeStruct((b, seq, d), F32),
            jax.ShapeDtypeStruct((b, CONV_K - 1, cw), F32),
            jax.ShapeDtypeStruct((b, POOL_HIST, pw), F32),
        ],
        scratch_shapes=[
            pltpu.VMEM((CONV_PAD + tile, cw), F32),
            pltpu.VMEM((POOL_PAD + tile, pw), F32),
        ],
        compiler_params=_params("arbitrary", "arbitrary"),
        name="mix_prompt",
    )(x, sh, sc, gt, g.reshape(1, d), w_in, conv_w, pool_w, pool_scale.reshape(1, pw),
      g_co.reshape(1, cw), g_po.reshape(1, pw), w_out)


def _mix_sample_kernel(x_ref, sh_ref, sc_ref, gt_ref, g_ref, win_ref, convw_ref, poolw_ref,
                       pscale_ref, gco_ref, gpo_ref, wout_ref, cb_ref, pb_ref,
                       xo_ref, v_ref, u_ref):
    cw = convw_ref.shape[-1]
    pg = poolw_ref.shape[-1]
    x = x_ref[...]
    h = _mod_norm(x, g_ref[...], sc_ref[...], sh_ref[...]).astype(BF16)
    p = _dot(h, win_ref[...])
    bg = p[:, 0:cw]
    v = p[:, cw:2 * cw] * p[:, 2 * cw:3 * cw]
    u = p[:, 3 * cw:]
    v_ref[...] = v
    u_ref[...] = u

    w = convw_ref[...]
    y = w[CONV_K - 1:CONV_K] * v
    for k in range(1, CONV_K):
        y = y + w[CONV_K - 1 - k:CONV_K - k] * cb_ref[CONV_K - 1 - k]

    d_groups = []
    for g, win in enumerate(POOL_WINDOWS):
        ug = u[:, g * pg:(g + 1) * pg]
        acc = ug
        for k in range(1, win):
            acc = acc + pb_ref[POOL_HIST - k, :, g * pg:(g + 1) * pg]
        d_groups.append(acc / float(win) - ug)

    xo_ref[...] = _mix_tail(x, gt_ref[...], bg, y, d_groups, poolw_ref, pscale_ref,
                            gco_ref, gpo_ref, wout_ref)


def _mix_sample(x, sh, sc, gt, g, w_in, conv_w, pool_w, pool_scale, g_co, g_po, w_out, cb, pb):
    n, d = x.shape
    cw = conv_w.shape[-1]
    pw = pool_scale.shape[-1]
    return pl.pallas_call(
        _mix_sample_kernel,
        out_shape=[
            jax.ShapeDtypeStruct((n, d), F32),
            jax.ShapeDtypeStruct((n, cw), F32),
            jax.ShapeDtypeStruct((n, pw), F32),
        ],
        compiler_params=pltpu.CompilerParams(vmem_limit_bytes=VMEM_LIMIT_BYTES),
        name="mix_sample",
    )(x, sh, sc, gt, g.reshape(1, d), w_in, conv_w, pool_w, pool_scale.reshape(1, pw),
      g_co.reshape(1, cw), g_po.reshape(1, pw), w_out, cb, pb)


def _ffn_kernel(x_ref, sh_ref, sc_ref, gt_ref, g_ref, wg_ref, wu_ref, wd_ref, o_ref, *, chunks):
    x = x_ref[0]
    h = _mod_norm(x, g_ref[...], sc_ref[0], sh_ref[0]).astype(BF16)
    acc = None
    lo = 0
    for fc in chunks:
        a = _dot(h, wg_ref[:, lo:lo + fc])
        b = _dot(h, wu_ref[:, lo:lo + fc])
        part = _dot(_silu_mul(a, b).astype(BF16), wd_ref[lo:lo + fc, :])
        acc = part if acc is None else acc + part
        lo += fc
    o_ref[0] = x + gt_ref[0] * acc


def _ffn(x, sh, sc, gt, g, wg, wu, wd, tile, chunks):
    b, seq, d = x.shape
    tm = sh.shape[1]
    tmod = tile if tm > 1 else 1
    mod_map = (lambda bi, li: (bi, li, 0)) if tm > 1 else (lambda bi, li: (bi, 0, 0))
    return pl.pallas_call(
        functools.partial(_ffn_kernel, chunks=chunks),
        grid=(b, seq // tile),
        in_specs=[
            pl.BlockSpec((1, tile, d), lambda bi, li: (bi, li, 0)),
            pl.BlockSpec((1, tmod, d), mod_map),
            pl.BlockSpec((1, tmod, d), mod_map),
            pl.BlockSpec((1, tmod, d), mod_map),
            _resident((1, d)),
            _resident(wg.shape),
            _resident(wu.shape),
            _resident(wd.shape),
        ],
        out_specs=pl.BlockSpec((1, tile, d), lambda bi, li: (bi, li, 0)),
        out_shape=jax.ShapeDtypeStruct((b, seq, d), F32),
        compiler_params=_params("arbitrary", "arbitrary"),
        name="ffn",
    )(x, sh, sc, gt, g.reshape(1, d), wg, wu, wd)


def _split_bf16(a):
    hi = a.astype(BF16)
    return hi, (a - hi.astype(F32)).astype(BF16)


def _router_kernel(x_ref, sh_ref, sc_ref, g_ref, rw_ref, base_ref, h_ref, route_ref, cnt_ref,
                   tri, carry, *, n_steps):
    t = x_ref.shape[0]
    step = pl.program_id(0)

    @pl.when(step == 0)
    def _():
        r = lax.broadcasted_iota(I32, (t, t), 0)
        c = lax.broadcasted_iota(I32, (t, t), 1)
        tri[...] = jnp.where(r < c, 1.0, 0.0).astype(BF16)
        carry[...] = base_ref[...]

    @pl.when(step >= n_steps)
    def _():
        h_ref[...] = jnp.zeros_like(h_ref)

    @pl.when(step < n_steps)
    def _():
        h = _mod_norm(x_ref[...], g_ref[...], sc_ref[0], sh_ref[0])
        h_ref[...] = h
        h_hi, h_lo = _split_bf16(h)
        rw_hi, rw_lo = _split_bf16(rw_ref[...])
        logits = _dot_nt(rw_hi, h_hi) + _dot_nt(rw_hi, h_lo) + _dot_nt(rw_lo, h_hi)

        ne = logits.shape[0]
        idx = lax.broadcasted_iota(I32, logits.shape, 0)
        m1 = jnp.max(logits, axis=0, keepdims=True)
        i1 = jnp.min(jnp.where(logits == m1, idx, ne), axis=0, keepdims=True)
        sel1 = idx == i1
        rest = jnp.where(sel1, -jnp.inf, logits)
        m2 = jnp.max(rest, axis=0, keepdims=True)
        i2 = jnp.min(jnp.where(rest == m2, idx, ne), axis=0, keepdims=True)
        sel2 = idx == i2
        e = jnp.exp(m2 - m1)
        p1 = 1.0 / (1.0 + e)
        p2 = e / (1.0 + e)

        chosen = jnp.where(sel1, 1.0, jnp.where(sel2, 1.0, 0.0))
        before = _dot(chosen.astype(BF16), tri[...]) + carry[:, 0:1]
        r1 = jnp.sum(jnp.where(sel1, before, 0.0), axis=0, keepdims=True)
        r2 = jnp.sum(jnp.where(sel2, before, 0.0), axis=0, keepdims=True)
        carry[...] += jnp.sum(chosen, axis=1, keepdims=True)
        cnt_ref[...] = carry[...]

        rows = (i1.astype(F32), i2.astype(F32), r1, r2, p1, p2)
        route = jnp.zeros(logits.shape, F32)
        for k, row in enumerate(rows):
            route = jnp.where(idx == k, row, route)
        route_ref[...] = route


def _router(x2d, sh, sc, g, router_w, base_cnt, tile, rows_per_mod, h_rows, h_prev=None,
            row_off=0):
    n, d = x2d.shape
    ne = router_w.shape[-1]
    assert ne == ROUTE_ROWS and n % tile == 0 and row_off % tile == 0 and tile % LANES == 0
    assert base_cnt.shape == (ne, LANES)
    n_steps = n // tile
    tail = h_rows - (row_off + n)
    assert 0 <= tail < tile
    fill_tail = h_prev is None and tail > 0
    last = n_steps - 1
    clamp = (lambda i: jnp.minimum(i, last)) if fill_tail else (lambda i: i)
    per_row_mod = sh.shape[1] > 1
    if per_row_mod:
        mod_spec = pl.BlockSpec((1, tile, d), lambda i: (0, clamp(i), 0))
    else:
        assert rows_per_mod % tile == 0
        mod_spec = pl.BlockSpec((1, 1, d), lambda i: (clamp(i) // (rows_per_mod // tile), 0, 0))
    in_specs = [
        pl.BlockSpec((tile, d), lambda i: (clamp(i), 0)),
        mod_spec,
        mod_spec,
        _resident((1, d)),
        _resident((ne, d)),
        _resident((ne, LANES)),
    ]
    args = [x2d, sh, sc, g.reshape(1, d), router_w.T, base_cnt]
    n_in = len(args)
    aliases = {}
    body = functools.partial(_router_kernel, n_steps=n_steps)
    if h_prev is not None:
        assert h_prev.shape == (h_rows, d)
        in_specs.append(pl.BlockSpec(memory_space=pl.ANY))
        args.append(h_prev)
        aliases = {n_in: 0}

        def body(*refs):
            _router_kernel(*refs[:n_in], *refs[n_in + 1:], n_steps=n_steps)

    blk_off = row_off // tile
    return pl.pallas_call(
        body,
        grid=(n_steps + int(fill_tail),),
        in_specs=in_specs,
        out_specs=[
            pl.BlockSpec((tile, d), lambda i: (i + blk_off, 0)),
            pl.BlockSpec((ne, tile), lambda i: (0, clamp(i))),
            pl.BlockSpec((ne, LANES), lambda i: (0, 0)),
        ],
        out_shape=[
            jax.ShapeDtypeStruct((h_rows, d), F32),
            jax.ShapeDtypeStruct((ne, n), F32),
            jax.ShapeDtypeStruct((ne, LANES), F32),
        ],
        scratch_shapes=[pltpu.VMEM((tile, tile), BF16), pltpu.VMEM((ne, LANES), F32)],
        input_output_aliases=aliases,
        compiler_params=_params("arbitrary"),
        name="router",
    )(*args)


def _row_copy(src_hbm, row, dst, dst_row, sem):
    return pltpu.make_async_copy(src_hbm.at[pl.ds(row, 1), :], dst.at[pl.ds(dst_row, 1), :], sem)


def _row_copy_out(src, src_row, dst_hbm, row, sem):
    return pltpu.make_async_copy(src.at[pl.ds(src_row, 1), :], dst_hbm.at[pl.ds(row, 1), :], sem)


def _moe_grouped_kernel(te_ref, na_ref, src_ref, dst_ref, h_hbm, wg_ref, wu_ref, wd_ref, y_hbm,
                        xs, hbuf, obuf, gsem, ssem, fsem, *, tm, nc, rows_per_step, fill_row,
                        fill_blocks):
    i = pl.program_id(0)
    c = pl.program_id(1)
    n_tiles = pl.num_programs(0)
    n_active = na_ref[0]
    ahead = ROW_SLOTS - 1
    slot = i % ROW_SLOTS

    def request(tile, r):
        s = tile % ROW_SLOTS
        return _row_copy(h_hbm, src_ref[tile * tm + r], xs.at[s], r, gsem.at[s])

    def send(tile, r):
        s = (tile + ROW_SLOTS) % ROW_SLOTS
        return _row_copy_out(obuf.at[s], r, y_hbm, dst_ref[(tile + 1) * tm + r], ssem.at[s])

    @pl.when(jnp.logical_and(i == 0, c == 0))
    def _():
        last = ROW_SLOTS - 1
        obuf[last] = jnp.zeros((tm, obuf.shape[-1]), F32)
        fills = [pltpu.make_async_copy(obuf.at[last, pl.ds(0, FILL_ROWS), :],
                                       y_hbm.at[pl.ds(fill_row + q * FILL_ROWS, FILL_ROWS), :],
                                       fsem) for q in range(fill_blocks)]
        for f in fills:
            f.start()
        for f in fills:
            f.wait()

        def body(r, carry):
            for t in range(ahead):
                request(t, r).start()
            return carry

        lax.fori_loop(0, tm, body, 0)

    @pl.when(jnp.logical_and(i < n_active + ahead, c == 0))
    def _():
        pltpu.make_async_copy(h_hbm.at[pl.ds(0, tm), :], xs.at[slot], gsem.at[slot]).wait()

    @pl.when(jnp.logical_and(jnp.logical_and(i >= ahead, i < n_active + ROW_SLOTS), c == 0))
    def _():
        pltpu.make_async_copy(obuf.at[slot], y_hbm.at[pl.ds(0, tm), :], ssem.at[slot]).wait()

    @pl.when(jnp.logical_and(i < n_active, c == 0))
    def _():
        obuf[slot] = jnp.zeros((tm, obuf.shape[-1]), F32)
        hbuf[...] = xs[slot].astype(BF16)

    @pl.when(jnp.logical_and(i == n_active, c == 0))
    def _():
        def body(r, carry):
            send(i - 1, r).start()
            return carry

        lax.fori_loop(0, tm, body, 0)

    @pl.when(jnp.logical_and(i < n_active, c == 0))
    def _():
        for r in range(rows_per_step * nc, tm):
            request(i + ahead, r).start()
            send(i - 1, r).start()

    live_rows = te_ref[n_tiles + i]
    for m in range(tm, 0, -MXU_ROWS):
        @pl.when(jnp.logical_and(i < n_active, live_rows == m))
        def _(m=m):
            for k in range(rows_per_step):
                r = c * rows_per_step + k
                request(i + ahead, r).start()
                send(i - 1, r).start()

            rows = slice(None) if m == tm else slice(0, m)
            h = hbuf[rows, :]
            a = _dot(h, wg_ref[0].astype(BF16))
            b = _dot(h, wu_ref[0].astype(BF16))
            obuf[slot, rows, :] += _dot(_silu_mul(a, b).astype(BF16), wd_ref[0].astype(BF16))


def _moe_grouped(h_all, tile_info, n_active, src, dst, wg, wu, wd, tm, fc, y_rows, fill_row,
                 fill_rows):
    ne, d, dff = wg.shape
    n_tiles = tile_info.shape[0] // 2
    nc = dff // fc
    assert dff % fc == 0 and tm % MXU_ROWS == 0
    assert src.shape[0] == n_tiles * tm and dst.shape[0] == n_tiles * tm
    assert fill_rows % FILL_ROWS == 0 and tm >= FILL_ROWS
    rows_per_step = tm // nc

    def w_col(i, c, te, na, src_, dst_):
        return (te[i], 0, jnp.where(i < na[0], c, nc - 1))

    def w_row(i, c, te, na, src_, dst_):
        return (te[i], jnp.where(i < na[0], c, nc - 1), 0)

    grid_spec = pltpu.PrefetchScalarGridSpec(
        num_scalar_prefetch=4,
        grid=(n_tiles, nc),
        in_specs=[
            pl.BlockSpec(memory_space=pl.ANY),
            pl.BlockSpec((1, d, fc), w_col),
            pl.BlockSpec((1, d, fc), w_col),
            pl.BlockSpec((1, fc, d), w_row),
        ],
        out_specs=pl.BlockSpec(memory_space=pl.ANY),
        scratch_shapes=[
            pltpu.VMEM((ROW_SLOTS, tm, d), F32),
            pltpu.VMEM((tm, d), BF16),
            pltpu.VMEM((ROW_SLOTS, tm, d), F32),
            pltpu.SemaphoreType.DMA((ROW_SLOTS,)),
            pltpu.SemaphoreType.DMA((ROW_SLOTS,)),
            pltpu.SemaphoreType.DMA,
        ],
    )
    return pl.pallas_call(
        functools.partial(_moe_grouped_kernel, tm=tm, nc=nc, rows_per_step=rows_per_step,
                          fill_row=fill_row, fill_blocks=fill_rows // FILL_ROWS),
        grid_spec=grid_spec,
        out_shape=jax.ShapeDtypeStruct((y_rows, d), F32),
        compiler_params=_params("arbitrary", "arbitrary"),
        name="moe_grouped",
    )(tile_info, n_active, src, dst, h_all, wg, wu, wd)


def _combine_kernel(x_ref, gt_ref, route_ref, gf_ref, y0_ref, y1_ref, o_ref):
    route = route_ref[...]
    pad = jnp.zeros((LANES - route.shape[0], route.shape[1]), F32)
    cols = jnp.concatenate([route, pad], axis=0).T
    f = (cols[:, ROW_PROB:ROW_PROB + 1] * y0_ref[...]
         + cols[:, ROW_PROB + 1:ROW_PROB + 2] * y1_ref[...])
    o_ref[...] = _rms(x_ref[...] + gt_ref[0] * f, gf_ref[...])


def _combine(x2d, gt, route, g_final, y_tok, tile, rows_per_mod, tok_off, plane_rows):
    n, d = x2d.shape
    assert n % tile == 0 and tok_off % tile == 0 and plane_rows % tile == 0
    per_row_mod = gt.shape[1] > 1
    if per_row_mod:
        mod_spec = pl.BlockSpec((1, tile, d), lambda j: (0, j, 0))
    else:
        assert rows_per_mod % tile == 0
        mod_spec = pl.BlockSpec((1, 1, d), lambda j: (j // (rows_per_mod // tile), 0, 0))
    first = tok_off // tile
    second = (plane_rows + tok_off) // tile
    return pl.pallas_call(
        _combine_kernel,
        grid=(n // tile,),
        in_specs=[
            pl.BlockSpec((tile, d), lambda j: (j, 0)),
            mod_spec,
            pl.BlockSpec((ROUTE_ROWS, tile), lambda j: (0, j)),
            pl.BlockSpec((1, d), lambda j: (0, 0)),
            pl.BlockSpec((tile, d), lambda j: (first + j, 0)),
            pl.BlockSpec((tile, d), lambda j: (second + j, 0)),
        ],
        out_specs=pl.BlockSpec((tile, d), lambda j: (j, 0)),
        out_shape=jax.ShapeDtypeStruct((n, d), F32),
        compiler_params=_params("arbitrary"),
        name="combine",
    )(x2d, gt, route, g_final.reshape(1, d), y_tok, y_tok)


def _invert_kernel(perm_ref, inv_ref):
    def place(a, carry):
        inv_ref[perm_ref[a]] = a
        return carry

    lax.fori_loop(0, perm_ref.shape[0], place, 0, unroll=32)


def _invert(perm):
    assert perm.shape[0] % 32 == 0
    smem = pl.BlockSpec(memory_space=pltpu.SMEM)
    return pl.pallas_call(
        _invert_kernel,
        in_specs=[smem],
        out_specs=smem,
        out_shape=jax.ShapeDtypeStruct(perm.shape, I32),
        name="invert",
    )(perm)


def _routing_tables(route_all, counts, tm, n_tiles, plane_rows):
    ne = counts.shape[0]
    n = route_all.shape[1]
    e_idx = route_all[ROW_EXPERT:ROW_EXPERT + TOP_K].astype(I32)
    rank = route_all[ROW_RANK:ROW_RANK + TOP_K].astype(I32)
    padded = (counts + tm - 1) // tm * tm
    ends = jnp.cumsum(padded)
    offs = ends - padded
    onehot = e_idx[..., None] == jnp.arange(ne, dtype=I32)
    pos = jnp.sum(jnp.where(onehot, offs, 0), axis=-1) + rank
    n_active = ends[-1] // tm
    starts = jnp.arange(n_tiles, dtype=I32) * tm
    tile_expert = jnp.sum(starts[:, None] >= ends[None, :], axis=1).astype(I32)
    last_expert = jnp.sum((n_active - 1) * tm >= ends).astype(I32)
    tile_expert = jnp.minimum(tile_expert, last_expert)
    n_rows = n_tiles * tm
    n_real = n * TOP_K
    pad_ends = jnp.cumsum(padded - counts)
    j = jnp.arange(n_rows - n_real, dtype=I32)[:, None]
    in_expert = jnp.logical_and(j >= pad_ends - (padded - counts), j < pad_ends)
    pad_pos = jnp.where(j[:, 0] < pad_ends[-1],
                        jnp.sum(jnp.where(in_expert, offs + counts + j - pad_ends + padded - counts, 0),
                                axis=1),
                        ends[-1] + j[:, 0] - pad_ends[-1])
    inv = _invert(jnp.concatenate([pos.reshape(-1), pad_pos]))
    is_real = inv < n_real
    tok = inv % n
    choice = inv // n
    spare = n + jnp.minimum(inv - n_real, ne * tm - 1)
    src = jnp.where(is_real, tok, 0)
    dst = jnp.where(is_real, choice * plane_rows + tok, spare)
    lead = n + ne * tm + jnp.arange(tm, dtype=I32)
    dst = jnp.concatenate([lead, dst])[:n_rows]
    group_end = jnp.sum(jnp.where(tile_expert[:, None] == jnp.arange(ne, dtype=I32),
                                  offs + counts, 0), axis=1)
    live = jnp.clip(group_end - starts, 1, tm)
    live = (live + MXU_ROWS - 1) // MXU_ROWS * MXU_ROWS
    tile_info = jnp.concatenate([tile_expert, live.astype(I32)])
    return src, dst, tile_info, n_active.astype(I32).reshape(1)


def _moe_layer(xp, xs, mod_p, mod_s, g, router_w, wg, wu, wd, g_final):
    nb, seq, d = xp.shape
    ns = xs.shape[0]
    ne = router_w.shape[-1]
    n_all = nb * seq + ns
    sh_p, sc_p, gt_p = mod_p
    sh_s, sc_s, gt_s = mod_s
    xp2 = xp.reshape(nb * seq, d)

    zero_cnt = jnp.zeros((ne, LANES), F32)
    h_all, route_p, cnt_p = _router(xp2, sh_p, sc_p, g, router_w, zero_cnt, ROUTER_TILE, seq,
                                    n_all)
    h_all, route_s, cnt_s = _router(xs, sh_s, sc_s, g, router_w, cnt_p, ns, ns, n_all,
                                    h_prev=h_all, row_off=nb * seq)

    tm = MOE_ROW_TILE
    n_tiles = (n_all * TOP_K) // tm + ne + ROW_SLOTS
    plane_rows = -(-(n_all + (ne + 1) * tm) // COMBINE_TILE) * COMBINE_TILE
    route_all = jnp.concatenate([route_p, route_s], axis=1)
    src, dst, tile_info, n_active = _routing_tables(
        route_all, cnt_s[:, 0].astype(I32), tm, n_tiles, plane_rows)

    y_tok = _moe_grouped(h_all, tile_info, n_active, src, dst, wg, wu, wd, tm, MOE_FF_CHUNK,
                         plane_rows + n_all, n_all, plane_rows - n_all)
    yp = _combine(xp2, gt_p, route_p, g_final, y_tok, COMBINE_TILE, seq, 0, plane_rows)
    ys = _combine(xs, gt_s, route_s, g_final, y_tok, ns, ns, nb * seq, plane_rows)
    return yp.reshape(nb, seq, d), ys


def kernel(x_prompt, x_sample, c_prompt, c_sample, state_conv, state_pool, w_ada, b_ada, g_mix,
           g_ffn, w_in, conv_w, pool_w, pool_scale, g_conv_out, g_pool_out, w_out, dense_w_gate,
           dense_w_up, dense_w_down, router_w, moe_w_gate, moe_w_up, moe_w_down, g_final):
    depth = w_ada.shape[0]
    nb, seq, d = x_prompt.shape
    ns = x_sample.shape[0]
    assert x_sample.shape[1] == 1 and depth == 2
    assert sum(DENSE_FF_CHUNKS) == dense_w_gate.shape[-1]

    mod = _ada(jnp.concatenate([c_prompt, c_sample], axis=0), w_ada, b_ada)
    mod = mod.reshape(depth, nb + ns, 6, d)
    mod_p = [jnp.transpose(mod[i, :nb], (1, 0, 2))[:, :, None, :] for i in range(depth)]
    mod_s = [jnp.transpose(mod[i, nb:], (1, 0, 2))[:, None, :, :] for i in range(depth)]

    w_in_b = w_in.astype(BF16)
    w_out_b = w_out.astype(BF16)
    pool_w_b = pool_w.astype(BF16)
    dense_b = [w.astype(BF16) for w in (dense_w_gate, dense_w_up, dense_w_down)]
    cb_t = jnp.transpose(state_conv, (0, 2, 1, 3))
    pb_t = jnp.transpose(state_pool, (0, 2, 1, 3))

    xp = x_prompt
    xs = x_sample.reshape(ns, d)
    conv_p, pool_p, conv_s, pool_s = [], [], [], []
    for i in range(depth):
        sh1, sc1, gt1, sh2, sc2, gt2 = mod_p[i]
        mix_w = (g_mix[i], w_in_b[i], conv_w[i], pool_w_b[i], pool_scale[i], g_conv_out[i],
                 g_pool_out[i], w_out_b[i])
        xp, cs, ps = _mix_prompt(xp, sh1, sc1, gt1, *mix_w, tile=MIX_TILE, sub=MIX_ROW_BLOCK)
        conv_p.append(cs)
        pool_p.append(ps)
        s1, c1, t1, s2, c2, t2 = mod_s[i]
        xs, v_new, u_new = _mix_sample(xs, s1[0], c1[0], t1[0], *mix_w, cb_t[i], pb_t[i])
        conv_s.append(jnp.concatenate([state_conv[i][:, 1:], v_new[:, None, :]], axis=1))
        pool_s.append(jnp.concatenate([state_pool[i][:, 1:], u_new[:, None, :]], axis=1))
        j = i // 2
        if i % 2 == 0:
            wg, wu, wd = (w[j] for w in dense_b)
            xp = _ffn(xp, sh2, sc2, gt2, g_ffn[i], wg, wu, wd, PROMPT_TILE, DENSE_FF_CHUNKS)
            xs = _ffn(xs[None], s2, c2, t2, g_ffn[i], wg, wu, wd, ns, DENSE_FF_CHUNKS)[0]
        else:
            xp, xs = _moe_layer(xp, xs, (sh2, sc2, gt2), (s2, c2, t2), g_ffn[i], router_w[j],
                                moe_w_gate[j], moe_w_up[j], moe_w_down[j], g_final)

    return (xp, xs.reshape(ns, 1, d), jnp.stack(conv_p), jnp.stack(pool_p),
            jnp.stack(conv_s), jnp.stack(pool_s))
```

```python
import functools

import jax
import jax.numpy as jnp
from jax import lax
from jax.experimental import pallas as pl
from jax.experimental.pallas import tpu as pltpu

F32 = jnp.float32
BF16 = jnp.bfloat16
I32 = jnp.int32

EPS = 1e-6
CONV_K = 3
POOL_WINDOWS = (2, 4, 8, 16)
POOL_HIST = max(POOL_WINDOWS) - 1
TOP_K = 2

CONV_PAD = 8
POOL_PAD = 16

LANES = 128

ROUTE_ROWS = 8
ROW_EXPERT, ROW_RANK, ROW_PROB = 0, 2, 4

VMEM_LIMIT_BYTES = 56 * 1024 * 1024

PROMPT_TILE = 512
MIX_TILE = 2048
MIX_ROW_BLOCK = 512
ROUTER_TILE = 1024
COMBINE_TILE = 512
MOE_ROW_TILE = 1024
MOE_FF_CHUNK = 512
FILL_ROWS = 128
ROW_SLOTS = 3
DENSE_FF_CHUNKS = (768, 768, 768, 512)


def _params(*sem):
    return pltpu.CompilerParams(dimension_semantics=sem, vmem_limit_bytes=VMEM_LIMIT_BYTES)


def _resident(shape):
    nd = len(shape)
    return pl.BlockSpec(shape, lambda *_: (0,) * nd, pipeline_mode=pl.Buffered(1))


def _rms(x, g):
    ms = jnp.mean(x * x, axis=-1, keepdims=True)
    return x * lax.rsqrt(ms + EPS) * g


def _mod_norm(x, g, sc, sh):
    ms = jnp.mean(x * x, axis=-1, keepdims=True)
    return x * lax.rsqrt(ms + EPS) * (g * (1.0 + sc)) + sh


def _dot(a, b):
    return jnp.dot(a, b, preferred_element_type=F32)


def _dot_nt(a, b):
    return lax.dot_general(a, b, (((1,), (1,)), ((), ())), preferred_element_type=F32)


def _silu_mul(a, b):
    return a * jax.nn.sigmoid(a) * b


def _ada_kernel(c_ref, w_ref, b_ref, o_ref):
    c = c_ref[...]
    a = (c * jax.nn.sigmoid(c)).astype(BF16)
    o_ref[0] = _dot(a, w_ref[0].astype(BF16)) + b_ref[0]


def _ada(c_all, w_ada, b_ada):
    depth, d, n = w_ada.shape
    m = c_all.shape[0]
    tn = 1024
    return pl.pallas_call(
        _ada_kernel,
        grid=(depth, n // tn),
        in_specs=[
            pl.BlockSpec((m, d), lambda i, j: (0, 0)),
            pl.BlockSpec((1, d, tn), lambda i, j: (i, 0, j)),
            pl.BlockSpec((1, 1, tn), lambda i, j: (i, 0, j)),
        ],
        out_specs=pl.BlockSpec((1, m, tn), lambda i, j: (i, 0, j)),
        out_shape=jax.ShapeDtypeStruct((depth, m, n), F32),
        compiler_params=_params("arbitrary", "arbitrary"),
        name="ada",
    )(c_all, w_ada, b_ada.reshape(depth, 1, n))


def _mix_tail(x, gt, bg, y, d_groups, poolw_ref, pscale_ref, gco_ref, gpo_ref, wout_ref):
    cw = gco_ref.shape[-1]
    ya = bg * y
    yb = jnp.concatenate(
        [_dot(d.astype(BF16), poolw_ref[g]) for g, d in enumerate(d_groups)], axis=-1
    ) * pscale_ref[...]
    ma = _rms(ya, gco_ref[...]).astype(BF16)
    mb = _rms(yb, gpo_ref[...]).astype(BF16)
    o = _dot(ma, wout_ref[0:cw, :]) + _dot(mb, wout_ref[cw:, :])
    return x + gt * o


def _mix_prompt_kernel(x_ref, sh_ref, sc_ref, gt_ref, g_ref, win_ref, convw_ref, poolw_ref,
                       pscale_ref, gco_ref, gpo_ref, wout_ref,
                       xo_ref, cs_ref, ps_ref, vbuf, ubuf, *, tile, sub):
    l = pl.program_id(1)
    cw = convw_ref.shape[-1]
    pg = poolw_ref.shape[-1]

    @pl.when(l == 0)
    def _():
        vbuf[0:CONV_PAD, :] = jnp.zeros((CONV_PAD, cw), F32)
        ubuf[0:POOL_PAD, :] = jnp.zeros((POOL_PAD, ubuf.shape[-1]), F32)

    gates = []
    for lo in range(0, tile, sub):
        x = x_ref[0, lo:lo + sub, :]
        h = _mod_norm(x, g_ref[...], sc_ref[0], sh_ref[0]).astype(BF16)
        p = _dot(h, win_ref[...])
        gates.append(p[:, 0:cw])
        vbuf[CONV_PAD + lo:CONV_PAD + lo + sub, :] = p[:, cw:2 * cw] * p[:, 2 * cw:3 * cw]
        ubuf[POOL_PAD + lo:POOL_PAD + lo + sub, :] = p[:, 3 * cw:]

    w = convw_ref[...]
    for bg, lo in zip(gates, range(0, tile, sub)):
        y = w[CONV_K - 1:CONV_K] * vbuf[CONV_PAD + lo:CONV_PAD + lo + sub, :]
        for k in range(1, CONV_K):
            y = y + w[CONV_K - 1 - k:CONV_K - k] * vbuf[CONV_PAD + lo - k:CONV_PAD + lo - k + sub, :]

        pos = l * tile + lo + lax.broadcasted_iota(I32, (sub, 1), 0)
        d_groups = []
        for g, win in enumerate(POOL_WINDOWS):
            ug = ubuf[POOL_PAD + lo:POOL_PAD + lo + sub, g * pg:(g + 1) * pg]
            acc = ug
            for k in range(1, win):
                acc = acc + ubuf[POOL_PAD + lo - k:POOL_PAD + lo - k + sub, g * pg:(g + 1) * pg]
            cnt = jnp.minimum(pos + 1, win).astype(F32)
            d_groups.append(acc / cnt - ug)

        xo_ref[0, lo:lo + sub, :] = _mix_tail(x_ref[0, lo:lo + sub, :], gt_ref[0], bg, y, d_groups,
                                              poolw_ref, pscale_ref, gco_ref, gpo_ref, wout_ref)

    cs_ref[0] = vbuf[CONV_PAD + tile - (CONV_K - 1):CONV_PAD + tile, :]
    ps_ref[0] = ubuf[POOL_PAD + tile - POOL_HIST:POOL_PAD + tile, :]
    vbuf[0:CONV_PAD, :] = vbuf[tile:tile + CONV_PAD, :]
    ubuf[0:POOL_PAD, :] = ubuf[tile:tile + POOL_PAD, :]


def _mix_prompt(x, sh, sc, gt, g, w_in, conv_w, pool_w, pool_scale, g_co, g_po, w_out, tile, sub):
    b, seq, d = x.shape
    assert seq % tile == 0 and tile % sub == 0 and sub >= POOL_PAD
    cw = conv_w.shape[-1]
    pw = pool_scale.shape[-1]
    row = lambda bi, li: (bi, 0, 0)
    return pl.pallas_call(
        functools.partial(_mix_prompt_kernel, tile=tile, sub=sub),
        grid=(b, seq // tile),
        in_specs=[
            pl.BlockSpec((1, tile, d), lambda bi, li: (bi, li, 0)),
            pl.BlockSpec((1, 1, d), row),
            pl.BlockSpec((1, 1, d), row),
            pl.BlockSpec((1, 1, d), row),
            _resident((1, d)),
            _resident(w_in.shape),
            _resident(conv_w.shape),
            _resident(pool_w.shape),
            _resident((1, pw)),
            _resident((1, cw)),
            _resident((1, pw)),
            _resident(w_out.shape),
        ],
        out_specs=[
            pl.BlockSpec((1, tile, d), lambda bi, li: (bi, li, 0)),
            pl.BlockSpec((1, CONV_K - 1, cw), row),
            pl.BlockSpec((1, POOL_HIST, pw), row),
        ],
        out_shape=[
            jax.ShapeDtypeStruct((b, seq, d), F32),
            jax.ShapeDtypeStruct((b, CONV_K - 1, cw), F32),
            jax.ShapeDtypeStruct((b, POOL_HIST, pw), F32),
        ],
        scratch_shapes=[
            pltpu.VMEM((CONV_PAD + tile, cw), F32),
            pltpu.VMEM((POOL_PAD + tile, pw), F32),
        ],
        compiler_params=_params("arbitrary", "arbitrary"),
        name="mix_prompt",
    )(x, sh, sc, gt, g.reshape(1, d), w_in, conv_w, pool_w, pool_scale.reshape(1, pw),
      g_co.reshape(1, cw), g_po.reshape(1, pw), w_out)


def _mix_sample_kernel(x_ref, sh_ref, sc_ref, gt_ref, g_ref, win_ref, convw_ref, poolw_ref,
                       pscale_ref, gco_ref, gpo_ref, wout_ref, cb_ref, pb_ref,
                       xo_ref, v_ref, u_ref):
    cw = convw_ref.shape[-1]
    pg = poolw_ref.shape[-1]
    x = x_ref[...]
    h = _mod_norm(x, g_ref[...], sc_ref[...], sh_ref[...]).astype(BF16)
    p = _dot(h, win_ref[...])
    bg = p[:, 0:cw]
    v = p[:, cw:2 * cw] * p[:, 2 * cw:3 * cw]
    u = p[:, 3 * cw:]
    v_ref[...] = v
    u_ref[...] = u

    w = convw_ref[...]
    y = w[CONV_K - 1:CONV_K] * v
    for k in range(1, CONV_K):
        y = y + w[CONV_K - 1 - k:CONV_K - k] * cb_ref[CONV_K - 1 - k]

    d_groups = []
    for g, win in enumerate(POOL_WINDOWS):
        ug = u[:, g * pg:(g + 1) * pg]
        acc = ug
        for k in range(1, win):
            acc = acc + pb_ref[POOL_HIST - k, :, g * pg:(g + 1) * pg]
        d_groups.append(acc / float(win) - ug)

    xo_ref[...] = _mix_tail(x, gt_ref[...], bg, y, d_groups, poolw_ref, pscale_ref,
                            gco_ref, gpo_ref, wout_ref)


def _mix_sample(x, sh, sc, gt, g, w_in, conv_w, pool_w, pool_scale, g_co, g_po, w_out, cb, pb):
    n, d = x.shape
    cw = conv_w.shape[-1]
    pw = pool_scale.shape[-1]
    return pl.pallas_call(
        _mix_sample_kernel,
        out_shape=[
            jax.ShapeDtypeStruct((n, d), F32),
            jax.ShapeDtypeStruct((n, cw), F32),
            jax.ShapeDtypeStruct((n, pw), F32),
        ],
        compiler_params=pltpu.CompilerParams(vmem_limit_bytes=VMEM_LIMIT_BYTES),
        name="mix_sample",
    )(x, sh, sc, gt, g.reshape(1, d), w_in, conv_w, pool_w, pool_scale.reshape(1, pw),
      g_co.reshape(1, cw), g_po.reshape(1, pw), w_out, cb, pb)


def _ffn_kernel(x_ref, sh_ref, sc_ref, gt_ref, g_ref, wg_ref, wu_ref, wd_ref, o_ref, *, chunks):
    x = x_ref[0]
    h = _mod_norm(x, g_ref[...], sc_ref[0], sh_ref[0]).astype(BF16)
    acc = None
    lo = 0
    for fc in chunks:
        a = _dot(h, wg_ref[:, lo:lo + fc])
        b = _dot(h, wu_ref[:, lo:lo + fc])
        part = _dot(_silu_mul(a, b).astype(BF16), wd_ref[lo:lo + fc, :])
        acc = part if acc is None else acc + part
        lo += fc
    o_ref[0] = x + gt_ref[0] * acc


def _ffn(x, sh, sc, gt, g, wg, wu, wd, tile, chunks):
    b, seq, d = x.shape
    tm = sh.shape[1]
    tmod = tile if tm > 1 else 1
    mod_map = (lambda bi, li: (bi, li, 0)) if tm > 1 else (lambda bi, li: (bi, 0, 0))
    return pl.pallas_call(
        functools.partial(_ffn_kernel, chunks=chunks),
        grid=(b, seq // tile),
        in_specs=[
            pl.BlockSpec((1, tile, d), lambda bi, li: (bi, li, 0)),
            pl.BlockSpec((1, tmod, d), mod_map),
            pl.BlockSpec((1, tmod, d), mod_map),
            pl.BlockSpec((1, tmod, d), mod_map),
            _resident((1, d)),
            _resident(wg.shape),
            _resident(wu.shape),
            _resident(wd.shape),
        ],
        out_specs=pl.BlockSpec((1, tile, d), lambda bi, li: (bi, li, 0)),
        out_shape=jax.ShapeDtypeStruct((b, seq, d), F32),
        compiler_params=_params("arbitrary", "arbitrary"),
        name="ffn",
    )(x, sh, sc, gt, g.reshape(1, d), wg, wu, wd)


def _split_bf16(a):
    hi = a.astype(BF16)
    return hi, (a - hi.astype(F32)).astype(BF16)


def _router_kernel(x_ref, sh_ref, sc_ref, g_ref, rw_ref, base_ref, h_ref, route_ref, cnt_ref,
                   tri, carry, *, n_steps):
    t = x_ref.shape[0]
    step = pl.program_id(0)

    @pl.when(step == 0)
    def _():
        r = lax.broadcasted_iota(I32, (t, t), 0)
        c = lax.broadcasted_iota(I32, (t, t), 1)
        tri[...] = jnp.where(r < c, 1.0, 0.0).astype(BF16)
        carry[...] = base_ref[...]

    @pl.when(step >= n_steps)
    def _():
        h_ref[...] = jnp.zeros_like(h_ref)

    @pl.when(step < n_steps)
    def _():
        h = _mod_norm(x_ref[...], g_ref[...], sc_ref[0], sh_ref[0])
        h_ref[...] = h
        h_hi, h_lo = _split_bf16(h)
        rw_hi, rw_lo = _split_bf16(rw_ref[...])
        logits = _dot_nt(rw_hi, h_hi) + _dot_nt(rw_hi, h_lo) + _dot_nt(rw_lo, h_hi)

        ne = logits.shape[0]
        idx = lax.broadcasted_iota(I32, logits.shape, 0)
        m1 = jnp.max(logits, axis=0, keepdims=True)
        i1 = jnp.min(jnp.where(logits == m1, idx, ne), axis=0, keepdims=True)
        sel1 = idx == i1
        rest = jnp.where(sel1, -jnp.inf, logits)
        m2 = jnp.max(rest, axis=0, keepdims=True)
        i2 = jnp.min(jnp.where(rest == m2, idx, ne), axis=0, keepdims=True)
        sel2 = idx == i2
        e = jnp.exp(m2 - m1)
        p1 = 1.0 / (1.0 + e)
        p2 = e / (1.0 + e)

        chosen = jnp.where(sel1, 1.0, jnp.where(sel2, 1.0, 0.0))
        before = _dot(chosen.astype(BF16), tri[...]) + carry[:, 0:1]
        r1 = jnp.sum(jnp.where(sel1, before, 0.0), axis=0, keepdims=True)
        r2 = jnp.sum(jnp.where(sel2, before, 0.0), axis=0, keepdims=True)
        carry[...] += jnp.sum(chosen, axis=1, keepdims=True)
        cnt_ref[...] = carry[...]

        rows = (i1.astype(F32), i2.astype(F32), r1, r2, p1, p2)
        route = jnp.zeros(logits.shape, F32)
        for k, row in enumerate(rows):
            route = jnp.where(idx == k, row, route)
        route_ref[...] = route


def _router(x2d, sh, sc, g, router_w, base_cnt, tile, rows_per_mod, h_rows, h_prev=None,
            row_off=0):
    n, d = x2d.shape
    ne = router_w.shape[-1]
    assert ne == ROUTE_ROWS and n % tile == 0 and row_off % tile == 0 and tile % LANES == 0
    assert base_cnt.shape == (ne, LANES)
    n_steps = n // tile
    tail = h_rows - (row_off + n)
    assert 0 <= tail < tile
    fill_tail = h_prev is None and tail > 0
    last = n_steps - 1
    clamp = (lambda i: jnp.minimum(i, last)) if fill_tail else (lambda i: i)
    per_row_mod = sh.shape[1] > 1
    if per_row_mod:
        mod_spec = pl.BlockSpec((1, tile, d), lambda i: (0, clamp(i), 0))
    else:
        assert rows_per_mod % tile == 0
        mod_spec = pl.BlockSpec((1, 1, d), lambda i: (clamp(i) // (rows_per_mod // tile), 0, 0))
    in_specs = [
        pl.BlockSpec((tile, d), lambda i: (clamp(i), 0)),
        mod_spec,
        mod_spec,
        _resident((1, d)),
        _resident((ne, d)),
        _resident((ne, LANES)),
    ]
    args = [x2d, sh, sc, g.reshape(1, d), router_w.T, base_cnt]
    n_in = len(args)
    aliases = {}
    body = functools.partial(_router_kernel, n_steps=n_steps)
    if h_prev is not None:
        assert h_prev.shape == (h_rows, d)
        in_specs.append(pl.BlockSpec(memory_space=pl.ANY))
        args.append(h_prev)
        aliases = {n_in: 0}

        def body(*refs):
            _router_kernel(*refs[:n_in], *refs[n_in + 1:], n_steps=n_steps)

    blk_off = row_off // tile
    return pl.pallas_call(
        body,
        grid=(n_steps + int(fill_tail),),
        in_specs=in_specs,
        out_specs=[
            pl.BlockSpec((tile, d), lambda i: (i + blk_off, 0)),
            pl.BlockSpec((ne, tile), lambda i: (0, clamp(i))),
            pl.BlockSpec((ne, LANES), lambda i: (0, 0)),
        ],
        out_shape=[
            jax.ShapeDtypeStruct((h_rows, d), F32),
            jax.ShapeDtypeStruct((ne, n), F32),
            jax.ShapeDtypeStruct((ne, LANES), F32),
        ],
        scratch_shapes=[pltpu.VMEM((tile, tile), BF16), pltpu.VMEM((ne, LANES), F32)],
        input_output_aliases=aliases,
        compiler_params=_params("arbitrary"),
        name="router",
    )(*args)


def _row_copy(src_hbm, row, dst, dst_row, sem):
    return pltpu.make_async_copy(src_hbm.at[pl.ds(row, 1), :], dst.at[pl.ds(dst_row, 1), :], sem)


def _row_copy_out(src, src_row, dst_hbm, row, sem):
    return pltpu.make_async_copy(src.at[pl.ds(src_row, 1), :], dst_hbm.at[pl.ds(row, 1), :], sem)


def _moe_grouped_kernel(te_ref, na_ref, src_ref, dst_ref, h_hbm, wg_ref, wu_ref, wd_ref, y_hbm,
                        xs, hbuf, obuf, gsem, ssem, fsem, *, tm, nc, rows_per_step, fill_row,
                        fill_blocks):
    del te_ref
    i = pl.program_id(0)
    c = pl.program_id(1)
    n_active = na_ref[0]
    ahead = ROW_SLOTS - 1
    slot = i % ROW_SLOTS

    def request(tile, r):
        s = tile % ROW_SLOTS
        return _row_copy(h_hbm, src_ref[tile * tm + r], xs.at[s], r, gsem.at[s])

    def send(tile, r):
        s = (tile + ROW_SLOTS) % ROW_SLOTS
        return _row_copy_out(obuf.at[s], r, y_hbm, dst_ref[(tile + 1) * tm + r], ssem.at[s])

    @pl.when(c == 0)
    def _():
        @pl.when(i == 0)
        def _():
            last = ROW_SLOTS - 1
            obuf[last] = jnp.zeros((tm, obuf.shape[-1]), F32)
            fills = [pltpu.make_async_copy(obuf.at[last, pl.ds(0, FILL_ROWS), :],
                                           y_hbm.at[pl.ds(fill_row + q * FILL_ROWS, FILL_ROWS), :],
                                           fsem) for q in range(fill_blocks)]
            for f in fills:
                f.start()
            for f in fills:
                f.wait()

            def body(r, carry):
                for t in range(ahead):
                    request(t, r).start()
                return carry

            lax.fori_loop(0, tm, body, 0)

        @pl.when(i < n_active + ahead)
        def _():
            pltpu.make_async_copy(h_hbm.at[pl.ds(0, tm), :], xs.at[slot], gsem.at[slot]).wait()

        @pl.when(jnp.logical_and(i >= ahead, i < n_active + ROW_SLOTS))
        def _():
            pltpu.make_async_copy(obuf.at[slot], y_hbm.at[pl.ds(0, tm), :], ssem.at[slot]).wait()

        @pl.when(i < n_active)
        def _():
            obuf[slot] = jnp.zeros((tm, obuf.shape[-1]), F32)
            hbuf[...] = xs[slot].astype(BF16)
            for r in range(rows_per_step * nc, tm):
                request(i + ahead, r).start()
                send(i - 1, r).start()

        @pl.when(i == n_active)
        def _():
            def body(r, carry):
                send(i - 1, r).start()
                return carry

            lax.fori_loop(0, tm, body, 0)

    @pl.when(i < n_active)
    def _():
        for k in range(rows_per_step):
            r = c * rows_per_step + k
            request(i + ahead, r).start()
            send(i - 1, r).start()

        h = hbuf[...]
        a = _dot(h, wg_ref[0].astype(BF16))
        b = _dot(h, wu_ref[0].astype(BF16))
        obuf[slot] += _dot(_silu_mul(a, b).astype(BF16), wd_ref[0].astype(BF16))


def _moe_grouped(h_all, tile_expert, n_active, src, dst, wg, wu, wd, tm, fc, y_rows, fill_row,
                 fill_rows):
    ne, d, dff = wg.shape
    n_tiles = tile_expert.shape[0]
    nc = dff // fc
    assert dff % fc == 0
    assert src.shape[0] == n_tiles * tm and dst.shape[0] == n_tiles * tm
    assert fill_rows % FILL_ROWS == 0 and tm >= FILL_ROWS
    rows_per_step = tm // nc

    def w_col(i, c, te, na, src_, dst_):
        return (te[i], 0, jnp.where(i < na[0], c, nc - 1))

    def w_row(i, c, te, na, src_, dst_):
        return (te[i], jnp.where(i < na[0], c, nc - 1), 0)

    grid_spec = pltpu.PrefetchScalarGridSpec(
        num_scalar_prefetch=4,
        grid=(n_tiles, nc),
        in_specs=[
            pl.BlockSpec(memory_space=pl.ANY),
            pl.BlockSpec((1, d, fc), w_col),
            pl.BlockSpec((1, d, fc), w_col),
            pl.BlockSpec((1, fc, d), w_row),
        ],
        out_specs=pl.BlockSpec(memory_space=pl.ANY),
        scratch_shapes=[
            pltpu.VMEM((ROW_SLOTS, tm, d), F32),
            pltpu.VMEM((tm, d), BF16),
            pltpu.VMEM((ROW_SLOTS, tm, d), F32),
            pltpu.SemaphoreType.DMA((ROW_SLOTS,)),
            pltpu.SemaphoreType.DMA((ROW_SLOTS,)),
            pltpu.SemaphoreType.DMA,
        ],
    )
    return pl.pallas_call(
        functools.partial(_moe_grouped_kernel, tm=tm, nc=nc, rows_per_step=rows_per_step,
                          fill_row=fill_row, fill_blocks=fill_rows // FILL_ROWS),
        grid_spec=grid_spec,
        out_shape=jax.ShapeDtypeStruct((y_rows, d), F32),
        compiler_params=_params("arbitrary", "arbitrary"),
        name="moe_grouped",
    )(tile_expert, n_active, src, dst, h_all, wg, wu, wd)


def _combine_kernel(x_ref, gt_ref, route_ref, gf_ref, y0_ref, y1_ref, o_ref):
    route = route_ref[...]
    pad = jnp.zeros((LANES - route.shape[0], route.shape[1]), F32)
    cols = jnp.concatenate([route, pad], axis=0).T
    f = (cols[:, ROW_PROB:ROW_PROB + 1] * y0_ref[...]
         + cols[:, ROW_PROB + 1:ROW_PROB + 2] * y1_ref[...])
    o_ref[...] = _rms(x_ref[...] + gt_ref[0] * f, gf_ref[...])


def _combine(x2d, gt, route, g_final, y_tok, tile, rows_per_mod, tok_off, plane_rows):
    n, d = x2d.shape
    assert n % tile == 0 and tok_off % tile == 0 and plane_rows % tile == 0
    per_row_mod = gt.shape[1] > 1
    if per_row_mod:
        mod_spec = pl.BlockSpec((1, tile, d), lambda j: (0, j, 0))
    else:
        assert rows_per_mod % tile == 0
        mod_spec = pl.BlockSpec((1, 1, d), lambda j: (j // (rows_per_mod // tile), 0, 0))
    first = tok_off // tile
    second = (plane_rows + tok_off) // tile
    return pl.pallas_call(
        _combine_kernel,
        grid=(n // tile,),
        in_specs=[
            pl.BlockSpec((tile, d), lambda j: (j, 0)),
            mod_spec,
            pl.BlockSpec((ROUTE_ROWS, tile), lambda j: (0, j)),
            pl.BlockSpec((1, d), lambda j: (0, 0)),
            pl.BlockSpec((tile, d), lambda j: (first + j, 0)),
            pl.BlockSpec((tile, d), lambda j: (second + j, 0)),
        ],
        out_specs=pl.BlockSpec((tile, d), lambda j: (j, 0)),
        out_shape=jax.ShapeDtypeStruct((n, d), F32),
        compiler_params=_params("arbitrary"),
        name="combine",
    )(x2d, gt, route, g_final.reshape(1, d), y_tok, y_tok)


def _invert_kernel(perm_ref, inv_ref):
    def place(a, carry):
        inv_ref[perm_ref[a]] = a
        return carry

    lax.fori_loop(0, perm_ref.shape[0], place, 0, unroll=32)


def _invert(perm):
    assert perm.shape[0] % 32 == 0
    smem = pl.BlockSpec(memory_space=pltpu.SMEM)
    return pl.pallas_call(
        _invert_kernel,
        in_specs=[smem],
        out_specs=smem,
        out_shape=jax.ShapeDtypeStruct(perm.shape, I32),
        name="invert",
    )(perm)


def _routing_tables(route_all, counts, tm, n_tiles, plane_rows):
    ne = counts.shape[0]
    n = route_all.shape[1]
    e_idx = route_all[ROW_EXPERT:ROW_EXPERT + TOP_K].astype(I32)
    rank = route_all[ROW_RANK:ROW_RANK + TOP_K].astype(I32)
    padded = (counts + tm - 1) // tm * tm
    ends = jnp.cumsum(padded)
    offs = ends - padded
    onehot = e_idx[..., None] == jnp.arange(ne, dtype=I32)
    pos = jnp.sum(jnp.where(onehot, offs, 0), axis=-1) + rank
    n_active = ends[-1] // tm
    starts = jnp.arange(n_tiles, dtype=I32) * tm
    tile_expert = jnp.sum(starts[:, None] >= ends[None, :], axis=1).astype(I32)
    last_expert = jnp.sum((n_active - 1) * tm >= ends).astype(I32)
    tile_expert = jnp.minimum(tile_expert, last_expert)
    n_rows = n_tiles * tm
    n_real = n * TOP_K
    pad_ends = jnp.cumsum(padded - counts)
    j = jnp.arange(n_rows - n_real, dtype=I32)[:, None]
    in_expert = jnp.logical_and(j >= pad_ends - (padded - counts), j < pad_ends)
    pad_pos = jnp.where(j[:, 0] < pad_ends[-1],
                        jnp.sum(jnp.where(in_expert, offs + counts + j - pad_ends + padded - counts, 0),
                                axis=1),
                        ends[-1] + j[:, 0] - pad_ends[-1])
    inv = _invert(jnp.concatenate([pos.reshape(-1), pad_pos]))
    is_real = inv < n_real
    tok = inv % n
    choice = inv // n
    spare = n + jnp.minimum(inv - n_real, ne * tm - 1)
    src = jnp.where(is_real, tok, 0)
    dst = jnp.where(is_real, choice * plane_rows + tok, spare)
    lead = n + ne * tm + jnp.arange(tm, dtype=I32)
    dst = jnp.concatenate([lead, dst])[:n_rows]
    return src, dst, tile_expert, n_active.astype(I32).reshape(1)


def _moe_layer(xp, xs, mod_p, mod_s, g, router_w, wg, wu, wd, g_final):
    nb, seq, d = xp.shape
    ns = xs.shape[0]
    ne = router_w.shape[-1]
    n_all = nb * seq + ns
    sh_p, sc_p, gt_p = mod_p
    sh_s, sc_s, gt_s = mod_s
    xp2 = xp.reshape(nb * seq, d)

    zero_cnt = jnp.zeros((ne, LANES), F32)
    h_all, route_p, cnt_p = _router(xp2, sh_p, sc_p, g, router_w, zero_cnt, ROUTER_TILE, seq,
                                    n_all)
    h_all, route_s, cnt_s = _router(xs, sh_s, sc_s, g, router_w, cnt_p, ns, ns, n_all,
                                    h_prev=h_all, row_off=nb * seq)

    tm = MOE_ROW_TILE
    n_tiles = (n_all * TOP_K) // tm + ne + ROW_SLOTS
    plane_rows = -(-(n_all + (ne + 1) * tm) // COMBINE_TILE) * COMBINE_TILE
    route_all = jnp.concatenate([route_p, route_s], axis=1)
    src, dst, tile_expert, n_active = _routing_tables(
        route_all, cnt_s[:, 0].astype(I32), tm, n_tiles, plane_rows)

    y_tok = _moe_grouped(h_all, tile_expert, n_active, src, dst, wg, wu, wd, tm, MOE_FF_CHUNK,
                         plane_rows + n_all, n_all, plane_rows - n_all)
    yp = _combine(xp2, gt_p, route_p, g_final, y_tok, COMBINE_TILE, seq, 0, plane_rows)
    ys = _combine(xs, gt_s, route_s, g_final, y_tok, ns, ns, nb * seq, plane_rows)
    return yp.reshape(nb, seq, d), ys


def kernel(x_prompt, x_sample, c_prompt, c_sample, state_conv, state_pool, w_ada, b_ada, g_mix,
           g_ffn, w_in, conv_w, pool_w, pool_scale, g_conv_out, g_pool_out, w_out, dense_w_gate,
           dense_w_up, dense_w_down, router_w, moe_w_gate, moe_w_up, moe_w_down, g_final):
    depth = w_ada.shape[0]
    nb, seq, d = x_prompt.shape
    ns = x_sample.shape[0]
    assert x_sample.shape[1] == 1 and depth == 2
    assert sum(DENSE_FF_CHUNKS) == dense_w_gate.shape[-1]

    mod = _ada(jnp.concatenate([c_prompt, c_sample], axis=0), w_ada, b_ada)
    mod = mod.reshape(depth, nb + ns, 6, d)
    mod_p = [jnp.transpose(mod[i, :nb], (1, 0, 2))[:, :, None, :] for i in range(depth)]
    mod_s = [jnp.transpose(mod[i, nb:], (1, 0, 2))[:, None, :, :] for i in range(depth)]

    w_in_b = w_in.astype(BF16)
    w_out_b = w_out.astype(BF16)
    pool_w_b = pool_w.astype(BF16)
    dense_b = [w.astype(BF16) for w in (dense_w_gate, dense_w_up, dense_w_down)]
    cb_t = jnp.transpose(state_conv, (0, 2, 1, 3))
    pb_t = jnp.transpose(state_pool, (0, 2, 1, 3))

    xp = x_prompt
    xs = x_sample.reshape(ns, d)
    conv_p, pool_p, conv_s, pool_s = [], [], [], []
    for i in range(depth):
        sh1, sc1, gt1, sh2, sc2, gt2 = mod_p[i]
        mix_w = (g_mix[i], w_in_b[i], conv_w[i], pool_w_b[i], pool_scale[i], g_conv_out[i],
                 g_pool_out[i], w_out_b[i])
        xp, cs, ps = _mix_prompt(xp, sh1, sc1, gt1, *mix_w, tile=MIX_TILE, sub=MIX_ROW_BLOCK)
        conv_p.append(cs)
        pool_p.append(ps)
        s1, c1, t1, s2, c2, t2 = mod_s[i]
        xs, v_new, u_new = _mix_sample(xs, s1[0], c1[0], t1[0], *mix_w, cb_t[i], pb_t[i])
        conv_s.append(jnp.concatenate([state_conv[i][:, 1:], v_new[:, None, :]], axis=1))
        pool_s.append(jnp.concatenate([state_pool[i][:, 1:], u_new[:, None, :]], axis=1))
        j = i // 2
        if i % 2 == 0:
            wg, wu, wd = (w[j] for w in dense_b)
            xp = _ffn(xp, sh2, sc2, gt2, g_ffn[i], wg, wu, wd, PROMPT_TILE, DENSE_FF_CHUNKS)
            xs = _ffn(xs[None], s2, c2, t2, g_ffn[i], wg, wu, wd, ns, DENSE_FF_CHUNKS)[0]
        else:
            xp, xs = _moe_layer(xp, xs, (sh2, sc2, gt2), (s2, c2, t2), g_ffn[i], router_w[j],
                                moe_w_gate[j], moe_w_up[j], moe_w_down[j], g_final)

    return (xp, xs.reshape(ns, 1, d), jnp.stack(conv_p), jnp.stack(pool_p),
            jnp.stack(conv_s), jnp.stack(pool_s))
```

```python
import functools

import jax
import jax.numpy as jnp
from jax import lax
from jax.experimental import pallas as pl
from jax.experimental.pallas import tpu as pltpu

F32 = jnp.float32
BF16 = jnp.bfloat16
I32 = jnp.int32

EPS = 1e-6
CONV_K = 3
POOL_WINDOWS = (2, 4, 8, 16)
POOL_HIST = max(POOL_WINDOWS) - 1
TOP_K = 2

CONV_PAD = 8
POOL_PAD = 16

LANES = 128

ROUTE_ROWS = 8
ROW_EXPERT, ROW_RANK, ROW_PROB = 0, 2, 4

VMEM_LIMIT_BYTES = 56 * 1024 * 1024

PROMPT_TILE = 512
MIX_TILE = 2048
MIX_ROW_BLOCK = 512
ROUTER_TILE = 1024
COMBINE_TILE = 512
MOE_ROW_TILE = 1024
MOE_FF_CHUNK = 512
FILL_ROWS = 128
ROW_SLOTS = 3
DENSE_FF_CHUNKS = (768, 768, 768, 512)


def _params(*sem):
    return pltpu.CompilerParams(dimension_semantics=sem, vmem_limit_bytes=VMEM_LIMIT_BYTES)


def _resident(shape):
    nd = len(shape)
    return pl.BlockSpec(shape, lambda *_: (0,) * nd, pipeline_mode=pl.Buffered(1))


def _rms(x, g):
    ms = jnp.mean(x * x, axis=-1, keepdims=True)
    return x * lax.rsqrt(ms + EPS) * g


def _mod_norm(x, g, sc, sh):
    ms = jnp.mean(x * x, axis=-1, keepdims=True)
    return x * lax.rsqrt(ms + EPS) * (g * (1.0 + sc)) + sh


def _dot(a, b):
    return jnp.dot(a, b, preferred_element_type=F32)


def _dot_nt(a, b):
    return lax.dot_general(a, b, (((1,), (1,)), ((), ())), preferred_element_type=F32)


def _silu_mul(a, b):
    return a * jax.nn.sigmoid(a) * b


def _ada_kernel(c_ref, w_ref, b_ref, o_ref):
    c = c_ref[...]
    a = (c * jax.nn.sigmoid(c)).astype(BF16)
    o_ref[0] = _dot(a, w_ref[0].astype(BF16)) + b_ref[0]


def _ada(c_all, w_ada, b_ada):
    depth, d, n = w_ada.shape
    m = c_all.shape[0]
    tn = 1024
    return pl.pallas_call(
        _ada_kernel,
        grid=(depth, n // tn),
        in_specs=[
            pl.BlockSpec((m, d), lambda i, j: (0, 0)),
            pl.BlockSpec((1, d, tn), lambda i, j: (i, 0, j)),
            pl.BlockSpec((1, 1, tn), lambda i, j: (i, 0, j)),
        ],
        out_specs=pl.BlockSpec((1, m, tn), lambda i, j: (i, 0, j)),
        out_shape=jax.ShapeDtypeStruct((depth, m, n), F32),
        compiler_params=_params("arbitrary", "arbitrary"),
        name="ada",
    )(c_all, w_ada, b_ada.reshape(depth, 1, n))


def _mix_tail(x, gt, bg, y, d_groups, poolw_ref, pscale_ref, gco_ref, gpo_ref, wout_ref):
    cw = gco_ref.shape[-1]
    ya = bg * y
    yb = jnp.concatenate(
        [_dot(d.astype(BF16), poolw_ref[g]) for g, d in enumerate(d_groups)], axis=-1
    ) * pscale_ref[...]
    ma = _rms(ya, gco_ref[...]).astype(BF16)
    mb = _rms(yb, gpo_ref[...]).astype(BF16)
    o = _dot(ma, wout_ref[0:cw, :]) + _dot(mb, wout_ref[cw:, :])
    return x + gt * o


def _mix_prompt_kernel(x_ref, sh_ref, sc_ref, gt_ref, g_ref, win_ref, convw_ref, poolw_ref,
                       pscale_ref, gco_ref, gpo_ref, wout_ref,
                       xo_ref, cs_ref, ps_ref, vbuf, ubuf, *, tile, sub):
    l = pl.program_id(1)
    cw = convw_ref.shape[-1]
    pg = poolw_ref.shape[-1]

    @pl.when(l == 0)
    def _():
        vbuf[0:CONV_PAD, :] = jnp.zeros((CONV_PAD, cw), F32)
        ubuf[0:POOL_PAD, :] = jnp.zeros((POOL_PAD, ubuf.shape[-1]), F32)

    gates = []
    for lo in range(0, tile, sub):
        x = x_ref[0, lo:lo + sub, :]
        h = _mod_norm(x, g_ref[...], sc_ref[0], sh_ref[0]).astype(BF16)
        p = _dot(h, win_ref[...])
        gates.append(p[:, 0:cw])
        vbuf[CONV_PAD + lo:CONV_PAD + lo + sub, :] = p[:, cw:2 * cw] * p[:, 2 * cw:3 * cw]
        ubuf[POOL_PAD + lo:POOL_PAD + lo + sub, :] = p[:, 3 * cw:]

    w = convw_ref[...]
    for bg, lo in zip(gates, range(0, tile, sub)):
        y = w[CONV_K - 1:CONV_K] * vbuf[CONV_PAD + lo:CONV_PAD + lo + sub, :]
        for k in range(1, CONV_K):
            y = y + w[CONV_K - 1 - k:CONV_K - k] * vbuf[CONV_PAD + lo - k:CONV_PAD + lo - k + sub, :]

        pos = l * tile + lo + lax.broadcasted_iota(I32, (sub, 1), 0)
        d_groups = []
        for g, win in enumerate(POOL_WINDOWS):
            ug = ubuf[POOL_PAD + lo:POOL_PAD + lo + sub, g * pg:(g + 1) * pg]
            acc = ug
            for k in range(1, win):
                acc = acc + ubuf[POOL_PAD + lo - k:POOL_PAD + lo - k + sub, g * pg:(g + 1) * pg]
            cnt = jnp.minimum(pos + 1, win).astype(F32)
            d_groups.append(acc / cnt - ug)

        xo_ref[0, lo:lo + sub, :] = _mix_tail(x_ref[0, lo:lo + sub, :], gt_ref[0], bg, y, d_groups,
                                              poolw_ref, pscale_ref, gco_ref, gpo_ref, wout_ref)

    cs_ref[0] = vbuf[CONV_PAD + tile - (CONV_K - 1):CONV_PAD + tile, :]
    ps_ref[0] = ubuf[POOL_PAD + tile - POOL_HIST:POOL_PAD + tile, :]
    vbuf[0:CONV_PAD, :] = vbuf[tile:tile + CONV_PAD, :]
    ubuf[0:POOL_PAD, :] = ubuf[tile:tile + POOL_PAD, :]


def _mix_prompt(x, sh, sc, gt, g, w_in, conv_w, pool_w, pool_scale, g_co, g_po, w_out, tile, sub):
    b, seq, d = x.shape
    assert seq % tile == 0 and tile % sub == 0 and sub >= POOL_PAD
    cw = conv_w.shape[-1]
    pw = pool_scale.shape[-1]
    row = lambda bi, li: (bi, 0, 0)
    return pl.pallas_call(
        functools.partial(_mix_prompt_kernel, tile=tile, sub=sub),
        grid=(b, seq // tile),
        in_specs=[
            pl.BlockSpec((1, tile, d), lambda bi, li: (bi, li, 0)),
            pl.BlockSpec((1, 1, d), row),
            pl.BlockSpec((1, 1, d), row),
            pl.BlockSpec((1, 1, d), row),
            _resident((1, d)),
            _resident(w_in.shape),
            _resident(conv_w.shape),
            _resident(pool_w.shape),
            _resident((1, pw)),
            _resident((1, cw)),
            _resident((1, pw)),
            _resident(w_out.shape),
        ],
        out_specs=[
            pl.BlockSpec((1, tile, d), lambda bi, li: (bi, li, 0)),
            pl.BlockSpec((1, CONV_K - 1, cw), row),
            pl.BlockSpec((1, POOL_HIST, pw), row),
        ],
        out_shape=[
            jax.ShapeDtypeStruct((b, seq, d), F32),
            jax.ShapeDtypeStruct((b, CONV_K - 1, cw), F32),
            jax.ShapeDtypeStruct((b, POOL_HIST, pw), F32),
        ],
        scratch_shapes=[
            pltpu.VMEM((CONV_PAD + tile, cw), F32),
            pltpu.VMEM((POOL_PAD + tile, pw), F32),
        ],
        compiler_params=_params("arbitrary", "arbitrary"),
        name="mix_prompt",
    )(x, sh, sc, gt, g.reshape(1, d), w_in, conv_w, pool_w, pool_scale.reshape(1, pw),
      g_co.reshape(1, cw), g_po.reshape(1, pw), w_out)


def _mix_sample_kernel(x_ref, sh_ref, sc_ref, gt_ref, g_ref, win_ref, convw_ref, poolw_ref,
                       pscale_ref, gco_ref, gpo_ref, wout_ref, cb_ref, pb_ref,
                       xo_ref, cs_ref, ps_ref):
    cw = convw_ref.shape[-1]
    pg = poolw_ref.shape[-1]
    x = x_ref[...]
    h = _mod_norm(x, g_ref[...], sc_ref[...], sh_ref[...]).astype(BF16)
    p = _dot(h, win_ref[...])
    bg = p[:, 0:cw]
    v = p[:, cw:2 * cw] * p[:, 2 * cw:3 * cw]
    u = p[:, 3 * cw:]
    cs_ref[:, 0:CONV_K - 2, :] = cb_ref[:, 1:CONV_K - 1, :]
    cs_ref[:, CONV_K - 2, :] = v
    ps_ref[:, 0:POOL_HIST - 1, :] = pb_ref[:, 1:POOL_HIST, :]
    ps_ref[:, POOL_HIST - 1, :] = u

    w = convw_ref[...]
    y = w[CONV_K - 1:CONV_K] * v
    for k in range(1, CONV_K):
        y = y + w[CONV_K - 1 - k:CONV_K - k] * cb_ref[:, CONV_K - 1 - k, :]

    d_groups = []
    for g, win in enumerate(POOL_WINDOWS):
        ug = u[:, g * pg:(g + 1) * pg]
        acc = ug
        for k in range(1, win):
            acc = acc + pb_ref[:, POOL_HIST - k, g * pg:(g + 1) * pg]
        d_groups.append(acc / float(win) - ug)

    xo_ref[...] = _mix_tail(x, gt_ref[...], bg, y, d_groups, poolw_ref, pscale_ref,
                            gco_ref, gpo_ref, wout_ref)


def _mix_sample(x, sh, sc, gt, g, w_in, conv_w, pool_w, pool_scale, g_co, g_po, w_out, cb, pb):
    n, d = x.shape
    cw = conv_w.shape[-1]
    pw = pool_scale.shape[-1]
    return pl.pallas_call(
        _mix_sample_kernel,
        out_shape=[
            jax.ShapeDtypeStruct((n, d), F32),
            jax.ShapeDtypeStruct(cb.shape, F32),
            jax.ShapeDtypeStruct(pb.shape, F32),
        ],
        compiler_params=pltpu.CompilerParams(vmem_limit_bytes=VMEM_LIMIT_BYTES),
        name="mix_sample",
    )(x, sh, sc, gt, g.reshape(1, d), w_in, conv_w, pool_w, pool_scale.reshape(1, pw),
      g_co.reshape(1, cw), g_po.reshape(1, pw), w_out, cb, pb)


def _ffn_kernel(x_ref, sh_ref, sc_ref, gt_ref, g_ref, wg_ref, wu_ref, wd_ref, o_ref, *, chunks):
    x = x_ref[0]
    h = _mod_norm(x, g_ref[...], sc_ref[0], sh_ref[0]).astype(BF16)
    acc = None
    lo = 0
    for fc in chunks:
        a = _dot(h, wg_ref[:, lo:lo + fc])
        b = _dot(h, wu_ref[:, lo:lo + fc])
        part = _dot(_silu_mul(a, b).astype(BF16), wd_ref[lo:lo + fc, :])
        acc = part if acc is None else acc + part
        lo += fc
    o_ref[0] = x + gt_ref[0] * acc


def _ffn(x, sh, sc, gt, g, wg, wu, wd, tile, chunks):
    b, seq, d = x.shape
    tm = sh.shape[1]
    tmod = tile if tm > 1 else 1
    mod_map = (lambda bi, li: (bi, li, 0)) if tm > 1 else (lambda bi, li: (bi, 0, 0))
    return pl.pallas_call(
        functools.partial(_ffn_kernel, chunks=chunks),
        grid=(b, seq // tile),
        in_specs=[
            pl.BlockSpec((1, tile, d), lambda bi, li: (bi, li, 0)),
            pl.BlockSpec((1, tmod, d), mod_map),
            pl.BlockSpec((1, tmod, d), mod_map),
            pl.BlockSpec((1, tmod, d), mod_map),
            _resident((1, d)),
            _resident(wg.shape),
            _resident(wu.shape),
            _resident(wd.shape),
        ],
        out_specs=pl.BlockSpec((1, tile, d), lambda bi, li: (bi, li, 0)),
        out_shape=jax.ShapeDtypeStruct((b, seq, d), F32),
        compiler_params=_params("arbitrary", "arbitrary"),
        name="ffn",
    )(x, sh, sc, gt, g.reshape(1, d), wg, wu, wd)


def _split_bf16(a):
    hi = a.astype(BF16)
    return hi, (a - hi.astype(F32)).astype(BF16)


def _router_kernel(x_ref, sh_ref, sc_ref, g_ref, rw_ref, base_ref, h_ref, route_ref, cnt_ref,
                   tri, carry, *, n_steps):
    t = x_ref.shape[0]
    step = pl.program_id(0)

    @pl.when(step == 0)
    def _():
        r = lax.broadcasted_iota(I32, (t, t), 0)
        c = lax.broadcasted_iota(I32, (t, t), 1)
        tri[...] = jnp.where(r < c, 1.0, 0.0).astype(BF16)
        carry[...] = base_ref[...]

    @pl.when(step >= n_steps)
    def _():
        h_ref[...] = jnp.zeros_like(h_ref)

    @pl.when(step < n_steps)
    def _():
        h = _mod_norm(x_ref[...], g_ref[...], sc_ref[0], sh_ref[0])
        h_ref[...] = h
        h_hi, h_lo = _split_bf16(h)
        rw_hi, rw_lo = _split_bf16(rw_ref[...])
        logits = _dot_nt(rw_hi, h_hi) + _dot_nt(rw_hi, h_lo) + _dot_nt(rw_lo, h_hi)

        ne = logits.shape[0]
        idx = lax.broadcasted_iota(I32, logits.shape, 0)
        m1 = jnp.max(logits, axis=0, keepdims=True)
        i1 = jnp.min(jnp.where(logits == m1, idx, ne), axis=0, keepdims=True)
        sel1 = idx == i1
        rest = jnp.where(sel1, -jnp.inf, logits)
        m2 = jnp.max(rest, axis=0, keepdims=True)
        i2 = jnp.min(jnp.where(rest == m2, idx, ne), axis=0, keepdims=True)
        sel2 = idx == i2
        e = jnp.exp(m2 - m1)
        p1 = 1.0 / (1.0 + e)
        p2 = e / (1.0 + e)

        chosen = jnp.where(sel1, 1.0, jnp.where(sel2, 1.0, 0.0))
        before = _dot(chosen.astype(BF16), tri[...]) + carry[:, 0:1]
        r1 = jnp.sum(jnp.where(sel1, before, 0.0), axis=0, keepdims=True)
        r2 = jnp.sum(jnp.where(sel2, before, 0.0), axis=0, keepdims=True)
        carry[...] += jnp.sum(chosen, axis=1, keepdims=True)
        cnt_ref[...] = carry[...]

        rows = (i1.astype(F32), i2.astype(F32), r1, r2, p1, p2)
        route = jnp.zeros(logits.shape, F32)
        for k, row in enumerate(rows):
            route = jnp.where(idx == k, row, route)
        route_ref[...] = route


def _router(x2d, sh, sc, g, router_w, base_cnt, tile, rows_per_mod, h_rows, h_prev=None,
            row_off=0):
    n, d = x2d.shape
    ne = router_w.shape[-1]
    assert ne == ROUTE_ROWS and n % tile == 0 and row_off % tile == 0 and tile % LANES == 0
    assert base_cnt.shape == (ne, LANES)
    n_steps = n // tile
    tail = h_rows - (row_off + n)
    assert 0 <= tail < tile
    fill_tail = h_prev is None and tail > 0
    last = n_steps - 1
    clamp = (lambda i: jnp.minimum(i, last)) if fill_tail else (lambda i: i)
    per_row_mod = sh.shape[1] > 1
    if per_row_mod:
        mod_spec = pl.BlockSpec((1, tile, d), lambda i: (0, clamp(i), 0))
    else:
        assert rows_per_mod % tile == 0
        mod_spec = pl.BlockSpec((1, 1, d), lambda i: (clamp(i) // (rows_per_mod // tile), 0, 0))
    in_specs = [
        pl.BlockSpec((tile, d), lambda i: (clamp(i), 0)),
        mod_spec,
        mod_spec,
        _resident((1, d)),
        _resident((ne, d)),
        _resident((ne, LANES)),
    ]
    args = [x2d, sh, sc, g.reshape(1, d), router_w.T, base_cnt]
    n_in = len(args)
    aliases = {}
    body = functools.partial(_router_kernel, n_steps=n_steps)
    if h_prev is not None:
        assert h_prev.shape == (h_rows, d)
        in_specs.append(pl.BlockSpec(memory_space=pl.ANY))
        args.append(h_prev)
        aliases = {n_in: 0}

        def body(*refs):
            _router_kernel(*refs[:n_in], *refs[n_in + 1:], n_steps=n_steps)

    blk_off = row_off // tile
    return pl.pallas_call(
        body,
        grid=(n_steps + int(fill_tail),),
        in_specs=in_specs,
        out_specs=[
            pl.BlockSpec((tile, d), lambda i: (i + blk_off, 0)),
            pl.BlockSpec((ne, tile), lambda i: (0, clamp(i))),
            pl.BlockSpec((ne, LANES), lambda i: (0, 0)),
        ],
        out_shape=[
            jax.ShapeDtypeStruct((h_rows, d), F32),
            jax.ShapeDtypeStruct((ne, n), F32),
            jax.ShapeDtypeStruct((ne, LANES), F32),
        ],
        scratch_shapes=[pltpu.VMEM((tile, tile), BF16), pltpu.VMEM((ne, LANES), F32)],
        input_output_aliases=aliases,
        compiler_params=_params("arbitrary"),
        name="router",
    )(*args)


def _row_copy(src_hbm, row, dst, dst_row, sem):
    return pltpu.make_async_copy(src_hbm.at[pl.ds(row, 1), :], dst.at[pl.ds(dst_row, 1), :], sem)


def _row_copy_out(src, src_row, dst_hbm, row, sem):
    return pltpu.make_async_copy(src.at[pl.ds(src_row, 1), :], dst_hbm.at[pl.ds(row, 1), :], sem)


def _moe_grouped_kernel(te_ref, na_ref, src_ref, dst_ref, h_hbm, wg_ref, wu_ref, wd_ref, y_hbm,
                        xs, hbuf, obuf, gsem, ssem, fsem, *, tm, nc, rows_per_step, fill_row,
                        fill_blocks):
    del te_ref
    i = pl.program_id(0)
    c = pl.program_id(1)
    n_active = na_ref[0]
    ahead = ROW_SLOTS - 1
    slot = i % ROW_SLOTS

    def request(tile, r):
        s = tile % ROW_SLOTS
        return _row_copy(h_hbm, src_ref[tile * tm + r], xs.at[s], r, gsem.at[s])

    def send(tile, r):
        s = (tile + ROW_SLOTS) % ROW_SLOTS
        return _row_copy_out(obuf.at[s], r, y_hbm, dst_ref[(tile + 1) * tm + r], ssem.at[s])

    @pl.when(c == 0)
    def _():
        @pl.when(i == 0)
        def _():
            last = ROW_SLOTS - 1
            obuf[last] = jnp.zeros((tm, obuf.shape[-1]), F32)
            fills = [pltpu.make_async_copy(obuf.at[last, pl.ds(0, FILL_ROWS), :],
                                           y_hbm.at[pl.ds(fill_row + q * FILL_ROWS, FILL_ROWS), :],
                                           fsem) for q in range(fill_blocks)]
            for f in fills:
                f.start()
            for f in fills:
                f.wait()

            def body(r, carry):
                for t in range(ahead):
                    request(t, r).start()
                return carry

            lax.fori_loop(0, tm, body, 0)

        @pl.when(i < n_active + ahead)
        def _():
            pltpu.make_async_copy(h_hbm.at[pl.ds(0, tm), :], xs.at[slot], gsem.at[slot]).wait()

        @pl.when(jnp.logical_and(i >= ahead, i < n_active + ROW_SLOTS))
        def _():
            pltpu.make_async_copy(obuf.at[slot], y_hbm.at[pl.ds(0, tm), :], ssem.at[slot]).wait()

        @pl.when(i < n_active)
        def _():
            obuf[slot] = jnp.zeros((tm, obuf.shape[-1]), F32)
            hbuf[...] = xs[slot].astype(BF16)
            for r in range(rows_per_step * nc, tm):
                request(i + ahead, r).start()
                send(i - 1, r).start()

        @pl.when(i == n_active)
        def _():
            def body(r, carry):
                send(i - 1, r).start()
                return carry

            lax.fori_loop(0, tm, body, 0)

    @pl.when(i < n_active)
    def _():
        for k in range(rows_per_step):
            r = c * rows_per_step + k
            request(i + ahead, r).start()
            send(i - 1, r).start()

        h = hbuf[...]
        a = _dot(h, wg_ref[0].astype(BF16))
        b = _dot(h, wu_ref[0].astype(BF16))
        obuf[slot] += _dot(_silu_mul(a, b).astype(BF16), wd_ref[0].astype(BF16))


def _moe_grouped(h_all, tile_expert, n_active, src, dst, wg, wu, wd, tm, fc, y_rows, fill_row,
                 fill_rows):
    ne, d, dff = wg.shape
    n_tiles = tile_expert.shape[0]
    nc = dff // fc
    assert dff % fc == 0
    assert src.shape[0] == n_tiles * tm and dst.shape[0] == n_tiles * tm
    assert fill_rows % FILL_ROWS == 0 and tm >= FILL_ROWS
    rows_per_step = tm // nc

    def w_col(i, c, te, na, src_, dst_):
        return (te[i], 0, jnp.where(i < na[0], c, nc - 1))

    def w_row(i, c, te, na, src_, dst_):
        return (te[i], jnp.where(i < na[0], c, nc - 1), 0)

    grid_spec = pltpu.PrefetchScalarGridSpec(
        num_scalar_prefetch=4,
        grid=(n_tiles, nc),
        in_specs=[
            pl.BlockSpec(memory_space=pl.ANY),
            pl.BlockSpec((1, d, fc), w_col),
            pl.BlockSpec((1, d, fc), w_col),
            pl.BlockSpec((1, fc, d), w_row),
        ],
        out_specs=pl.BlockSpec(memory_space=pl.ANY),
        scratch_shapes=[
            pltpu.VMEM((ROW_SLOTS, tm, d), F32),
            pltpu.VMEM((tm, d), BF16),
            pltpu.VMEM((ROW_SLOTS, tm, d), F32),
            pltpu.SemaphoreType.DMA((ROW_SLOTS,)),
            pltpu.SemaphoreType.DMA((ROW_SLOTS,)),
            pltpu.SemaphoreType.DMA,
        ],
    )
    return pl.pallas_call(
        functools.partial(_moe_grouped_kernel, tm=tm, nc=nc, rows_per_step=rows_per_step,
                          fill_row=fill_row, fill_blocks=fill_rows // FILL_ROWS),
        grid_spec=grid_spec,
        out_shape=jax.ShapeDtypeStruct((y_rows, d), F32),
        compiler_params=_params("arbitrary", "arbitrary"),
        name="moe_grouped",
    )(tile_expert, n_active, src, dst, h_all, wg, wu, wd)


def _combine_kernel(x_ref, gt_ref, route_ref, gf_ref, y0_ref, y1_ref, o_ref):
    route = route_ref[...]
    pad = jnp.zeros((LANES - route.shape[0], route.shape[1]), F32)
    cols = jnp.concatenate([route, pad], axis=0).T
    f = (cols[:, ROW_PROB:ROW_PROB + 1] * y0_ref[...]
         + cols[:, ROW_PROB + 1:ROW_PROB + 2] * y1_ref[...])
    o_ref[...] = _rms(x_ref[...] + gt_ref[0] * f, gf_ref[...])


def _combine(x2d, gt, route, g_final, y_tok, tile, rows_per_mod, tok_off, plane_rows):
    n, d = x2d.shape
    assert n % tile == 0 and tok_off % tile == 0 and plane_rows % tile == 0
    per_row_mod = gt.shape[1] > 1
    if per_row_mod:
        mod_spec = pl.BlockSpec((1, tile, d), lambda j: (0, j, 0))
    else:
        assert rows_per_mod % tile == 0
        mod_spec = pl.BlockSpec((1, 1, d), lambda j: (j // (rows_per_mod // tile), 0, 0))
    first = tok_off // tile
    second = (plane_rows + tok_off) // tile
    return pl.pallas_call(
        _combine_kernel,
        grid=(n // tile,),
        in_specs=[
            pl.BlockSpec((tile, d), lambda j: (j, 0)),
            mod_spec,
            pl.BlockSpec((ROUTE_ROWS, tile), lambda j: (0, j)),
            pl.BlockSpec((1, d), lambda j: (0, 0)),
            pl.BlockSpec((tile, d), lambda j: (first + j, 0)),
            pl.BlockSpec((tile, d), lambda j: (second + j, 0)),
        ],
        out_specs=pl.BlockSpec((tile, d), lambda j: (j, 0)),
        out_shape=jax.ShapeDtypeStruct((n, d), F32),
        compiler_params=_params("arbitrary"),
        name="combine",
    )(x2d, gt, route, g_final.reshape(1, d), y_tok, y_tok)


def _invert_kernel(perm_ref, inv_ref):
    def place(a, carry):
        inv_ref[perm_ref[a]] = a
        return carry

    lax.fori_loop(0, perm_ref.shape[0], place, 0, unroll=32)


def _invert(perm):
    assert perm.shape[0] % 32 == 0
    smem = pl.BlockSpec(memory_space=pltpu.SMEM)
    return pl.pallas_call(
        _invert_kernel,
        in_specs=[smem],
        out_specs=smem,
        out_shape=jax.ShapeDtypeStruct(perm.shape, I32),
        name="invert",
    )(perm)


def _routing_tables(route_all, counts, tm, n_tiles, plane_rows):
    ne = counts.shape[0]
    n = route_all.shape[1]
    e_idx = route_all[ROW_EXPERT:ROW_EXPERT + TOP_K].astype(I32)
    rank = route_all[ROW_RANK:ROW_RANK + TOP_K].astype(I32)
    padded = (counts + tm - 1) // tm * tm
    ends = jnp.cumsum(padded)
    offs = ends - padded
    onehot = e_idx[..., None] == jnp.arange(ne, dtype=I32)
    pos = jnp.sum(jnp.where(onehot, offs, 0), axis=-1) + rank
    n_active = ends[-1] // tm
    starts = jnp.arange(n_tiles, dtype=I32) * tm
    tile_expert = jnp.sum(starts[:, None] >= ends[None, :], axis=1).astype(I32)
    last_expert = jnp.sum((n_active - 1) * tm >= ends).astype(I32)
    tile_expert = jnp.minimum(tile_expert, last_expert)
    n_rows = n_tiles * tm
    n_real = n * TOP_K
    pad_ends = jnp.cumsum(padded - counts)
    j = jnp.arange(n_rows - n_real, dtype=I32)[:, None]
    in_expert = jnp.logical_and(j >= pad_ends - (padded - counts), j < pad_ends)
    pad_pos = jnp.where(j[:, 0] < pad_ends[-1],
                        jnp.sum(jnp.where(in_expert, offs + counts + j - pad_ends + padded - counts, 0),
                                axis=1),
                        ends[-1] + j[:, 0] - pad_ends[-1])
    inv = _invert(jnp.concatenate([pos.reshape(-1), pad_pos]))
    is_real = inv < n_real
    tok = inv % n
    choice = inv // n
    spare = n + jnp.minimum(inv - n_real, ne * tm - 1)
    src = jnp.where(is_real, tok, 0)
    dst = jnp.where(is_real, choice * plane_rows + tok, spare)
    lead = n + ne * tm + jnp.arange(tm, dtype=I32)
    dst = jnp.concatenate([lead, dst])[:n_rows]
    return src, dst, tile_expert, n_active.astype(I32).reshape(1)


def _moe_layer(xp, xs, mod_p, mod_s, g, router_w, wg, wu, wd, g_final):
    nb, seq, d = xp.shape
    ns = xs.shape[0]
    ne = router_w.shape[-1]
    n_all = nb * seq + ns
    sh_p, sc_p, gt_p = mod_p
    sh_s, sc_s, gt_s = mod_s
    xp2 = xp.reshape(nb * seq, d)

    zero_cnt = jnp.zeros((ne, LANES), F32)
    h_all, route_p, cnt_p = _router(xp2, sh_p, sc_p, g, router_w, zero_cnt, ROUTER_TILE, seq,
                                    n_all)
    h_all, route_s, cnt_s = _router(xs, sh_s, sc_s, g, router_w, cnt_p, ns, ns, n_all,
                                    h_prev=h_all, row_off=nb * seq)

    tm = MOE_ROW_TILE
    n_tiles = (n_all * TOP_K) // tm + ne + ROW_SLOTS
    plane_rows = -(-(n_all + (ne + 1) * tm) // COMBINE_TILE) * COMBINE_TILE
    route_all = jnp.concatenate([route_p, route_s], axis=1)
    src, dst, tile_expert, n_active = _routing_tables(
        route_all, cnt_s[:, 0].astype(I32), tm, n_tiles, plane_rows)

    y_tok = _moe_grouped(h_all, tile_expert, n_active, src, dst, wg, wu, wd, tm, MOE_FF_CHUNK,
                         plane_rows + n_all, n_all, plane_rows - n_all)
    yp = _combine(xp2, gt_p, route_p, g_final, y_tok, COMBINE_TILE, seq, 0, plane_rows)
    ys = _combine(xs, gt_s, route_s, g_final, y_tok, ns, ns, nb * seq, plane_rows)
    return yp.reshape(nb, seq, d), ys


def kernel(x_prompt, x_sample, c_prompt, c_sample, state_conv, state_pool, w_ada, b_ada, g_mix,
           g_ffn, w_in, conv_w, pool_w, pool_scale, g_conv_out, g_pool_out, w_out, dense_w_gate,
           dense_w_up, dense_w_down, router_w, moe_w_gate, moe_w_up, moe_w_down, g_final):
    depth = w_ada.shape[0]
    nb, seq, d = x_prompt.shape
    ns = x_sample.shape[0]
    assert x_sample.shape[1] == 1 and depth == 2
    assert sum(DENSE_FF_CHUNKS) == dense_w_gate.shape[-1]

    mod = _ada(jnp.concatenate([c_prompt, c_sample], axis=0), w_ada, b_ada)
    mod = mod.reshape(depth, nb + ns, 6, d)
    mod_p = [jnp.transpose(mod[i, :nb], (1, 0, 2))[:, :, None, :] for i in range(depth)]
    mod_s = [jnp.transpose(mod[i, nb:], (1, 0, 2))[:, None, :, :] for i in range(depth)]

    w_in_b = w_in.astype(BF16)
    w_out_b = w_out.astype(BF16)
    pool_w_b = pool_w.astype(BF16)
    dense_b = [w.astype(BF16) for w in (dense_w_gate, dense_w_up, dense_w_down)]

    xp = x_prompt
    xs = x_sample.reshape(ns, d)
    conv_p, pool_p, conv_s, pool_s = [], [], [], []
    for i in range(depth):
        sh1, sc1, gt1, sh2, sc2, gt2 = mod_p[i]
        mix_w = (g_mix[i], w_in_b[i], conv_w[i], pool_w_b[i], pool_scale[i], g_conv_out[i],
                 g_pool_out[i], w_out_b[i])
        xp, cs, ps = _mix_prompt(xp, sh1, sc1, gt1, *mix_w, tile=MIX_TILE, sub=MIX_ROW_BLOCK)
        conv_p.append(cs)
        pool_p.append(ps)
        s1, c1, t1, s2, c2, t2 = mod_s[i]
        xs, cs, ps = _mix_sample(xs, s1[0], c1[0], t1[0], *mix_w, state_conv[i], state_pool[i])
        conv_s.append(cs)
        pool_s.append(ps)
        j = i // 2
        if i % 2 == 0:
            wg, wu, wd = (w[j] for w in dense_b)
            xp = _ffn(xp, sh2, sc2, gt2, g_ffn[i], wg, wu, wd, PROMPT_TILE, DENSE_FF_CHUNKS)
            xs = _ffn(xs[None], s2, c2, t2, g_ffn[i], wg, wu, wd, ns, DENSE_FF_CHUNKS)[0]
        else:
            xp, xs = _moe_layer(xp, xs, (sh2, sc2, gt2), (s2, c2, t2), g_ffn[i], router_w[j],
                                moe_w_gate[j], moe_w_up[j], moe_w_down[j], g_final)

    return (xp, xs.reshape(ns, 1, d), jnp.stack(conv_p), jnp.stack(pool_p),
            jnp.stack(conv_s), jnp.stack(pool_s))
```

```python
import functools

import jax
import jax.numpy as jnp
from jax import lax
from jax.experimental import pallas as pl
from jax.experimental.pallas import tpu as pltpu

F32 = jnp.float32
BF16 = jnp.bfloat16
I32 = jnp.int32

EPS = 1e-6
CONV_K = 3
POOL_WINDOWS = (2, 4, 8, 16)
POOL_HIST = max(POOL_WINDOWS) - 1
TOP_K = 2

CONV_PAD = 8
POOL_PAD = 16

LANES = 128

ROUTE_ROWS = 8
ROW_EXPERT, ROW_RANK, ROW_PROB = 0, 2, 4

VMEM_LIMIT_BYTES = 56 * 1024 * 1024

ADA_COL_TILE = 2048
PROMPT_TILE = 512
MIX_TILE = 2048
MIX_ROW_BLOCK = 512
ROUTER_TILE = 1024
COMBINE_TILE = 1024
MOE_ROW_TILE = 1024
MOE_FF_CHUNK = 512
FILL_ROWS = 128
ROW_SLOTS = 3
DENSE_FF_CHUNKS = (768, 768, 768, 512)


def _params(*sem):
    return pltpu.CompilerParams(dimension_semantics=sem, vmem_limit_bytes=VMEM_LIMIT_BYTES)


def _resident(shape):
    nd = len(shape)
    return pl.BlockSpec(shape, lambda *_: (0,) * nd, pipeline_mode=pl.Buffered(1))


def _rms(x, g):
    ms = jnp.mean(x * x, axis=-1, keepdims=True)
    return x * lax.rsqrt(ms + EPS) * g


def _mod_norm(x, g, sc, sh):
    ms = jnp.mean(x * x, axis=-1, keepdims=True)
    return x * lax.rsqrt(ms + EPS) * (g * (1.0 + sc)) + sh


def _dot(a, b):
    return jnp.dot(a, b, preferred_element_type=F32)


def _dot_nt(a, b):
    return lax.dot_general(a, b, (((1,), (1,)), ((), ())), preferred_element_type=F32)


def _silu_mul(a, b):
    return a * jax.nn.sigmoid(a) * b


def _ada_kernel(c_ref, w_ref, b_ref, o_ref):
    c = c_ref[...]
    a = (c * jax.nn.sigmoid(c)).astype(BF16)
    o_ref[0] = _dot(a, w_ref[0].astype(BF16)) + b_ref[0]


def _ada(c_all, w_ada, b_ada):
    depth, d, n = w_ada.shape
    m = c_all.shape[0]
    tn = ADA_COL_TILE
    return pl.pallas_call(
        _ada_kernel,
        grid=(depth, n // tn),
        in_specs=[
            pl.BlockSpec((m, d), lambda i, j: (0, 0)),
            pl.BlockSpec((1, d, tn), lambda i, j: (i, 0, j)),
            pl.BlockSpec((1, 1, tn), lambda i, j: (i, 0, j)),
        ],
        out_specs=pl.BlockSpec((1, m, tn), lambda i, j: (i, 0, j)),
        out_shape=jax.ShapeDtypeStruct((depth, m, n), F32),
        compiler_params=_params("arbitrary", "arbitrary"),
        name="ada",
    )(c_all, w_ada, b_ada.reshape(depth, 1, n))


def _mix_tail(x, gt, bg, y, d_groups, poolw_ref, pscale_ref, gco_ref, gpo_ref, wout_ref):
    cw = gco_ref.shape[-1]
    ya = bg * y
    yb = jnp.concatenate(
        [_dot(d.astype(BF16), poolw_ref[g]) for g, d in enumerate(d_groups)], axis=-1
    ) * pscale_ref[...]
    ma = _rms(ya, gco_ref[...]).astype(BF16)
    mb = _rms(yb, gpo_ref[...]).astype(BF16)
    o = _dot(ma, wout_ref[0:cw, :]) + _dot(mb, wout_ref[cw:, :])
    return x + gt * o


def _mix_prompt_kernel(x_ref, sh_ref, sc_ref, gt_ref, g_ref, win_ref, convw_ref, poolw_ref,
                       pscale_ref, gco_ref, gpo_ref, wout_ref,
                       xo_ref, cs_ref, ps_ref, vbuf, ubuf, *, tile, sub):
    l = pl.program_id(1)
    cw = convw_ref.shape[-1]
    pg = poolw_ref.shape[-1]

    @pl.when(l == 0)
    def _():
        vbuf[0:CONV_PAD, :] = jnp.zeros((CONV_PAD, cw), F32)
        ubuf[0:POOL_PAD, :] = jnp.zeros((POOL_PAD, ubuf.shape[-1]), F32)

    gates = []
    for lo in range(0, tile, sub):
        x = x_ref[0, lo:lo + sub, :]
        h = _mod_norm(x, g_ref[...], sc_ref[0], sh_ref[0]).astype(BF16)
        p = _dot(h, win_ref[...])
        gates.append(p[:, 0:cw])
        vbuf[CONV_PAD + lo:CONV_PAD + lo + sub, :] = p[:, cw:2 * cw] * p[:, 2 * cw:3 * cw]
        ubuf[POOL_PAD + lo:POOL_PAD + lo + sub, :] = p[:, 3 * cw:]

    w = convw_ref[...]
    for bg, lo in zip(gates, range(0, tile, sub)):
        y = w[CONV_K - 1:CONV_K] * vbuf[CONV_PAD + lo:CONV_PAD + lo + sub, :]
        for k in range(1, CONV_K):
            y = y + w[CONV_K - 1 - k:CONV_K - k] * vbuf[CONV_PAD + lo - k:CONV_PAD + lo - k + sub, :]

        pos = l * tile + lo + lax.broadcasted_iota(I32, (sub, 1), 0)
        d_groups = []
        for g, win in enumerate(POOL_WINDOWS):
            ug = ubuf[POOL_PAD + lo:POOL_PAD + lo + sub, g * pg:(g + 1) * pg]
            acc = ug
            for k in range(1, win):
                acc = acc + ubuf[POOL_PAD + lo - k:POOL_PAD + lo - k + sub, g * pg:(g + 1) * pg]
            cnt = jnp.minimum(pos + 1, win).astype(F32)
            d_groups.append(acc / cnt - ug)

        xo_ref[0, lo:lo + sub, :] = _mix_tail(x_ref[0, lo:lo + sub, :], gt_ref[0], bg, y, d_groups,
                                              poolw_ref, pscale_ref, gco_ref, gpo_ref, wout_ref)

    cs_ref[0] = vbuf[CONV_PAD + tile - (CONV_K - 1):CONV_PAD + tile, :]
    ps_ref[0] = ubuf[POOL_PAD + tile - POOL_HIST:POOL_PAD + tile, :]
    vbuf[0:CONV_PAD, :] = vbuf[tile:tile + CONV_PAD, :]
    ubuf[0:POOL_PAD, :] = ubuf[tile:tile + POOL_PAD, :]


def _mix_prompt(x, sh, sc, gt, g, w_in, conv_w, pool_w, pool_scale, g_co, g_po, w_out, tile, sub):
    b, seq, d = x.shape
    assert seq % tile == 0 and tile % sub == 0 and sub >= POOL_PAD
    cw = conv_w.shape[-1]
    pw = pool_scale.shape[-1]
    row = lambda bi, li: (bi, 0, 0)
    return pl.pallas_call(
        functools.partial(_mix_prompt_kernel, tile=tile, sub=sub),
        grid=(b, seq // tile),
        in_specs=[
            pl.BlockSpec((1, tile, d), lambda bi, li: (bi, li, 0)),
            pl.BlockSpec((1, 1, d), row),
            pl.BlockSpec((1, 1, d), row),
            pl.BlockSpec((1, 1, d), row),
            _resident((1, d)),
            _resident(w_in.shape),
            _resident(conv_w.shape),
            _resident(pool_w.shape),
            _resident((1, pw)),
            _resident((1, cw)),
            _resident((1, pw)),
            _resident(w_out.shape),
        ],
        out_specs=[
            pl.BlockSpec((1, tile, d), lambda bi, li: (bi, li, 0)),
            pl.BlockSpec((1, CONV_K - 1, cw), row),
            pl.BlockSpec((1, POOL_HIST, pw), row),
        ],
        out_shape=[
            jax.ShapeDtypeStruct((b, seq, d), F32),
            jax.ShapeDtypeStruct((b, CONV_K - 1, cw), F32),
            jax.ShapeDtypeStruct((b, POOL_HIST, pw), F32),
        ],
        scratch_shapes=[
            pltpu.VMEM((CONV_PAD + tile, cw), F32),
            pltpu.VMEM((POOL_PAD + tile, pw), F32),
        ],
        compiler_params=_params("arbitrary", "arbitrary"),
        name="mix_prompt",
    )(x, sh, sc, gt, g.reshape(1, d), w_in, conv_w, pool_w, pool_scale.reshape(1, pw),
      g_co.reshape(1, cw), g_po.reshape(1, pw), w_out)


def _mix_sample_kernel(x_ref, sh_ref, sc_ref, gt_ref, g_ref, win_ref, convw_ref, poolw_ref,
                       pscale_ref, gco_ref, gpo_ref, wout_ref, cb_ref, pb_ref,
                       xo_ref, v_ref, u_ref):
    cw = convw_ref.shape[-1]
    pg = poolw_ref.shape[-1]
    x = x_ref[...]
    h = _mod_norm(x, g_ref[...], sc_ref[...], sh_ref[...]).astype(BF16)
    p = _dot(h, win_ref[...])
    bg = p[:, 0:cw]
    v = p[:, cw:2 * cw] * p[:, 2 * cw:3 * cw]
    u = p[:, 3 * cw:]
    v_ref[...] = v
    u_ref[...] = u

    w = convw_ref[...]
    y = w[CONV_K - 1:CONV_K] * v
    for k in range(1, CONV_K):
        y = y + w[CONV_K - 1 - k:CONV_K - k] * cb_ref[CONV_K - 1 - k]

    d_groups = []
    for g, win in enumerate(POOL_WINDOWS):
        ug = u[:, g * pg:(g + 1) * pg]
        acc = ug
        for k in range(1, win):
            acc = acc + pb_ref[POOL_HIST - k, :, g * pg:(g + 1) * pg]
        d_groups.append(acc / float(win) - ug)

    xo_ref[...] = _mix_tail(x, gt_ref[...], bg, y, d_groups, poolw_ref, pscale_ref,
                            gco_ref, gpo_ref, wout_ref)


def _mix_sample(x, sh, sc, gt, g, w_in, conv_w, pool_w, pool_scale, g_co, g_po, w_out, cb, pb):
    n, d = x.shape
    cw = conv_w.shape[-1]
    pw = pool_scale.shape[-1]
    return pl.pallas_call(
        _mix_sample_kernel,
        out_shape=[
            jax.ShapeDtypeStruct((n, d), F32),
            jax.ShapeDtypeStruct((n, cw), F32),
            jax.ShapeDtypeStruct((n, pw), F32),
        ],
        compiler_params=pltpu.CompilerParams(vmem_limit_bytes=VMEM_LIMIT_BYTES),
        name="mix_sample",
    )(x, sh, sc, gt, g.reshape(1, d), w_in, conv_w, pool_w, pool_scale.reshape(1, pw),
      g_co.reshape(1, cw), g_po.reshape(1, pw), w_out, cb, pb)


def _ffn_kernel(x_ref, sh_ref, sc_ref, gt_ref, g_ref, wg_ref, wu_ref, wd_ref, o_ref, *, chunks):
    x = x_ref[0]
    h = _mod_norm(x, g_ref[...], sc_ref[0], sh_ref[0]).astype(BF16)
    acc = None
    lo = 0
    for fc in chunks:
        a = _dot(h, wg_ref[:, lo:lo + fc])
        b = _dot(h, wu_ref[:, lo:lo + fc])
        part = _dot(_silu_mul(a, b).astype(BF16), wd_ref[lo:lo + fc, :])
        acc = part if acc is None else acc + part
        lo += fc
    o_ref[0] = x + gt_ref[0] * acc


def _ffn(x, sh, sc, gt, g, wg, wu, wd, tile, chunks):
    b, seq, d = x.shape
    tm = sh.shape[1]
    tmod = tile if tm > 1 else 1
    mod_map = (lambda bi, li: (bi, li, 0)) if tm > 1 else (lambda bi, li: (bi, 0, 0))
    return pl.pallas_call(
        functools.partial(_ffn_kernel, chunks=chunks),
        grid=(b, seq // tile),
        in_specs=[
            pl.BlockSpec((1, tile, d), lambda bi, li: (bi, li, 0)),
            pl.BlockSpec((1, tmod, d), mod_map),
            pl.BlockSpec((1, tmod, d), mod_map),
            pl.BlockSpec((1, tmod, d), mod_map),
            _resident((1, d)),
            _resident(wg.shape),
            _resident(wu.shape),
            _resident(wd.shape),
        ],
        out_specs=pl.BlockSpec((1, tile, d), lambda bi, li: (bi, li, 0)),
        out_shape=jax.ShapeDtypeStruct((b, seq, d), F32),
        compiler_params=_params("arbitrary", "arbitrary"),
        name="ffn",
    )(x, sh, sc, gt, g.reshape(1, d), wg, wu, wd)


def _split_bf16(a):
    hi = a.astype(BF16)
    return hi, (a - hi.astype(F32)).astype(BF16)


def _router_kernel(x_ref, sh_ref, sc_ref, g_ref, rw_ref, base_ref, h_ref, route_ref, cnt_ref,
                   tri, carry, *, n_steps):
    t = x_ref.shape[0]
    step = pl.program_id(0)

    @pl.when(step == 0)
    def _():
        r = lax.broadcasted_iota(I32, (t, t), 0)
        c = lax.broadcasted_iota(I32, (t, t), 1)
        tri[...] = jnp.where(r < c, 1.0, 0.0).astype(BF16)
        carry[...] = base_ref[...]

    @pl.when(step >= n_steps)
    def _():
        h_ref[...] = jnp.zeros_like(h_ref)

    @pl.when(step < n_steps)
    def _():
        h = _mod_norm(x_ref[...], g_ref[...], sc_ref[0], sh_ref[0])
        h_ref[...] = h
        h_hi, h_lo = _split_bf16(h)
        rw_hi, rw_lo = _split_bf16(rw_ref[...])
        logits = _dot_nt(rw_hi, h_hi) + _dot_nt(rw_hi, h_lo) + _dot_nt(rw_lo, h_hi)

        ne = logits.shape[0]
        idx = lax.broadcasted_iota(I32, logits.shape, 0)
        m1 = jnp.max(logits, axis=0, keepdims=True)
        i1 = jnp.min(jnp.where(logits == m1, idx, ne), axis=0, keepdims=True)
        sel1 = idx == i1
        rest = jnp.where(sel1, -jnp.inf, logits)
        m2 = jnp.max(rest, axis=0, keepdims=True)
        i2 = jnp.min(jnp.where(rest == m2, idx, ne), axis=0, keepdims=True)
        sel2 = idx == i2
        e = jnp.exp(m2 - m1)
        p1 = 1.0 / (1.0 + e)
        p2 = e / (1.0 + e)

        chosen = jnp.where(sel1, 1.0, jnp.where(sel2, 1.0, 0.0))
        before = _dot(chosen.astype(BF16), tri[...]) + carry[:, 0:1]
        r1 = jnp.sum(jnp.where(sel1, before, 0.0), axis=0, keepdims=True)
        r2 = jnp.sum(jnp.where(sel2, before, 0.0), axis=0, keepdims=True)
        carry[...] += jnp.sum(chosen, axis=1, keepdims=True)
        cnt_ref[...] = carry[...]

        rows = (i1.astype(F32), i2.astype(F32), r1, r2, p1, p2)
        route = jnp.zeros(logits.shape, F32)
        for k, row in enumerate(rows):
            route = jnp.where(idx == k, row, route)
        route_ref[...] = route


def _router(x2d, sh, sc, g, router_w, base_cnt, tile, rows_per_mod, h_rows, h_prev=None,
            row_off=0):
    n, d = x2d.shape
    ne = router_w.shape[-1]
    assert ne == ROUTE_ROWS and n % tile == 0 and row_off % tile == 0 and tile % LANES == 0
    assert base_cnt.shape == (ne, LANES)
    n_steps = n // tile
    tail = h_rows - (row_off + n)
    assert 0 <= tail < tile
    fill_tail = h_prev is None and tail > 0
    last = n_steps - 1
    clamp = (lambda i: jnp.minimum(i, last)) if fill_tail else (lambda i: i)
    per_row_mod = sh.shape[1] > 1
    if per_row_mod:
        mod_spec = pl.BlockSpec((1, tile, d), lambda i: (0, clamp(i), 0))
    else:
        assert rows_per_mod % tile == 0
        mod_spec = pl.BlockSpec((1, 1, d), lambda i: (clamp(i) // (rows_per_mod // tile), 0, 0))
    in_specs = [
        pl.BlockSpec((tile, d), lambda i: (clamp(i), 0)),
        mod_spec,
        mod_spec,
        _resident((1, d)),
        _resident((ne, d)),
        _resident((ne, LANES)),
    ]
    args = [x2d, sh, sc, g.reshape(1, d), router_w.T, base_cnt]
    n_in = len(args)
    aliases = {}
    body = functools.partial(_router_kernel, n_steps=n_steps)
    if h_prev is not None:
        assert h_prev.shape == (h_rows, d)
        in_specs.append(pl.BlockSpec(memory_space=pl.ANY))
        args.append(h_prev)
        aliases = {n_in: 0}

        def body(*refs):
            _router_kernel(*refs[:n_in], *refs[n_in + 1:], n_steps=n_steps)

    blk_off = row_off // tile
    return pl.pallas_call(
        body,
        grid=(n_steps + int(fill_tail),),
        in_specs=in_specs,
        out_specs=[
            pl.BlockSpec((tile, d), lambda i: (i + blk_off, 0)),
            pl.BlockSpec((ne, tile), lambda i: (0, clamp(i))),
            pl.BlockSpec((ne, LANES), lambda i: (0, 0)),
        ],
        out_shape=[
            jax.ShapeDtypeStruct((h_rows, d), F32),
            jax.ShapeDtypeStruct((ne, n), F32),
            jax.ShapeDtypeStruct((ne, LANES), F32),
        ],
        scratch_shapes=[pltpu.VMEM((tile, tile), BF16), pltpu.VMEM((ne, LANES), F32)],
        input_output_aliases=aliases,
        compiler_params=_params("arbitrary"),
        name="router",
    )(*args)


def _row_copy(src_hbm, row, dst, dst_row, sem):
    return pltpu.make_async_copy(src_hbm.at[pl.ds(row, 1), :], dst.at[pl.ds(dst_row, 1), :], sem)


def _row_copy_out(src, src_row, dst_hbm, row, sem):
    return pltpu.make_async_copy(src.at[pl.ds(src_row, 1), :], dst_hbm.at[pl.ds(row, 1), :], sem)


def _moe_grouped_kernel(te_ref, na_ref, src_ref, dst_ref, h_hbm, wg_ref, wu_ref, wd_ref, y_hbm,
                        xs, hbuf, obuf, gsem, ssem, fsem, *, tm, nc, rows_per_step, fill_row,
                        fill_blocks):
    del te_ref
    i = pl.program_id(0)
    c = pl.program_id(1)
    n_active = na_ref[0]
    ahead = ROW_SLOTS - 1
    slot = i % ROW_SLOTS

    def request(tile, r):
        s = tile % ROW_SLOTS
        return _row_copy(h_hbm, src_ref[tile * tm + r], xs.at[s], r, gsem.at[s])

    def send(tile, r):
        s = (tile + ROW_SLOTS) % ROW_SLOTS
        return _row_copy_out(obuf.at[s], r, y_hbm, dst_ref[(tile + 1) * tm + r], ssem.at[s])

    @pl.when(c == 0)
    def _():
        @pl.when(i == 0)
        def _():
            last = ROW_SLOTS - 1
            obuf[last] = jnp.zeros((tm, obuf.shape[-1]), F32)
            fills = [pltpu.make_async_copy(obuf.at[last, pl.ds(0, FILL_ROWS), :],
                                           y_hbm.at[pl.ds(fill_row + q * FILL_ROWS, FILL_ROWS), :],
                                           fsem) for q in range(fill_blocks)]
            for f in fills:
                f.start()
            for f in fills:
                f.wait()

            def body(r, carry):
                for t in range(ahead):
                    request(t, r).start()
                return carry

            lax.fori_loop(0, tm, body, 0)

        @pl.when(i < n_active + ahead)
        def _():
            pltpu.make_async_copy(h_hbm.at[pl.ds(0, tm), :], xs.at[slot], gsem.at[slot]).wait()

        @pl.when(jnp.logical_and(i >= ahead, i < n_active + ROW_SLOTS))
        def _():
            pltpu.make_async_copy(obuf.at[slot], y_hbm.at[pl.ds(0, tm), :], ssem.at[slot]).wait()

        @pl.when(i < n_active)
        def _():
            obuf[slot] = jnp.zeros((tm, obuf.shape[-1]), F32)
            hbuf[...] = xs[slot].astype(BF16)
            for r in range(rows_per_step * nc, tm):
                request(i + ahead, r).start()
                send(i - 1, r).start()

        @pl.when(i == n_active)
        def _():
            def body(r, carry):
                send(i - 1, r).start()
                return carry

            lax.fori_loop(0, tm, body, 0)

    @pl.when(i < n_active)
    def _():
        for k in range(rows_per_step):
            r = c * rows_per_step + k
            request(i + ahead, r).start()
            send(i - 1, r).start()

        h = hbuf[...]
        a = _dot(h, wg_ref[0].astype(BF16))
        b = _dot(h, wu_ref[0].astype(BF16))
        obuf[slot] += _dot(_silu_mul(a, b).astype(BF16), wd_ref[0].astype(BF16))


def _moe_grouped(h_all, tile_expert, n_active, src, dst, wg, wu, wd, tm, fc, y_rows, fill_row,
                 fill_rows):
    ne, d, dff = wg.shape
    n_tiles = tile_expert.shape[0]
    nc = dff // fc
    assert dff % fc == 0
    assert src.shape[0] == n_tiles * tm and dst.shape[0] == n_tiles * tm
    assert fill_rows % FILL_ROWS == 0 and tm >= FILL_ROWS
    rows_per_step = tm // nc

    def w_col(i, c, te, na, src_, dst_):
        return (te[i], 0, jnp.where(i < na[0], c, nc - 1))

    def w_row(i, c, te, na, src_, dst_):
        return (te[i], jnp.where(i < na[0], c, nc - 1), 0)

    grid_spec = pltpu.PrefetchScalarGridSpec(
        num_scalar_prefetch=4,
        grid=(n_tiles, nc),
        in_specs=[
            pl.BlockSpec(memory_space=pl.ANY),
            pl.BlockSpec((1, d, fc), w_col),
            pl.BlockSpec((1, d, fc), w_col),
            pl.BlockSpec((1, fc, d), w_row),
        ],
        out_specs=pl.BlockSpec(memory_space=pl.ANY),
        scratch_shapes=[
            pltpu.VMEM((ROW_SLOTS, tm, d), F32),
            pltpu.VMEM((tm, d), BF16),
            pltpu.VMEM((ROW_SLOTS, tm, d), F32),
            pltpu.SemaphoreType.DMA((ROW_SLOTS,)),
            pltpu.SemaphoreType.DMA((ROW_SLOTS,)),
            pltpu.SemaphoreType.DMA,
        ],
    )
    return pl.pallas_call(
        functools.partial(_moe_grouped_kernel, tm=tm, nc=nc, rows_per_step=rows_per_step,
                          fill_row=fill_row, fill_blocks=fill_rows // FILL_ROWS),
        grid_spec=grid_spec,
        out_shape=jax.ShapeDtypeStruct((y_rows, d), F32),
        compiler_params=_params("arbitrary", "arbitrary"),
        name="moe_grouped",
    )(tile_expert, n_active, src, dst, h_all, wg, wu, wd)


def _combine_kernel(x_ref, gt_ref, route_ref, gf_ref, y0_ref, y1_ref, o_ref):
    route = route_ref[...]
    pad = jnp.zeros((LANES - route.shape[0], route.shape[1]), F32)
    cols = jnp.concatenate([route, pad], axis=0).T
    f = (cols[:, ROW_PROB:ROW_PROB + 1] * y0_ref[...]
         + cols[:, ROW_PROB + 1:ROW_PROB + 2] * y1_ref[...])
    o_ref[...] = _rms(x_ref[...] + gt_ref[0] * f, gf_ref[...])


def _combine(x2d, gt, route, g_final, y_tok, tile, rows_per_mod, tok_off, plane_rows):
    n, d = x2d.shape
    assert n % tile == 0 and tok_off % tile == 0 and plane_rows % tile == 0
    per_row_mod = gt.shape[1] > 1
    if per_row_mod:
        mod_spec = pl.BlockSpec((1, tile, d), lambda j: (0, j, 0))
    else:
        assert rows_per_mod % tile == 0
        mod_spec = pl.BlockSpec((1, 1, d), lambda j: (j // (rows_per_mod // tile), 0, 0))
    first = tok_off // tile
    second = (plane_rows + tok_off) // tile
    return pl.pallas_call(
        _combine_kernel,
        grid=(n // tile,),
        in_specs=[
            pl.BlockSpec((tile, d), lambda j: (j, 0)),
            mod_spec,
            pl.BlockSpec((ROUTE_ROWS, tile), lambda j: (0, j)),
            pl.BlockSpec((1, d), lambda j: (0, 0)),
            pl.BlockSpec((tile, d), lambda j: (first + j, 0)),
            pl.BlockSpec((tile, d), lambda j: (second + j, 0)),
        ],
        out_specs=pl.BlockSpec((tile, d), lambda j: (j, 0)),
        out_shape=jax.ShapeDtypeStruct((n, d), F32),
        compiler_params=_params("arbitrary"),
        name="combine",
    )(x2d, gt, route, g_final.reshape(1, d), y_tok, y_tok)


def _invert_kernel(perm_ref, inv_ref):
    def place(a, carry):
        inv_ref[perm_ref[a]] = a
        return carry

    lax.fori_loop(0, perm_ref.shape[0], place, 0, unroll=32)


def _invert(perm):
    assert perm.shape[0] % 32 == 0
    smem = pl.BlockSpec(memory_space=pltpu.SMEM)
    return pl.pallas_call(
        _invert_kernel,
        in_specs=[smem],
        out_specs=smem,
        out_shape=jax.ShapeDtypeStruct(perm.shape, I32),
        name="invert",
    )(perm)


def _routing_tables(route_all, counts, tm, n_tiles, plane_rows):
    ne = counts.shape[0]
    n = route_all.shape[1]
    e_idx = route_all[ROW_EXPERT:ROW_EXPERT + TOP_K].astype(I32)
    rank = route_all[ROW_RANK:ROW_RANK + TOP_K].astype(I32)
    padded = (counts + tm - 1) // tm * tm
    ends = jnp.cumsum(padded)
    offs = ends - padded
    onehot = e_idx[..., None] == jnp.arange(ne, dtype=I32)
    pos = jnp.sum(jnp.where(onehot, offs, 0), axis=-1) + rank
    n_active = ends[-1] // tm
    starts = jnp.arange(n_tiles, dtype=I32) * tm
    tile_expert = jnp.sum(starts[:, None] >= ends[None, :], axis=1).astype(I32)
    last_expert = jnp.sum((n_active - 1) * tm >= ends).astype(I32)
    tile_expert = jnp.minimum(tile_expert, last_expert)
    n_rows = n_tiles * tm
    n_real = n * TOP_K
    pad_ends = jnp.cumsum(padded - counts)
    j = jnp.arange(n_rows - n_real, dtype=I32)[:, None]
    in_expert = jnp.logical_and(j >= pad_ends - (padded - counts), j < pad_ends)
    pad_pos = jnp.where(j[:, 0] < pad_ends[-1],
                        jnp.sum(jnp.where(in_expert, offs + counts + j - pad_ends + padded - counts, 0),
                                axis=1),
                        ends[-1] + j[:, 0] - pad_ends[-1])
    inv = _invert(jnp.concatenate([pos.reshape(-1), pad_pos]))
    is_real = inv < n_real
    tok = inv % n
    choice = inv // n
    spare = n + jnp.minimum(inv - n_real, ne * tm - 1)
    src = jnp.where(is_real, tok, 0)
    dst = jnp.where(is_real, choice * plane_rows + tok, spare)
    lead = n + ne * tm + jnp.arange(tm, dtype=I32)
    dst = jnp.concatenate([lead, dst])[:n_rows]
    return src, dst, tile_expert, n_active.astype(I32).reshape(1)


def _moe_layer(xp, xs, mod_p, mod_s, g, router_w, wg, wu, wd, g_final):
    nb, seq, d = xp.shape
    ns = xs.shape[0]
    ne = router_w.shape[-1]
    n_all = nb * seq + ns
    sh_p, sc_p, gt_p = mod_p
    sh_s, sc_s, gt_s = mod_s
    xp2 = xp.reshape(nb * seq, d)

    zero_cnt = jnp.zeros((ne, LANES), F32)
    h_all, route_p, cnt_p = _router(xp2, sh_p, sc_p, g, router_w, zero_cnt, ROUTER_TILE, seq,
                                    n_all)
    h_all, route_s, cnt_s = _router(xs, sh_s, sc_s, g, router_w, cnt_p, ns, ns, n_all,
                                    h_prev=h_all, row_off=nb * seq)

    tm = MOE_ROW_TILE
    n_tiles = (n_all * TOP_K) // tm + ne + ROW_SLOTS
    plane_rows = -(-(n_all + (ne + 1) * tm) // COMBINE_TILE) * COMBINE_TILE
    route_all = jnp.concatenate([route_p, route_s], axis=1)
    src, dst, tile_expert, n_active = _routing_tables(
        route_all, cnt_s[:, 0].astype(I32), tm, n_tiles, plane_rows)

    y_tok = _moe_grouped(h_all, tile_expert, n_active, src, dst, wg, wu, wd, tm, MOE_FF_CHUNK,
                         plane_rows + n_all, n_all, plane_rows - n_all)
    yp = _combine(xp2, gt_p, route_p, g_final, y_tok, COMBINE_TILE, seq, 0, plane_rows)
    ys = _combine(xs, gt_s, route_s, g_final, y_tok, ns, ns, nb * seq, plane_rows)
    return yp.reshape(nb, seq, d), ys


def kernel(x_prompt, x_sample, c_prompt, c_sample, state_conv, state_pool, w_ada, b_ada, g_mix,
           g_ffn, w_in, conv_w, pool_w, pool_scale, g_conv_out, g_pool_out, w_out, dense_w_gate,
           dense_w_up, dense_w_down, router_w, moe_w_gate, moe_w_up, moe_w_down, g_final):
    depth = w_ada.shape[0]
    nb, seq, d = x_prompt.shape
    ns = x_sample.shape[0]
    assert x_sample.shape[1] == 1 and depth == 2
    assert sum(DENSE_FF_CHUNKS) == dense_w_gate.shape[-1]

    mod = _ada(jnp.concatenate([c_prompt, c_sample], axis=0), w_ada, b_ada)
    mod = mod.reshape(depth, nb + ns, 6, d)
    mod_p = [jnp.transpose(mod[i, :nb], (1, 0, 2))[:, :, None, :] for i in range(depth)]
    mod_s = [jnp.transpose(mod[i, nb:], (1, 0, 2))[:, None, :, :] for i in range(depth)]

    w_in_b = w_in.astype(BF16)
    w_out_b = w_out.astype(BF16)
    pool_w_b = pool_w.astype(BF16)
    dense_b = [w.astype(BF16) for w in (dense_w_gate, dense_w_up, dense_w_down)]
    cb_t = jnp.transpose(state_conv, (0, 2, 1, 3))
    pb_t = jnp.transpose(state_pool, (0, 2, 1, 3))

    xp = x_prompt
    xs = x_sample.reshape(ns, d)
    conv_p, pool_p, conv_s, pool_s = [], [], [], []
    for i in range(depth):
        sh1, sc1, gt1, sh2, sc2, gt2 = mod_p[i]
        mix_w = (g_mix[i], w_in_b[i], conv_w[i], pool_w_b[i], pool_scale[i], g_conv_out[i],
                 g_pool_out[i], w_out_b[i])
        xp, cs, ps = _mix_prompt(xp, sh1, sc1, gt1, *mix_w, tile=MIX_TILE, sub=MIX_ROW_BLOCK)
        conv_p.append(cs)
        pool_p.append(ps)
        s1, c1, t1, s2, c2, t2 = mod_s[i]
        xs, v_new, u_new = _mix_sample(xs, s1[0], c1[0], t1[0], *mix_w, cb_t[i], pb_t[i])
        conv_s.append(jnp.concatenate([state_conv[i][:, 1:], v_new[:, None, :]], axis=1))
        pool_s.append(jnp.concatenate([state_pool[i][:, 1:], u_new[:, None, :]], axis=1))
        j = i // 2
        if i % 2 == 0:
            wg, wu, wd = (w[j] for w in dense_b)
            xp = _ffn(xp, sh2, sc2, gt2, g_ffn[i], wg, wu, wd, PROMPT_TILE, DENSE_FF_CHUNKS)
            xs = _ffn(xs[None], s2, c2, t2, g_ffn[i], wg, wu, wd, ns, DENSE_FF_CHUNKS)[0]
        else:
            xp, xs = _moe_layer(xp, xs, (sh2, sc2, gt2), (s2, c2, t2), g_ffn[i], router_w[j],
                                moe_w_gate[j], moe_w_up[j], moe_w_down[j], g_final)

    return (xp, xs.reshape(ns, 1, d), jnp.stack(conv_p), jnp.stack(pool_p),
            jnp.stack(conv_s), jnp.stack(pool_s))
```

```python
import functools

import jax
import jax.numpy as jnp
from jax import lax
from jax.experimental import pallas as pl
from jax.experimental.pallas import tpu as pltpu

F32 = jnp.float32
BF16 = jnp.bfloat16
I32 = jnp.int32

EPS = 1e-6
CONV_K = 3
POOL_WINDOWS = (2, 4, 8, 16)
POOL_HIST = max(POOL_WINDOWS) - 1
TOP_K = 2

CONV_PAD = 8
POOL_PAD = 16

LANES = 128

ROUTE_ROWS = 8
ROW_EXPERT, ROW_RANK, ROW_PROB = 0, 2, 4

VMEM_LIMIT_BYTES = 56 * 1024 * 1024

ADA_COL_TILE = 2048
PROMPT_TILE = 512
MIX_TILE = 2048
MIX_ROW_BLOCK = 512
ROUTER_TILE = 1024
COMBINE_TILE = 1024
MOE_ROW_TILE = 1280
MOE_FF_CHUNK = 512
FILL_ROWS = 128
ROW_SLOTS = 3
DENSE_FF_CHUNKS = (768, 768, 768, 512)


def _params(*sem):
    return pltpu.CompilerParams(dimension_semantics=sem, vmem_limit_bytes=VMEM_LIMIT_BYTES)


def _resident(shape):
    nd = len(shape)
    return pl.BlockSpec(shape, lambda *_: (0,) * nd, pipeline_mode=pl.Buffered(1))


def _rms(x, g):
    ms = jnp.mean(x * x, axis=-1, keepdims=True)
    return x * lax.rsqrt(ms + EPS) * g


def _mod_norm(x, g, sc, sh):
    ms = jnp.mean(x * x, axis=-1, keepdims=True)
    return x * lax.rsqrt(ms + EPS) * (g * (1.0 + sc)) + sh


def _dot(a, b):
    return jnp.dot(a, b, preferred_element_type=F32)


def _dot_nt(a, b):
    return lax.dot_general(a, b, (((1,), (1,)), ((), ())), preferred_element_type=F32)


def _silu_mul(a, b):
    return a * jax.nn.sigmoid(a) * b


def _ada_kernel(c_ref, w_ref, b_ref, o_ref):
    c = c_ref[...]
    a = (c * jax.nn.sigmoid(c)).astype(BF16)
    o_ref[0] = _dot(a, w_ref[0].astype(BF16)) + b_ref[0]


def _ada(c_all, w_ada, b_ada):
    depth, d, n = w_ada.shape
    m = c_all.shape[0]
    tn = ADA_COL_TILE
    return pl.pallas_call(
        _ada_kernel,
        grid=(depth, n // tn),
        in_specs=[
            pl.BlockSpec((m, d), lambda i, j: (0, 0)),
            pl.BlockSpec((1, d, tn), lambda i, j: (i, 0, j)),
            pl.BlockSpec((1, 1, tn), lambda i, j: (i, 0, j)),
        ],
        out_specs=pl.BlockSpec((1, m, tn), lambda i, j: (i, 0, j)),
        out_shape=jax.ShapeDtypeStruct((depth, m, n), F32),
        compiler_params=_params("arbitrary", "arbitrary"),
        name="ada",
    )(c_all, w_ada, b_ada.reshape(depth, 1, n))


def _mix_tail(x, gt, bg, y, d_groups, poolw_ref, pscale_ref, gco_ref, gpo_ref, wout_ref):
    cw = gco_ref.shape[-1]
    ya = bg * y
    yb = jnp.concatenate(
        [_dot(d.astype(BF16), poolw_ref[g]) for g, d in enumerate(d_groups)], axis=-1
    ) * pscale_ref[...]
    ma = _rms(ya, gco_ref[...]).astype(BF16)
    mb = _rms(yb, gpo_ref[...]).astype(BF16)
    o = _dot(ma, wout_ref[0:cw, :]) + _dot(mb, wout_ref[cw:, :])
    return x + gt * o


def _mix_prompt_kernel(x_ref, sh_ref, sc_ref, gt_ref, g_ref, win_ref, convw_ref, poolw_ref,
                       pscale_ref, gco_ref, gpo_ref, wout_ref,
                       xo_ref, cs_ref, ps_ref, vbuf, ubuf, *, tile, sub):
    l = pl.program_id(1)
    cw = convw_ref.shape[-1]
    pg = poolw_ref.shape[-1]

    @pl.when(l == 0)
    def _():
        vbuf[0:CONV_PAD, :] = jnp.zeros((CONV_PAD, cw), F32)
        ubuf[0:POOL_PAD, :] = jnp.zeros((POOL_PAD, ubuf.shape[-1]), F32)

    gates = []
    for lo in range(0, tile, sub):
        x = x_ref[0, lo:lo + sub, :]
        h = _mod_norm(x, g_ref[...], sc_ref[0], sh_ref[0]).astype(BF16)
        p = _dot(h, win_ref[...])
        gates.append(p[:, 0:cw])
        vbuf[CONV_PAD + lo:CONV_PAD + lo + sub, :] = p[:, cw:2 * cw] * p[:, 2 * cw:3 * cw]
        ubuf[POOL_PAD + lo:POOL_PAD + lo + sub, :] = p[:, 3 * cw:]

    w = convw_ref[...]
    for bg, lo in zip(gates, range(0, tile, sub)):
        y = w[CONV_K - 1:CONV_K] * vbuf[CONV_PAD + lo:CONV_PAD + lo + sub, :]
        for k in range(1, CONV_K):
            y = y + w[CONV_K - 1 - k:CONV_K - k] * vbuf[CONV_PAD + lo - k:CONV_PAD + lo - k + sub, :]

        pos = l * tile + lo + lax.broadcasted_iota(I32, (sub, 1), 0)
        d_groups = []
        for g, win in enumerate(POOL_WINDOWS):
            ug = ubuf[POOL_PAD + lo:POOL_PAD + lo + sub, g * pg:(g + 1) * pg]
            acc = ug
            for k in range(1, win):
                acc = acc + ubuf[POOL_PAD + lo - k:POOL_PAD + lo - k + sub, g * pg:(g + 1) * pg]
            cnt = jnp.minimum(pos + 1, win).astype(F32)
            d_groups.append(acc / cnt - ug)

        xo_ref[0, lo:lo + sub, :] = _mix_tail(x_ref[0, lo:lo + sub, :], gt_ref[0], bg, y, d_groups,
                                              poolw_ref, pscale_ref, gco_ref, gpo_ref, wout_ref)

    cs_ref[0] = vbuf[CONV_PAD + tile - (CONV_K - 1):CONV_PAD + tile, :]
    ps_ref[0] = ubuf[POOL_PAD + tile - POOL_HIST:POOL_PAD + tile, :]
    vbuf[0:CONV_PAD, :] = vbuf[tile:tile + CONV_PAD, :]
    ubuf[0:POOL_PAD, :] = ubuf[tile:tile + POOL_PAD, :]


def _mix_prompt(x, sh, sc, gt, g, w_in, conv_w, pool_w, pool_scale, g_co, g_po, w_out, tile, sub):
    b, seq, d = x.shape
    assert seq % tile == 0 and tile % sub == 0 and sub >= POOL_PAD
    cw = conv_w.shape[-1]
    pw = pool_scale.shape[-1]
    row = lambda bi, li: (bi, 0, 0)
    return pl.pallas_call(
        functools.partial(_mix_prompt_kernel, tile=tile, sub=sub),
        grid=(b, seq // tile),
        in_specs=[
            pl.BlockSpec((1, tile, d), lambda bi, li: (bi, li, 0)),
            pl.BlockSpec((1, 1, d), row),
            pl.BlockSpec((1, 1, d), row),
            pl.BlockSpec((1, 1, d), row),
            _resident((1, d)),
            _resident(w_in.shape),
            _resident(conv_w.shape),
            _resident(pool_w.shape),
            _resident((1, pw)),
            _resident((1, cw)),
            _resident((1, pw)),
            _resident(w_out.shape),
        ],
        out_specs=[
            pl.BlockSpec((1, tile, d), lambda bi, li: (bi, li, 0)),
            pl.BlockSpec((1, CONV_K - 1, cw), row),
            pl.BlockSpec((1, POOL_HIST, pw), row),
        ],
        out_shape=[
            jax.ShapeDtypeStruct((b, seq, d), F32),
            jax.ShapeDtypeStruct((b, CONV_K - 1, cw), F32),
            jax.ShapeDtypeStruct((b, POOL_HIST, pw), F32),
        ],
        scratch_shapes=[
            pltpu.VMEM((CONV_PAD + tile, cw), F32),
            pltpu.VMEM((POOL_PAD + tile, pw), F32),
        ],
        compiler_params=_params("arbitrary", "arbitrary"),
        name="mix_prompt",
    )(x, sh, sc, gt, g.reshape(1, d), w_in, conv_w, pool_w, pool_scale.reshape(1, pw),
      g_co.reshape(1, cw), g_po.reshape(1, pw), w_out)


def _mix_sample_kernel(x_ref, sh_ref, sc_ref, gt_ref, g_ref, win_ref, convw_ref, poolw_ref,
                       pscale_ref, gco_ref, gpo_ref, wout_ref, cb_ref, pb_ref,
                       xo_ref, v_ref, u_ref):
    cw = convw_ref.shape[-1]
    pg = poolw_ref.shape[-1]
    x = x_ref[...]
    h = _mod_norm(x, g_ref[...], sc_ref[...], sh_ref[...]).astype(BF16)
    p = _dot(h, win_ref[...])
    bg = p[:, 0:cw]
    v = p[:, cw:2 * cw] * p[:, 2 * cw:3 * cw]
    u = p[:, 3 * cw:]
    v_ref[...] = v
    u_ref[...] = u

    w = convw_ref[...]
    y = w[CONV_K - 1:CONV_K] * v
    for k in range(1, CONV_K):
        y = y + w[CONV_K - 1 - k:CONV_K - k] * cb_ref[CONV_K - 1 - k]

    d_groups = []
    for g, win in enumerate(POOL_WINDOWS):
        ug = u[:, g * pg:(g + 1) * pg]
        acc = ug
        for k in range(1, win):
            acc = acc + pb_ref[POOL_HIST - k, :, g * pg:(g + 1) * pg]
        d_groups.append(acc / float(win) - ug)

    xo_ref[...] = _mix_tail(x, gt_ref[...], bg, y, d_groups, poolw_ref, pscale_ref,
                            gco_ref, gpo_ref, wout_ref)


def _mix_sample(x, sh, sc, gt, g, w_in, conv_w, pool_w, pool_scale, g_co, g_po, w_out, cb, pb):
    n, d = x.shape
    cw = conv_w.shape[-1]
    pw = pool_scale.shape[-1]
    return pl.pallas_call(
        _mix_sample_kernel,
        out_shape=[
            jax.ShapeDtypeStruct((n, d), F32),
            jax.ShapeDtypeStruct((n, cw), F32),
            jax.ShapeDtypeStruct((n, pw), F32),
        ],
        compiler_params=pltpu.CompilerParams(vmem_limit_bytes=VMEM_LIMIT_BYTES),
        name="mix_sample",
    )(x, sh, sc, gt, g.reshape(1, d), w_in, conv_w, pool_w, pool_scale.reshape(1, pw),
      g_co.reshape(1, cw), g_po.reshape(1, pw), w_out, cb, pb)


def _ffn_kernel(x_ref, sh_ref, sc_ref, gt_ref, g_ref, wg_ref, wu_ref, wd_ref, o_ref, *, chunks):
    x = x_ref[0]
    h = _mod_norm(x, g_ref[...], sc_ref[0], sh_ref[0]).astype(BF16)
    acc = None
    lo = 0
    for fc in chunks:
        a = _dot(h, wg_ref[:, lo:lo + fc])
        b = _dot(h, wu_ref[:, lo:lo + fc])
        part = _dot(_silu_mul(a, b).astype(BF16), wd_ref[lo:lo + fc, :])
        acc = part if acc is None else acc + part
        lo += fc
    o_ref[0] = x + gt_ref[0] * acc


def _ffn(x, sh, sc, gt, g, wg, wu, wd, tile, chunks):
    b, seq, d = x.shape
    tm = sh.shape[1]
    tmod = tile if tm > 1 else 1
    mod_map = (lambda bi, li: (bi, li, 0)) if tm > 1 else (lambda bi, li: (bi, 0, 0))
    return pl.pallas_call(
        functools.partial(_ffn_kernel, chunks=chunks),
        grid=(b, seq // tile),
        in_specs=[
            pl.BlockSpec((1, tile, d), lambda bi, li: (bi, li, 0)),
            pl.BlockSpec((1, tmod, d), mod_map),
            pl.BlockSpec((1, tmod, d), mod_map),
            pl.BlockSpec((1, tmod, d), mod_map),
            _resident((1, d)),
            _resident(wg.shape),
            _resident(wu.shape),
            _resident(wd.shape),
        ],
        out_specs=pl.BlockSpec((1, tile, d), lambda bi, li: (bi, li, 0)),
        out_shape=jax.ShapeDtypeStruct((b, seq, d), F32),
        compiler_params=_params("arbitrary", "arbitrary"),
        name="ffn",
    )(x, sh, sc, gt, g.reshape(1, d), wg, wu, wd)


def _split_bf16(a):
    hi = a.astype(BF16)
    return hi, (a - hi.astype(F32)).astype(BF16)


def _router_kernel(x_ref, sh_ref, sc_ref, g_ref, rw_ref, base_ref, h_ref, route_ref, cnt_ref,
                   tri, carry, *, n_steps):
    t = x_ref.shape[0]
    step = pl.program_id(0)

    @pl.when(step == 0)
    def _():
        r = lax.broadcasted_iota(I32, (t, t), 0)
        c = lax.broadcasted_iota(I32, (t, t), 1)
        tri[...] = jnp.where(r < c, 1.0, 0.0).astype(BF16)
        carry[...] = base_ref[...]

    @pl.when(step >= n_steps)
    def _():
        h_ref[...] = jnp.zeros_like(h_ref)

    @pl.when(step < n_steps)
    def _():
        h = _mod_norm(x_ref[...], g_ref[...], sc_ref[0], sh_ref[0])
        h_ref[...] = h
        h_hi, h_lo = _split_bf16(h)
        rw_hi, rw_lo = _split_bf16(rw_ref[...])
        logits = _dot_nt(rw_hi, h_hi) + _dot_nt(rw_hi, h_lo) + _dot_nt(rw_lo, h_hi)

        ne = logits.shape[0]
        idx = lax.broadcasted_iota(I32, logits.shape, 0)
        m1 = jnp.max(logits, axis=0, keepdims=True)
        i1 = jnp.min(jnp.where(logits == m1, idx, ne), axis=0, keepdims=True)
        sel1 = idx == i1
        rest = jnp.where(sel1, -jnp.inf, logits)
        m2 = jnp.max(rest, axis=0, keepdims=True)
        i2 = jnp.min(jnp.where(rest == m2, idx, ne), axis=0, keepdims=True)
        sel2 = idx == i2
        e = jnp.exp(m2 - m1)
        p1 = 1.0 / (1.0 + e)
        p2 = e / (1.0 + e)

        chosen = jnp.where(sel1, 1.0, jnp.where(sel2, 1.0, 0.0))
        before = _dot(chosen.astype(BF16), tri[...]) + carry[:, 0:1]
        r1 = jnp.sum(jnp.where(sel1, before, 0.0), axis=0, keepdims=True)
        r2 = jnp.sum(jnp.where(sel2, before, 0.0), axis=0, keepdims=True)
        carry[...] += jnp.sum(chosen, axis=1, keepdims=True)
        cnt_ref[...] = carry[...]

        rows = (i1.astype(F32), i2.astype(F32), r1, r2, p1, p2)
        route = jnp.zeros(logits.shape, F32)
        for k, row in enumerate(rows):
            route = jnp.where(idx == k, row, route)
        route_ref[...] = route


def _router(x2d, sh, sc, g, router_w, base_cnt, tile, rows_per_mod, h_rows, h_prev=None,
            row_off=0):
    n, d = x2d.shape
    ne = router_w.shape[-1]
    assert ne == ROUTE_ROWS and n % tile == 0 and row_off % tile == 0 and tile % LANES == 0
    assert base_cnt.shape == (ne, LANES)
    n_steps = n // tile
    tail = h_rows - (row_off + n)
    assert 0 <= tail < tile
    fill_tail = h_prev is None and tail > 0
    last = n_steps - 1
    clamp = (lambda i: jnp.minimum(i, last)) if fill_tail else (lambda i: i)
    per_row_mod = sh.shape[1] > 1
    if per_row_mod:
        mod_spec = pl.BlockSpec((1, tile, d), lambda i: (0, clamp(i), 0))
    else:
        assert rows_per_mod % tile == 0
        mod_spec = pl.BlockSpec((1, 1, d), lambda i: (clamp(i) // (rows_per_mod // tile), 0, 0))
    in_specs = [
        pl.BlockSpec((tile, d), lambda i: (clamp(i), 0)),
        mod_spec,
        mod_spec,
        _resident((1, d)),
        _resident((ne, d)),
        _resident((ne, LANES)),
    ]
    args = [x2d, sh, sc, g.reshape(1, d), router_w.T, base_cnt]
    n_in = len(args)
    aliases = {}
    body = functools.partial(_router_kernel, n_steps=n_steps)
    if h_prev is not None:
        assert h_prev.shape == (h_rows, d)
        in_specs.append(pl.BlockSpec(memory_space=pl.ANY))
        args.append(h_prev)
        aliases = {n_in: 0}

        def body(*refs):
            _router_kernel(*refs[:n_in], *refs[n_in + 1:], n_steps=n_steps)

    blk_off = row_off // tile
    return pl.pallas_call(
        body,
        grid=(n_steps + int(fill_tail),),
        in_specs=in_specs,
        out_specs=[
            pl.BlockSpec((tile, d), lambda i: (i + blk_off, 0)),
            pl.BlockSpec((ne, tile), lambda i: (0, clamp(i))),
            pl.BlockSpec((ne, LANES), lambda i: (0, 0)),
        ],
        out_shape=[
            jax.ShapeDtypeStruct((h_rows, d), F32),
            jax.ShapeDtypeStruct((ne, n), F32),
            jax.ShapeDtypeStruct((ne, LANES), F32),
        ],
        scratch_shapes=[pltpu.VMEM((tile, tile), BF16), pltpu.VMEM((ne, LANES), F32)],
        input_output_aliases=aliases,
        compiler_params=_params("arbitrary"),
        name="router",
    )(*args)


def _row_copy(src_hbm, row, dst, dst_row, sem):
    return pltpu.make_async_copy(src_hbm.at[pl.ds(row, 1), :], dst.at[pl.ds(dst_row, 1), :], sem)


def _row_copy_out(src, src_row, dst_hbm, row, sem):
    return pltpu.make_async_copy(src.at[pl.ds(src_row, 1), :], dst_hbm.at[pl.ds(row, 1), :], sem)


def _moe_grouped_kernel(te_ref, na_ref, src_ref, dst_ref, h_hbm, wg_ref, wu_ref, wd_ref, y_hbm,
                        xs, hbuf, obuf, gsem, ssem, fsem, *, tm, nc, rows_per_step, fill_row,
                        fill_blocks):
    del te_ref
    i = pl.program_id(0)
    c = pl.program_id(1)
    n_active = na_ref[0]
    ahead = ROW_SLOTS - 1
    slot = i % ROW_SLOTS

    def request(tile, r):
        s = tile % ROW_SLOTS
        return _row_copy(h_hbm, src_ref[tile * tm + r], xs.at[s], r, gsem.at[s])

    def send(tile, r):
        s = (tile + ROW_SLOTS) % ROW_SLOTS
        return _row_copy_out(obuf.at[s], r, y_hbm, dst_ref[(tile + 1) * tm + r], ssem.at[s])

    @pl.when(c == 0)
    def _():
        @pl.when(i == 0)
        def _():
            last = ROW_SLOTS - 1
            obuf[last] = jnp.zeros((tm, obuf.shape[-1]), F32)
            fills = [pltpu.make_async_copy(obuf.at[last, pl.ds(0, FILL_ROWS), :],
                                           y_hbm.at[pl.ds(fill_row + q * FILL_ROWS, FILL_ROWS), :],
                                           fsem) for q in range(fill_blocks)]
            for f in fills:
                f.start()
            for f in fills:
                f.wait()

            def body(r, carry):
                for t in range(ahead):
                    request(t, r).start()
                return carry

            lax.fori_loop(0, tm, body, 0)

        @pl.when(i < n_active + ahead)
        def _():
            pltpu.make_async_copy(h_hbm.at[pl.ds(0, tm), :], xs.at[slot], gsem.at[slot]).wait()

        @pl.when(jnp.logical_and(i >= ahead, i < n_active + ROW_SLOTS))
        def _():
            pltpu.make_async_copy(obuf.at[slot], y_hbm.at[pl.ds(0, tm), :], ssem.at[slot]).wait()

        @pl.when(i < n_active)
        def _():
            obuf[slot] = jnp.zeros((tm, obuf.shape[-1]), F32)
            hbuf[...] = xs[slot].astype(BF16)
            for r in range(rows_per_step * nc, tm):
                request(i + ahead, r).start()
                send(i - 1, r).start()

        @pl.when(i == n_active)
        def _():
            def body(r, carry):
                send(i - 1, r).start()
                return carry

            lax.fori_loop(0, tm, body, 0)

    @pl.when(i < n_active)
    def _():
        for k in range(rows_per_step):
            r = c * rows_per_step + k
            request(i + ahead, r).start()
            send(i - 1, r).start()

        h = hbuf[...]
        a = _dot(h, wg_ref[0].astype(BF16))
        b = _dot(h, wu_ref[0].astype(BF16))
        obuf[slot] += _dot(_silu_mul(a, b).astype(BF16), wd_ref[0].astype(BF16))


def _moe_grouped(h_all, tile_expert, n_active, src, dst, wg, wu, wd, tm, fc, y_rows, fill_row,
                 fill_rows):
    ne, d, dff = wg.shape
    n_tiles = tile_expert.shape[0]
    nc = dff // fc
    assert dff % fc == 0
    assert src.shape[0] == n_tiles * tm and dst.shape[0] == n_tiles * tm
    assert fill_rows % FILL_ROWS == 0 and tm >= FILL_ROWS
    rows_per_step = tm // nc

    def w_col(i, c, te, na, src_, dst_):
        return (te[i], 0, jnp.where(i < na[0], c, nc - 1))

    def w_row(i, c, te, na, src_, dst_):
        return (te[i], jnp.where(i < na[0], c, nc - 1), 0)

    grid_spec = pltpu.PrefetchScalarGridSpec(
        num_scalar_prefetch=4,
        grid=(n_tiles, nc),
        in_specs=[
            pl.BlockSpec(memory_space=pl.ANY),
            pl.BlockSpec((1, d, fc), w_col),
            pl.BlockSpec((1, d, fc), w_col),
            pl.BlockSpec((1, fc, d), w_row),
        ],
        out_specs=pl.BlockSpec(memory_space=pl.ANY),
        scratch_shapes=[
            pltpu.VMEM((ROW_SLOTS, tm, d), F32),
            pltpu.VMEM((tm, d), BF16),
            pltpu.VMEM((ROW_SLOTS, tm, d), F32),
            pltpu.SemaphoreType.DMA((ROW_SLOTS,)),
            pltpu.SemaphoreType.DMA((ROW_SLOTS,)),
            pltpu.SemaphoreType.DMA,
        ],
    )
    return pl.pallas_call(
        functools.partial(_moe_grouped_kernel, tm=tm, nc=nc, rows_per_step=rows_per_step,
                          fill_row=fill_row, fill_blocks=fill_rows // FILL_ROWS),
        grid_spec=grid_spec,
        out_shape=jax.ShapeDtypeStruct((y_rows, d), F32),
        compiler_params=_params("arbitrary", "arbitrary"),
        name="moe_grouped",
    )(tile_expert, n_active, src, dst, h_all, wg, wu, wd)


def _combine_kernel(x_ref, gt_ref, route_ref, gf_ref, y0_ref, y1_ref, o_ref):
    route = route_ref[...]
    pad = jnp.zeros((LANES - route.shape[0], route.shape[1]), F32)
    cols = jnp.concatenate([route, pad], axis=0).T
    f = (cols[:, ROW_PROB:ROW_PROB + 1] * y0_ref[...]
         + cols[:, ROW_PROB + 1:ROW_PROB + 2] * y1_ref[...])
    o_ref[...] = _rms(x_ref[...] + gt_ref[0] * f, gf_ref[...])


def _combine(x2d, gt, route, g_final, y_tok, tile, rows_per_mod, tok_off, plane_rows):
    n, d = x2d.shape
    assert n % tile == 0 and tok_off % tile == 0 and plane_rows % tile == 0
    per_row_mod = gt.shape[1] > 1
    if per_row_mod:
        mod_spec = pl.BlockSpec((1, tile, d), lambda j: (0, j, 0))
    else:
        assert rows_per_mod % tile == 0
        mod_spec = pl.BlockSpec((1, 1, d), lambda j: (j // (rows_per_mod // tile), 0, 0))
    first = tok_off // tile
    second = (plane_rows + tok_off) // tile
    return pl.pallas_call(
        _combine_kernel,
        grid=(n // tile,),
        in_specs=[
            pl.BlockSpec((tile, d), lambda j: (j, 0)),
            mod_spec,
            pl.BlockSpec((ROUTE_ROWS, tile), lambda j: (0, j)),
            pl.BlockSpec((1, d), lambda j: (0, 0)),
            pl.BlockSpec((tile, d), lambda j: (first + j, 0)),
            pl.BlockSpec((tile, d), lambda j: (second + j, 0)),
        ],
        out_specs=pl.BlockSpec((tile, d), lambda j: (j, 0)),
        out_shape=jax.ShapeDtypeStruct((n, d), F32),
        compiler_params=_params("arbitrary"),
        name="combine",
    )(x2d, gt, route, g_final.reshape(1, d), y_tok, y_tok)


def _invert_kernel(perm_ref, inv_ref):
    def place(a, carry):
        inv_ref[perm_ref[a]] = a
        return carry

    lax.fori_loop(0, perm_ref.shape[0], place, 0, unroll=32)


def _invert(perm):
    assert perm.shape[0] % 32 == 0
    smem = pl.BlockSpec(memory_space=pltpu.SMEM)
    return pl.pallas_call(
        _invert_kernel,
        in_specs=[smem],
        out_specs=smem,
        out_shape=jax.ShapeDtypeStruct(perm.shape, I32),
        name="invert",
    )(perm)


def _routing_tables(route_all, counts, tm, n_tiles, plane_rows):
    ne = counts.shape[0]
    n = route_all.shape[1]
    e_idx = route_all[ROW_EXPERT:ROW_EXPERT + TOP_K].astype(I32)
    rank = route_all[ROW_RANK:ROW_RANK + TOP_K].astype(I32)
    padded = (counts + tm - 1) // tm * tm
    ends = jnp.cumsum(padded)
    offs = ends - padded
    onehot = e_idx[..., None] == jnp.arange(ne, dtype=I32)
    pos = jnp.sum(jnp.where(onehot, offs, 0), axis=-1) + rank
    n_active = ends[-1] // tm
    starts = jnp.arange(n_tiles, dtype=I32) * tm
    tile_expert = jnp.sum(starts[:, None] >= ends[None, :], axis=1).astype(I32)
    last_expert = jnp.sum((n_active - 1) * tm >= ends).astype(I32)
    tile_expert = jnp.minimum(tile_expert, last_expert)
    n_rows = n_tiles * tm
    n_real = n * TOP_K
    pad_ends = jnp.cumsum(padded - counts)
    j = jnp.arange(n_rows - n_real, dtype=I32)[:, None]
    in_expert = jnp.logical_and(j >= pad_ends - (padded - counts), j < pad_ends)
    pad_pos = jnp.where(j[:, 0] < pad_ends[-1],
                        jnp.sum(jnp.where(in_expert, offs + counts + j - pad_ends + padded - counts, 0),
                                axis=1),
                        ends[-1] + j[:, 0] - pad_ends[-1])
    inv = _invert(jnp.concatenate([pos.reshape(-1), pad_pos]))
    is_real = inv < n_real
    tok = inv % n
    choice = inv // n
    spare = n + jnp.minimum(inv - n_real, ne * tm - 1)
    src = jnp.where(is_real, tok, 0)
    dst = jnp.where(is_real, choice * plane_rows + tok, spare)
    lead = n + ne * tm + jnp.arange(tm, dtype=I32)
    dst = jnp.concatenate([lead, dst])[:n_rows]
    return src, dst, tile_expert, n_active.astype(I32).reshape(1)


def _moe_layer(xp, xs, mod_p, mod_s, g, router_w, wg, wu, wd, g_final):
    nb, seq, d = xp.shape
    ns = xs.shape[0]
    ne = router_w.shape[-1]
    n_all = nb * seq + ns
    sh_p, sc_p, gt_p = mod_p
    sh_s, sc_s, gt_s = mod_s
    xp2 = xp.reshape(nb * seq, d)

    zero_cnt = jnp.zeros((ne, LANES), F32)
    h_all, route_p, cnt_p = _router(xp2, sh_p, sc_p, g, router_w, zero_cnt, ROUTER_TILE, seq,
                                    n_all)
    h_all, route_s, cnt_s = _router(xs, sh_s, sc_s, g, router_w, cnt_p, ns, ns, n_all,
                                    h_prev=h_all, row_off=nb * seq)

    tm = MOE_ROW_TILE
    n_tiles = (n_all * TOP_K) // tm + ne + ROW_SLOTS
    plane_rows = -(-(n_all + (ne + 1) * tm) // COMBINE_TILE) * COMBINE_TILE
    route_all = jnp.concatenate([route_p, route_s], axis=1)
    src, dst, tile_expert, n_active = _routing_tables(
        route_all, cnt_s[:, 0].astype(I32), tm, n_tiles, plane_rows)

    y_tok = _moe_grouped(h_all, tile_expert, n_active, src, dst, wg, wu, wd, tm, MOE_FF_CHUNK,
                         plane_rows + n_all, n_all, plane_rows - n_all)
    yp = _combine(xp2, gt_p, route_p, g_final, y_tok, COMBINE_TILE, seq, 0, plane_rows)
    ys = _combine(xs, gt_s, route_s, g_final, y_tok, ns, ns, nb * seq, plane_rows)
    return yp.reshape(nb, seq, d), ys


def kernel(x_prompt, x_sample, c_prompt, c_sample, state_conv, state_pool, w_ada, b_ada, g_mix,
           g_ffn, w_in, conv_w, pool_w, pool_scale, g_conv_out, g_pool_out, w_out, dense_w_gate,
           dense_w_up, dense_w_down, router_w, moe_w_gate, moe_w_up, moe_w_down, g_final):
    depth = w_ada.shape[0]
    nb, seq, d = x_prompt.shape
    ns = x_sample.shape[0]
    assert x_sample.shape[1] == 1 and depth == 2
    assert sum(DENSE_FF_CHUNKS) == dense_w_gate.shape[-1]

    mod = _ada(jnp.concatenate([c_prompt, c_sample], axis=0), w_ada, b_ada)
    mod = mod.reshape(depth, nb + ns, 6, d)
    mod_p = [jnp.transpose(mod[i, :nb], (1, 0, 2))[:, :, None, :] for i in range(depth)]
    mod_s = [jnp.transpose(mod[i, nb:], (1, 0, 2))[:, None, :, :] for i in range(depth)]

    w_in_b = w_in.astype(BF16)
    w_out_b = w_out.astype(BF16)
    pool_w_b = pool_w.astype(BF16)
    dense_b = [w.astype(BF16) for w in (dense_w_gate, dense_w_up, dense_w_down)]
    cb_t = jnp.transpose(state_conv, (0, 2, 1, 3))
    pb_t = jnp.transpose(state_pool, (0, 2, 1, 3))

    xp = x_prompt
    xs = x_sample.reshape(ns, d)
    conv_p, pool_p, conv_s, pool_s = [], [], [], []
    for i in range(depth):
        sh1, sc1, gt1, sh2, sc2, gt2 = mod_p[i]
        mix_w = (g_mix[i], w_in_b[i], conv_w[i], pool_w_b[i], pool_scale[i], g_conv_out[i],
                 g_pool_out[i], w_out_b[i])
        xp, cs, ps = _mix_prompt(xp, sh1, sc1, gt1, *mix_w, tile=MIX_TILE, sub=MIX_ROW_BLOCK)
        conv_p.append(cs)
        pool_p.append(ps)
        s1, c1, t1, s2, c2, t2 = mod_s[i]
        xs, v_new, u_new = _mix_sample(xs, s1[0], c1[0], t1[0], *mix_w, cb_t[i], pb_t[i])
        conv_s.append(jnp.concatenate([state_conv[i][:, 1:], v_new[:, None, :]], axis=1))
        pool_s.append(jnp.concatenate([state_pool[i][:, 1:], u_new[:, None, :]], axis=1))
        j = i // 2
        if i % 2 == 0:
            wg, wu, wd = (w[j] for w in dense_b)
            xp = _ffn(xp, sh2, sc2, gt2, g_ffn[i], wg, wu, wd, PROMPT_TILE, DENSE_FF_CHUNKS)
            xs = _ffn(xs[None], s2, c2, t2, g_ffn[i], wg, wu, wd, ns, DENSE_FF_CHUNKS)[0]
        else:
            xp, xs = _moe_layer(xp, xs, (sh2, sc2, gt2), (s2, c2, t2), g_ffn[i], router_w[j],
                                moe_w_gate[j], moe_w_up[j], moe_w_down[j], g_final)

    return (xp, xs.reshape(ns, 1, d), jnp.stack(conv_p), jnp.stack(pool_p),
            jnp.stack(conv_s), jnp.stack(pool_s))
```

```python
import functools

import jax
import jax.numpy as jnp
from jax import lax
from jax.experimental import pallas as pl
from jax.experimental.pallas import tpu as pltpu

F32 = jnp.float32
BF16 = jnp.bfloat16
I32 = jnp.int32

EPS = 1e-6
CONV_K = 3
POOL_WINDOWS = (2, 4, 8, 16)
POOL_HIST = max(POOL_WINDOWS) - 1
TOP_K = 2

CONV_PAD = 8
POOL_PAD = 16

LANES = 128

ROUTE_ROWS = 8
ROW_EXPERT, ROW_RANK, ROW_PROB = 0, 2, 4

VMEM_LIMIT_BYTES = 56 * 1024 * 1024

ADA_COL_TILE = 2048
PROMPT_TILE = 512
MIX_TILE = 2048
MIX_ROW_BLOCK = 512
ROUTER_TILE = 1024
COMBINE_TILE = 1024
MOE_ROW_TILE = 1024
MOE_FF_CHUNK = 512
FILL_ROWS = 128
ROW_SLOTS = 3
DENSE_FF_CHUNKS = (768, 768, 768, 512)


def _params(*sem):
    return pltpu.CompilerParams(dimension_semantics=sem, vmem_limit_bytes=VMEM_LIMIT_BYTES)


def _resident(shape):
    nd = len(shape)
    return pl.BlockSpec(shape, lambda *_: (0,) * nd, pipeline_mode=pl.Buffered(1))


def _rms(x, g):
    ms = jnp.mean(x * x, axis=-1, keepdims=True)
    return x * lax.rsqrt(ms + EPS) * g


def _mod_norm(x, g, sc, sh):
    ms = jnp.mean(x * x, axis=-1, keepdims=True)
    return x * lax.rsqrt(ms + EPS) * (g * (1.0 + sc)) + sh


def _dot(a, b):
    return jnp.dot(a, b, preferred_element_type=F32)


def _dot_nt(a, b):
    return lax.dot_general(a, b, (((1,), (1,)), ((), ())), preferred_element_type=F32)


def _silu_mul(a, b):
    return a * jax.nn.sigmoid(a) * b


def _ada_kernel(c_ref, w_ref, b_ref, o_ref):
    c = c_ref[...]
    a = (c * jax.nn.sigmoid(c)).astype(BF16)
    o_ref[0] = _dot(a, w_ref[0].astype(BF16)) + b_ref[0]


def _ada(c_all, w_ada, b_ada):
    depth, d, n = w_ada.shape
    m = c_all.shape[0]
    tn = ADA_COL_TILE
    return pl.pallas_call(
        _ada_kernel,
        grid=(depth, n // tn),
        in_specs=[
            pl.BlockSpec((m, d), lambda i, j: (0, 0)),
            pl.BlockSpec((1, d, tn), lambda i, j: (i, 0, j)),
            pl.BlockSpec((1, 1, tn), lambda i, j: (i, 0, j)),
        ],
        out_specs=pl.BlockSpec((1, m, tn), lambda i, j: (i, 0, j)),
        out_shape=jax.ShapeDtypeStruct((depth, m, n), F32),
        compiler_params=_params("arbitrary", "arbitrary"),
        name="ada",
    )(c_all, w_ada, b_ada.reshape(depth, 1, n))


def _mix_tail(x, gt, bg, y, d_groups, poolw_ref, pscale_ref, gco_ref, gpo_ref, wout_ref):
    cw = gco_ref.shape[-1]
    ya = bg * y
    yb = jnp.concatenate(
        [_dot(d.astype(BF16), poolw_ref[g]) for g, d in enumerate(d_groups)], axis=-1
    ) * pscale_ref[...]
    ma = _rms(ya, gco_ref[...]).astype(BF16)
    mb = _rms(yb, gpo_ref[...]).astype(BF16)
    o = _dot(ma, wout_ref[0:cw, :]) + _dot(mb, wout_ref[cw:, :])
    return x + gt * o


def _mix_prompt_kernel(x_ref, sh_ref, sc_ref, gt_ref, g_ref, win_ref, convw_ref, poolw_ref,
                       pscale_ref, gco_ref, gpo_ref, wout_ref,
                       xo_ref, cs_ref, ps_ref, vbuf, ubuf, *, tile, sub):
    l = pl.program_id(1)
    cw = convw_ref.shape[-1]
    pg = poolw_ref.shape[-1]

    @pl.when(l == 0)
    def _():
        vbuf[0:CONV_PAD, :] = jnp.zeros((CONV_PAD, cw), F32)
        ubuf[0:POOL_PAD, :] = jnp.zeros((POOL_PAD, ubuf.shape[-1]), F32)

    gates = []
    for lo in range(0, tile, sub):
        x = x_ref[0, lo:lo + sub, :]
        h = _mod_norm(x, g_ref[...], sc_ref[0], sh_ref[0]).astype(BF16)
        p = _dot(h, win_ref[...])
        gates.append(p[:, 0:cw])
        vbuf[CONV_PAD + lo:CONV_PAD + lo + sub, :] = p[:, cw:2 * cw] * p[:, 2 * cw:3 * cw]
        ubuf[POOL_PAD + lo:POOL_PAD + lo + sub, :] = p[:, 3 * cw:]

    w = convw_ref[...]
    for bg, lo in zip(gates, range(0, tile, sub)):
        y = w[CONV_K - 1:CONV_K] * vbuf[CONV_PAD + lo:CONV_PAD + lo + sub, :]
        for k in range(1, CONV_K):
            y = y + w[CONV_K - 1 - k:CONV_K - k] * vbuf[CONV_PAD + lo - k:CONV_PAD + lo - k + sub, :]

        pos = l * tile + lo + lax.broadcasted_iota(I32, (sub, 1), 0)
        d_groups = []
        for g, win in enumerate(POOL_WINDOWS):
            ug = ubuf[POOL_PAD + lo:POOL_PAD + lo + sub, g * pg:(g + 1) * pg]
            acc = ug
            for k in range(1, win):
                acc = acc + ubuf[POOL_PAD + lo - k:POOL_PAD + lo - k + sub, g * pg:(g + 1) * pg]
            cnt = jnp.minimum(pos + 1, win).astype(F32)
            d_groups.append(acc / cnt - ug)

        xo_ref[0, lo:lo + sub, :] = _mix_tail(x_ref[0, lo:lo + sub, :], gt_ref[0], bg, y, d_groups,
                                              poolw_ref, pscale_ref, gco_ref, gpo_ref, wout_ref)

    cs_ref[0] = vbuf[CONV_PAD + tile - (CONV_K - 1):CONV_PAD + tile, :]
    ps_ref[0] = ubuf[POOL_PAD + tile - POOL_HIST:POOL_PAD + tile, :]
    vbuf[0:CONV_PAD, :] = vbuf[tile:tile + CONV_PAD, :]
    ubuf[0:POOL_PAD, :] = ubuf[tile:tile + POOL_PAD, :]


def _mix_prompt(x, sh, sc, gt, g, w_in, conv_w, pool_w, pool_scale, g_co, g_po, w_out, tile, sub):
    b, seq, d = x.shape
    assert seq % tile == 0 and tile % sub == 0 and sub >= POOL_PAD
    cw = conv_w.shape[-1]
    pw = pool_scale.shape[-1]
    row = lambda bi, li: (bi, 0, 0)
    return pl.pallas_call(
        functools.partial(_mix_prompt_kernel, tile=tile, sub=sub),
        grid=(b, seq // tile),
        in_specs=[
            pl.BlockSpec((1, tile, d), lambda bi, li: (bi, li, 0)),
            pl.BlockSpec((1, 1, d), row),
            pl.BlockSpec((1, 1, d), row),
            pl.BlockSpec((1, 1, d), row),
            _resident((1, d)),
            _resident(w_in.shape),
            _resident(conv_w.shape),
            _resident(pool_w.shape),
            _resident((1, pw)),
            _resident((1, cw)),
            _resident((1, pw)),
            _resident(w_out.shape),
        ],
        out_specs=[
            pl.BlockSpec((1, tile, d), lambda bi, li: (bi, li, 0)),
            pl.BlockSpec((1, CONV_K - 1, cw), row),
            pl.BlockSpec((1, POOL_HIST, pw), row),
        ],
        out_shape=[
            jax.ShapeDtypeStruct((b, seq, d), F32),
            jax.ShapeDtypeStruct((b, CONV_K - 1, cw), F32),
            jax.ShapeDtypeStruct((b, POOL_HIST, pw), F32),
        ],
        scratch_shapes=[
            pltpu.VMEM((CONV_PAD + tile, cw), F32),
            pltpu.VMEM((POOL_PAD + tile, pw), F32),
        ],
        compiler_params=_params("arbitrary", "arbitrary"),
        name="mix_prompt",
    )(x, sh, sc, gt, g.reshape(1, d), w_in, conv_w, pool_w, pool_scale.reshape(1, pw),
      g_co.reshape(1, cw), g_po.reshape(1, pw), w_out)


def _mix_sample_kernel(x_ref, sh_ref, sc_ref, gt_ref, g_ref, win_ref, convw_ref, poolw_ref,
                       pscale_ref, gco_ref, gpo_ref, wout_ref, cb_ref, pb_ref,
                       xo_ref, v_ref, u_ref):
    cw = convw_ref.shape[-1]
    pg = poolw_ref.shape[-1]
    x = x_ref[...]
    h = _mod_norm(x, g_ref[...], sc_ref[...], sh_ref[...]).astype(BF16)
    p = _dot(h, win_ref[...])
    bg = p[:, 0:cw]
    v = p[:, cw:2 * cw] * p[:, 2 * cw:3 * cw]
    u = p[:, 3 * cw:]
    v_ref[...] = v
    u_ref[...] = u

    w = convw_ref[...]
    y = w[CONV_K - 1:CONV_K] * v
    for k in range(1, CONV_K):
        y = y + w[CONV_K - 1 - k:CONV_K - k] * cb_ref[CONV_K - 1 - k]

    d_groups = []
    for g, win in enumerate(POOL_WINDOWS):
        ug = u[:, g * pg:(g + 1) * pg]
        acc = ug
        for k in range(1, win):
            acc = acc + pb_ref[POOL_HIST - k, :, g * pg:(g + 1) * pg]
        d_groups.append(acc / float(win) - ug)

    xo_ref[...] = _mix_tail(x, gt_ref[...], bg, y, d_groups, poolw_ref, pscale_ref,
                            gco_ref, gpo_ref, wout_ref)


def _mix_sample(x, sh, sc, gt, g, w_in, conv_w, pool_w, pool_scale, g_co, g_po, w_out, cb, pb):
    n, d = x.shape
    cw = conv_w.shape[-1]
    pw = pool_scale.shape[-1]
    return pl.pallas_call(
        _mix_sample_kernel,
        out_shape=[
            jax.ShapeDtypeStruct((n, d), F32),
            jax.ShapeDtypeStruct((n, cw), F32),
            jax.ShapeDtypeStruct((n, pw), F32),
        ],
        compiler_params=pltpu.CompilerParams(vmem_limit_bytes=VMEM_LIMIT_BYTES),
        name="mix_sample",
    )(x, sh, sc, gt, g.reshape(1, d), w_in, conv_w, pool_w, pool_scale.reshape(1, pw),
      g_co.reshape(1, cw), g_po.reshape(1, pw), w_out, cb, pb)


def _ffn_kernel(x_ref, sh_ref, sc_ref, gt_ref, g_ref, wg_ref, wu_ref, wd_ref, o_ref, *, chunks):
    x = x_ref[0]
    h = _mod_norm(x, g_ref[...], sc_ref[0], sh_ref[0]).astype(BF16)
    acc = None
    lo = 0
    for fc in chunks:
        a = _dot(h, wg_ref[:, lo:lo + fc])
        b = _dot(h, wu_ref[:, lo:lo + fc])
        part = _dot(_silu_mul(a, b).astype(BF16), wd_ref[lo:lo + fc, :])
        acc = part if acc is None else acc + part
        lo += fc
    o_ref[0] = x + gt_ref[0] * acc


def _ffn(x, sh, sc, gt, g, wg, wu, wd, tile, chunks):
    b, seq, d = x.shape
    tm = sh.shape[1]
    tmod = tile if tm > 1 else 1
    mod_map = (lambda bi, li: (bi, li, 0)) if tm > 1 else (lambda bi, li: (bi, 0, 0))
    return pl.pallas_call(
        functools.partial(_ffn_kernel, chunks=chunks),
        grid=(b, seq // tile),
        in_specs=[
            pl.BlockSpec((1, tile, d), lambda bi, li: (bi, li, 0)),
            pl.BlockSpec((1, tmod, d), mod_map),
            pl.BlockSpec((1, tmod, d), mod_map),
            pl.BlockSpec((1, tmod, d), mod_map),
            _resident((1, d)),
            _resident(wg.shape),
            _resident(wu.shape),
            _resident(wd.shape),
        ],
        out_specs=pl.BlockSpec((1, tile, d), lambda bi, li: (bi, li, 0)),
        out_shape=jax.ShapeDtypeStruct((b, seq, d), F32),
        compiler_params=_params("arbitrary", "arbitrary"),
        name="ffn",
    )(x, sh, sc, gt, g.reshape(1, d), wg, wu, wd)


def _split_bf16(a):
    hi = a.astype(BF16)
    return hi, (a - hi.astype(F32)).astype(BF16)


def _router_kernel(x_ref, sh_ref, sc_ref, g_ref, rw_ref, base_ref, h_ref, route_ref, cnt_ref,
                   tri, carry, *, n_steps):
    t = x_ref.shape[0]
    step = pl.program_id(0)

    @pl.when(step == 0)
    def _():
        r = lax.broadcasted_iota(I32, (t, t), 0)
        c = lax.broadcasted_iota(I32, (t, t), 1)
        tri[...] = jnp.where(r < c, 1.0, 0.0).astype(BF16)
        carry[...] = base_ref[...]

    @pl.when(step >= n_steps)
    def _():
        h_ref[...] = jnp.zeros_like(h_ref)

    @pl.when(step < n_steps)
    def _():
        h = _mod_norm(x_ref[...], g_ref[...], sc_ref[0], sh_ref[0])
        h_ref[...] = h
        h_hi, h_lo = _split_bf16(h)
        rw_hi, rw_lo = _split_bf16(rw_ref[...])
        logits = _dot_nt(rw_hi, h_hi) + _dot_nt(rw_hi, h_lo) + _dot_nt(rw_lo, h_hi)

        ne = logits.shape[0]
        idx = lax.broadcasted_iota(I32, logits.shape, 0)
        m1 = jnp.max(logits, axis=0, keepdims=True)
        i1 = jnp.min(jnp.where(logits == m1, idx, ne), axis=0, keepdims=True)
        sel1 = idx == i1
        rest = jnp.where(sel1, -jnp.inf, logits)
        m2 = jnp.max(rest, axis=0, keepdims=True)
        i2 = jnp.min(jnp.where(rest == m2, idx, ne), axis=0, keepdims=True)
        sel2 = idx == i2
        e = jnp.exp(m2 - m1)
        p1 = 1.0 / (1.0 + e)
        p2 = e / (1.0 + e)

        chosen = jnp.where(sel1, 1.0, jnp.where(sel2, 1.0, 0.0))
        before = _dot(chosen.astype(BF16), tri[...]) + carry[:, 0:1]
        r1 = jnp.sum(jnp.where(sel1, before, 0.0), axis=0, keepdims=True)
        r2 = jnp.sum(jnp.where(sel2, before, 0.0), axis=0, keepdims=True)
        carry[...] += jnp.sum(chosen, axis=1, keepdims=True)
        cnt_ref[...] = carry[...]

        rows = (i1.astype(F32), i2.astype(F32), r1, r2, p1, p2)
        route = jnp.zeros(logits.shape, F32)
        for k, row in enumerate(rows):
            route = jnp.where(idx == k, row, route)
        route_ref[...] = route


def _router(x2d, sh, sc, g, router_w, base_cnt, tile, rows_per_mod, h_rows, h_prev=None,
            row_off=0):
    n, d = x2d.shape
    ne = router_w.shape[-1]
    assert ne == ROUTE_ROWS and n % tile == 0 and row_off % tile == 0 and tile % LANES == 0
    assert base_cnt.shape == (ne, LANES)
    n_steps = n // tile
    tail = h_rows - (row_off + n)
    assert 0 <= tail < tile
    fill_tail = h_prev is None and tail > 0
    last = n_steps - 1
    clamp = (lambda i: jnp.minimum(i, last)) if fill_tail else (lambda i: i)
    per_row_mod = sh.shape[1] > 1
    if per_row_mod:
        mod_spec = pl.BlockSpec((1, tile, d), lambda i: (0, clamp(i), 0))
    else:
        assert rows_per_mod % tile == 0
        mod_spec = pl.BlockSpec((1, 1, d), lambda i: (clamp(i) // (rows_per_mod // tile), 0, 0))
    in_specs = [
        pl.BlockSpec((tile, d), lambda i: (clamp(i), 0)),
        mod_spec,
        mod_spec,
        _resident((1, d)),
        _resident((ne, d)),
        _resident((ne, LANES)),
    ]
    args = [x2d, sh, sc, g.reshape(1, d), router_w.T, base_cnt]
    n_in = len(args)
    aliases = {}
    body = functools.partial(_router_kernel, n_steps=n_steps)
    if h_prev is not None:
        assert h_prev.shape == (h_rows, d)
        in_specs.append(pl.BlockSpec(memory_space=pl.ANY))
        args.append(h_prev)
        aliases = {n_in: 0}

        def body(*refs):
            _router_kernel(*refs[:n_in], *refs[n_in + 1:], n_steps=n_steps)

    blk_off = row_off // tile
    return pl.pallas_call(
        body,
        grid=(n_steps + int(fill_tail),),
        in_specs=in_specs,
        out_specs=[
            pl.BlockSpec((tile, d), lambda i: (i + blk_off, 0)),
            pl.BlockSpec((ne, tile), lambda i: (0, clamp(i))),
            pl.BlockSpec((ne, LANES), lambda i: (0, 0)),
        ],
        out_shape=[
            jax.ShapeDtypeStruct((h_rows, d), F32),
            jax.ShapeDtypeStruct((ne, n), F32),
            jax.ShapeDtypeStruct((ne, LANES), F32),
        ],
        scratch_shapes=[pltpu.VMEM((tile, tile), BF16), pltpu.VMEM((ne, LANES), F32)],
        input_output_aliases=aliases,
        compiler_params=_params("arbitrary"),
        name="router",
    )(*args)


def _row_copy(src_hbm, row, dst, dst_row, sem):
    return pltpu.make_async_copy(src_hbm.at[pl.ds(row, 1), :], dst.at[pl.ds(dst_row, 1), :], sem)


def _row_copy_out(src, src_row, dst_hbm, row, sem):
    return pltpu.make_async_copy(src.at[pl.ds(src_row, 1), :], dst_hbm.at[pl.ds(row, 1), :], sem)


def _moe_grouped_kernel(te_ref, na_ref, src_ref, dst_ref, h_hbm, wg_ref, wu_ref, wd_ref, y_hbm,
                        xs, hbuf, obuf, gsem, ssem, fsem, *, tm, nc, rows_per_step, fill_row,
                        fill_blocks):
    del te_ref
    i = pl.program_id(0)
    c = pl.program_id(1)
    n_active = na_ref[0]
    ahead = ROW_SLOTS - 1
    slot = i % ROW_SLOTS

    def request(tile, r):
        s = tile % ROW_SLOTS
        return _row_copy(h_hbm, src_ref[tile * tm + r], xs.at[s], r, gsem.at[s])

    def send(tile, r):
        s = (tile + ROW_SLOTS) % ROW_SLOTS
        return _row_copy_out(obuf.at[s], r, y_hbm, dst_ref[(tile + 1) * tm + r], ssem.at[s])

    @pl.when(c == 0)
    def _():
        @pl.when(i == 0)
        def _():
            last = ROW_SLOTS - 1
            obuf[last] = jnp.zeros((tm, obuf.shape[-1]), F32)
            fills = [pltpu.make_async_copy(obuf.at[last, pl.ds(0, FILL_ROWS), :],
                                           y_hbm.at[pl.ds(fill_row + q * FILL_ROWS, FILL_ROWS), :],
                                           fsem) for q in range(fill_blocks)]
            for f in fills:
                f.start()
            for f in fills:
                f.wait()

            def body(r, carry):
                for t in range(ahead):
                    request(t, r).start()
                return carry

            lax.fori_loop(0, tm, body, 0)

        @pl.when(i < n_active + ahead)
        def _():
            pltpu.make_async_copy(h_hbm.at[pl.ds(0, tm), :], xs.at[slot], gsem.at[slot]).wait()

        @pl.when(jnp.logical_and(i >= ahead, i < n_active + ROW_SLOTS))
        def _():
            pltpu.make_async_copy(obuf.at[slot], y_hbm.at[pl.ds(0, tm), :], ssem.at[slot]).wait()

        @pl.when(i < n_active)
        def _():
            obuf[slot] = jnp.zeros((tm, obuf.shape[-1]), F32)
            hbuf[...] = xs[slot].astype(BF16)
            for r in range(rows_per_step * nc, tm):
                request(i + ahead, r).start()
                send(i - 1, r).start()

        @pl.when(i == n_active)
        def _():
            def body(r, carry):
                send(i - 1, r).start()
                return carry

            lax.fori_loop(0, tm, body, 0)

    @pl.when(i < n_active)
    def _():
        for k in range(rows_per_step):
            r = c * rows_per_step + k
            request(i + ahead, r).start(priority=k % 2)
            send(i - 1, r).start(priority=(k + 1) % 2)

        h = hbuf[...]
        a = _dot(h, wg_ref[0].astype(BF16))
        b = _dot(h, wu_ref[0].astype(BF16))
        obuf[slot] += _dot(_silu_mul(a, b).astype(BF16), wd_ref[0].astype(BF16))


def _moe_grouped(h_all, tile_expert, n_active, src, dst, wg, wu, wd, tm, fc, y_rows, fill_row,
                 fill_rows):
    ne, d, dff = wg.shape
    n_tiles = tile_expert.shape[0]
    nc = dff // fc
    assert dff % fc == 0
    assert src.shape[0] == n_tiles * tm and dst.shape[0] == n_tiles * tm
    assert fill_rows % FILL_ROWS == 0 and tm >= FILL_ROWS
    rows_per_step = tm // nc

    def w_col(i, c, te, na, src_, dst_):
        return (te[i], 0, jnp.where(i < na[0], c, nc - 1))

    def w_row(i, c, te, na, src_, dst_):
        return (te[i], jnp.where(i < na[0], c, nc - 1), 0)

    grid_spec = pltpu.PrefetchScalarGridSpec(
        num_scalar_prefetch=4,
        grid=(n_tiles, nc),
        in_specs=[
            pl.BlockSpec(memory_space=pl.ANY),
            pl.BlockSpec((1, d, fc), w_col),
            pl.BlockSpec((1, d, fc), w_col),
            pl.BlockSpec((1, fc, d), w_row),
        ],
        out_specs=pl.BlockSpec(memory_space=pl.ANY),
        scratch_shapes=[
            pltpu.VMEM((ROW_SLOTS, tm, d), F32),
            pltpu.VMEM((tm, d), BF16),
            pltpu.VMEM((ROW_SLOTS, tm, d), F32),
            pltpu.SemaphoreType.DMA((ROW_SLOTS,)),
            pltpu.SemaphoreType.DMA((ROW_SLOTS,)),
            pltpu.SemaphoreType.DMA,
        ],
    )
    return pl.pallas_call(
        functools.partial(_moe_grouped_kernel, tm=tm, nc=nc, rows_per_step=rows_per_step,
                          fill_row=fill_row, fill_blocks=fill_rows // FILL_ROWS),
        grid_spec=grid_spec,
        out_shape=jax.ShapeDtypeStruct((y_rows, d), F32),
        compiler_params=_params("arbitrary", "arbitrary"),
        name="moe_grouped",
    )(tile_expert, n_active, src, dst, h_all, wg, wu, wd)


def _combine_kernel(x_ref, gt_ref, route_ref, gf_ref, y0_ref, y1_ref, o_ref):
    route = route_ref[...]
    pad = jnp.zeros((LANES - route.shape[0], route.shape[1]), F32)
    cols = jnp.concatenate([route, pad], axis=0).T
    f = (cols[:, ROW_PROB:ROW_PROB + 1] * y0_ref[...]
         + cols[:, ROW_PROB + 1:ROW_PROB + 2] * y1_ref[...])
    o_ref[...] = _rms(x_ref[...] + gt_ref[0] * f, gf_ref[...])


def _combine(x2d, gt, route, g_final, y_tok, tile, rows_per_mod, tok_off, plane_rows):
    n, d = x2d.shape
    assert n % tile == 0 and tok_off % tile == 0 and plane_rows % tile == 0
    per_row_mod = gt.shape[1] > 1
    if per_row_mod:
        mod_spec = pl.BlockSpec((1, tile, d), lambda j: (0, j, 0))
    else:
        assert rows_per_mod % tile == 0
        mod_spec = pl.BlockSpec((1, 1, d), lambda j: (j // (rows_per_mod // tile), 0, 0))
    first = tok_off // tile
    second = (plane_rows + tok_off) // tile
    return pl.pallas_call(
        _combine_kernel,
        grid=(n // tile,),
        in_specs=[
            pl.BlockSpec((tile, d), lambda j: (j, 0)),
            mod_spec,
            pl.BlockSpec((ROUTE_ROWS, tile), lambda j: (0, j)),
            pl.BlockSpec((1, d), lambda j: (0, 0)),
            pl.BlockSpec((tile, d), lambda j: (first + j, 0)),
            pl.BlockSpec((tile, d), lambda j: (second + j, 0)),
        ],
        out_specs=pl.BlockSpec((tile, d), lambda j: (j, 0)),
        out_shape=jax.ShapeDtypeStruct((n, d), F32),
        compiler_params=_params("arbitrary"),
        name="combine",
    )(x2d, gt, route, g_final.reshape(1, d), y_tok, y_tok)


def _invert_kernel(perm_ref, inv_ref):
    def place(a, carry):
        inv_ref[perm_ref[a]] = a
        return carry

    lax.fori_loop(0, perm_ref.shape[0], place, 0, unroll=32)


def _invert(perm):
    assert perm.shape[0] % 32 == 0
    smem = pl.BlockSpec(memory_space=pltpu.SMEM)
    return pl.pallas_call(
        _invert_kernel,
        in_specs=[smem],
        out_specs=smem,
        out_shape=jax.ShapeDtypeStruct(perm.shape, I32),
        name="invert",
    )(perm)


def _routing_tables(route_all, counts, tm, n_tiles, plane_rows):
    ne = counts.shape[0]
    n = route_all.shape[1]
    e_idx = route_all[ROW_EXPERT:ROW_EXPERT + TOP_K].astype(I32)
    rank = route_all[ROW_RANK:ROW_RANK + TOP_K].astype(I32)
    padded = (counts + tm - 1) // tm * tm
    ends = jnp.cumsum(padded)
    offs = ends - padded
    onehot = e_idx[..., None] == jnp.arange(ne, dtype=I32)
    pos = jnp.sum(jnp.where(onehot, offs, 0), axis=-1) + rank
    n_active = ends[-1] // tm
    starts = jnp.arange(n_tiles, dtype=I32) * tm
    tile_expert = jnp.sum(starts[:, None] >= ends[None, :], axis=1).astype(I32)
    last_expert = jnp.sum((n_active - 1) * tm >= ends).astype(I32)
    tile_expert = jnp.minimum(tile_expert, last_expert)
    n_rows = n_tiles * tm
    n_real = n * TOP_K
    pad_ends = jnp.cumsum(padded - counts)
    j = jnp.arange(n_rows - n_real, dtype=I32)[:, None]
    in_expert = jnp.logical_and(j >= pad_ends - (padded - counts), j < pad_ends)
    pad_pos = jnp.where(j[:, 0] < pad_ends[-1],
                        jnp.sum(jnp.where(in_expert, offs + counts + j - pad_ends + padded - counts, 0),
                                axis=1),
                        ends[-1] + j[:, 0] - pad_ends[-1])
    inv = _invert(jnp.concatenate([pos.reshape(-1), pad_pos]))
    is_real = inv < n_real
    tok = inv % n
    choice = inv // n
    spare = n + jnp.minimum(inv - n_real, ne * tm - 1)
    src = jnp.where(is_real, tok, 0)
    dst = jnp.where(is_real, choice * plane_rows + tok, spare)
    lead = n + ne * tm + jnp.arange(tm, dtype=I32)
    dst = jnp.concatenate([lead, dst])[:n_rows]
    return src, dst, tile_expert, n_active.astype(I32).reshape(1)


def _moe_layer(xp, xs, mod_p, mod_s, g, router_w, wg, wu, wd, g_final):
    nb, seq, d = xp.shape
    ns = xs.shape[0]
    ne = router_w.shape[-1]
    n_all = nb * seq + ns
    sh_p, sc_p, gt_p = mod_p
    sh_s, sc_s, gt_s = mod_s
    xp2 = xp.reshape(nb * seq, d)

    zero_cnt = jnp.zeros((ne, LANES), F32)
    h_all, route_p, cnt_p = _router(xp2, sh_p, sc_p, g, router_w, zero_cnt, ROUTER_TILE, seq,
                                    n_all)
    h_all, route_s, cnt_s = _router(xs, sh_s, sc_s, g, router_w, cnt_p, ns, ns, n_all,
                                    h_prev=h_all, row_off=nb * seq)

    tm = MOE_ROW_TILE
    n_tiles = (n_all * TOP_K) // tm + ne + ROW_SLOTS
    plane_rows = -(-(n_all + (ne + 1) * tm) // COMBINE_TILE) * COMBINE_TILE
    route_all = jnp.concatenate([route_p, route_s], axis=1)
    src, dst, tile_expert, n_active = _routing_tables(
        route_all, cnt_s[:, 0].astype(I32), tm, n_tiles, plane_rows)

    y_tok = _moe_grouped(h_all, tile_expert, n_active, src, dst, wg, wu, wd, tm, MOE_FF_CHUNK,
                         plane_rows + n_all, n_all, plane_rows - n_all)
    yp = _combine(xp2, gt_p, route_p, g_final, y_tok, COMBINE_TILE, seq, 0, plane_rows)
    ys = _combine(xs, gt_s, route_s, g_final, y_tok, ns, ns, nb * seq, plane_rows)
    return yp.reshape(nb, seq, d), ys


def kernel(x_prompt, x_sample, c_prompt, c_sample, state_conv, state_pool, w_ada, b_ada, g_mix,
           g_ffn, w_in, conv_w, pool_w, pool_scale, g_conv_out, g_pool_out, w_out, dense_w_gate,
           dense_w_up, dense_w_down, router_w, moe_w_gate, moe_w_up, moe_w_down, g_final):
    depth = w_ada.shape[0]
    nb, seq, d = x_prompt.shape
    ns = x_sample.shape[0]
    assert x_sample.shape[1] == 1 and depth == 2
    assert sum(DENSE_FF_CHUNKS) == dense_w_gate.shape[-1]

    mod = _ada(jnp.concatenate([c_prompt, c_sample], axis=0), w_ada, b_ada)
    mod = mod.reshape(depth, nb + ns, 6, d)
    mod_p = [jnp.transpose(mod[i, :nb], (1, 0, 2))[:, :, None, :] for i in range(depth)]
    mod_s = [jnp.transpose(mod[i, nb:], (1, 0, 2))[:, None, :, :] for i in range(depth)]

    w_in_b = w_in.astype(BF16)
    w_out_b = w_out.astype(BF16)
    pool_w_b = pool_w.astype(BF16)
    dense_b = [w.astype(BF16) for w in (dense_w_gate, dense_w_up, dense_w_down)]
    cb_t = jnp.transpose(state_conv, (0, 2, 1, 3))
    pb_t = jnp.transpose(state_pool, (0, 2, 1, 3))

    xp = x_prompt
    xs = x_sample.reshape(ns, d)
    conv_p, pool_p, conv_s, pool_s = [], [], [], []
    for i in range(depth):
        sh1, sc1, gt1, sh2, sc2, gt2 = mod_p[i]
        mix_w = (g_mix[i], w_in_b[i], conv_w[i], pool_w_b[i], pool_scale[i], g_conv_out[i],
                 g_pool_out[i], w_out_b[i])
        xp, cs, ps = _mix_prompt(xp, sh1, sc1, gt1, *mix_w, tile=MIX_TILE, sub=MIX_ROW_BLOCK)
        conv_p.append(cs)
        pool_p.append(ps)
        s1, c1, t1, s2, c2, t2 = mod_s[i]
        xs, v_new, u_new = _mix_sample(xs, s1[0], c1[0], t1[0], *mix_w, cb_t[i], pb_t[i])
        conv_s.append(jnp.concatenate([state_conv[i][:, 1:], v_new[:, None, :]], axis=1))
        pool_s.append(jnp.concatenate([state_pool[i][:, 1:], u_new[:, None, :]], axis=1))
        j = i // 2
        if i % 2 == 0:
            wg, wu, wd = (w[j] for w in dense_b)
            xp = _ffn(xp, sh2, sc2, gt2, g_ffn[i], wg, wu, wd, PROMPT_TILE, DENSE_FF_CHUNKS)
            xs = _ffn(xs[None], s2, c2, t2, g_ffn[i], wg, wu, wd, ns, DENSE_FF_CHUNKS)[0]
        else:
            xp, xs = _moe_layer(xp, xs, (sh2, sc2, gt2), (s2, c2, t2), g_ffn[i], router_w[j],
                                moe_w_gate[j], moe_w_up[j], moe_w_down[j], g_final)

    return (xp, xs.reshape(ns, 1, d), jnp.stack(conv_p), jnp.stack(pool_p),
            jnp.stack(conv_s), jnp.stack(pool_s))
```

```python
import functools

import jax
import jax.numpy as jnp
from jax import lax
from jax.experimental import pallas as pl
from jax.experimental.pallas import tpu as pltpu

F32 = jnp.float32
BF16 = jnp.bfloat16
I32 = jnp.int32

EPS = 1e-6
CONV_K = 3
POOL_WINDOWS = (2, 4, 8, 16)
POOL_HIST = max(POOL_WINDOWS) - 1
TOP_K = 2

CONV_PAD = 8
POOL_PAD = 16

LANES = 128

ROUTE_ROWS = 8
ROW_EXPERT, ROW_RANK, ROW_PROB = 0, 2, 4

VMEM_LIMIT_BYTES = 56 * 1024 * 1024

ADA_COL_TILE = 2048
PROMPT_TILE = 512
MIX_TILE = 2048
MIX_ROW_BLOCK = 512
ROUTER_TILE = 1024
COMBINE_TILE = 1024
MOE_ROW_TILE = 1024
MOE_FF_CHUNK = 512
FILL_ROWS = 128
ROW_SLOTS = 3
DENSE_FF_CHUNKS = (768, 768, 768, 512)


def _params(*sem):
    return pltpu.CompilerParams(dimension_semantics=sem, vmem_limit_bytes=VMEM_LIMIT_BYTES)


def _resident(shape):
    nd = len(shape)
    return pl.BlockSpec(shape, lambda *_: (0,) * nd, pipeline_mode=pl.Buffered(1))


def _rms(x, g):
    ms = jnp.mean(x * x, axis=-1, keepdims=True)
    return x * lax.rsqrt(ms + EPS) * g


def _mod_norm(x, g, sc, sh):
    ms = jnp.mean(x * x, axis=-1, keepdims=True)
    return x * lax.rsqrt(ms + EPS) * (g * (1.0 + sc)) + sh


def _dot(a, b):
    return jnp.dot(a, b, preferred_element_type=F32)


def _dot_nt(a, b):
    return lax.dot_general(a, b, (((1,), (1,)), ((), ())), preferred_element_type=F32)


def _silu_mul(a, b):
    return a * jax.nn.sigmoid(a) * b


def _ada_kernel(c_ref, w_ref, b_ref, o_ref):
    c = c_ref[...]
    a = (c * jax.nn.sigmoid(c)).astype(BF16)
    o_ref[0] = _dot(a, w_ref[0].astype(BF16)) + b_ref[0]


def _ada(c_all, w_ada, b_ada):
    depth, d, n = w_ada.shape
    m = c_all.shape[0]
    tn = ADA_COL_TILE
    return pl.pallas_call(
        _ada_kernel,
        grid=(depth, n // tn),
        in_specs=[
            pl.BlockSpec((m, d), lambda i, j: (0, 0)),
            pl.BlockSpec((1, d, tn), lambda i, j: (i, 0, j)),
            pl.BlockSpec((1, 1, tn), lambda i, j: (i, 0, j)),
        ],
        out_specs=pl.BlockSpec((1, m, tn), lambda i, j: (i, 0, j)),
        out_shape=jax.ShapeDtypeStruct((depth, m, n), F32),
        compiler_params=_params("arbitrary", "arbitrary"),
        name="ada",
    )(c_all, w_ada, b_ada.reshape(depth, 1, n))


def _mix_tail(x, gt, bg, y, d_groups, poolw_ref, pscale_ref, gco_ref, gpo_ref, wout_ref):
    cw = gco_ref.shape[-1]
    ya = bg * y
    yb = jnp.concatenate(
        [_dot(d.astype(BF16), poolw_ref[g]) for g, d in enumerate(d_groups)], axis=-1
    ) * pscale_ref[...]
    ma = _rms(ya, gco_ref[...]).astype(BF16)
    mb = _rms(yb, gpo_ref[...]).astype(BF16)
    o = _dot(ma, wout_ref[0:cw, :]) + _dot(mb, wout_ref[cw:, :])
    return x + gt * o


def _mix_prompt_kernel(x_ref, sh_ref, sc_ref, gt_ref, g_ref, win_ref, convw_ref, poolw_ref,
                       pscale_ref, gco_ref, gpo_ref, wout_ref,
                       xo_ref, cs_ref, ps_ref, vbuf, ubuf, *, tile, sub):
    l = pl.program_id(1)
    cw = convw_ref.shape[-1]
    pg = poolw_ref.shape[-1]

    @pl.when(l == 0)
    def _():
        vbuf[0:CONV_PAD, :] = jnp.zeros((CONV_PAD, cw), F32)
        ubuf[0:POOL_PAD, :] = jnp.zeros((POOL_PAD, ubuf.shape[-1]), F32)

    gates = []
    for lo in range(0, tile, sub):
        x = x_ref[0, lo:lo + sub, :]
        h = _mod_norm(x, g_ref[...], sc_ref[0], sh_ref[0]).astype(BF16)
        p = _dot(h, win_ref[...])
        gates.append(p[:, 0:cw])
        vbuf[CONV_PAD + lo:CONV_PAD + lo + sub, :] = p[:, cw:2 * cw] * p[:, 2 * cw:3 * cw]
        ubuf[POOL_PAD + lo:POOL_PAD + lo + sub, :] = p[:, 3 * cw:]

    w = convw_ref[...]
    for bg, lo in zip(gates, range(0, tile, sub)):
        y = w[CONV_K - 1:CONV_K] * vbuf[CONV_PAD + lo:CONV_PAD + lo + sub, :]
        for k in range(1, CONV_K):
            y = y + w[CONV_K - 1 - k:CONV_K - k] * vbuf[CONV_PAD + lo - k:CONV_PAD + lo - k + sub, :]

        pos = l * tile + lo + lax.broadcasted_iota(I32, (sub, 1), 0)
        d_groups = []
        for g, win in enumerate(POOL_WINDOWS):
            ug = ubuf[POOL_PAD + lo:POOL_PAD + lo + sub, g * pg:(g + 1) * pg]
            acc = ug
            for k in range(1, win):
                acc = acc + ubuf[POOL_PAD + lo - k:POOL_PAD + lo - k + sub, g * pg:(g + 1) * pg]
            cnt = jnp.minimum(pos + 1, win).astype(F32)
            d_groups.append(acc / cnt - ug)

        xo_ref[0, lo:lo + sub, :] = _mix_tail(x_ref[0, lo:lo + sub, :], gt_ref[0], bg, y, d_groups,
                                              poolw_ref, pscale_ref, gco_ref, gpo_ref, wout_ref)

    cs_ref[0] = vbuf[CONV_PAD + tile - (CONV_K - 1):CONV_PAD + tile, :]
    ps_ref[0] = ubuf[POOL_PAD + tile - POOL_HIST:POOL_PAD + tile, :]
    vbuf[0:CONV_PAD, :] = vbuf[tile:tile + CONV_PAD, :]
    ubuf[0:POOL_PAD, :] = ubuf[tile:tile + POOL_PAD, :]


def _mix_prompt(x, sh, sc, gt, g, w_in, conv_w, pool_w, pool_scale, g_co, g_po, w_out, tile, sub):
    b, seq, d = x.shape
    assert seq % tile == 0 and tile % sub == 0 and sub >= POOL_PAD
    cw = conv_w.shape[-1]
    pw = pool_scale.shape[-1]
    row = lambda bi, li: (bi, 0, 0)
    return pl.pallas_call(
        functools.partial(_mix_prompt_kernel, tile=tile, sub=sub),
        grid=(b, seq // tile),
        in_specs=[
            pl.BlockSpec((1, tile, d), lambda bi, li: (bi, li, 0)),
            pl.BlockSpec((1, 1, d), row),
            pl.BlockSpec((1, 1, d), row),
            pl.BlockSpec((1, 1, d), row),
            _resident((1, d)),
            _resident(w_in.shape),
            _resident(conv_w.shape),
            _resident(pool_w.shape),
            _resident((1, pw)),
            _resident((1, cw)),
            _resident((1, pw)),
            _resident(w_out.shape),
        ],
        out_specs=[
            pl.BlockSpec((1, tile, d), lambda bi, li: (bi, li, 0)),
            pl.BlockSpec((1, CONV_K - 1, cw), row),
            pl.BlockSpec((1, POOL_HIST, pw), row),
        ],
        out_shape=[
            jax.ShapeDtypeStruct((b, seq, d), F32),
            jax.ShapeDtypeStruct((b, CONV_K - 1, cw), F32),
            jax.ShapeDtypeStruct((b, POOL_HIST, pw), F32),
        ],
        scratch_shapes=[
            pltpu.VMEM((CONV_PAD + tile, cw), F32),
            pltpu.VMEM((POOL_PAD + tile, pw), F32),
        ],
        compiler_params=_params("arbitrary", "arbitrary"),
        name="mix_prompt",
    )(x, sh, sc, gt, g.reshape(1, d), w_in, conv_w, pool_w, pool_scale.reshape(1, pw),
      g_co.reshape(1, cw), g_po.reshape(1, pw), w_out)


def _mix_sample_kernel(x_ref, sh_ref, sc_ref, gt_ref, g_ref, win_ref, convw_ref, poolw_ref,
                       pscale_ref, gco_ref, gpo_ref, wout_ref, cb_ref, pb_ref,
                       xo_ref, v_ref, u_ref):
    cw = convw_ref.shape[-1]
    pg = poolw_ref.shape[-1]
    x = x_ref[...]
    h = _mod_norm(x, g_ref[...], sc_ref[...], sh_ref[...]).astype(BF16)
    p = _dot(h, win_ref[...])
    bg = p[:, 0:cw]
    v = p[:, cw:2 * cw] * p[:, 2 * cw:3 * cw]
    u = p[:, 3 * cw:]
    v_ref[...] = v
    u_ref[...] = u

    w = convw_ref[...]
    y = w[CONV_K - 1:CONV_K] * v
    for k in range(1, CONV_K):
        y = y + w[CONV_K - 1 - k:CONV_K - k] * cb_ref[CONV_K - 1 - k]

    d_groups = []
    for g, win in enumerate(POOL_WINDOWS):
        ug = u[:, g * pg:(g + 1) * pg]
        acc = ug
        for k in range(1, win):
            acc = acc + pb_ref[POOL_HIST - k, :, g * pg:(g + 1) * pg]
        d_groups.append(acc / float(win) - ug)

    xo_ref[...] = _mix_tail(x, gt_ref[...], bg, y, d_groups, poolw_ref, pscale_ref,
                            gco_ref, gpo_ref, wout_ref)


def _mix_sample(x, sh, sc, gt, g, w_in, conv_w, pool_w, pool_scale, g_co, g_po, w_out, cb, pb):
    n, d = x.shape
    cw = conv_w.shape[-1]
    pw = pool_scale.shape[-1]
    return pl.pallas_call(
        _mix_sample_kernel,
        out_shape=[
            jax.ShapeDtypeStruct((n, d), F32),
            jax.ShapeDtypeStruct((n, cw), F32),
            jax.ShapeDtypeStruct((n, pw), F32),
        ],
        compiler_params=pltpu.CompilerParams(vmem_limit_bytes=VMEM_LIMIT_BYTES),
        name="mix_sample",
    )(x, sh, sc, gt, g.reshape(1, d), w_in, conv_w, pool_w, pool_scale.reshape(1, pw),
      g_co.reshape(1, cw), g_po.reshape(1, pw), w_out, cb, pb)


def _ffn_kernel(x_ref, sh_ref, sc_ref, gt_ref, g_ref, wg_ref, wu_ref, wd_ref, o_ref, *, chunks):
    x = x_ref[0]
    h = _mod_norm(x, g_ref[...], sc_ref[0], sh_ref[0]).astype(BF16)
    acc = None
    lo = 0
    for fc in chunks:
        a = _dot(h, wg_ref[:, lo:lo + fc])
        b = _dot(h, wu_ref[:, lo:lo + fc])
        part = _dot(_silu_mul(a, b).astype(BF16), wd_ref[lo:lo + fc, :])
        acc = part if acc is None else acc + part
        lo += fc
    o_ref[0] = x + gt_ref[0] * acc


def _ffn(x, sh, sc, gt, g, wg, wu, wd, tile, chunks):
    b, seq, d = x.shape
    tm = sh.shape[1]
    tmod = tile if tm > 1 else 1
    mod_map = (lambda bi, li: (bi, li, 0)) if tm > 1 else (lambda bi, li: (bi, 0, 0))
    return pl.pallas_call(
        functools.partial(_ffn_kernel, chunks=chunks),
        grid=(b, seq // tile),
        in_specs=[
            pl.BlockSpec((1, tile, d), lambda bi, li: (bi, li, 0)),
            pl.BlockSpec((1, tmod, d), mod_map),
            pl.BlockSpec((1, tmod, d), mod_map),
            pl.BlockSpec((1, tmod, d), mod_map),
            _resident((1, d)),
            _resident(wg.shape),
            _resident(wu.shape),
            _resident(wd.shape),
        ],
        out_specs=pl.BlockSpec((1, tile, d), lambda bi, li: (bi, li, 0)),
        out_shape=jax.ShapeDtypeStruct((b, seq, d), F32),
        compiler_params=_params("arbitrary", "arbitrary"),
        name="ffn",
    )(x, sh, sc, gt, g.reshape(1, d), wg, wu, wd)


def _split_bf16(a):
    hi = a.astype(BF16)
    return hi, (a - hi.astype(F32)).astype(BF16)


def _router_kernel(x_ref, sh_ref, sc_ref, g_ref, rw_ref, base_ref, h_ref, route_ref, cnt_ref,
                   tri, carry, *, n_steps):
    t = x_ref.shape[0]
    step = pl.program_id(0)

    @pl.when(step == 0)
    def _():
        r = lax.broadcasted_iota(I32, (t, t), 0)
        c = lax.broadcasted_iota(I32, (t, t), 1)
        tri[...] = jnp.where(r < c, 1.0, 0.0).astype(BF16)
        carry[...] = base_ref[...]

    @pl.when(step >= n_steps)
    def _():
        h_ref[...] = jnp.zeros_like(h_ref)

    @pl.when(step < n_steps)
    def _():
        h = _mod_norm(x_ref[...], g_ref[...], sc_ref[0], sh_ref[0])
        h_ref[...] = h
        h_hi, h_lo = _split_bf16(h)
        rw_hi, rw_lo = _split_bf16(rw_ref[...])
        logits = _dot_nt(rw_hi, h_hi) + _dot_nt(rw_hi, h_lo) + _dot_nt(rw_lo, h_hi)

        ne = logits.shape[0]
        idx = lax.broadcasted_iota(I32, logits.shape, 0)
        m1 = jnp.max(logits, axis=0, keepdims=True)
        i1 = jnp.min(jnp.where(logits == m1, idx, ne), axis=0, keepdims=True)
        sel1 = idx == i1
        rest = jnp.where(sel1, -jnp.inf, logits)
        m2 = jnp.max(rest, axis=0, keepdims=True)
        i2 = jnp.min(jnp.where(rest == m2, idx, ne), axis=0, keepdims=True)
        sel2 = idx == i2
        e = jnp.exp(m2 - m1)
        p1 = 1.0 / (1.0 + e)
        p2 = e / (1.0 + e)

        chosen = jnp.where(sel1, 1.0, jnp.where(sel2, 1.0, 0.0))
        before = _dot(chosen.astype(BF16), tri[...]) + carry[:, 0:1]
        r1 = jnp.sum(jnp.where(sel1, before, 0.0), axis=0, keepdims=True)
        r2 = jnp.sum(jnp.where(sel2, before, 0.0), axis=0, keepdims=True)
        carry[...] += jnp.sum(chosen, axis=1, keepdims=True)
        cnt_ref[...] = carry[...]

        rows = (i1.astype(F32), i2.astype(F32), r1, r2, p1, p2)
        route = jnp.zeros(logits.shape, F32)
        for k, row in enumerate(rows):
            route = jnp.where(idx == k, row, route)
        route_ref[...] = route


def _router(x2d, sh, sc, g, router_w, base_cnt, tile, rows_per_mod, h_rows, h_prev=None,
            row_off=0):
    n, d = x2d.shape
    ne = router_w.shape[-1]
    assert ne == ROUTE_ROWS and n % tile == 0 and row_off % tile == 0 and tile % LANES == 0
    assert base_cnt.shape == (ne, LANES)
    n_steps = n // tile
    tail = h_rows - (row_off + n)
    assert 0 <= tail < tile
    fill_tail = h_prev is None and tail > 0
    last = n_steps - 1
    clamp = (lambda i: jnp.minimum(i, last)) if fill_tail else (lambda i: i)
    per_row_mod = sh.shape[1] > 1
    if per_row_mod:
        mod_spec = pl.BlockSpec((1, tile, d), lambda i: (0, clamp(i), 0))
    else:
        assert rows_per_mod % tile == 0
        mod_spec = pl.BlockSpec((1, 1, d), lambda i: (clamp(i) // (rows_per_mod // tile), 0, 0))
    in_specs = [
        pl.BlockSpec((tile, d), lambda i: (clamp(i), 0)),
        mod_spec,
        mod_spec,
        _resident((1, d)),
        _resident((ne, d)),
        _resident((ne, LANES)),
    ]
    args = [x2d, sh, sc, g.reshape(1, d), router_w.T, base_cnt]
    n_in = len(args)
    aliases = {}
    body = functools.partial(_router_kernel, n_steps=n_steps)
    if h_prev is not None:
        assert h_prev.shape == (h_rows, d)
        in_specs.append(pl.BlockSpec(memory_space=pl.ANY))
        args.append(h_prev)
        aliases = {n_in: 0}

        def body(*refs):
            _router_kernel(*refs[:n_in], *refs[n_in + 1:], n_steps=n_steps)

    blk_off = row_off // tile
    return pl.pallas_call(
        body,
        grid=(n_steps + int(fill_tail),),
        in_specs=in_specs,
        out_specs=[
            pl.BlockSpec((tile, d), lambda i: (i + blk_off, 0)),
            pl.BlockSpec((ne, tile), lambda i: (0, clamp(i))),
            pl.BlockSpec((ne, LANES), lambda i: (0, 0)),
        ],
        out_shape=[
            jax.ShapeDtypeStruct((h_rows, d), F32),
            jax.ShapeDtypeStruct((ne, n), F32),
            jax.ShapeDtypeStruct((ne, LANES), F32),
        ],
        scratch_shapes=[pltpu.VMEM((tile, tile), BF16), pltpu.VMEM((ne, LANES), F32)],
        input_output_aliases=aliases,
        compiler_params=_params("arbitrary"),
        name="router",
    )(*args)


def _row_copy(src_hbm, row, dst, dst_row, sem):
    return pltpu.make_async_copy(src_hbm.at[pl.ds(row, 1), :], dst.at[pl.ds(dst_row, 1), :], sem)


def _row_copy_out(src, src_row, dst_hbm, row, sem):
    return pltpu.make_async_copy(src.at[pl.ds(src_row, 1), :], dst_hbm.at[pl.ds(row, 1), :], sem)


def _moe_grouped_kernel(te_ref, na_ref, src_ref, dst_ref, h_hbm, wg_ref, wu_ref, wd_ref, y_hbm,
                        xs, hbuf, obuf, gsem, ssem, fsem, *, tm, nc, rows_per_step, fill_row,
                        fill_blocks):
    del te_ref
    i = pl.program_id(0)
    c = pl.program_id(1)
    n_active = na_ref[0]
    ahead = ROW_SLOTS - 1
    slot = i % ROW_SLOTS

    def request(tile, r):
        s = tile % ROW_SLOTS
        return _row_copy(h_hbm, src_ref[tile * tm + r], xs.at[s], r, gsem.at[s])

    def send(tile, r):
        s = (tile + ROW_SLOTS) % ROW_SLOTS
        return _row_copy_out(obuf.at[s], r, y_hbm, dst_ref[(tile + 1) * tm + r], ssem.at[s])

    @pl.when(c == 0)
    def _():
        @pl.when(i == 0)
        def _():
            last = ROW_SLOTS - 1
            obuf[last] = jnp.zeros((tm, obuf.shape[-1]), F32)
            fills = [pltpu.make_async_copy(obuf.at[last, pl.ds(0, FILL_ROWS), :],
                                           y_hbm.at[pl.ds(fill_row + q * FILL_ROWS, FILL_ROWS), :],
                                           fsem) for q in range(fill_blocks)]
            for f in fills:
                f.start()
            for f in fills:
                f.wait()

            def body(r, carry):
                for t in range(ahead):
                    request(t, r).start()
                return carry

            lax.fori_loop(0, tm, body, 0)

        @pl.when(i < n_active + ahead)
        def _():
            pltpu.make_async_copy(h_hbm.at[pl.ds(0, tm), :], xs.at[slot], gsem.at[slot]).wait()

        @pl.when(jnp.logical_and(i >= ahead, i < n_active + ROW_SLOTS))
        def _():
            pltpu.make_async_copy(obuf.at[slot], y_hbm.at[pl.ds(0, tm), :], ssem.at[slot]).wait()

        @pl.when(i < n_active)
        def _():
            obuf[slot] = jnp.zeros((tm, obuf.shape[-1]), F32)
            hbuf[...] = xs[slot].astype(BF16)
            for r in range(rows_per_step * nc, tm):
                request(i + ahead, r).start()
                send(i - 1, r).start()

        @pl.when(i == n_active)
        def _():
            def body(r, carry):
                send(i - 1, r).start()
                return carry

            lax.fori_loop(0, tm, body, 0)

    @pl.when(i < n_active)
    def _():
        for k in range(rows_per_step):
            r = c * rows_per_step + k
            request(i + ahead, r).start(priority=1)
            send(i - 1, r).start()

        h = hbuf[...]
        a = _dot(h, wg_ref[0].astype(BF16))
        b = _dot(h, wu_ref[0].astype(BF16))
        obuf[slot] += _dot(_silu_mul(a, b).astype(BF16), wd_ref[0].astype(BF16))


def _moe_grouped(h_all, tile_expert, n_active, src, dst, wg, wu, wd, tm, fc, y_rows, fill_row,
                 fill_rows):
    ne, d, dff = wg.shape
    n_tiles = tile_expert.shape[0]
    nc = dff // fc
    assert dff % fc == 0
    assert src.shape[0] == n_tiles * tm and dst.shape[0] == n_tiles * tm
    assert fill_rows % FILL_ROWS == 0 and tm >= FILL_ROWS
    rows_per_step = tm // nc

    def w_col(i, c, te, na, src_, dst_):
        return (te[i], 0, jnp.where(i < na[0], c, nc - 1))

    def w_row(i, c, te, na, src_, dst_):
        return (te[i], jnp.where(i < na[0], c, nc - 1), 0)

    grid_spec = pltpu.PrefetchScalarGridSpec(
        num_scalar_prefetch=4,
        grid=(n_tiles, nc),
        in_specs=[
            pl.BlockSpec(memory_space=pl.ANY),
            pl.BlockSpec((1, d, fc), w_col),
            pl.BlockSpec((1, d, fc), w_col),
            pl.BlockSpec((1, fc, d), w_row),
        ],
        out_specs=pl.BlockSpec(memory_space=pl.ANY),
        scratch_shapes=[
            pltpu.VMEM((ROW_SLOTS, tm, d), F32),
            pltpu.VMEM((tm, d), BF16),
            pltpu.VMEM((ROW_SLOTS, tm, d), F32),
            pltpu.SemaphoreType.DMA((ROW_SLOTS,)),
            pltpu.SemaphoreType.DMA((ROW_SLOTS,)),
            pltpu.SemaphoreType.DMA,
        ],
    )
    return pl.pallas_call(
        functools.partial(_moe_grouped_kernel, tm=tm, nc=nc, rows_per_step=rows_per_step,
                          fill_row=fill_row, fill_blocks=fill_rows // FILL_ROWS),
        grid_spec=grid_spec,
        out_shape=jax.ShapeDtypeStruct((y_rows, d), F32),
        compiler_params=_params("arbitrary", "arbitrary"),
        name="moe_grouped",
    )(tile_expert, n_active, src, dst, h_all, wg, wu, wd)


def _combine_kernel(x_ref, gt_ref, route_ref, gf_ref, y0_ref, y1_ref, o_ref):
    route = route_ref[...]
    pad = jnp.zeros((LANES - route.shape[0], route.shape[1]), F32)
    cols = jnp.concatenate([route, pad], axis=0).T
    f = (cols[:, ROW_PROB:ROW_PROB + 1] * y0_ref[...]
         + cols[:, ROW_PROB + 1:ROW_PROB + 2] * y1_ref[...])
    o_ref[...] = _rms(x_ref[...] + gt_ref[0] * f, gf_ref[...])


def _combine(x2d, gt, route, g_final, y_tok, tile, rows_per_mod, tok_off, plane_rows):
    n, d = x2d.shape
    assert n % tile == 0 and tok_off % tile == 0 and plane_rows % tile == 0
    per_row_mod = gt.shape[1] > 1
    if per_row_mod:
        mod_spec = pl.BlockSpec((1, tile, d), lambda j: (0, j, 0))
    else:
        assert rows_per_mod % tile == 0
        mod_spec = pl.BlockSpec((1, 1, d), lambda j: (j // (rows_per_mod // tile), 0, 0))
    first = tok_off // tile
    second = (plane_rows + tok_off) // tile
    return pl.pallas_call(
        _combine_kernel,
        grid=(n // tile,),
        in_specs=[
            pl.BlockSpec((tile, d), lambda j: (j, 0)),
            mod_spec,
            pl.BlockSpec((ROUTE_ROWS, tile), lambda j: (0, j)),
            pl.BlockSpec((1, d), lambda j: (0, 0)),
            pl.BlockSpec((tile, d), lambda j: (first + j, 0)),
            pl.BlockSpec((tile, d), lambda j: (second + j, 0)),
        ],
        out_specs=pl.BlockSpec((tile, d), lambda j: (j, 0)),
        out_shape=jax.ShapeDtypeStruct((n, d), F32),
        compiler_params=_params("arbitrary"),
        name="combine",
    )(x2d, gt, route, g_final.reshape(1, d), y_tok, y_tok)


def _invert_kernel(perm_ref, inv_ref):
    def place(a, carry):
        inv_ref[perm_ref[a]] = a
        return carry

    lax.fori_loop(0, perm_ref.shape[0], place, 0, unroll=32)


def _invert(perm):
    assert perm.shape[0] % 32 == 0
    smem = pl.BlockSpec(memory_space=pltpu.SMEM)
    return pl.pallas_call(
        _invert_kernel,
        in_specs=[smem],
        out_specs=smem,
        out_shape=jax.ShapeDtypeStruct(perm.shape, I32),
        name="invert",
    )(perm)


def _routing_tables(route_all, counts, tm, n_tiles, plane_rows):
    ne = counts.shape[0]
    n = route_all.shape[1]
    e_idx = route_all[ROW_EXPERT:ROW_EXPERT + TOP_K].astype(I32)
    rank = route_all[ROW_RANK:ROW_RANK + TOP_K].astype(I32)
    padded = (counts + tm - 1) // tm * tm
    ends = jnp.cumsum(padded)
    offs = ends - padded
    onehot = e_idx[..., None] == jnp.arange(ne, dtype=I32)
    pos = jnp.sum(jnp.where(onehot, offs, 0), axis=-1) + rank
    n_active = ends[-1] // tm
    starts = jnp.arange(n_tiles, dtype=I32) * tm
    tile_expert = jnp.sum(starts[:, None] >= ends[None, :], axis=1).astype(I32)
    last_expert = jnp.sum((n_active - 1) * tm >= ends).astype(I32)
    tile_expert = jnp.minimum(tile_expert, last_expert)
    n_rows = n_tiles * tm
    n_real = n * TOP_K
    pad_ends = jnp.cumsum(padded - counts)
    j = jnp.arange(n_rows - n_real, dtype=I32)[:, None]
    in_expert = jnp.logical_and(j >= pad_ends - (padded - counts), j < pad_ends)
    pad_pos = jnp.where(j[:, 0] < pad_ends[-1],
                        jnp.sum(jnp.where(in_expert, offs + counts + j - pad_ends + padded - counts, 0),
                                axis=1),
                        ends[-1] + j[:, 0] - pad_ends[-1])
    inv = _invert(jnp.concatenate([pos.reshape(-1), pad_pos]))
    is_real = inv < n_real
    tok = inv % n
    choice = inv // n
    spare = n + jnp.minimum(inv - n_real, ne * tm - 1)
    src = jnp.where(is_real, tok, 0)
    dst = jnp.where(is_real, choice * plane_rows + tok, spare)
    lead = n + ne * tm + jnp.arange(tm, dtype=I32)
    dst = jnp.concatenate([lead, dst])[:n_rows]
    return src, dst, tile_expert, n_active.astype(I32).reshape(1)


def _moe_layer(xp, xs, mod_p, mod_s, g, router_w, wg, wu, wd, g_final):
    nb, seq, d = xp.shape
    ns = xs.shape[0]
    ne = router_w.shape[-1]
    n_all = nb * seq + ns
    sh_p, sc_p, gt_p = mod_p
    sh_s, sc_s, gt_s = mod_s
    xp2 = xp.reshape(nb * seq, d)

    zero_cnt = jnp.zeros((ne, LANES), F32)
    h_all, route_p, cnt_p = _router(xp2, sh_p, sc_p, g, router_w, zero_cnt, ROUTER_TILE, seq,
                                    n_all)
    h_all, route_s, cnt_s = _router(xs, sh_s, sc_s, g, router_w, cnt_p, ns, ns, n_all,
                                    h_prev=h_all, row_off=nb * seq)

    tm = MOE_ROW_TILE
    n_tiles = (n_all * TOP_K) // tm + ne + ROW_SLOTS
    plane_rows = -(-(n_all + (ne + 1) * tm) // COMBINE_TILE) * COMBINE_TILE
    route_all = jnp.concatenate([route_p, route_s], axis=1)
    src, dst, tile_expert, n_active = _routing_tables(
        route_all, cnt_s[:, 0].astype(I32), tm, n_tiles, plane_rows)

    y_tok = _moe_grouped(h_all, tile_expert, n_active, src, dst, wg, wu, wd, tm, MOE_FF_CHUNK,
                         plane_rows + n_all, n_all, plane_rows - n_all)
    yp = _combine(xp2, gt_p, route_p, g_final, y_tok, COMBINE_TILE, seq, 0, plane_rows)
    ys = _combine(xs, gt_s, route_s, g_final, y_tok, ns, ns, nb * seq, plane_rows)
    return yp.reshape(nb, seq, d), ys


def kernel(x_prompt, x_sample, c_prompt, c_sample, state_conv, state_pool, w_ada, b_ada, g_mix,
           g_ffn, w_in, conv_w, pool_w, pool_scale, g_conv_out, g_pool_out, w_out, dense_w_gate,
           dense_w_up, dense_w_down, router_w, moe_w_gate, moe_w_up, moe_w_down, g_final):
    depth = w_ada.shape[0]
    nb, seq, d = x_prompt.shape
    ns = x_sample.shape[0]
    assert x_sample.shape[1] == 1 and depth == 2
    assert sum(DENSE_FF_CHUNKS) == dense_w_gate.shape[-1]

    mod = _ada(jnp.concatenate([c_prompt, c_sample], axis=0), w_ada, b_ada)
    mod = mod.reshape(depth, nb + ns, 6, d)
    mod_p = [jnp.transpose(mod[i, :nb], (1, 0, 2))[:, :, None, :] for i in range(depth)]
    mod_s = [jnp.transpose(mod[i, nb:], (1, 0, 2))[:, None, :, :] for i in range(depth)]

    w_in_b = w_in.astype(BF16)
    w_out_b = w_out.astype(BF16)
    pool_w_b = pool_w.astype(BF16)
    dense_b = [w.astype(BF16) for w in (dense_w_gate, dense_w_up, dense_w_down)]
    cb_t = jnp.transpose(state_conv, (0, 2, 1, 3))
    pb_t = jnp.transpose(state_pool, (0, 2, 1, 3))

    xp = x_prompt
    xs = x_sample.reshape(ns, d)
    conv_p, pool_p, conv_s, pool_s = [], [], [], []
    for i in range(depth):
        sh1, sc1, gt1, sh2, sc2, gt2 = mod_p[i]
        mix_w = (g_mix[i], w_in_b[i], conv_w[i], pool_w_b[i], pool_scale[i], g_conv_out[i],
                 g_pool_out[i], w_out_b[i])
        xp, cs, ps = _mix_prompt(xp, sh1, sc1, gt1, *mix_w, tile=MIX_TILE, sub=MIX_ROW_BLOCK)
        conv_p.append(cs)
        pool_p.append(ps)
        s1, c1, t1, s2, c2, t2 = mod_s[i]
        xs, v_new, u_new = _mix_sample(xs, s1[0], c1[0], t1[0], *mix_w, cb_t[i], pb_t[i])
        conv_s.append(jnp.concatenate([state_conv[i][:, 1:], v_new[:, None, :]], axis=1))
        pool_s.append(jnp.concatenate([state_pool[i][:, 1:], u_new[:, None, :]], axis=1))
        j = i // 2
        if i % 2 == 0:
            wg, wu, wd = (w[j] for w in dense_b)
            xp = _ffn(xp, sh2, sc2, gt2, g_ffn[i], wg, wu, wd, PROMPT_TILE, DENSE_FF_CHUNKS)
            xs = _ffn(xs[None], s2, c2, t2, g_ffn[i], wg, wu, wd, ns, DENSE_FF_CHUNKS)[0]
        else:
            xp, xs = _moe_layer(xp, xs, (sh2, sc2, gt2), (s2, c2, t2), g_ffn[i], router_w[j],
                                moe_w_gate[j], moe_w_up[j], moe_w_down[j], g_final)

    return (xp, xs.reshape(ns, 1, d), jnp.stack(conv_p), jnp.stack(pool_p),
            jnp.stack(conv_s), jnp.stack(pool_s))
```

```python
import functools

import jax
import jax.numpy as jnp
from jax import lax
from jax.experimental import pallas as pl
from jax.experimental.pallas import tpu as pltpu

F32 = jnp.float32
BF16 = jnp.bfloat16
I32 = jnp.int32

EPS = 1e-6
CONV_K = 3
POOL_WINDOWS = (2, 4, 8, 16)
POOL_HIST = max(POOL_WINDOWS) - 1
TOP_K = 2

CONV_PAD = 8
POOL_PAD = 16

LANES = 128

ROUTE_ROWS = 8
ROW_EXPERT, ROW_RANK, ROW_PROB = 0, 2, 4

VMEM_LIMIT_BYTES = 56 * 1024 * 1024

ADA_COL_TILE = 2048
PROMPT_TILE = 512
MIX_TILE = 2048
MIX_ROW_BLOCK = 512
ROUTER_TILE = 1024
COMBINE_TILE = 1024
MOE_ROW_TILE = 1024
MOE_FF_CHUNK = 512
FILL_ROWS = 128
ROW_SLOTS = 3
DENSE_FF_CHUNKS = (768, 768, 768, 512)


def _params(*sem):
    return pltpu.CompilerParams(dimension_semantics=sem, vmem_limit_bytes=VMEM_LIMIT_BYTES)


def _resident(shape):
    nd = len(shape)
    return pl.BlockSpec(shape, lambda *_: (0,) * nd, pipeline_mode=pl.Buffered(1))


def _rms(x, g):
    ms = jnp.mean(x * x, axis=-1, keepdims=True)
    return x * lax.rsqrt(ms + EPS) * g


def _mod_norm(x, g, sc, sh):
    ms = jnp.mean(x * x, axis=-1, keepdims=True)
    return x * lax.rsqrt(ms + EPS) * (g * (1.0 + sc)) + sh


def _dot(a, b):
    return jnp.dot(a, b, preferred_element_type=F32)


def _dot_nt(a, b):
    return lax.dot_general(a, b, (((1,), (1,)), ((), ())), preferred_element_type=F32)


def _silu_mul(a, b):
    return a * jax.nn.sigmoid(a) * b


def _ada_kernel(c_ref, w_ref, b_ref, o_ref):
    c = c_ref[...]
    a = (c * jax.nn.sigmoid(c)).astype(BF16)
    o_ref[0] = _dot(a, w_ref[0].astype(BF16)) + b_ref[0]


def _ada(c_all, w_ada, b_ada):
    depth, d, n = w_ada.shape
    m = c_all.shape[0]
    tn = ADA_COL_TILE
    return pl.pallas_call(
        _ada_kernel,
        grid=(depth, n // tn),
        in_specs=[
            pl.BlockSpec((m, d), lambda i, j: (0, 0)),
            pl.BlockSpec((1, d, tn), lambda i, j: (i, 0, j)),
            pl.BlockSpec((1, 1, tn), lambda i, j: (i, 0, j)),
        ],
        out_specs=pl.BlockSpec((1, m, tn), lambda i, j: (i, 0, j)),
        out_shape=jax.ShapeDtypeStruct((depth, m, n), F32),
        compiler_params=_params("arbitrary", "arbitrary"),
        name="ada",
    )(c_all, w_ada, b_ada.reshape(depth, 1, n))


def _mix_tail(x, gt, bg, y, d_groups, poolw_ref, pscale_ref, gco_ref, gpo_ref, wout_ref):
    cw = gco_ref.shape[-1]
    ya = bg * y
    yb = jnp.concatenate(
        [_dot(d.astype(BF16), poolw_ref[g]) for g, d in enumerate(d_groups)], axis=-1
    ) * pscale_ref[...]
    ma = _rms(ya, gco_ref[...]).astype(BF16)
    mb = _rms(yb, gpo_ref[...]).astype(BF16)
    o = _dot(ma, wout_ref[0:cw, :]) + _dot(mb, wout_ref[cw:, :])
    return x + gt * o


def _mix_prompt_kernel(x_ref, sh_ref, sc_ref, gt_ref, g_ref, win_ref, convw_ref, poolw_ref,
                       pscale_ref, gco_ref, gpo_ref, wout_ref,
                       xo_ref, cs_ref, ps_ref, vbuf, ubuf, *, tile, sub):
    l = pl.program_id(1)
    cw = convw_ref.shape[-1]
    pg = poolw_ref.shape[-1]

    @pl.when(l == 0)
    def _():
        vbuf[0:CONV_PAD, :] = jnp.zeros((CONV_PAD, cw), F32)
        ubuf[0:POOL_PAD, :] = jnp.zeros((POOL_PAD, ubuf.shape[-1]), F32)

    gates = []
    for lo in range(0, tile, sub):
        x = x_ref[0, lo:lo + sub, :]
        h = _mod_norm(x, g_ref[...], sc_ref[0], sh_ref[0]).astype(BF16)
        p = _dot(h, win_ref[...])
        gates.append(p[:, 0:cw])
        vbuf[CONV_PAD + lo:CONV_PAD + lo + sub, :] = p[:, cw:2 * cw] * p[:, 2 * cw:3 * cw]
        ubuf[POOL_PAD + lo:POOL_PAD + lo + sub, :] = p[:, 3 * cw:]

    w = convw_ref[...]
    for bg, lo in zip(gates, range(0, tile, sub)):
        y = w[CONV_K - 1:CONV_K] * vbuf[CONV_PAD + lo:CONV_PAD + lo + sub, :]
        for k in range(1, CONV_K):
            y = y + w[CONV_K - 1 - k:CONV_K - k] * vbuf[CONV_PAD + lo - k:CONV_PAD + lo - k + sub, :]

        pos = l * tile + lo + lax.broadcasted_iota(I32, (sub, 1), 0)
        d_groups = []
        for g, win in enumerate(POOL_WINDOWS):
            ug = ubuf[POOL_PAD + lo:POOL_PAD + lo + sub, g * pg:(g + 1) * pg]
            acc = ug
            for k in range(1, win):
                acc = acc + ubuf[POOL_PAD + lo - k:POOL_PAD + lo - k + sub, g * pg:(g + 1) * pg]
            cnt = jnp.minimum(pos + 1, win).astype(F32)
            d_groups.append(acc / cnt - ug)

        xo_ref[0, lo:lo + sub, :] = _mix_tail(x_ref[0, lo:lo + sub, :], gt_ref[0], bg, y, d_groups,
                                              poolw_ref, pscale_ref, gco_ref, gpo_ref, wout_ref)

    cs_ref[0] = vbuf[CONV_PAD + tile - (CONV_K - 1):CONV_PAD + tile, :]
    ps_ref[0] = ubuf[POOL_PAD + tile - POOL_HIST:POOL_PAD + tile, :]
    vbuf[0:CONV_PAD, :] = vbuf[tile:tile + CONV_PAD, :]
    ubuf[0:POOL_PAD, :] = ubuf[tile:tile + POOL_PAD, :]


def _mix_prompt(x, sh, sc, gt, g, w_in, conv_w, pool_w, pool_scale, g_co, g_po, w_out, tile, sub):
    b, seq, d = x.shape
    assert seq % tile == 0 and tile % sub == 0 and sub >= POOL_PAD
    cw = conv_w.shape[-1]
    pw = pool_scale.shape[-1]
    row = lambda bi, li: (bi, 0, 0)
    return pl.pallas_call(
        functools.partial(_mix_prompt_kernel, tile=tile, sub=sub),
        grid=(b, seq // tile),
        in_specs=[
            pl.BlockSpec((1, tile, d), lambda bi, li: (bi, li, 0)),
            pl.BlockSpec((1, 1, d), row),
            pl.BlockSpec((1, 1, d), row),
            pl.BlockSpec((1, 1, d), row),
            _resident((1, d)),
            _resident(w_in.shape),
            _resident(conv_w.shape),
            _resident(pool_w.shape),
            _resident((1, pw)),
            _resident((1, cw)),
            _resident((1, pw)),
            _resident(w_out.shape),
        ],
        out_specs=[
            pl.BlockSpec((1, tile, d), lambda bi, li: (bi, li, 0)),
            pl.BlockSpec((1, CONV_K - 1, cw), row),
            pl.BlockSpec((1, POOL_HIST, pw), row),
        ],
        out_shape=[
            jax.ShapeDtypeStruct((b, seq, d), F32),
            jax.ShapeDtypeStruct((b, CONV_K - 1, cw), F32),
            jax.ShapeDtypeStruct((b, POOL_HIST, pw), F32),
        ],
        scratch_shapes=[
            pltpu.VMEM((CONV_PAD + tile, cw), F32),
            pltpu.VMEM((POOL_PAD + tile, pw), F32),
        ],
        compiler_params=_params("arbitrary", "arbitrary"),
        name="mix_prompt",
    )(x, sh, sc, gt, g.reshape(1, d), w_in, conv_w, pool_w, pool_scale.reshape(1, pw),
      g_co.reshape(1, cw), g_po.reshape(1, pw), w_out)


def _mix_sample_kernel(x_ref, sh_ref, sc_ref, gt_ref, g_ref, win_ref, convw_ref, poolw_ref,
                       pscale_ref, gco_ref, gpo_ref, wout_ref, cb_ref, pb_ref,
                       xo_ref, v_ref, u_ref):
    cw = convw_ref.shape[-1]
    pg = poolw_ref.shape[-1]
    x = x_ref[...]
    h = _mod_norm(x, g_ref[...], sc_ref[...], sh_ref[...]).astype(BF16)
    p = _dot(h, win_ref[...])
    bg = p[:, 0:cw]
    v = p[:, cw:2 * cw] * p[:, 2 * cw:3 * cw]
    u = p[:, 3 * cw:]
    v_ref[...] = v
    u_ref[...] = u

    w = convw_ref[...]
    y = w[CONV_K - 1:CONV_K] * v
    for k in range(1, CONV_K):
        y = y + w[CONV_K - 1 - k:CONV_K - k] * cb_ref[CONV_K - 1 - k]

    d_groups = []
    for g, win in enumerate(POOL_WINDOWS):
        ug = u[:, g * pg:(g + 1) * pg]
        acc = ug
        for k in range(1, win):
            acc = acc + pb_ref[POOL_HIST - k, :, g * pg:(g + 1) * pg]
        d_groups.append(acc / float(win) - ug)

    xo_ref[...] = _mix_tail(x, gt_ref[...], bg, y, d_groups, poolw_ref, pscale_ref,
                            gco_ref, gpo_ref, wout_ref)


def _mix_sample(x, sh, sc, gt, g, w_in, conv_w, pool_w, pool_scale, g_co, g_po, w_out, cb, pb):
    n, d = x.shape
    cw = conv_w.shape[-1]
    pw = pool_scale.shape[-1]
    return pl.pallas_call(
        _mix_sample_kernel,
        out_shape=[
            jax.ShapeDtypeStruct((n, d), F32),
            jax.ShapeDtypeStruct((n, cw), F32),
            jax.ShapeDtypeStruct((n, pw), F32),
        ],
        compiler_params=pltpu.CompilerParams(vmem_limit_bytes=VMEM_LIMIT_BYTES),
        name="mix_sample",
    )(x, sh, sc, gt, g.reshape(1, d), w_in, conv_w, pool_w, pool_scale.reshape(1, pw),
      g_co.reshape(1, cw), g_po.reshape(1, pw), w_out, cb, pb)


def _ffn_kernel(x_ref, sh_ref, sc_ref, gt_ref, g_ref, wg_ref, wu_ref, wd_ref, o_ref, *, chunks):
    x = x_ref[0]
    h = _mod_norm(x, g_ref[...], sc_ref[0], sh_ref[0]).astype(BF16)
    acc = None
    lo = 0
    for fc in chunks:
        a = _dot(h, wg_ref[:, lo:lo + fc])
        b = _dot(h, wu_ref[:, lo:lo + fc])
        part = _dot(_silu_mul(a, b).astype(BF16), wd_ref[lo:lo + fc, :])
        acc = part if acc is None else acc + part
        lo += fc
    o_ref[0] = x + gt_ref[0] * acc


def _ffn(x, sh, sc, gt, g, wg, wu, wd, tile, chunks):
    b, seq, d = x.shape
    tm = sh.shape[1]
    tmod = tile if tm > 1 else 1
    mod_map = (lambda bi, li: (bi, li, 0)) if tm > 1 else (lambda bi, li: (bi, 0, 0))
    return pl.pallas_call(
        functools.partial(_ffn_kernel, chunks=chunks),
        grid=(b, seq // tile),
        in_specs=[
            pl.BlockSpec((1, tile, d), lambda bi, li: (bi, li, 0)),
            pl.BlockSpec((1, tmod, d), mod_map),
            pl.BlockSpec((1, tmod, d), mod_map),
            pl.BlockSpec((1, tmod, d), mod_map),
            _resident((1, d)),
            _resident(wg.shape),
            _resident(wu.shape),
            _resident(wd.shape),
        ],
        out_specs=pl.BlockSpec((1, tile, d), lambda bi, li: (bi, li, 0)),
        out_shape=jax.ShapeDtypeStruct((b, seq, d), F32),
        compiler_params=_params("arbitrary", "arbitrary"),
        name="ffn",
    )(x, sh, sc, gt, g.reshape(1, d), wg, wu, wd)


def _split_bf16(a):
    hi = a.astype(BF16)
    return hi, (a - hi.astype(F32)).astype(BF16)


def _router_kernel(x_ref, sh_ref, sc_ref, g_ref, rw_ref, base_ref, h_ref, route_ref, cnt_ref,
                   tri, carry, *, n_steps):
    t = x_ref.shape[0]
    step = pl.program_id(0)

    @pl.when(step == 0)
    def _():
        r = lax.broadcasted_iota(I32, (t, t), 0)
        c = lax.broadcasted_iota(I32, (t, t), 1)
        tri[...] = jnp.where(r < c, 1.0, 0.0).astype(BF16)
        carry[...] = base_ref[...]

    @pl.when(step >= n_steps)
    def _():
        h_ref[...] = jnp.zeros_like(h_ref)

    @pl.when(step < n_steps)
    def _():
        h = _mod_norm(x_ref[...], g_ref[...], sc_ref[0], sh_ref[0])
        h_ref[...] = h
        h_hi, h_lo = _split_bf16(h)
        rw_hi, rw_lo = _split_bf16(rw_ref[...])
        logits = _dot_nt(rw_hi, h_hi) + _dot_nt(rw_hi, h_lo) + _dot_nt(rw_lo, h_hi)

        ne = logits.shape[0]
        idx = lax.broadcasted_iota(I32, logits.shape, 0)
        m1 = jnp.max(logits, axis=0, keepdims=True)
        i1 = jnp.min(jnp.where(logits == m1, idx, ne), axis=0, keepdims=True)
        sel1 = idx == i1
        rest = jnp.where(sel1, -jnp.inf, logits)
        m2 = jnp.max(rest, axis=0, keepdims=True)
        i2 = jnp.min(jnp.where(rest == m2, idx, ne), axis=0, keepdims=True)
        sel2 = idx == i2
        e = jnp.exp(m2 - m1)
        p1 = 1.0 / (1.0 + e)
        p2 = e / (1.0 + e)

        chosen = jnp.where(sel1, 1.0, jnp.where(sel2, 1.0, 0.0))
        before = _dot(chosen.astype(BF16), tri[...]) + carry[:, 0:1]
        r1 = jnp.sum(jnp.where(sel1, before, 0.0), axis=0, keepdims=True)
        r2 = jnp.sum(jnp.where(sel2, before, 0.0), axis=0, keepdims=True)
        carry[...] += jnp.sum(chosen, axis=1, keepdims=True)
        cnt_ref[...] = carry[...]

        rows = (i1.astype(F32), i2.astype(F32), r1, r2, p1, p2)
        route = jnp.zeros(logits.shape, F32)
        for k, row in enumerate(rows):
            route = jnp.where(idx == k, row, route)
        route_ref[...] = route


def _router(x2d, sh, sc, g, router_w, base_cnt, tile, rows_per_mod, h_rows, h_prev=None,
            row_off=0):
    n, d = x2d.shape
    ne = router_w.shape[-1]
    assert ne == ROUTE_ROWS and n % tile == 0 and row_off % tile == 0 and tile % LANES == 0
    assert base_cnt.shape == (ne, LANES)
    n_steps = n // tile
    tail = h_rows - (row_off + n)
    assert 0 <= tail < tile
    fill_tail = h_prev is None and tail > 0
    last = n_steps - 1
    clamp = (lambda i: jnp.minimum(i, last)) if fill_tail else (lambda i: i)
    per_row_mod = sh.shape[1] > 1
    if per_row_mod:
        mod_spec = pl.BlockSpec((1, tile, d), lambda i: (0, clamp(i), 0))
    else:
        assert rows_per_mod % tile == 0
        mod_spec = pl.BlockSpec((1, 1, d), lambda i: (clamp(i) // (rows_per_mod // tile), 0, 0))
    in_specs = [
        pl.BlockSpec((tile, d), lambda i: (clamp(i), 0)),
        mod_spec,
        mod_spec,
        _resident((1, d)),
        _resident((ne, d)),
        _resident((ne, LANES)),
    ]
    args = [x2d, sh, sc, g.reshape(1, d), router_w.T, base_cnt]
    n_in = len(args)
    aliases = {}
    body = functools.partial(_router_kernel, n_steps=n_steps)
    if h_prev is not None:
        assert h_prev.shape == (h_rows, d)
        in_specs.append(pl.BlockSpec(memory_space=pl.ANY))
        args.append(h_prev)
        aliases = {n_in: 0}

        def body(*refs):
            _router_kernel(*refs[:n_in], *refs[n_in + 1:], n_steps=n_steps)

    blk_off = row_off // tile
    return pl.pallas_call(
        body,
        grid=(n_steps + int(fill_tail),),
        in_specs=in_specs,
        out_specs=[
            pl.BlockSpec((tile, d), lambda i: (i + blk_off, 0)),
            pl.BlockSpec((ne, tile), lambda i: (0, clamp(i))),
            pl.BlockSpec((ne, LANES), lambda i: (0, 0)),
        ],
        out_shape=[
            jax.ShapeDtypeStruct((h_rows, d), F32),
            jax.ShapeDtypeStruct((ne, n), F32),
            jax.ShapeDtypeStruct((ne, LANES), F32),
        ],
        scratch_shapes=[pltpu.VMEM((tile, tile), BF16), pltpu.VMEM((ne, LANES), F32)],
        input_output_aliases=aliases,
        compiler_params=_params("arbitrary"),
        name="router",
    )(*args)


def _row_copy(src_hbm, row, dst, dst_row, sem):
    return pltpu.make_async_copy(src_hbm.at[pl.ds(row, 1), :], dst.at[pl.ds(dst_row, 1), :], sem)


def _row_copy_out(src, src_row, dst_hbm, row, sem):
    return pltpu.make_async_copy(src.at[pl.ds(src_row, 1), :], dst_hbm.at[pl.ds(row, 1), :], sem)


def _moe_grouped_kernel(te_ref, na_ref, src_ref, dst_ref, h_hbm, wg_ref, wu_ref, wd_ref, y_hbm,
                        xs, hbuf, obuf, gsem, ssem, fsem, *, tm, nc, rows_per_step, fill_row,
                        fill_blocks):
    del te_ref
    i = pl.program_id(0)
    c = pl.program_id(1)
    n_active = na_ref[0]
    ahead = ROW_SLOTS - 1
    slot = i % ROW_SLOTS

    def request(tile, r):
        s = tile % ROW_SLOTS
        return _row_copy(h_hbm, src_ref[tile * tm + r], xs.at[s], r, gsem.at[s])

    def send(tile, r):
        s = (tile + ROW_SLOTS) % ROW_SLOTS
        return _row_copy_out(obuf.at[s], r, y_hbm, dst_ref[(tile + 1) * tm + r], ssem.at[s])

    @pl.when(c == 0)
    def _():
        @pl.when(i == 0)
        def _():
            last = ROW_SLOTS - 1
            obuf[last] = jnp.zeros((tm, obuf.shape[-1]), F32)
            fills = [pltpu.make_async_copy(obuf.at[last, pl.ds(0, FILL_ROWS), :],
                                           y_hbm.at[pl.ds(fill_row + q * FILL_ROWS, FILL_ROWS), :],
                                           fsem) for q in range(fill_blocks)]
            for f in fills:
                f.start()
            for f in fills:
                f.wait()

            def body(r, carry):
                for t in range(ahead):
                    request(t, r).start()
                return carry

            lax.fori_loop(0, tm, body, 0)

        @pl.when(i < n_active + ahead)
        def _():
            pltpu.make_async_copy(h_hbm.at[pl.ds(0, tm), :], xs.at[slot], gsem.at[slot]).wait()

        @pl.when(jnp.logical_and(i >= ahead, i < n_active + ROW_SLOTS))
        def _():
            pltpu.make_async_copy(obuf.at[slot], y_hbm.at[pl.ds(0, tm), :], ssem.at[slot]).wait()

        @pl.when(i < n_active)
        def _():
            obuf[slot] = jnp.zeros((tm, obuf.shape[-1]), F32)
            hbuf[...] = xs[slot].astype(BF16)
            for r in range(rows_per_step * nc, tm):
                request(i + ahead, r).start()
                send(i - 1, r).start()

        @pl.when(i == n_active)
        def _():
            def body(r, carry):
                send(i - 1, r).start()
                return carry

            lax.fori_loop(0, tm, body, 0)

    @pl.when(i < n_active)
    def _():
        for k in range(rows_per_step):
            r = c * rows_per_step + k
            request(i + ahead, r).start(priority=1)
            send(i - 1, r).start(priority=1)

        h = hbuf[...]
        a = _dot(h, wg_ref[0].astype(BF16))
        b = _dot(h, wu_ref[0].astype(BF16))
        obuf[slot] += _dot(_silu_mul(a, b).astype(BF16), wd_ref[0].astype(BF16))


def _moe_grouped(h_all, tile_expert, n_active, src, dst, wg, wu, wd, tm, fc, y_rows, fill_row,
                 fill_rows):
    ne, d, dff = wg.shape
    n_tiles = tile_expert.shape[0]
    nc = dff // fc
    assert dff % fc == 0
    assert src.shape[0] == n_tiles * tm and dst.shape[0] == n_tiles * tm
    assert fill_rows % FILL_ROWS == 0 and tm >= FILL_ROWS
    rows_per_step = tm // nc

    def w_col(i, c, te, na, src_, dst_):
        return (te[i], 0, jnp.where(i < na[0], c, nc - 1))

    def w_row(i, c, te, na, src_, dst_):
        return (te[i], jnp.where(i < na[0], c, nc - 1), 0)

    grid_spec = pltpu.PrefetchScalarGridSpec(
        num_scalar_prefetch=4,
        grid=(n_tiles, nc),
        in_specs=[
            pl.BlockSpec(memory_space=pl.ANY),
            pl.BlockSpec((1, d, fc), w_col),
            pl.BlockSpec((1, d, fc), w_col),
            pl.BlockSpec((1, fc, d), w_row),
        ],
        out_specs=pl.BlockSpec(memory_space=pl.ANY),
        scratch_shapes=[
            pltpu.VMEM((ROW_SLOTS, tm, d), F32),
            pltpu.VMEM((tm, d), BF16),
            pltpu.VMEM((ROW_SLOTS, tm, d), F32),
            pltpu.SemaphoreType.DMA((ROW_SLOTS,)),
            pltpu.SemaphoreType.DMA((ROW_SLOTS,)),
            pltpu.SemaphoreType.DMA,
        ],
    )
    return pl.pallas_call(
        functools.partial(_moe_grouped_kernel, tm=tm, nc=nc, rows_per_step=rows_per_step,
                          fill_row=fill_row, fill_blocks=fill_rows // FILL_ROWS),
        grid_spec=grid_spec,
        out_shape=jax.ShapeDtypeStruct((y_rows, d), F32),
        compiler_params=_params("arbitrary", "arbitrary"),
        name="moe_grouped",
    )(tile_expert, n_active, src, dst, h_all, wg, wu, wd)


def _combine_kernel(x_ref, gt_ref, route_ref, gf_ref, y0_ref, y1_ref, o_ref):
    route = route_ref[...]
    pad = jnp.zeros((LANES - route.shape[0], route.shape[1]), F32)
    cols = jnp.concatenate([route, pad], axis=0).T
    f = (cols[:, ROW_PROB:ROW_PROB + 1] * y0_ref[...]
         + cols[:, ROW_PROB + 1:ROW_PROB + 2] * y1_ref[...])
    o_ref[...] = _rms(x_ref[...] + gt_ref[0] * f, gf_ref[...])


def _combine(x2d, gt, route, g_final, y_tok, tile, rows_per_mod, tok_off, plane_rows):
    n, d = x2d.shape
    assert n % tile == 0 and tok_off % tile == 0 and plane_rows % tile == 0
    per_row_mod = gt.shape[1] > 1
    if per_row_mod:
        mod_spec = pl.BlockSpec((1, tile, d), lambda j: (0, j, 0))
    else:
        assert rows_per_mod % tile == 0
        mod_spec = pl.BlockSpec((1, 1, d), lambda j: (j // (rows_per_mod // tile), 0, 0))
    first = tok_off // tile
    second = (plane_rows + tok_off) // tile
    return pl.pallas_call(
        _combine_kernel,
        grid=(n // tile,),
        in_specs=[
            pl.BlockSpec((tile, d), lambda j: (j, 0)),
            mod_spec,
            pl.BlockSpec((ROUTE_ROWS, tile), lambda j: (0, j)),
            pl.BlockSpec((1, d), lambda j: (0, 0)),
            pl.BlockSpec((tile, d), lambda j: (first + j, 0)),
            pl.BlockSpec((tile, d), lambda j: (second + j, 0)),
        ],
        out_specs=pl.BlockSpec((tile, d), lambda j: (j, 0)),
        out_shape=jax.ShapeDtypeStruct((n, d), F32),
        compiler_params=_params("arbitrary"),
        name="combine",
    )(x2d, gt, route, g_final.reshape(1, d), y_tok, y_tok)


def _invert_kernel(perm_ref, inv_ref):
    def place(a, carry):
        inv_ref[perm_ref[a]] = a
        return carry

    lax.fori_loop(0, perm_ref.shape[0], place, 0, unroll=32)


def _invert(perm):
    assert perm.shape[0] % 32 == 0
    smem = pl.BlockSpec(memory_space=pltpu.SMEM)
    return pl.pallas_call(
        _invert_kernel,
        in_specs=[smem],
        out_specs=smem,
        out_shape=jax.ShapeDtypeStruct(perm.shape, I32),
        name="invert",
    )(perm)


def _routing_tables(route_all, counts, tm, n_tiles, plane_rows):
    ne = counts.shape[0]
    n = route_all.shape[1]
    e_idx = route_all[ROW_EXPERT:ROW_EXPERT + TOP_K].astype(I32)
    rank = route_all[ROW_RANK:ROW_RANK + TOP_K].astype(I32)
    padded = (counts + tm - 1) // tm * tm
    ends = jnp.cumsum(padded)
    offs = ends - padded
    onehot = e_idx[..., None] == jnp.arange(ne, dtype=I32)
    pos = jnp.sum(jnp.where(onehot, offs, 0), axis=-1) + rank
    n_active = ends[-1] // tm
    starts = jnp.arange(n_tiles, dtype=I32) * tm
    tile_expert = jnp.sum(starts[:, None] >= ends[None, :], axis=1).astype(I32)
    last_expert = jnp.sum((n_active - 1) * tm >= ends).astype(I32)
    tile_expert = jnp.minimum(tile_expert, last_expert)
    n_rows = n_tiles * tm
    n_real = n * TOP_K
    pad_ends = jnp.cumsum(padded - counts)
    j = jnp.arange(n_rows - n_real, dtype=I32)[:, None]
    in_expert = jnp.logical_and(j >= pad_ends - (padded - counts), j < pad_ends)
    pad_pos = jnp.where(j[:, 0] < pad_ends[-1],
                        jnp.sum(jnp.where(in_expert, offs + counts + j - pad_ends + padded - counts, 0),
                                axis=1),
                        ends[-1] + j[:, 0] - pad_ends[-1])
    inv = _invert(jnp.concatenate([pos.reshape(-1), pad_pos]))
    is_real = inv < n_real
    tok = inv % n
    choice = inv // n
    spare = n + jnp.minimum(inv - n_real, ne * tm - 1)
    src = jnp.where(is_real, tok, 0)
    dst = jnp.where(is_real, choice * plane_rows + tok, spare)
    lead = n + ne * tm + jnp.arange(tm, dtype=I32)
    dst = jnp.concatenate([lead, dst])[:n_rows]
    return src, dst, tile_expert, n_active.astype(I32).reshape(1)


def _moe_layer(xp, xs, mod_p, mod_s, g, router_w, wg, wu, wd, g_final):
    nb, seq, d = xp.shape
    ns = xs.shape[0]
    ne = router_w.shape[-1]
    n_all = nb * seq + ns
    sh_p, sc_p, gt_p = mod_p
    sh_s, sc_s, gt_s = mod_s
    xp2 = xp.reshape(nb * seq, d)

    zero_cnt = jnp.zeros((ne, LANES), F32)
    h_all, route_p, cnt_p = _router(xp2, sh_p, sc_p, g, router_w, zero_cnt, ROUTER_TILE, seq,
                                    n_all)
    h_all, route_s, cnt_s = _router(xs, sh_s, sc_s, g, router_w, cnt_p, ns, ns, n_all,
                                    h_prev=h_all, row_off=nb * seq)

    tm = MOE_ROW_TILE
    n_tiles = (n_all * TOP_K) // tm + ne + ROW_SLOTS
    plane_rows = -(-(n_all + (ne + 1) * tm) // COMBINE_TILE) * COMBINE_TILE
    route_all = jnp.concatenate([route_p, route_s], axis=1)
    src, dst, tile_expert, n_active = _routing_tables(
        route_all, cnt_s[:, 0].astype(I32), tm, n_tiles, plane_rows)

    y_tok = _moe_grouped(h_all, tile_expert, n_active, src, dst, wg, wu, wd, tm, MOE_FF_CHUNK,
                         plane_rows + n_all, n_all, plane_rows - n_all)
    yp = _combine(xp2, gt_p, route_p, g_final, y_tok, COMBINE_TILE, seq, 0, plane_rows)
    ys = _combine(xs, gt_s, route_s, g_final, y_tok, ns, ns, nb * seq, plane_rows)
    return yp.reshape(nb, seq, d), ys


def kernel(x_prompt, x_sample, c_prompt, c_sample, state_conv, state_pool, w_ada, b_ada, g_mix,
           g_ffn, w_in, conv_w, pool_w, pool_scale, g_conv_out, g_pool_out, w_out, dense_w_gate,
           dense_w_up, dense_w_down, router_w, moe_w_gate, moe_w_up, moe_w_down, g_final):
    depth = w_ada.shape[0]
    nb, seq, d = x_prompt.shape
    ns = x_sample.shape[0]
    assert x_sample.shape[1] == 1 and depth == 2
    assert sum(DENSE_FF_CHUNKS) == dense_w_gate.shape[-1]

    mod = _ada(jnp.concatenate([c_prompt, c_sample], axis=0), w_ada, b_ada)
    mod = mod.reshape(depth, nb + ns, 6, d)
    mod_p = [jnp.transpose(mod[i, :nb], (1, 0, 2))[:, :, None, :] for i in range(depth)]
    mod_s = [jnp.transpose(mod[i, nb:], (1, 0, 2))[:, None, :, :] for i in range(depth)]

    w_in_b = w_in.astype(BF16)
    w_out_b = w_out.astype(BF16)
    pool_w_b = pool_w.astype(BF16)
    dense_b = [w.astype(BF16) for w in (dense_w_gate, dense_w_up, dense_w_down)]
    cb_t = jnp.transpose(state_conv, (0, 2, 1, 3))
    pb_t = jnp.transpose(state_pool, (0, 2, 1, 3))

    xp = x_prompt
    xs = x_sample.reshape(ns, d)
    conv_p, pool_p, conv_s, pool_s = [], [], [], []
    for i in range(depth):
        sh1, sc1, gt1, sh2, sc2, gt2 = mod_p[i]
        mix_w = (g_mix[i], w_in_b[i], conv_w[i], pool_w_b[i], pool_scale[i], g_conv_out[i],
                 g_pool_out[i], w_out_b[i])
        xp, cs, ps = _mix_prompt(xp, sh1, sc1, gt1, *mix_w, tile=MIX_TILE, sub=MIX_ROW_BLOCK)
        conv_p.append(cs)
        pool_p.append(ps)
        s1, c1, t1, s2, c2, t2 = mod_s[i]
        xs, v_new, u_new = _mix_sample(xs, s1[0], c1[0], t1[0], *mix_w, cb_t[i], pb_t[i])
        conv_s.append(jnp.concatenate([state_conv[i][:, 1:], v_new[:, None, :]], axis=1))
        pool_s.append(jnp.concatenate([state_pool[i][:, 1:], u_new[:, None, :]], axis=1))
        j = i // 2
        if i % 2 == 0:
            wg, wu, wd = (w[j] for w in dense_b)
            xp = _ffn(xp, sh2, sc2, gt2, g_ffn[i], wg, wu, wd, PROMPT_TILE, DENSE_FF_CHUNKS)
            xs = _ffn(xs[None], s2, c2, t2, g_ffn[i], wg, wu, wd, ns, DENSE_FF_CHUNKS)[0]
        else:
            xp, xs = _moe_layer(xp, xs, (sh2, sc2, gt2), (s2, c2, t2), g_ffn[i], router_w[j],
                                moe_w_gate[j], moe_w_up[j], moe_w_down[j], g_final)

    return (xp, xs.reshape(ns, 1, d), jnp.stack(conv_p), jnp.stack(pool_p),
            jnp.stack(conv_s), jnp.stack(pool_s))
```

```python
import functools

import jax
import jax.numpy as jnp
from jax import lax
from jax.experimental import pallas as pl
from jax.experimental.pallas import tpu as pltpu

F32 = jnp.float32
BF16 = jnp.bfloat16
I32 = jnp.int32

EPS = 1e-6
CONV_K = 3
POOL_WINDOWS = (2, 4, 8, 16)
POOL_HIST = max(POOL_WINDOWS) - 1
TOP_K = 2

CONV_PAD = 8
POOL_PAD = 16

LANES = 128

ROUTE_ROWS = 8
ROW_EXPERT, ROW_RANK, ROW_PROB = 0, 2, 4

VMEM_LIMIT_BYTES = 56 * 1024 * 1024

ADA_COL_TILE = 2048
PROMPT_TILE = 1024
MIX_TILE = 2048
MIX_ROW_BLOCK = 512
ROUTER_TILE = 1024
COMBINE_TILE = 1024
MOE_ROW_TILE = 1024
MOE_FF_CHUNK = 512
FILL_ROWS = 128
ROW_SLOTS = 3
DENSE_FF_CHUNKS = (768, 768, 768, 512)


def _params(*sem):
    return pltpu.CompilerParams(dimension_semantics=sem, vmem_limit_bytes=VMEM_LIMIT_BYTES)


def _resident(shape):
    nd = len(shape)
    return pl.BlockSpec(shape, lambda *_: (0,) * nd, pipeline_mode=pl.Buffered(1))


def _rms(x, g):
    ms = jnp.mean(x * x, axis=-1, keepdims=True)
    return x * lax.rsqrt(ms + EPS) * g


def _mod_norm(x, g, sc, sh):
    ms = jnp.mean(x * x, axis=-1, keepdims=True)
    return x * lax.rsqrt(ms + EPS) * (g * (1.0 + sc)) + sh


def _dot(a, b):
    return jnp.dot(a, b, preferred_element_type=F32)


def _dot_nt(a, b):
    return lax.dot_general(a, b, (((1,), (1,)), ((), ())), preferred_element_type=F32)


def _silu_mul(a, b):
    return a * jax.nn.sigmoid(a) * b


def _ada_kernel(c_ref, w_ref, b_ref, o_ref):
    c = c_ref[...]
    a = (c * jax.nn.sigmoid(c)).astype(BF16)
    o_ref[0] = _dot(a, w_ref[0].astype(BF16)) + b_ref[0]


def _ada(c_all, w_ada, b_ada):
    depth, d, n = w_ada.shape
    m = c_all.shape[0]
    tn = ADA_COL_TILE
    return pl.pallas_call(
        _ada_kernel,
        grid=(depth, n // tn),
        in_specs=[
            pl.BlockSpec((m, d), lambda i, j: (0, 0)),
            pl.BlockSpec((1, d, tn), lambda i, j: (i, 0, j)),
            pl.BlockSpec((1, 1, tn), lambda i, j: (i, 0, j)),
        ],
        out_specs=pl.BlockSpec((1, m, tn), lambda i, j: (i, 0, j)),
        out_shape=jax.ShapeDtypeStruct((depth, m, n), F32),
        compiler_params=_params("arbitrary", "arbitrary"),
        name="ada",
    )(c_all, w_ada, b_ada.reshape(depth, 1, n))


def _mix_tail(x, gt, bg, y, d_groups, poolw_ref, pscale_ref, gco_ref, gpo_ref, wout_ref):
    cw = gco_ref.shape[-1]
    ya = bg * y
    yb = jnp.concatenate(
        [_dot(d.astype(BF16), poolw_ref[g]) for g, d in enumerate(d_groups)], axis=-1
    ) * pscale_ref[...]
    ma = _rms(ya, gco_ref[...]).astype(BF16)
    mb = _rms(yb, gpo_ref[...]).astype(BF16)
    o = _dot(ma, wout_ref[0:cw, :]) + _dot(mb, wout_ref[cw:, :])
    return x + gt * o


def _mix_prompt_kernel(x_ref, sh_ref, sc_ref, gt_ref, g_ref, win_ref, convw_ref, poolw_ref,
                       pscale_ref, gco_ref, gpo_ref, wout_ref,
                       xo_ref, cs_ref, ps_ref, vbuf, ubuf, *, tile, sub):
    l = pl.program_id(1)
    cw = convw_ref.shape[-1]
    pg = poolw_ref.shape[-1]

    @pl.when(l == 0)
    def _():
        vbuf[0:CONV_PAD, :] = jnp.zeros((CONV_PAD, cw), F32)
        ubuf[0:POOL_PAD, :] = jnp.zeros((POOL_PAD, ubuf.shape[-1]), F32)

    gates = []
    for lo in range(0, tile, sub):
        x = x_ref[0, lo:lo + sub, :]
        h = _mod_norm(x, g_ref[...], sc_ref[0], sh_ref[0]).astype(BF16)
        p = _dot(h, win_ref[...])
        gates.append(p[:, 0:cw])
        vbuf[CONV_PAD + lo:CONV_PAD + lo + sub, :] = p[:, cw:2 * cw] * p[:, 2 * cw:3 * cw]
        ubuf[POOL_PAD + lo:POOL_PAD + lo + sub, :] = p[:, 3 * cw:]

    w = convw_ref[...]
    for bg, lo in zip(gates, range(0, tile, sub)):
        y = w[CONV_K - 1:CONV_K] * vbuf[CONV_PAD + lo:CONV_PAD + lo + sub, :]
        for k in range(1, CONV_K):
            y = y + w[CONV_K - 1 - k:CONV_K - k] * vbuf[CONV_PAD + lo - k:CONV_PAD + lo - k + sub, :]

        pos = l * tile + lo + lax.broadcasted_iota(I32, (sub, 1), 0)
        d_groups = []
        for g, win in enumerate(POOL_WINDOWS):
            ug = ubuf[POOL_PAD + lo:POOL_PAD + lo + sub, g * pg:(g + 1) * pg]
            acc = ug
            for k in range(1, win):
                acc = acc + ubuf[POOL_PAD + lo - k:POOL_PAD + lo - k + sub, g * pg:(g + 1) * pg]
            cnt = jnp.minimum(pos + 1, win).astype(F32)
            d_groups.append(acc / cnt - ug)

        xo_ref[0, lo:lo + sub, :] = _mix_tail(x_ref[0, lo:lo + sub, :], gt_ref[0], bg, y, d_groups,
                                              poolw_ref, pscale_ref, gco_ref, gpo_ref, wout_ref)

    cs_ref[0] = vbuf[CONV_PAD + tile - (CONV_K - 1):CONV_PAD + tile, :]
    ps_ref[0] = ubuf[POOL_PAD + tile - POOL_HIST:POOL_PAD + tile, :]
    vbuf[0:CONV_PAD, :] = vbuf[tile:tile + CONV_PAD, :]
    ubuf[0:POOL_PAD, :] = ubuf[tile:tile + POOL_PAD, :]


def _mix_prompt(x, sh, sc, gt, g, w_in, conv_w, pool_w, pool_scale, g_co, g_po, w_out, tile, sub):
    b, seq, d = x.shape
    assert seq % tile == 0 and tile % sub == 0 and sub >= POOL_PAD
    cw = conv_w.shape[-1]
    pw = pool_scale.shape[-1]
    row = lambda bi, li: (bi, 0, 0)
    return pl.pallas_call(
        functools.partial(_mix_prompt_kernel, tile=tile, sub=sub),
        grid=(b, seq // tile),
        in_specs=[
            pl.BlockSpec((1, tile, d), lambda bi, li: (bi, li, 0)),
            pl.BlockSpec((1, 1, d), row),
            pl.BlockSpec((1, 1, d), row),
            pl.BlockSpec((1, 1, d), row),
            _resident((1, d)),
            _resident(w_in.shape),
            _resident(conv_w.shape),
            _resident(pool_w.shape),
            _resident((1, pw)),
            _resident((1, cw)),
            _resident((1, pw)),
            _resident(w_out.shape),
        ],
        out_specs=[
            pl.BlockSpec((1, tile, d), lambda bi, li: (bi, li, 0)),
            pl.BlockSpec((1, CONV_K - 1, cw), row),
            pl.BlockSpec((1, POOL_HIST, pw), row),
        ],
        out_shape=[
            jax.ShapeDtypeStruct((b, seq, d), F32),
            jax.ShapeDtypeStruct((b, CONV_K - 1, cw), F32),
            jax.ShapeDtypeStruct((b, POOL_HIST, pw), F32),
        ],
        scratch_shapes=[
            pltpu.VMEM((CONV_PAD + tile, cw), F32),
            pltpu.VMEM((POOL_PAD + tile, pw), F32),
        ],
        compiler_params=_params("arbitrary", "arbitrary"),
        name="mix_prompt",
    )(x, sh, sc, gt, g.reshape(1, d), w_in, conv_w, pool_w, pool_scale.reshape(1, pw),
      g_co.reshape(1, cw), g_po.reshape(1, pw), w_out)


def _mix_sample_kernel(x_ref, sh_ref, sc_ref, gt_ref, g_ref, win_ref, convw_ref, poolw_ref,
                       pscale_ref, gco_ref, gpo_ref, wout_ref, cb_ref, pb_ref,
                       xo_ref, v_ref, u_ref):
    cw = convw_ref.shape[-1]
    pg = poolw_ref.shape[-1]
    x = x_ref[...]
    h = _mod_norm(x, g_ref[...], sc_ref[...], sh_ref[...]).astype(BF16)
    p = _dot(h, win_ref[...])
    bg = p[:, 0:cw]
    v = p[:, cw:2 * cw] * p[:, 2 * cw:3 * cw]
    u = p[:, 3 * cw:]
    v_ref[...] = v
    u_ref[...] = u

    w = convw_ref[...]
    y = w[CONV_K - 1:CONV_K] * v
    for k in range(1, CONV_K):
        y = y + w[CONV_K - 1 - k:CONV_K - k] * cb_ref[CONV_K - 1 - k]

    d_groups = []
    for g, win in enumerate(POOL_WINDOWS):
        ug = u[:, g * pg:(g + 1) * pg]
        acc = ug
        for k in range(1, win):
            acc = acc + pb_ref[POOL_HIST - k, :, g * pg:(g + 1) * pg]
        d_groups.append(acc / float(win) - ug)

    xo_ref[...] = _mix_tail(x, gt_ref[...], bg, y, d_groups, poolw_ref, pscale_ref,
                            gco_ref, gpo_ref, wout_ref)


def _mix_sample(x, sh, sc, gt, g, w_in, conv_w, pool_w, pool_scale, g_co, g_po, w_out, cb, pb):
    n, d = x.shape
    cw = conv_w.shape[-1]
    pw = pool_scale.shape[-1]
    return pl.pallas_call(
        _mix_sample_kernel,
        out_shape=[
            jax.ShapeDtypeStruct((n, d), F32),
            jax.ShapeDtypeStruct((n, cw), F32),
            jax.ShapeDtypeStruct((n, pw), F32),
        ],
        compiler_params=pltpu.CompilerParams(vmem_limit_bytes=VMEM_LIMIT_BYTES),
        name="mix_sample",
    )(x, sh, sc, gt, g.reshape(1, d), w_in, conv_w, pool_w, pool_scale.reshape(1, pw),
      g_co.reshape(1, cw), g_po.reshape(1, pw), w_out, cb, pb)


def _ffn_kernel(x_ref, sh_ref, sc_ref, gt_ref, g_ref, wg_ref, wu_ref, wd_ref, o_ref, *, chunks):
    x = x_ref[0]
    h = _mod_norm(x, g_ref[...], sc_ref[0], sh_ref[0]).astype(BF16)
    acc = None
    lo = 0
    for fc in chunks:
        a = _dot(h, wg_ref[:, lo:lo + fc])
        b = _dot(h, wu_ref[:, lo:lo + fc])
        part = _dot(_silu_mul(a, b).astype(BF16), wd_ref[lo:lo + fc, :])
        acc = part if acc is None else acc + part
        lo += fc
    o_ref[0] = x + gt_ref[0] * acc


def _ffn(x, sh, sc, gt, g, wg, wu, wd, tile, chunks):
    b, seq, d = x.shape
    tm = sh.shape[1]
    tmod = tile if tm > 1 else 1
    mod_map = (lambda bi, li: (bi, li, 0)) if tm > 1 else (lambda bi, li: (bi, 0, 0))
    return pl.pallas_call(
        functools.partial(_ffn_kernel, chunks=chunks),
        grid=(b, seq // tile),
        in_specs=[
            pl.BlockSpec((1, tile, d), lambda bi, li: (bi, li, 0)),
            pl.BlockSpec((1, tmod, d), mod_map),
            pl.BlockSpec((1, tmod, d), mod_map),
            pl.BlockSpec((1, tmod, d), mod_map),
            _resident((1, d)),
            _resident(wg.shape),
            _resident(wu.shape),
            _resident(wd.shape),
        ],
        out_specs=pl.BlockSpec((1, tile, d), lambda bi, li: (bi, li, 0)),
        out_shape=jax.ShapeDtypeStruct((b, seq, d), F32),
        compiler_params=_params("arbitrary", "arbitrary"),
        name="ffn",
    )(x, sh, sc, gt, g.reshape(1, d), wg, wu, wd)


def _split_bf16(a):
    hi = a.astype(BF16)
    return hi, (a - hi.astype(F32)).astype(BF16)


def _router_kernel(x_ref, sh_ref, sc_ref, g_ref, rw_ref, base_ref, h_ref, route_ref, cnt_ref,
                   tri, carry, *, n_steps):
    t = x_ref.shape[0]
    step = pl.program_id(0)

    @pl.when(step == 0)
    def _():
        r = lax.broadcasted_iota(I32, (t, t), 0)
        c = lax.broadcasted_iota(I32, (t, t), 1)
        tri[...] = jnp.where(r < c, 1.0, 0.0).astype(BF16)
        carry[...] = base_ref[...]

    @pl.when(step >= n_steps)
    def _():
        h_ref[...] = jnp.zeros_like(h_ref)

    @pl.when(step < n_steps)
    def _():
        h = _mod_norm(x_ref[...], g_ref[...], sc_ref[0], sh_ref[0])
        h_ref[...] = h
        h_hi, h_lo = _split_bf16(h)
        rw_hi, rw_lo = _split_bf16(rw_ref[...])
        logits = _dot_nt(rw_hi, h_hi) + _dot_nt(rw_hi, h_lo) + _dot_nt(rw_lo, h_hi)

        ne = logits.shape[0]
        idx = lax.broadcasted_iota(I32, logits.shape, 0)
        m1 = jnp.max(logits, axis=0, keepdims=True)
        i1 = jnp.min(jnp.where(logits == m1, idx, ne), axis=0, keepdims=True)
        sel1 = idx == i1
        rest = jnp.where(sel1, -jnp.inf, logits)
        m2 = jnp.max(rest, axis=0, keepdims=True)
        i2 = jnp.min(jnp.where(rest == m2, idx, ne), axis=0, keepdims=True)
        sel2 = idx == i2
        e = jnp.exp(m2 - m1)
        p1 = 1.0 / (1.0 + e)
        p2 = e / (1.0 + e)

        chosen = jnp.where(sel1, 1.0, jnp.where(sel2, 1.0, 0.0))
        before = _dot(chosen.astype(BF16), tri[...]) + carry[:, 0:1]
        r1 = jnp.sum(jnp.where(sel1, before, 0.0), axis=0, keepdims=True)
        r2 = jnp.sum(jnp.where(sel2, before, 0.0), axis=0, keepdims=True)
        carry[...] += jnp.sum(chosen, axis=1, keepdims=True)
        cnt_ref[...] = carry[...]

        rows = (i1.astype(F32), i2.astype(F32), r1, r2, p1, p2)
        route = jnp.zeros(logits.shape, F32)
        for k, row in enumerate(rows):
            route = jnp.where(idx == k, row, route)
        route_ref[...] = route


def _router(x2d, sh, sc, g, router_w, base_cnt, tile, rows_per_mod, h_rows, h_prev=None,
            row_off=0):
    n, d = x2d.shape
    ne = router_w.shape[-1]
    assert ne == ROUTE_ROWS and n % tile == 0 and row_off % tile == 0 and tile % LANES == 0
    assert base_cnt.shape == (ne, LANES)
    n_steps = n // tile
    tail = h_rows - (row_off + n)
    assert 0 <= tail < tile
    fill_tail = h_prev is None and tail > 0
    last = n_steps - 1
    clamp = (lambda i: jnp.minimum(i, last)) if fill_tail else (lambda i: i)
    per_row_mod = sh.shape[1] > 1
    if per_row_mod:
        mod_spec = pl.BlockSpec((1, tile, d), lambda i: (0, clamp(i), 0))
    else:
        assert rows_per_mod % tile == 0
        mod_spec = pl.BlockSpec((1, 1, d), lambda i: (clamp(i) // (rows_per_mod // tile), 0, 0))
    in_specs = [
        pl.BlockSpec((tile, d), lambda i: (clamp(i), 0)),
        mod_spec,
        mod_spec,
        _resident((1, d)),
        _resident((ne, d)),
        _resident((ne, LANES)),
    ]
    args = [x2d, sh, sc, g.reshape(1, d), router_w.T, base_cnt]
    n_in = len(args)
    aliases = {}
    body = functools.partial(_router_kernel, n_steps=n_steps)
    if h_prev is not None:
        assert h_prev.shape == (h_rows, d)
        in_specs.append(pl.BlockSpec(memory_space=pl.ANY))
        args.append(h_prev)
        aliases = {n_in: 0}

        def body(*refs):
            _router_kernel(*refs[:n_in], *refs[n_in + 1:], n_steps=n_steps)

    blk_off = row_off // tile
    return pl.pallas_call(
        body,
        grid=(n_steps + int(fill_tail),),
        in_specs=in_specs,
        out_specs=[
            pl.BlockSpec((tile, d), lambda i: (i + blk_off, 0)),
            pl.BlockSpec((ne, tile), lambda i: (0, clamp(i))),
            pl.BlockSpec((ne, LANES), lambda i: (0, 0)),
        ],
        out_shape=[
            jax.ShapeDtypeStruct((h_rows, d), F32),
            jax.ShapeDtypeStruct((ne, n), F32),
            jax.ShapeDtypeStruct((ne, LANES), F32),
        ],
        scratch_shapes=[pltpu.VMEM((tile, tile), BF16), pltpu.VMEM((ne, LANES), F32)],
        input_output_aliases=aliases,
        compiler_params=_params("arbitrary"),
        name="router",
    )(*args)


def _row_copy(src_hbm, row, dst, dst_row, sem):
    return pltpu.make_async_copy(src_hbm.at[pl.ds(row, 1), :], dst.at[pl.ds(dst_row, 1), :], sem)


def _row_copy_out(src, src_row, dst_hbm, row, sem):
    return pltpu.make_async_copy(src.at[pl.ds(src_row, 1), :], dst_hbm.at[pl.ds(row, 1), :], sem)


def _moe_grouped_kernel(te_ref, na_ref, src_ref, dst_ref, h_hbm, wg_ref, wu_ref, wd_ref, y_hbm,
                        xs, hbuf, obuf, gsem, ssem, fsem, *, tm, nc, rows_per_step, fill_row,
                        fill_blocks):
    del te_ref
    i = pl.program_id(0)
    c = pl.program_id(1)
    n_active = na_ref[0]
    ahead = ROW_SLOTS - 1
    slot = i % ROW_SLOTS

    def request(tile, r):
        s = tile % ROW_SLOTS
        return _row_copy(h_hbm, src_ref[tile * tm + r], xs.at[s], r, gsem.at[s])

    def send(tile, r):
        s = (tile + ROW_SLOTS) % ROW_SLOTS
        return _row_copy_out(obuf.at[s], r, y_hbm, dst_ref[(tile + 1) * tm + r], ssem.at[s])

    @pl.when(c == 0)
    def _():
        @pl.when(i == 0)
        def _():
            last = ROW_SLOTS - 1
            obuf[last] = jnp.zeros((tm, obuf.shape[-1]), F32)
            fills = [pltpu.make_async_copy(obuf.at[last, pl.ds(0, FILL_ROWS), :],
                                           y_hbm.at[pl.ds(fill_row + q * FILL_ROWS, FILL_ROWS), :],
                                           fsem) for q in range(fill_blocks)]
            for f in fills:
                f.start()
            for f in fills:
                f.wait()

            def body(r, carry):
                for t in range(ahead):
                    request(t, r).start()
                return carry

            lax.fori_loop(0, tm, body, 0)

        @pl.when(i < n_active + ahead)
        def _():
            pltpu.make_async_copy(h_hbm.at[pl.ds(0, tm), :], xs.at[slot], gsem.at[slot]).wait()

        @pl.when(jnp.logical_and(i >= ahead, i < n_active + ROW_SLOTS))
        def _():
            pltpu.make_async_copy(obuf.at[slot], y_hbm.at[pl.ds(0, tm), :], ssem.at[slot]).wait()

        @pl.when(i < n_active)
        def _():
            obuf[slot] = jnp.zeros((tm, obuf.shape[-1]), F32)
            hbuf[...] = xs[slot].astype(BF16)
            for r in range(rows_per_step * nc, tm):
                request(i + ahead, r).start()
                send(i - 1, r).start()

        @pl.when(i == n_active)
        def _():
            def body(r, carry):
                send(i - 1, r).start()
                return carry

            lax.fori_loop(0, tm, body, 0)

    @pl.when(i < n_active)
    def _():
        for k in range(rows_per_step):
            r = c * rows_per_step + k
            request(i + ahead, r).start(priority=1)
            send(i - 1, r).start()

        h = hbuf[...]
        a = _dot(h, wg_ref[0].astype(BF16))
        b = _dot(h, wu_ref[0].astype(BF16))
        obuf[slot] += _dot(_silu_mul(a, b).astype(BF16), wd_ref[0].astype(BF16))


def _moe_grouped(h_all, tile_expert, n_active, src, dst, wg, wu, wd, tm, fc, y_rows, fill_row,
                 fill_rows):
    ne, d, dff = wg.shape
    n_tiles = tile_expert.shape[0]
    nc = dff // fc
    assert dff % fc == 0
    assert src.shape[0] == n_tiles * tm and dst.shape[0] == n_tiles * tm
    assert fill_rows % FILL_ROWS == 0 and tm >= FILL_ROWS
    rows_per_step = tm // nc

    def w_col(i, c, te, na, src_, dst_):
        return (te[i], 0, jnp.where(i < na[0], c, nc - 1))

    def w_row(i, c, te, na, src_, dst_):
        return (te[i], jnp.where(i < na[0], c, nc - 1), 0)

    grid_spec = pltpu.PrefetchScalarGridSpec(
        num_scalar_prefetch=4,
        grid=(n_tiles, nc),
        in_specs=[
            pl.BlockSpec(memory_space=pl.ANY),
            pl.BlockSpec((1, d, fc), w_col),
            pl.BlockSpec((1, d, fc), w_col),
            pl.BlockSpec((1, fc, d), w_row),
        ],
        out_specs=pl.BlockSpec(memory_space=pl.ANY),
        scratch_shapes=[
            pltpu.VMEM((ROW_SLOTS, tm, d), F32),
            pltpu.VMEM((tm, d), BF16),
            pltpu.VMEM((ROW_SLOTS, tm, d), F32),
            pltpu.SemaphoreType.DMA((ROW_SLOTS,)),
            pltpu.SemaphoreType.DMA((ROW_SLOTS,)),
            pltpu.SemaphoreType.DMA,
        ],
    )
    return pl.pallas_call(
        functools.partial(_moe_grouped_kernel, tm=tm, nc=nc, rows_per_step=rows_per_step,
                          fill_row=fill_row, fill_blocks=fill_rows // FILL_ROWS),
        grid_spec=grid_spec,
        out_shape=jax.ShapeDtypeStruct((y_rows, d), F32),
        compiler_params=_params("arbitrary", "arbitrary"),
        name="moe_grouped",
    )(tile_expert, n_active, src, dst, h_all, wg, wu, wd)


def _combine_kernel(x_ref, gt_ref, route_ref, gf_ref, y0_ref, y1_ref, o_ref):
    route = route_ref[...]
    pad = jnp.zeros((LANES - route.shape[0], route.shape[1]), F32)
    cols = jnp.concatenate([route, pad], axis=0).T
    f = (cols[:, ROW_PROB:ROW_PROB + 1] * y0_ref[...]
         + cols[:, ROW_PROB + 1:ROW_PROB + 2] * y1_ref[...])
    o_ref[...] = _rms(x_ref[...] + gt_ref[0] * f, gf_ref[...])


def _combine(x2d, gt, route, g_final, y_tok, tile, rows_per_mod, tok_off, plane_rows):
    n, d = x2d.shape
    assert n % tile == 0 and tok_off % tile == 0 and plane_rows % tile == 0
    per_row_mod = gt.shape[1] > 1
    if per_row_mod:
        mod_spec = pl.BlockSpec((1, tile, d), lambda j: (0, j, 0))
    else:
        assert rows_per_mod % tile == 0
        mod_spec = pl.BlockSpec((1, 1, d), lambda j: (j // (rows_per_mod // tile), 0, 0))
    first = tok_off // tile
    second = (plane_rows + tok_off) // tile
    return pl.pallas_call(
        _combine_kernel,
        grid=(n // tile,),
        in_specs=[
            pl.BlockSpec((tile, d), lambda j: (j, 0)),
            mod_spec,
            pl.BlockSpec((ROUTE_ROWS, tile), lambda j: (0, j)),
            pl.BlockSpec((1, d), lambda j: (0, 0)),
            pl.BlockSpec((tile, d), lambda j: (first + j, 0)),
            pl.BlockSpec((tile, d), lambda j: (second + j, 0)),
        ],
        out_specs=pl.BlockSpec((tile, d), lambda j: (j, 0)),
        out_shape=jax.ShapeDtypeStruct((n, d), F32),
        compiler_params=_params("arbitrary"),
        name="combine",
    )(x2d, gt, route, g_final.reshape(1, d), y_tok, y_tok)


def _invert_kernel(perm_ref, inv_ref):
    def place(a, carry):
        inv_ref[perm_ref[a]] = a
        return carry

    lax.fori_loop(0, perm_ref.shape[0], place, 0, unroll=32)


def _invert(perm):
    assert perm.shape[0] % 32 == 0
    smem = pl.BlockSpec(memory_space=pltpu.SMEM)
    return pl.pallas_call(
        _invert_kernel,
        in_specs=[smem],
        out_specs=smem,
        out_shape=jax.ShapeDtypeStruct(perm.shape, I32),
        name="invert",
    )(perm)


def _routing_tables(route_all, counts, tm, n_tiles, plane_rows):
    ne = counts.shape[0]
    n = route_all.shape[1]
    e_idx = route_all[ROW_EXPERT:ROW_EXPERT + TOP_K].astype(I32)
    rank = route_all[ROW_RANK:ROW_RANK + TOP_K].astype(I32)
    padded = (counts + tm - 1) // tm * tm
    ends = jnp.cumsum(padded)
    offs = ends - padded
    onehot = e_idx[..., None] == jnp.arange(ne, dtype=I32)
    pos = jnp.sum(jnp.where(onehot, offs, 0), axis=-1) + rank
    n_active = ends[-1] // tm
    starts = jnp.arange(n_tiles, dtype=I32) * tm
    tile_expert = jnp.sum(starts[:, None] >= ends[None, :], axis=1).astype(I32)
    last_expert = jnp.sum((n_active - 1) * tm >= ends).astype(I32)
    tile_expert = jnp.minimum(tile_expert, last_expert)
    n_rows = n_tiles * tm
    n_real = n * TOP_K
    pad_ends = jnp.cumsum(padded - counts)
    j = jnp.arange(n_rows - n_real, dtype=I32)[:, None]
    in_expert = jnp.logical_and(j >= pad_ends - (padded - counts), j < pad_ends)
    pad_pos = jnp.where(j[:, 0] < pad_ends[-1],
                        jnp.sum(jnp.where(in_expert, offs + counts + j - pad_ends + padded - counts, 0),
                                axis=1),
                        ends[-1] + j[:, 0] - pad_ends[-1])
    inv = _invert(jnp.concatenate([pos.reshape(-1), pad_pos]))
    is_real = inv < n_real
    tok = inv % n
    choice = inv // n
    spare = n + jnp.minimum(inv - n_real, ne * tm - 1)
    src = jnp.where(is_real, tok, 0)
    dst = jnp.where(is_real, choice * plane_rows + tok, spare)
    lead = n + ne * tm + jnp.arange(tm, dtype=I32)
    dst = jnp.concatenate([lead, dst])[:n_rows]
    return src, dst, tile_expert, n_active.astype(I32).reshape(1)


def _moe_layer(xp, xs, mod_p, mod_s, g, router_w, wg, wu, wd, g_final):
    nb, seq, d = xp.shape
    ns = xs.shape[0]
    ne = router_w.shape[-1]
    n_all = nb * seq + ns
    sh_p, sc_p, gt_p = mod_p
    sh_s, sc_s, gt_s = mod_s
    xp2 = xp.reshape(nb * seq, d)

    zero_cnt = jnp.zeros((ne, LANES), F32)
    h_all, route_p, cnt_p = _router(xp2, sh_p, sc_p, g, router_w, zero_cnt, ROUTER_TILE, seq,
                                    n_all)
    h_all, route_s, cnt_s = _router(xs, sh_s, sc_s, g, router_w, cnt_p, ns, ns, n_all,
                                    h_prev=h_all, row_off=nb * seq)

    tm = MOE_ROW_TILE
    n_tiles = (n_all * TOP_K) // tm + ne + ROW_SLOTS
    plane_rows = -(-(n_all + (ne + 1) * tm) // COMBINE_TILE) * COMBINE_TILE
    route_all = jnp.concatenate([route_p, route_s], axis=1)
    src, dst, tile_expert, n_active = _routing_tables(
        route_all, cnt_s[:, 0].astype(I32), tm, n_tiles, plane_rows)

    y_tok = _moe_grouped(h_all, tile_expert, n_active, src, dst, wg, wu, wd, tm, MOE_FF_CHUNK,
                         plane_rows + n_all, n_all, plane_rows - n_all)
    yp = _combine(xp2, gt_p, route_p, g_final, y_tok, COMBINE_TILE, seq, 0, plane_rows)
    ys = _combine(xs, gt_s, route_s, g_final, y_tok, ns, ns, nb * seq, plane_rows)
    return yp.reshape(nb, seq, d), ys


def kernel(x_prompt, x_sample, c_prompt, c_sample, state_conv, state_pool, w_ada, b_ada, g_mix,
           g_ffn, w_in, conv_w, pool_w, pool_scale, g_conv_out, g_pool_out, w_out, dense_w_gate,
           dense_w_up, dense_w_down, router_w, moe_w_gate, moe_w_up, moe_w_down, g_final):
    depth = w_ada.shape[0]
    nb, seq, d = x_prompt.shape
    ns = x_sample.shape[0]
    assert x_sample.shape[1] == 1 and depth == 2
    assert sum(DENSE_FF_CHUNKS) == dense_w_gate.shape[-1]

    mod = _ada(jnp.concatenate([c_prompt, c_sample], axis=0), w_ada, b_ada)
    mod = mod.reshape(depth, nb + ns, 6, d)
    mod_p = [jnp.transpose(mod[i, :nb], (1, 0, 2))[:, :, None, :] for i in range(depth)]
    mod_s = [jnp.transpose(mod[i, nb:], (1, 0, 2))[:, None, :, :] for i in range(depth)]

    w_in_b = w_in.astype(BF16)
    w_out_b = w_out.astype(BF16)
    pool_w_b = pool_w.astype(BF16)
    dense_b = [w.astype(BF16) for w in (dense_w_gate, dense_w_up, dense_w_down)]
    cb_t = jnp.transpose(state_conv, (0, 2, 1, 3))
    pb_t = jnp.transpose(state_pool, (0, 2, 1, 3))

    xp = x_prompt
    xs = x_sample.reshape(ns, d)
    conv_p, pool_p, conv_s, pool_s = [], [], [], []
    for i in range(depth):
        sh1, sc1, gt1, sh2, sc2, gt2 = mod_p[i]
        mix_w = (g_mix[i], w_in_b[i], conv_w[i], pool_w_b[i], pool_scale[i], g_conv_out[i],
                 g_pool_out[i], w_out_b[i])
        xp, cs, ps = _mix_prompt(xp, sh1, sc1, gt1, *mix_w, tile=MIX_TILE, sub=MIX_ROW_BLOCK)
        conv_p.append(cs)
        pool_p.append(ps)
        s1, c1, t1, s2, c2, t2 = mod_s[i]
        xs, v_new, u_new = _mix_sample(xs, s1[0], c1[0], t1[0], *mix_w, cb_t[i], pb_t[i])
        conv_s.append(jnp.concatenate([state_conv[i][:, 1:], v_new[:, None, :]], axis=1))
        pool_s.append(jnp.concatenate([state_pool[i][:, 1:], u_new[:, None, :]], axis=1))
        j = i // 2
        if i % 2 == 0:
            wg, wu, wd = (w[j] for w in dense_b)
            xp = _ffn(xp, sh2, sc2, gt2, g_ffn[i], wg, wu, wd, PROMPT_TILE, DENSE_FF_CHUNKS)
            xs = _ffn(xs[None], s2, c2, t2, g_ffn[i], wg, wu, wd, ns, DENSE_FF_CHUNKS)[0]
        else:
            xp, xs = _moe_layer(xp, xs, (sh2, sc2, gt2), (s2, c2, t2), g_ffn[i], router_w[j],
                                moe_w_gate[j], moe_w_up[j], moe_w_down[j], g_final)

    return (xp, xs.reshape(ns, 1, d), jnp.stack(conv_p), jnp.stack(pool_p),
            jnp.stack(conv_s), jnp.stack(pool_s))
```
